```python
import math
import jax, jax.numpy as jnp
from jax import lax
import numpy as np

D_MODEL = 1024
BATCH = 8
SEQ = 2048
DEPTH = 1

HEAD_DIM = 64
N_HEADS_A = 8
N_HEADS_B = 4
N_HEADS_MEM = 4
W_A = N_HEADS_A * HEAD_DIM
W_B = N_HEADS_B * HEAD_DIM
W_M = N_HEADS_MEM * HEAD_DIM
MIX_WIDTH = W_A + W_B + W_M
DILATED_PATTERNS = ((128, 1), (512, 4), (2048, 16))
N_IDX_HEADS = 8
IDX_DIM = 64
TOPK_MAX = 256
N_MEM = 256
Q_BLOCK = 128
RMS_EPS = 1e-6
SPLIT_SIZES = (W_A, W_A, W_A, W_B, HEAD_DIM, HEAD_DIM, W_M, MIX_WIDTH,
               N_IDX_HEADS * IDX_DIM, IDX_DIM, N_IDX_HEADS)
IN_COLS = W_A * 3 + W_B + 2 * HEAD_DIM + W_M + MIX_WIDTH + N_IDX_HEADS * IDX_DIM + IDX_DIM + N_IDX_HEADS

kernel_name = "hybrid_dilated_dsa_memory_block"


def rms_norm(x, g):
    x32 = x.astype(jnp.float32)
    y = x32 * lax.rsqrt(jnp.mean(x32 * x32, axis=-1, keepdims=True) + RMS_EPS)
    return (y * g.astype(jnp.float32)).astype(x.dtype)


def alibi_slopes(n):
    return 2.0 ** (-8.0 * jnp.arange(1, n + 1, dtype=jnp.float32) / n)


def dilated_band_attention(q, k, v, slopes, window, dilation):
    B, T, H, Dh = q.shape
    band = window // dilation
    n = T // dilation
    nb = -(-n // band)
    n_pad = nb * band

    def to_sub(a):
        a = a.reshape(B, n, dilation, H, Dh).transpose(0, 2, 3, 1, 4)
        a = jnp.pad(a, ((0, 0), (0, 0), (0, 0), (0, n_pad - n), (0, 0)))
        return a.reshape(B, dilation, H, nb, band, Dh)

    def with_prev(a):
        prev = jnp.pad(a, ((0, 0), (0, 0), (0, 0), (1, 0), (0, 0), (0, 0)))[:, :, :, :-1]
        return jnp.concatenate([prev, a], axis=4)

    qs = to_sub(q)
    kb = with_prev(to_sub(k))
    vb = with_prev(to_sub(v))
    s = jnp.einsum('brhnqc,brhnkc->brhnqk', qs, kb).astype(jnp.float32) * (Dh ** -0.5)
    qi = jnp.arange(band)[:, None]
    ki = jnp.arange(2 * band)[None, :]
    dist = band + qi - ki
    blk = jnp.arange(nb)[:, None, None]
    valid = (dist >= 0) & (dist <= band) & (blk * band + ki - band >= 0)
    bias = -slopes[:, None, None, None] * (dist * dilation).astype(jnp.float32)[None, None]
    s = jnp.where(valid, s + bias, -jnp.inf)
    m = jnp.max(s, axis=-1, keepdims=True)
    p = jnp.exp(s - m)
    l = jnp.sum(p, axis=-1, keepdims=True)
    o = jnp.einsum('brhnqk,brhnkc->brhnqc', p.astype(vb.dtype), vb).astype(jnp.float32) / l
    lse = (m + jnp.log(l))[..., 0]
    o = o.reshape(B, dilation, H, n_pad, Dh)[:, :, :, :n].transpose(0, 3, 1, 2, 4).reshape(B, T, H, Dh)
    lse = lse.reshape(B, dilation, H, n_pad)[..., :n].transpose(0, 3, 1, 2).reshape(B, T, H)
    return o, lse


def dilated_mixture_attention(q, k, v, slopes):
    outs, lses = [], []
    for window, dilation in DILATED_PATTERNS:
        o, lse = dilated_band_attention(q, k, v, slopes, window, dilation)
        outs.append(o)
        lses.append(lse)
    wts = jax.nn.softmax(jnp.stack(lses, axis=0), axis=0)
    return jnp.sum(wts[..., None] * jnp.stack(outs, axis=0), axis=0)


def dsa_attention(qB, kB, vB, qI, kI, wI, slopes):
    B, T, H, Dh = qB.shape
    k_sel = min(TOPK_MAX, T // 4)
    nblk = T // Q_BLOCK
    pos = jnp.arange(T)

    def blocks(a):
        return a.reshape(B, nblk, Q_BLOCK, *a.shape[2:]).swapaxes(0, 1)

    def one_block(args):
        qb, qib, wib, tq = args
        logits = jnp.einsum('bqhc,bsc->bqhs', qib, kI).astype(jnp.float32) * (IDX_DIM ** -0.5)
        score = jnp.einsum('bqh,bqhs->bqs', wib.astype(jnp.float32) * (N_IDX_HEADS ** -0.5),
                           jax.nn.relu(logits))
        causal = pos[None, :] <= tq[:, None]
        score = jnp.where(causal, score, -jnp.inf)
        _, idx = lax.top_k(score, k_sel)
        gather = jax.vmap(lambda kv, ii: kv[ii])
        k_g = gather(kB, idx)
        v_g = gather(vB, idx)
        s = jnp.einsum('bqhc,bqkc->bhqk', qb, k_g).astype(jnp.float32) * (Dh ** -0.5)
        dist = (tq[:, None] - idx).astype(jnp.float32)
        s = s - slopes[None, :, None, None] * dist[:, None]
        admissible = (idx <= tq[:, None])[:, None]
        s = jnp.where(admissible, s, -jnp.inf)
        p = jax.nn.softmax(s, axis=-1)
        return jnp.einsum('bhqk,bqkc->bqhc', p.astype(v_g.dtype), v_g)

    out = lax.map(one_block, (blocks(qB), blocks(qI), blocks(wI), pos.reshape(nblk, Q_BLOCK)))
    return out.swapaxes(0, 1).reshape(B, T, H, Dh)


def memory_attention(q, k, v):
    s = jnp.einsum('bthc,bmhc->bhtm', q, k).astype(jnp.float32) * (q.shape[-1] ** -0.5)
    p = jax.nn.softmax(s, axis=-1)
    return jnp.einsum('bhtm,bmhc->bthc', p.astype(v.dtype), v)


def setup_inputs(seed: int = 0) -> dict:
    key = jax.random.key(seed)
    ks = jax.random.split(key, 8)
    f32 = jnp.float32
    x = jax.random.normal(ks[0], (BATCH, SEQ, D_MODEL), f32)
    mem = jax.random.normal(ks[1], (BATCH, N_MEM, D_MODEL), f32)
    g_in = 1.0 + 0.01 * jax.random.normal(ks[2], (DEPTH, D_MODEL), f32)
    g_mem = 1.0 + 0.01 * jax.random.normal(ks[3], (DEPTH, D_MODEL), f32)
    w_in = jax.random.normal(ks[4], (DEPTH, D_MODEL, IN_COLS), f32) * (D_MODEL ** -0.5)
    w_mem_kv = jax.random.normal(ks[5], (DEPTH, D_MODEL, 2 * W_M), f32) * (D_MODEL ** -0.5)
    w_out = jax.random.normal(ks[6], (DEPTH, MIX_WIDTH, D_MODEL), f32) * (MIX_WIDTH ** -0.5)
    g_final = 1.0 + 0.01 * jax.random.normal(ks[7], (D_MODEL,), f32)
    return {"x": x, "mem": mem, "g_in": g_in, "g_mem": g_mem, "w_in": w_in,
            "w_mem_kv": w_mem_kv, "w_out": w_out, "g_final": g_final}


def reference(x, mem, g_in, g_mem, w_in, w_mem_kv, w_out, g_final):
    B, T, _ = x.shape
    split_points = np.cumsum(np.array(SPLIT_SIZES))[:-1].tolist()
    slopes_a = alibi_slopes(N_HEADS_A)
    slopes_b = alibi_slopes(N_HEADS_B)
    for layer in range(DEPTH):
        h = rms_norm(x, g_in[layer])
        proj = h @ w_in[layer]
        qA, kA, vA, qB, kB, vB, qM, gate, qI, kI, wI = jnp.split(proj, split_points, axis=-1)
        qA = qA.reshape(B, T, N_HEADS_A, HEAD_DIM)
        kA = kA.reshape(B, T, N_HEADS_A, HEAD_DIM)
        vA = vA.reshape(B, T, N_HEADS_A, HEAD_DIM)
        qB = qB.reshape(B, T, N_HEADS_B, HEAD_DIM)
        qM = qM.reshape(B, T, N_HEADS_MEM, HEAD_DIM)
        qI = qI.reshape(B, T, N_IDX_HEADS, IDX_DIM)

        mem_n = rms_norm(mem, g_mem[layer])
        kvM = mem_n @ w_mem_kv[layer]
        kM, vM = jnp.split(kvM, 2, axis=-1)
        kM = kM.reshape(B, -1, N_HEADS_MEM, HEAD_DIM)
        vM = vM.reshape(B, -1, N_HEADS_MEM, HEAD_DIM)

        oA = dilated_mixture_attention(qA, kA, vA, slopes_a)
        oB = dsa_attention(qB, kB, vB, qI, kI, wI, slopes_b)
        oM = memory_attention(qM, kM, vM)

        o = jnp.concatenate([oA.astype(x.dtype).reshape(B, T, W_A),
                             oB.astype(x.dtype).reshape(B, T, W_B),
                             oM.astype(x.dtype).reshape(B, T, W_M)], axis=-1)
        o = o * jax.nn.silu(gate)
        x = x + o @ w_out[layer]
    return rms_norm(x, g_final)
```

```python
import functools

import jax
import jax.numpy as jnp
from jax import lax
from jax.experimental import pallas as pl
from jax.experimental.pallas import tpu as pltpu

F32 = jnp.float32
BF16 = jnp.bfloat16
I32 = jnp.int32

D_MODEL = 1024
HEAD_DIM = 64
N_HEADS_A = 8
N_HEADS_B = 4
N_HEADS_MEM = 4
W_A = N_HEADS_A * HEAD_DIM
W_B = N_HEADS_B * HEAD_DIM
W_M = N_HEADS_MEM * HEAD_DIM
MIX_WIDTH = W_A + W_B + W_M
DILATIONS = (1, 4, 16)
BAND = 128
N_IDX_HEADS = 8
IDX_DIM = 64
TOPK_MAX = 256
RMS_EPS = 1e-6
SPLIT_SIZES = (W_A, W_A, W_A, W_B, HEAD_DIM, HEAD_DIM, W_M, MIX_WIDTH,
               N_IDX_HEADS * IDX_DIM, IDX_DIM, N_IDX_HEADS)

LANES = 128
BLK = 128
NEG = -1e30
INT_MIN = -(2 ** 31)
VMEM_LIMIT = 48 * 1024 * 1024

R_GATE = 0
R_QM = MIX_WIDTH
R_KB = R_QM + W_M
R_KI = R_KB + LANES
R_COLS = R_KI + LANES
T_QI = 0
T_QB = N_IDX_HEADS * IDX_DIM
T_VB = T_QB + W_B
T_ROWS = T_VB + HEAD_DIM
WI_ROWS = 16


def _dot(a, b):
    return jnp.dot(a, b, preferred_element_type=F32)


def _dot_nt(a, b):
    return lax.dot_general(a, b, (((1,), (1,)), ((), ())), preferred_element_type=F32)


def _rms(x, g):
    return x * lax.rsqrt(jnp.mean(x * x, axis=-1, keepdims=True) + RMS_EPS) * g


def _proj_kernel(x_ref, g_ref, wa_ref, wr_ref, wt_ref, ww_ref, oa_ref, or_ref, ot_ref, ow_ref):
    hb = _rms(x_ref[0], g_ref[...]).astype(BF16)
    oa_ref[0] = _dot(hb, wa_ref[...])
    or_ref[0] = _dot(hb, wr_ref[...]).astype(BF16)
    ot_ref[0] = _dot_nt(wt_ref[...], hb).astype(BF16)
    ow_ref[0] = _dot_nt(ww_ref[...], hb)


def _proj(x, g, wa, wr, wt, ww, tm=256):
    B, T, D = x.shape
    const = lambda b, i: (0, 0)
    return pl.pallas_call(
        _proj_kernel,
        grid=(B, T // tm),
        in_specs=[
            pl.BlockSpec((1, tm, D), lambda b, i: (b, i, 0)),
            pl.BlockSpec((1, D), const),
            pl.BlockSpec(wa.shape, const),
            pl.BlockSpec(wr.shape, const),
            pl.BlockSpec(wt.shape, const),
            pl.BlockSpec(ww.shape, const),
        ],
        out_specs=[
            pl.BlockSpec((1, tm, 3 * W_A), lambda b, i: (b, i, 0)),
            pl.BlockSpec((1, tm, R_COLS), lambda b, i: (b, i, 0)),
            pl.BlockSpec((1, T_ROWS, tm), lambda b, i: (b, 0, i)),
            pl.BlockSpec((1, WI_ROWS, tm), lambda b, i: (b, 0, i)),
        ],
        out_shape=[
            jax.ShapeDtypeStruct((B, T, 3 * W_A), F32),
            jax.ShapeDtypeStruct((B, T, R_COLS), BF16),
            jax.ShapeDtypeStruct((B, T_ROWS, T), BF16),
            jax.ShapeDtypeStruct((B, WI_ROWS, T), F32),
        ],
        compiler_params=pltpu.CompilerParams(
            dimension_semantics=("parallel", "parallel"), vmem_limit_bytes=VMEM_LIMIT),
        name="proj",
    )(x, g, wa, wr, wt, ww)


def _attn_a_kernel(slopes_ref, q_ref, k_ref, v_ref, o_ref, s_ref, *, seq):
    hp = pl.program_id(1)
    row = lax.broadcasted_iota(I32, (BLK, BLK), 0)
    col = lax.broadcasted_iota(I32, (BLK, BLK), 1)
    d_cur = (row - col).astype(F32)
    d_prev = d_cur + float(BAND)
    ok_cur = row >= col
    ok_prev = col >= row

    for hh in range(2):
        slope = slopes_ref[hp * 2 + hh]
        lo = hh * HEAD_DIM

        def head_cols(a):
            return a[:, lo:lo + HEAD_DIM].astype(BF16)

        for p, dil in enumerate(DILATIONS):
            n_blocks = seq // dil // BLK
            sd = slope * float(dil)
            bias_cur = jnp.where(ok_cur, -sd * d_cur, NEG)
            bias_prev = jnp.where(ok_prev, -sd * d_prev, NEG)

            def block(i, carry, dil=dil, n_blocks=n_blocks, bias_cur=bias_cur,
                      bias_prev=bias_prev, p=p):
                r = i // n_blocks
                n = i % n_blocks
                start_q = r + n * (dil * BLK)
                start_p = jnp.maximum(start_q - dil * BLK, r)
                pen = jnp.where(n > 0, 0.0, NEG).astype(F32)

                def rows(ref, start):
                    if dil == 1:
                        return ref[pl.ds(start, BLK), :]
                    return ref[pl.ds(start, BLK, stride=dil), :]

                qb = head_cols(rows(q_ref, start_q))
                kc = head_cols(rows(k_ref, start_q))
                vc = head_cols(rows(v_ref, start_q))
                kp = head_cols(rows(k_ref, start_p))
                vp = head_cols(rows(v_ref, start_p))
                sc = _dot_nt(qb, kc) + bias_cur
                sp = _dot_nt(qb, kp) + bias_prev + pen
                m = jnp.maximum(jnp.max(sc, axis=1, keepdims=True),
                                jnp.max(sp, axis=1, keepdims=True))
                pc = jnp.exp(sc - m)
                pp = jnp.exp(sp - m)
                l = jnp.sum(pc, axis=1, keepdims=True) + jnp.sum(pp, axis=1, keepdims=True)
                o = (_dot(pc.astype(BF16), vc) + _dot(pp.astype(BF16), vp)) / l
                lse = m + jnp.log(l)
                val = jnp.concatenate([o, jnp.broadcast_to(lse, (BLK, HEAD_DIM))], axis=1)
                if dil == 1:
                    s_ref[p, pl.ds(start_q, BLK), :] = val
                else:
                    s_ref[p, pl.ds(start_q, BLK, stride=dil), :] = val
                return carry

            lax.fori_loop(0, seq // BLK, block, 0)

        o1, l1 = s_ref[0, :, :HEAD_DIM], s_ref[0, :, HEAD_DIM:]
        o2, l2 = s_ref[1, :, :HEAD_DIM], s_ref[1, :, HEAD_DIM:]
        o3, l3 = s_ref[2, :, :HEAD_DIM], s_ref[2, :, HEAD_DIM:]
        mx = jnp.maximum(jnp.maximum(l1, l2), l3)
        w1 = jnp.exp(l1 - mx)
        w2 = jnp.exp(l2 - mx)
        w3 = jnp.exp(l3 - mx)
        o_ref[:, lo:lo + HEAD_DIM] = (w1 * o1 + w2 * o2 + w3 * o3) / (w1 + w2 + w3)


def _attn_a(slopes, proj_a):
    B, T, _ = proj_a.shape
    n_pairs = N_HEADS_A // 2

    def spec(off):
        return pl.BlockSpec((None, T, LANES), lambda b, h: (b, 0, off + h))

    return pl.pallas_call(
        functools.partial(_attn_a_kernel, seq=T),
        grid=(B, n_pairs),
        in_specs=[pl.BlockSpec(memory_space=pltpu.SMEM), spec(0), spec(n_pairs), spec(2 * n_pairs)],
        out_specs=pl.BlockSpec((None, T, LANES), lambda b, h: (b, 0, h)),
        out_shape=jax.ShapeDtypeStruct((B, T, W_A), F32),
        scratch_shapes=[pltpu.VMEM((len(DILATIONS), T, LANES), F32)],
        compiler_params=pltpu.CompilerParams(
            dimension_semantics=("parallel", "parallel"), vmem_limit_bytes=VMEM_LIMIT),
        name="attn_a",
    )(slopes, proj_a, proj_a, proj_a)


def _dsa_kernel(slopes_ref, kk_ref, tq_ref, vt_ref, wt_ref, o_ref, key_ref, *, seq, k_sel):
    qb = pl.program_id(1)
    n_chunks = qb + 1
    t_lane = qb * BLK + lax.broadcasted_iota(I32, (1, BLK), 1)
    row = lax.broadcasted_iota(I32, (BLK, BLK), 0)
    ws = wt_ref[0:N_IDX_HEADS, :] * (N_IDX_HEADS ** -0.5)

    def chunk_rows(c):
        return pl.ds(pl.multiple_of(c * BLK, BLK), BLK)

    def score_chunk(c, carry):
        ki = kk_ref[chunk_rows(c), R_KI - R_KB:R_KI - R_KB + IDX_DIM]
        acc = jnp.zeros((BLK, BLK), F32)
        for h in range(N_IDX_HEADS):
            lg = _dot(ki, tq_ref[T_QI + h * IDX_DIM:T_QI + (h + 1) * IDX_DIM, :])
            acc = acc + jnp.maximum(lg, 0.0) * ws[h:h + 1, :]
        bits = pltpu.bitcast(acc, I32)
        key = bits ^ ((bits >> 31) & 0x7FFFFFFF)
        causal = (c * BLK + row) <= t_lane
        key_ref[chunk_rows(c), :] = jnp.where(causal, key, INT_MIN)
        return carry

    lax.fori_loop(0, n_chunks, score_chunk, 0)

    def count(pred_fn):
        def body(c, cnt):
            hit = pred_fn(key_ref[chunk_rows(c), :], c).astype(I32)
            return cnt + jnp.sum(hit.reshape(BLK // 8, 8, BLK), axis=0)
        cnt = lax.fori_loop(0, n_chunks, body, jnp.zeros((8, BLK), I32))
        return jnp.sum(cnt, axis=0, keepdims=True)

    def bit_step(it, ans):
        cand = ans + lax.shift_left(jnp.int32(1), 31 - it)
        return jnp.where(count(lambda k, c: k >= cand) >= k_sel, cand, ans)

    kth = lax.fori_loop(0, 32, bit_step, jnp.full((1, BLK), INT_MIN, I32))

    need = k_sel - count(lambda k, c: k > kth)

    def idx_step(it, jp):
        cand = jp + lax.shift_left(jnp.int32(1), 10 - it)
        below = count(lambda k, c: (k == kth) & ((c * BLK + row) < cand))
        return jnp.where(below < need, cand, jp)

    n_idx_bits = seq.bit_length() - 1
    assert 1 << n_idx_bits == seq and n_idx_bits == 11
    tie_last = lax.fori_loop(0, n_idx_bits, idx_step, jnp.zeros((1, BLK), I32))

    slopes = [slopes_ref[h] for h in range(N_HEADS_B)]
    qts = [tq_ref[T_QB + h * HEAD_DIM:T_QB + (h + 1) * HEAD_DIM, :] for h in range(N_HEADS_B)]

    def attn_chunk(c, carry):
        ms, ls, accs = carry
        key = key_ref[chunk_rows(c), :]
        s_idx = c * BLK + row
        sel = ((key > kth) | ((key == kth) & (s_idx <= tie_last))) & (s_idx <= t_lane)
        rel = (s_idx - t_lane).astype(F32)
        kb = kk_ref[chunk_rows(c), 0:HEAD_DIM]
        vt = vt_ref[:, chunk_rows(c)]
        new_m, new_l, new_acc = [], [], []
        for h in range(N_HEADS_B):
            s = jnp.where(sel, _dot(kb, qts[h]) + slopes[h] * rel, NEG)
            m = jnp.maximum(ms[h], jnp.max(s, axis=0, keepdims=True))
            alpha = jnp.exp(ms[h] - m)
            p = jnp.exp(s - m)
            new_m.append(m)
            new_l.append(alpha * ls[h] + jnp.sum(p, axis=0, keepdims=True))
            new_acc.append(alpha * accs[h] + _dot(vt, p.astype(BF16)))
        return tuple(new_m), tuple(new_l), tuple(new_acc)

    init = (tuple(jnp.full((1, BLK), NEG, F32) for _ in range(N_HEADS_B)),
            tuple(jnp.zeros((1, BLK), F32) for _ in range(N_HEADS_B)),
            tuple(jnp.zeros((HEAD_DIM, BLK), F32) for _ in range(N_HEADS_B)))
    _, ls, accs = lax.fori_loop(0, n_chunks, attn_chunk, init)
    ot = jnp.concatenate([accs[h] / ls[h] for h in range(N_HEADS_B)], axis=0)
    o_ref[...] = ot.T


def _dsa(slopes, proj_r, proj_t, w_t):
    B, T, _ = proj_r.shape
    k_sel = min(TOPK_MAX, T // 4)
    return pl.pallas_call(
        functools.partial(_dsa_kernel, seq=T, k_sel=k_sel),
        grid=(B, T // BLK),
        in_specs=[
            pl.BlockSpec(memory_space=pltpu.SMEM),
            pl.BlockSpec((None, T, 2 * LANES), lambda b, q: (b, 0, R_KB // (2 * LANES))),
            pl.BlockSpec((None, T_VB, BLK), lambda b, q: (b, 0, q)),
            pl.BlockSpec((None, HEAD_DIM, T), lambda b, q: (b, T_VB // HEAD_DIM, 0)),
            pl.BlockSpec((None, WI_ROWS, BLK), lambda b, q: (b, 0, q)),
        ],
        out_specs=pl.BlockSpec((None, BLK, W_B), lambda b, q: (b, q, 0)),
        out_shape=jax.ShapeDtypeStruct((B, T, W_B), F32),
        scratch_shapes=[pltpu.VMEM((T, BLK), I32)],
        compiler_params=pltpu.CompilerParams(
            dimension_semantics=("parallel", "arbitrary"), vmem_limit_bytes=VMEM_LIMIT),
        name="dsa",
    )(slopes, proj_r, proj_t, proj_t, w_t)


def _mem_kv_kernel(mem_ref, g_ref, w_ref, kv_ref):
    kv_ref[0] = _dot(_rms(mem_ref[0], g_ref[...]).astype(BF16), w_ref[...]).astype(BF16)


def _mem_kv(mem, g, w):
    B, M, D = mem.shape
    return pl.pallas_call(
        _mem_kv_kernel,
        grid=(B,),
        in_specs=[
            pl.BlockSpec((1, M, D), lambda b: (b, 0, 0)),
            pl.BlockSpec((1, D), lambda b: (0, 0)),
            pl.BlockSpec(w.shape, lambda b: (0, 0)),
        ],
        out_specs=pl.BlockSpec((1, M, 2 * W_M), lambda b: (b, 0, 0)),
        out_shape=jax.ShapeDtypeStruct((B, M, 2 * W_M), BF16),
        compiler_params=pltpu.CompilerParams(
            dimension_semantics=("parallel",), vmem_limit_bytes=VMEM_LIMIT),
        name="mem_kv",
    )(mem, g, w)


def _out_kernel(x_ref, oa_ref, ob_ref, gate_ref, qm_ref, kv_ref, wo_ref, g_ref, y_ref):
    gate = gate_ref[...].astype(F32)
    sg = gate * (1.0 / (1.0 + jnp.exp(-gate)))

    om = []
    for h in range(N_HEADS_MEM):
        lo = h * HEAD_DIM
        s = _dot_nt(qm_ref[:, lo:lo + HEAD_DIM], kv_ref[:, lo:lo + HEAD_DIM])
        m = jnp.max(s, axis=1, keepdims=True)
        p = jnp.exp(s - m)
        l = jnp.sum(p, axis=1, keepdims=True)
        om.append(_dot(p.astype(BF16), kv_ref[:, W_M + lo:W_M + lo + HEAD_DIM]) / l)
    om = jnp.concatenate(om, axis=1)

    mix = (oa_ref[...] * sg[:, :W_A]).astype(BF16)
    y = _dot(mix, wo_ref[0:W_A, :])
    mix = (ob_ref[...] * sg[:, W_A:W_A + W_B]).astype(BF16)
    y = y + _dot(mix, wo_ref[W_A:W_A + W_B, :])
    mix = (om * sg[:, W_A + W_B:]).astype(BF16)
    y = y + _dot(mix, wo_ref[W_A + W_B:, :])
    y_ref[...] = _rms(x_ref[...] + y, g_ref[...])


def _out(x, o_a, o_b, proj_r, kv_m, w_out, g, tm=256):
    B, T, D = x.shape
    M = kv_m.shape[1]
    row = lambda b, i: (b, i, 0)
    return pl.pallas_call(
        _out_kernel,
        grid=(B, T // tm),
        in_specs=[
            pl.BlockSpec((None, tm, D), row),
            pl.BlockSpec((None, tm, W_A), row),
            pl.BlockSpec((None, tm, W_B), row),
            pl.BlockSpec((None, tm, MIX_WIDTH), lambda b, i: (b, i, R_GATE // MIX_WIDTH)),
            pl.BlockSpec((None, tm, W_M), lambda b, i: (b, i, R_QM // W_M)),
            pl.BlockSpec((None, M, 2 * W_M), lambda b, i: (b, 0, 0)),
            pl.BlockSpec(w_out.shape, lambda b, i: (0, 0)),
            pl.BlockSpec((1, D), lambda b, i: (0, 0)),
        ],
        out_specs=pl.BlockSpec((None, tm, D), row),
        out_shape=jax.ShapeDtypeStruct((B, T, D), F32),
        compiler_params=pltpu.CompilerParams(
            dimension_semantics=("parallel", "parallel"), vmem_limit_bytes=VMEM_LIMIT),
        name="out",
    )(x, o_a, o_b, proj_r, proj_r, kv_m, w_out, g)


def _alibi_slopes(n):
    return 2.0 ** (-8.0 * jnp.arange(1, n + 1, dtype=F32) / n)


def _split_weights(w):
    bounds = [0]
    for s in SPLIT_SIZES:
        bounds.append(bounds[-1] + s)
    q_a, k_a, v_a, q_b, k_b, v_b, q_m, gate, q_i, k_i, w_i = (
        w[:, bounds[i]:bounds[i + 1]] for i in range(len(SPLIT_SIZES)))
    scale = HEAD_DIM ** -0.5
    idx_scale = IDX_DIM ** -0.5
    zeros = jnp.zeros((w.shape[0], LANES - HEAD_DIM), w.dtype)
    wa = jnp.concatenate([q_a * scale, k_a, v_a], axis=1)
    wr = jnp.concatenate([gate, q_m * scale, k_b, zeros, k_i, zeros], axis=1)
    wt = jnp.concatenate([q_i * idx_scale, q_b * scale, v_b], axis=1).T
    ww = jnp.concatenate([w_i.T, jnp.zeros((WI_ROWS - N_IDX_HEADS, w.shape[0]), w.dtype)], axis=0)
    return wa.astype(BF16), wr.astype(BF16), wt.astype(BF16), ww.astype(BF16)


def kernel(x, mem, g_in, g_mem, w_in, w_mem_kv, w_out, g_final):
    assert g_in.shape[0] == 1, "single-layer block: the final RMSNorm is fused into the output kernel"
    wa, wr, wt, ww = _split_weights(w_in[0])
    proj_a, proj_r, proj_t, w_t = _proj(x, g_in, wa, wr, wt, ww)
    o_a = _attn_a(_alibi_slopes(N_HEADS_A), proj_a)
    o_b = _dsa(_alibi_slopes(N_HEADS_B), proj_r, proj_t, w_t)
    kv_m = _mem_kv(mem, g_mem, w_mem_kv[0].astype(BF16))
    return _out(x, o_a, o_b, proj_r, kv_m, w_out[0].astype(BF16), g_final[None, :])
```

```python
import functools

import jax
import jax.numpy as jnp
from jax import lax
from jax.experimental import pallas as pl
from jax.experimental.pallas import tpu as pltpu

F32 = jnp.float32
BF16 = jnp.bfloat16
I32 = jnp.int32

D_MODEL = 1024
HEAD_DIM = 64
N_HEADS_A = 8
N_HEADS_B = 4
N_HEADS_MEM = 4
W_A = N_HEADS_A * HEAD_DIM
W_B = N_HEADS_B * HEAD_DIM
W_M = N_HEADS_MEM * HEAD_DIM
MIX_WIDTH = W_A + W_B + W_M
DILATIONS = (1, 4, 16)
BAND = 128
N_IDX_HEADS = 8
IDX_DIM = 64
TOPK_MAX = 256
RMS_EPS = 1e-6
SPLIT_SIZES = (W_A, W_A, W_A, W_B, HEAD_DIM, HEAD_DIM, W_M, MIX_WIDTH,
               N_IDX_HEADS * IDX_DIM, IDX_DIM, N_IDX_HEADS)

LANES = 128
BLK = 128
NEG = -1e30
INT_MIN = -(2 ** 31)
VMEM_LIMIT = 48 * 1024 * 1024

R_GATE = 0
R_QM = MIX_WIDTH
R_KB = R_QM + W_M
R_KI = R_KB + LANES
R_COLS = R_KI + LANES
T_QI = 0
T_QB = N_IDX_HEADS * IDX_DIM
T_VB = T_QB + W_B
T_ROWS = T_VB + HEAD_DIM
WI_ROWS = 16
DSA_Q = 256
DSA_HALVINGS = 12


def _dot(a, b):
    return jnp.dot(a, b, preferred_element_type=F32)


def _dot_nt(a, b):
    return lax.dot_general(a, b, (((1,), (1,)), ((), ())), preferred_element_type=F32)


def _rms(x, g):
    return x * lax.rsqrt(jnp.mean(x * x, axis=-1, keepdims=True) + RMS_EPS) * g


def _proj_kernel(x_ref, g_ref, wa_ref, wr_ref, wt_ref, ww_ref, oa_ref, or_ref, ot_ref, ow_ref):
    hb = _rms(x_ref[0], g_ref[...]).astype(BF16)
    oa_ref[0] = _dot(hb, wa_ref[...])
    or_ref[0] = _dot(hb, wr_ref[...]).astype(BF16)
    ot_ref[0] = _dot_nt(wt_ref[...], hb).astype(BF16)
    ow_ref[0] = _dot_nt(ww_ref[...], hb)


def _proj(x, g, wa, wr, wt, ww, tm=256):
    B, T, D = x.shape
    const = lambda b, i: (0, 0)
    return pl.pallas_call(
        _proj_kernel,
        grid=(B, T // tm),
        in_specs=[
            pl.BlockSpec((1, tm, D), lambda b, i: (b, i, 0)),
            pl.BlockSpec((1, D), const),
            pl.BlockSpec(wa.shape, const),
            pl.BlockSpec(wr.shape, const),
            pl.BlockSpec(wt.shape, const),
            pl.BlockSpec(ww.shape, const),
        ],
        out_specs=[
            pl.BlockSpec((1, tm, 3 * W_A), lambda b, i: (b, i, 0)),
            pl.BlockSpec((1, tm, R_COLS), lambda b, i: (b, i, 0)),
            pl.BlockSpec((1, T_ROWS, tm), lambda b, i: (b, 0, i)),
            pl.BlockSpec((1, WI_ROWS, tm), lambda b, i: (b, 0, i)),
        ],
        out_shape=[
            jax.ShapeDtypeStruct((B, T, 3 * W_A), F32),
            jax.ShapeDtypeStruct((B, T, R_COLS), BF16),
            jax.ShapeDtypeStruct((B, T_ROWS, T), BF16),
            jax.ShapeDtypeStruct((B, WI_ROWS, T), F32),
        ],
        compiler_params=pltpu.CompilerParams(
            dimension_semantics=("parallel", "parallel"), vmem_limit_bytes=VMEM_LIMIT),
        name="proj",
    )(x, g, wa, wr, wt, ww)


def _attn_a_kernel(slopes_ref, q_ref, k_ref, v_ref, o_ref, acc_ref, m_ref, bias_ref, stage_ref,
                   s_ref, p_ref, *, seq, unroll):
    hp = pl.program_id(1)
    lane = lax.broadcasted_iota(I32, (1, LANES), 1)
    head_lanes = (lane < HEAD_DIM, lane >= HEAD_DIM)
    row = lax.broadcasted_iota(I32, (BLK, BLK), 0)
    col = lax.broadcasted_iota(I32, (BLK, BLK), 1)
    d_cur = (row - col).astype(F32)
    d_prev = d_cur + float(BAND)

    stage_ref[:, 0:BLK, :] = jnp.zeros((5, BLK, LANES), BF16)

    def staged(g):
        return pl.ds(pl.multiple_of((g + 1) * BLK, BLK), BLK)

    def stage(dil):
        n = seq // dil

        def body(g, carry):
            first = g * BLK
            start = first // n + dil * (first % n)
            sl = pl.ds(start, BLK) if dil == 1 else pl.ds(start, BLK, stride=dil)
            q, k, v = q_ref[sl, :], k_ref[sl, :], v_ref[sl, :]
            for h in range(2):
                stage_ref[h, staged(g), :] = jnp.where(head_lanes[h], q, 0.0).astype(BF16)
                stage_ref[3 + h, staged(g), :] = jnp.where(head_lanes[h], v, 1.0).astype(BF16)
            stage_ref[2, staged(g), :] = k.astype(BF16)
            return carry

        lax.fori_loop(0, seq // BLK, body, 0, unroll=2)

    def pitch(dil):
        return seq // dil + 1 if seq // dil == BLK and dil > 1 else None

    def put(ref, h, p, g, dil, val):
        n_blocks = seq // dil // BLK
        start = g // n_blocks + (g % n_blocks) * (dil * BLK)
        if dil == 1:
            ref[h, p, pl.ds(pl.multiple_of(start, BLK), BLK), :] = val
        elif pitch(dil):
            ref[h, p, pl.ds(start * pitch(dil), BLK), :] = val
        else:
            ref[h, p, pl.ds(start, BLK, stride=dil), :] = val

    n_all = seq // BLK
    for p, dil in enumerate(DILATIONS):
        n_blocks = n_all // dil
        n_keys = BLK if n_blocks == 1 else 2 * BLK
        for h in range(2):
            sd = slopes_ref[hp * 2 + h] * float(dil)
            cur = jnp.where(row >= col, -sd * d_cur, NEG)
            bias_ref[h, 0, :, BLK:] = cur
            bias_ref[h, 1, :, BLK:] = cur
            if n_keys > BLK:
                bias_ref[h, 0, :, :BLK] = jnp.where(col >= row, -sd * d_prev, NEG)
                bias_ref[h, 1, :, :BLK] = jnp.full((BLK, BLK), NEG, F32)
        stage(dil)

        def keys(g, n_keys=n_keys):
            return pl.ds(pl.multiple_of((g + 2) * BLK - n_keys, BLK), n_keys)

        def scores(g, carry, n_blocks=n_blocks, n_keys=n_keys, keys=keys):
            first = jnp.where(jnp.asarray(g, I32) % n_blocks == 0, 1, 0)
            kw = stage_ref[2, keys(g), :]
            for h in range(2):
                s = _dot_nt(stage_ref[h, staged(g), :], kw)
                s_ref[h, g, :, :n_keys] = s + bias_ref[h, first, :, 2 * BLK - n_keys:]
            return carry

        def softmax(g, carry, p=p, dil=dil, n_keys=n_keys):
            for h in range(2):
                s = s_ref[h, g, :, :n_keys]
                m = jnp.max(s, axis=1, keepdims=True)
                p_ref[h, g, :, :n_keys] = jnp.exp(s - m).astype(BF16)
                put(m_ref, h, p, g, dil, jnp.broadcast_to(m, (BLK, LANES)))
            return carry

        def values(g, carry, p=p, dil=dil, n_keys=n_keys, keys=keys):
            for h in range(2):
                put(acc_ref, h, p, g, dil, _dot(p_ref[h, g, :, :n_keys], stage_ref[3 + h, keys(g), :]))
            return carry

        lax.fori_loop(0, n_all, scores, 0, unroll=unroll)
        lax.fori_loop(0, n_all, softmax, 0, unroll=unroll)
        lax.fori_loop(0, n_all, values, 0, unroll=unroll)

    n_pat = len(DILATIONS)
    rows_per_step = 2 * BLK

    def natural_rows(ref, h, p, i):
        dil = DILATIONS[p]
        if not pitch(dil):
            return ref[h, p, pl.ds(pl.multiple_of(i * rows_per_step, rows_per_step), rows_per_step), :]
        per = rows_per_step // dil
        return jnp.concatenate(
            [ref[h, p, pl.ds(i * per + u, dil, stride=pitch(dil)), :] for u in range(per)], axis=0)

    def merge(i, carry):
        sl = pl.ds(pl.multiple_of(i * rows_per_step, rows_per_step), rows_per_step)
        nums = []
        for h in range(2):
            ms = [natural_rows(m_ref, h, p, i) for p in range(n_pat)]
            mx = functools.reduce(jnp.maximum, ms)
            nums.append(sum(jnp.exp(ms[p] - mx) * natural_rows(acc_ref, h, p, i) for p in range(n_pat)))
        acc = jnp.where(head_lanes[0], nums[0], nums[1])
        den = pltpu.roll(jnp.where(head_lanes[0], nums[1], nums[0]), HEAD_DIM, axis=1)
        o_ref[sl, :] = acc / den
        return carry

    lax.fori_loop(0, seq // rows_per_step, merge, 0)


def _attn_a(slopes, proj_a, unroll=8):
    B, T, _ = proj_a.shape
    n_pairs = N_HEADS_A // 2
    n_pat = len(DILATIONS)

    def spec(off):
        return pl.BlockSpec((None, T, LANES), lambda b, h: (b, 0, off + h))

    return pl.pallas_call(
        functools.partial(_attn_a_kernel, seq=T, unroll=unroll),
        grid=(B, n_pairs),
        in_specs=[pl.BlockSpec(memory_space=pltpu.SMEM), spec(0), spec(n_pairs), spec(2 * n_pairs)],
        out_specs=pl.BlockSpec((None, T, LANES), lambda b, h: (b, 0, h)),
        out_shape=jax.ShapeDtypeStruct((B, T, W_A), F32),
        scratch_shapes=[pltpu.VMEM((2, n_pat, T + max(DILATIONS), LANES), F32),
                        pltpu.VMEM((2, n_pat, T + max(DILATIONS), LANES), F32),
                        pltpu.VMEM((2, 2, BLK, 2 * BLK), F32),
                        pltpu.VMEM((5, T + BLK, LANES), BF16),
                        pltpu.VMEM((2, T // BLK, BLK, 2 * BLK), F32),
                        pltpu.VMEM((2, T // BLK, BLK, 2 * BLK), BF16)],
        compiler_params=pltpu.CompilerParams(
            dimension_semantics=("parallel", "parallel"), vmem_limit_bytes=VMEM_LIMIT),
        name="attn_a",
    )(slopes, proj_a, proj_a, proj_a)


def _fold(x, op):
    return functools.reduce(op, [x[r:r + 8] for r in range(0, x.shape[0], 8)])


def _dsa_kernel(slopes_ref, kk_ref, tq_ref, vt_ref, wt_ref, o_ref, sc_ref, sb_ref, s_ref, p_ref,
                acc_ref, *, seq, k_sel, snap_unroll):
    j = pl.program_id(1)
    n_pairs = j + 1
    t_lane = j * DSA_Q + lax.broadcasted_iota(I32, (1, DSA_Q), 1)
    row = lax.broadcasted_iota(I32, (BLK, DSA_Q), 0)
    ws = wt_ref[0:N_IDX_HEADS, :] * (N_IDX_HEADS ** -0.5)
    inf = float("inf")

    def rows(c):
        return pl.ds(pl.multiple_of(c * BLK, BLK), BLK)

    def each_chunk(fn, init):
        def pair(i, carry):
            return fn(2 * i + 1, fn(2 * i, carry))
        return lax.fori_loop(0, n_pairs, pair, init)

    def sub_reduce(x, op):
        return op(x, axis=0, keepdims=True)

    def score_chunk(c, carry):
        mn, mx = carry
        ki = kk_ref[rows(c), R_KI - R_KB:R_KI - R_KB + IDX_DIM]
        lgs = [_dot(ki, tq_ref[T_QI + h * IDX_DIM:T_QI + (h + 1) * IDX_DIM, :])
               for h in range(N_IDX_HEADS)]
        sc = functools.reduce(
            lambda a, b: a + b, [jnp.maximum(lg, 0.0) * ws[h:h + 1, :] for h, lg in enumerate(lgs)])
        causal = (c * BLK + row) <= t_lane
        sc_ref[rows(c), :] = jnp.where(causal, sc, -inf)
        mn = jnp.minimum(mn, _fold(jnp.where(causal, sc, inf), jnp.minimum))
        mx = jnp.maximum(mx, _fold(jnp.where(causal, sc, -inf), jnp.maximum))
        return mn, mx

    mn, mx = each_chunk(score_chunk, (jnp.full((8, DSA_Q), inf, F32), jnp.full((8, DSA_Q), -inf, F32)))
    few = t_lane < k_sel
    lo = jnp.where(few, -inf, sub_reduce(mn, jnp.min))
    hi = jnp.where(few, -inf, sub_reduce(mx, jnp.max))

    def count(pred):
        def body(c, cnt):
            return cnt + _fold(pred(sc_ref[rows(c), :], c).astype(I32), jnp.add)
        return sub_reduce(each_chunk(body, jnp.zeros((8, DSA_Q), I32)), jnp.sum)

    def halve(_, bounds):
        lo, hi = bounds
        mid = 0.5 * lo + 0.5 * hi
        cnt = count(lambda x, c: x >= mid)
        active = lo < hi
        lo = jnp.where(active & (cnt >= k_sel), mid, lo)
        hi = jnp.where(active & (cnt <= k_sel), mid, hi)
        return lo, hi

    lo, hi = lax.fori_loop(0, DSA_HALVINGS, halve, (lo, hi))

    def snap(bounds):
        lo, hi = bounds
        mid = 0.5 * lo + 0.5 * hi
        mid = jnp.where(mid > lo, mid, hi)

        def body(c, carry):
            cnt, above, below = carry
            x = sc_ref[rows(c), :]
            ge = x >= mid
            return (cnt + _fold(ge.astype(I32), jnp.add),
                    jnp.minimum(above, _fold(jnp.where(ge, x, inf), jnp.minimum)),
                    jnp.maximum(below, _fold(jnp.where(ge, -inf, x), jnp.maximum)))

        cnt, above, below = each_chunk(body, (jnp.zeros((8, DSA_Q), I32),
                                              jnp.full((8, DSA_Q), inf, F32),
                                              jnp.full((8, DSA_Q), -inf, F32)))
        cnt = sub_reduce(cnt, jnp.sum)
        above = sub_reduce(above, jnp.min)
        below = sub_reduce(below, jnp.max)
        active = lo < hi
        enough = cnt >= k_sel
        lo = jnp.where(active & enough, above, lo)
        hi = jnp.where(active & (cnt <= k_sel), jnp.where(enough, above, below), hi)
        return lo, hi

    def snaps(bounds):
        for _ in range(snap_unroll):
            bounds = snap(bounds)
        return bounds

    def unsettled(bounds):
        return jnp.max((bounds[0] < bounds[1]).astype(I32)) > 0

    tau, _ = lax.while_loop(unsettled, snaps, (lo, hi))

    n_ge = count(lambda x, c: x >= tau)
    need = k_sel - count(lambda x, c: x > tau)
    n_idx_bits = seq.bit_length() - 1
    assert 1 << n_idx_bits == seq

    def tie_break():
        def idx_step(it, jp):
            cand = jp + lax.shift_left(jnp.int32(1), n_idx_bits - 1 - it)
            below = count(lambda x, c: (x == tau) & ((c * BLK + row) < cand))
            return jnp.where(below < need, cand, jp)
        return lax.fori_loop(0, n_idx_bits, idx_step, jnp.zeros((1, DSA_Q), I32))

    surplus = jnp.max(((n_ge > k_sel) & jnp.logical_not(few)).astype(I32)) > 0
    tie_last = lax.cond(surplus, tie_break, lambda: jnp.full((1, DSA_Q), seq, I32))

    def mask_chunk(c, carry):
        x = sc_ref[rows(c), :]
        s_idx = c * BLK + row
        sel = ((x > tau) | ((x == tau) & (s_idx <= tie_last))) & (s_idx <= t_lane)
        sb_ref[rows(c), :] = jnp.where(sel, 0.0, NEG)
        return carry

    each_chunk(mask_chunk, 0)

    slopes = [slopes_ref[h] for h in range(N_HEADS_B)]

    def qk_chunk(c, ms):
        kb = kk_ref[rows(c), 0:HEAD_DIM]
        ss = [_dot(kb, tq_ref[T_QB + h * HEAD_DIM:T_QB + (h + 1) * HEAD_DIM, :])
              for h in range(N_HEADS_B)]
        rel = (c * BLK + row - t_lane).astype(F32)
        sb = sb_ref[rows(c), :]
        out = []
        for h in range(N_HEADS_B):
            s = ss[h] + (slopes[h] * rel + sb)
            s_ref[h, rows(c), :] = s
            out.append(jnp.maximum(ms[h], _fold(s, jnp.maximum)))
        return tuple(out)

    ms = each_chunk(qk_chunk, tuple(jnp.full((8, DSA_Q), NEG, F32) for _ in range(N_HEADS_B)))
    ms = [sub_reduce(m, jnp.max) for m in ms]

    def exp_chunk(c, ls):
        out = []
        for h in range(N_HEADS_B):
            p = jnp.exp(s_ref[h, rows(c), :] - ms[h])
            p_ref[h, rows(c), :] = p.astype(BF16)
            out.append(ls[h] + _fold(p, jnp.add))
        return tuple(out)

    ls = each_chunk(exp_chunk, tuple(jnp.zeros((8, DSA_Q), F32) for _ in range(N_HEADS_B)))

    acc_ref[...] = jnp.zeros((W_B, DSA_Q), F32)

    def pv_pair(i, carry):
        sl = pl.ds(pl.multiple_of(i * 2 * BLK, 2 * BLK), 2 * BLK)
        vt = vt_ref[:, sl]
        pvs = [_dot(vt, p_ref[h, sl, :]) for h in range(N_HEADS_B)]
        for h in range(N_HEADS_B):
            acc_ref[h * HEAD_DIM:(h + 1) * HEAD_DIM, :] += pvs[h]
        return carry

    lax.fori_loop(0, n_pairs, pv_pair, 0)
    l_all = jnp.concatenate(
        [jnp.broadcast_to(sub_reduce(ls[h], jnp.sum), (HEAD_DIM, DSA_Q)) for h in range(N_HEADS_B)],
        axis=0)
    o_ref[...] = (acc_ref[...] / l_all).T


def _dsa(slopes, proj_r, proj_t, w_t):
    B, T, _ = proj_r.shape
    k_sel = min(TOPK_MAX, T // 4)
    return pl.pallas_call(
        functools.partial(_dsa_kernel, seq=T, k_sel=k_sel, snap_unroll=2),
        grid=(B, T // DSA_Q),
        in_specs=[
            pl.BlockSpec(memory_space=pltpu.SMEM),
            pl.BlockSpec((None, T, 2 * LANES), lambda b, q: (b, 0, R_KB // (2 * LANES))),
            pl.BlockSpec((None, T_VB, DSA_Q), lambda b, q: (b, 0, q)),
            pl.BlockSpec((None, HEAD_DIM, T), lambda b, q: (b, T_VB // HEAD_DIM, 0)),
            pl.BlockSpec((None, WI_ROWS, DSA_Q), lambda b, q: (b, 0, q)),
        ],
        out_specs=pl.BlockSpec((None, DSA_Q, W_B), lambda b, q: (b, q, 0)),
        out_shape=jax.ShapeDtypeStruct((B, T, W_B), F32),
        scratch_shapes=[pltpu.VMEM((T, DSA_Q), F32),
                        pltpu.VMEM((T, DSA_Q), F32),
                        pltpu.VMEM((N_HEADS_B, T, DSA_Q), F32),
                        pltpu.VMEM((N_HEADS_B, T, DSA_Q), BF16),
                        pltpu.VMEM((W_B, DSA_Q), F32)],
        compiler_params=pltpu.CompilerParams(
            dimension_semantics=("parallel", "arbitrary"), vmem_limit_bytes=VMEM_LIMIT),
        name="dsa",
    )(slopes, proj_r, proj_t, proj_t, w_t)


def _mem_kv_kernel(mem_ref, g_ref, w_ref, kv_ref):
    kv_ref[0] = _dot(_rms(mem_ref[0], g_ref[...]).astype(BF16), w_ref[...]).astype(BF16)


def _mem_kv(mem, g, w):
    B, M, D = mem.shape
    return pl.pallas_call(
        _mem_kv_kernel,
        grid=(B,),
        in_specs=[
            pl.BlockSpec((1, M, D), lambda b: (b, 0, 0)),
            pl.BlockSpec((1, D), lambda b: (0, 0)),
            pl.BlockSpec(w.shape, lambda b: (0, 0)),
        ],
        out_specs=pl.BlockSpec((1, M, 2 * W_M), lambda b: (b, 0, 0)),
        out_shape=jax.ShapeDtypeStruct((B, M, 2 * W_M), BF16),
        compiler_params=pltpu.CompilerParams(
            dimension_semantics=("parallel",), vmem_limit_bytes=VMEM_LIMIT),
        name="mem_kv",
    )(mem, g, w)


def _out_kernel(x_ref, oa_ref, ob_ref, gate_ref, qm_ref, kv_ref, wo_ref, g_ref, y_ref):
    gate = gate_ref[...].astype(F32)
    sg = gate * (1.0 / (1.0 + jnp.exp(-gate)))

    om = []
    for h in range(N_HEADS_MEM):
        lo = h * HEAD_DIM
        s = _dot_nt(qm_ref[:, lo:lo + HEAD_DIM], kv_ref[:, lo:lo + HEAD_DIM])
        m = jnp.max(s, axis=1, keepdims=True)
        p = jnp.exp(s - m)
        l = jnp.sum(p, axis=1, keepdims=True)
        om.append(_dot(p.astype(BF16), kv_ref[:, W_M + lo:W_M + lo + HEAD_DIM]) / l)
    om = jnp.concatenate(om, axis=1)

    mix = (oa_ref[...] * sg[:, :W_A]).astype(BF16)
    y = _dot(mix, wo_ref[0:W_A, :])
    mix = (ob_ref[...] * sg[:, W_A:W_A + W_B]).astype(BF16)
    y = y + _dot(mix, wo_ref[W_A:W_A + W_B, :])
    mix = (om * sg[:, W_A + W_B:]).astype(BF16)
    y = y + _dot(mix, wo_ref[W_A + W_B:, :])
    y_ref[...] = _rms(x_ref[...] + y, g_ref[...])


def _out(x, o_a, o_b, proj_r, kv_m, w_out, g, tm=256):
    B, T, D = x.shape
    M = kv_m.shape[1]
    row = lambda b, i: (b, i, 0)
    return pl.pallas_call(
        _out_kernel,
        grid=(B, T // tm),
        in_specs=[
            pl.BlockSpec((None, tm, D), row),
            pl.BlockSpec((None, tm, W_A), row),
            pl.BlockSpec((None, tm, W_B), row),
            pl.BlockSpec((None, tm, MIX_WIDTH), lambda b, i: (b, i, R_GATE // MIX_WIDTH)),
            pl.BlockSpec((None, tm, W_M), lambda b, i: (b, i, R_QM // W_M)),
            pl.BlockSpec((None, M, 2 * W_M), lambda b, i: (b, 0, 0)),
            pl.BlockSpec(w_out.shape, lambda b, i: (0, 0)),
            pl.BlockSpec((1, D), lambda b, i: (0, 0)),
        ],
        out_specs=pl.BlockSpec((None, tm, D), row),
        out_shape=jax.ShapeDtypeStruct((B, T, D), F32),
        compiler_params=pltpu.CompilerParams(
            dimension_semantics=("parallel", "parallel"), vmem_limit_bytes=VMEM_LIMIT),
        name="out",
    )(x, o_a, o_b, proj_r, proj_r, kv_m, w_out, g)


def _alibi_slopes(n):
    return 2.0 ** (-8.0 * jnp.arange(1, n + 1, dtype=F32) / n)


def _split_weights(w):
    bounds = [0]
    for s in SPLIT_SIZES:
        bounds.append(bounds[-1] + s)
    q_a, k_a, v_a, q_b, k_b, v_b, q_m, gate, q_i, k_i, w_i = (
        w[:, bounds[i]:bounds[i + 1]] for i in range(len(SPLIT_SIZES)))
    scale = HEAD_DIM ** -0.5
    idx_scale = IDX_DIM ** -0.5
    zeros = jnp.zeros((w.shape[0], LANES - HEAD_DIM), w.dtype)
    wa = jnp.concatenate([q_a * scale, k_a, v_a], axis=1)
    wr = jnp.concatenate([gate, q_m * scale, k_b, zeros, k_i, zeros], axis=1)
    wt = jnp.concatenate([q_i * idx_scale, q_b * scale, v_b], axis=1).T
    ww = jnp.concatenate([w_i.T, jnp.zeros((WI_ROWS - N_IDX_HEADS, w.shape[0]), w.dtype)], axis=0)
    return wa.astype(BF16), wr.astype(BF16), wt.astype(BF16), ww.astype(BF16)


def kernel(x, mem, g_in, g_mem, w_in, w_mem_kv, w_out, g_final):
    assert g_in.shape[0] == 1, "single-layer block: the final RMSNorm is fused into the output kernel"
    wa, wr, wt, ww = _split_weights(w_in[0])
    proj_a, proj_r, proj_t, w_t = _proj(x, g_in, wa, wr, wt, ww)
    o_a = _attn_a(_alibi_slopes(N_HEADS_A), proj_a)
    o_b = _dsa(_alibi_slopes(N_HEADS_B), proj_r, proj_t, w_t)
    kv_m = _mem_kv(mem, g_mem, w_mem_kv[0].astype(BF16))
    return _out(x, o_a, o_b, proj_r, kv_m, w_out[0].astype(BF16), g_final[None, :])
```

```python
import functools

import jax
import jax.numpy as jnp
from jax import lax
from jax.experimental import pallas as pl
from jax.experimental.pallas import tpu as pltpu

F32 = jnp.float32
BF16 = jnp.bfloat16
I32 = jnp.int32

D_MODEL = 1024
HEAD_DIM = 64
N_HEADS_A = 8
N_HEADS_B = 4
N_HEADS_MEM = 4
W_A = N_HEADS_A * HEAD_DIM
W_B = N_HEADS_B * HEAD_DIM
W_M = N_HEADS_MEM * HEAD_DIM
MIX_WIDTH = W_A + W_B + W_M
DILATIONS = (1, 4, 16)
BAND = 128
N_IDX_HEADS = 8
IDX_DIM = 64
TOPK_MAX = 256
RMS_EPS = 1e-6
SPLIT_SIZES = (W_A, W_A, W_A, W_B, HEAD_DIM, HEAD_DIM, W_M, MIX_WIDTH,
               N_IDX_HEADS * IDX_DIM, IDX_DIM, N_IDX_HEADS)

LANES = 128
BLK = 128
NEG = -1e30
INT_MIN = -(2 ** 31)
VMEM_LIMIT = 48 * 1024 * 1024

R_GATE = 0
R_QM = MIX_WIDTH
R_KB = R_QM + W_M
R_KI = R_KB + LANES
R_COLS = R_KI + LANES
T_QI = 0
T_QB = N_IDX_HEADS * IDX_DIM
T_VB = T_QB + W_B
T_ROWS = T_VB + HEAD_DIM
WI_ROWS = 16
DSA_Q = 256
DSA_HALVINGS = 12


def _dot(a, b):
    return jnp.dot(a, b, preferred_element_type=F32)


def _dot_nt(a, b):
    return lax.dot_general(a, b, (((1,), (1,)), ((), ())), preferred_element_type=F32)


def _rms(x, g):
    return x * lax.rsqrt(jnp.mean(x * x, axis=-1, keepdims=True) + RMS_EPS) * g


def _proj_kernel(x_ref, g_ref, wa_ref, wr_ref, wt_ref, ww_ref, oa_ref, or_ref, ot_ref, ow_ref):
    hb = _rms(x_ref[0], g_ref[...]).astype(BF16)
    oa_ref[0] = _dot(hb, wa_ref[...])
    or_ref[0] = _dot(hb, wr_ref[...]).astype(BF16)
    ot_ref[0] = _dot_nt(wt_ref[...], hb).astype(BF16)
    ow_ref[0] = _dot_nt(ww_ref[...], hb)


def _proj(x, g, wa, wr, wt, ww, tm=512):
    B, T, D = x.shape
    const = lambda b, i: (0, 0)
    return pl.pallas_call(
        _proj_kernel,
        grid=(B, T // tm),
        in_specs=[
            pl.BlockSpec((1, tm, D), lambda b, i: (b, i, 0)),
            pl.BlockSpec((1, D), const),
            pl.BlockSpec(wa.shape, const),
            pl.BlockSpec(wr.shape, const),
            pl.BlockSpec(wt.shape, const),
            pl.BlockSpec(ww.shape, const),
        ],
        out_specs=[
            pl.BlockSpec((1, tm, 3 * W_A), lambda b, i: (b, i, 0)),
            pl.BlockSpec((1, tm, R_COLS), lambda b, i: (b, i, 0)),
            pl.BlockSpec((1, T_ROWS, tm), lambda b, i: (b, 0, i)),
            pl.BlockSpec((1, WI_ROWS, tm), lambda b, i: (b, 0, i)),
        ],
        out_shape=[
            jax.ShapeDtypeStruct((B, T, 3 * W_A), F32),
            jax.ShapeDtypeStruct((B, T, R_COLS), BF16),
            jax.ShapeDtypeStruct((B, T_ROWS, T), BF16),
            jax.ShapeDtypeStruct((B, WI_ROWS, T), F32),
        ],
        compiler_params=pltpu.CompilerParams(
            dimension_semantics=("parallel", "parallel"), vmem_limit_bytes=VMEM_LIMIT),
        name="proj",
    )(x, g, wa, wr, wt, ww)


def _attn_a_kernel(slopes_ref, q_ref, k_ref, v_ref, o_ref, acc_ref, m_ref, bias_ref, stage_ref,
                   s_ref, p_ref, *, seq, unroll):
    hp = pl.program_id(1)
    lane = lax.broadcasted_iota(I32, (1, LANES), 1)
    head_lanes = (lane < HEAD_DIM, lane >= HEAD_DIM)
    row = lax.broadcasted_iota(I32, (BLK, BLK), 0)
    col = lax.broadcasted_iota(I32, (BLK, BLK), 1)
    d_cur = (row - col).astype(F32)
    d_prev = d_cur + float(BAND)

    stage_ref[:, 0:BLK, :] = jnp.zeros((5, BLK, LANES), BF16)

    def staged(g):
        return pl.ds(pl.multiple_of((g + 1) * BLK, BLK), BLK)

    def stage(dil):
        n = seq // dil

        def body(g, carry):
            first = g * BLK
            start = first // n + dil * (first % n)
            sl = pl.ds(start, BLK) if dil == 1 else pl.ds(start, BLK, stride=dil)
            q, k, v = q_ref[sl, :], k_ref[sl, :], v_ref[sl, :]
            for h in range(2):
                stage_ref[h, staged(g), :] = jnp.where(head_lanes[h], q, 0.0).astype(BF16)
                stage_ref[3 + h, staged(g), :] = jnp.where(head_lanes[h], v, 1.0).astype(BF16)
            stage_ref[2, staged(g), :] = k.astype(BF16)
            return carry

        lax.fori_loop(0, seq // BLK, body, 0, unroll=2)

    def pitch(dil):
        return seq // dil + 1 if seq // dil == BLK and dil > 1 else None

    def put(ref, h, p, g, dil, val):
        n_blocks = seq // dil // BLK
        start = g // n_blocks + (g % n_blocks) * (dil * BLK)
        if dil == 1:
            ref[h, p, pl.ds(pl.multiple_of(start, BLK), BLK), :] = val
        elif pitch(dil):
            ref[h, p, pl.ds(start * pitch(dil), BLK), :] = val
        else:
            ref[h, p, pl.ds(start, BLK, stride=dil), :] = val

    n_all = seq // BLK
    for p, dil in enumerate(DILATIONS):
        n_blocks = n_all // dil
        n_keys = BLK if n_blocks == 1 else 2 * BLK
        for h in range(2):
            sd = slopes_ref[hp * 2 + h] * float(dil)
            cur = jnp.where(row >= col, -sd * d_cur, NEG)
            bias_ref[h, 0, :, BLK:] = cur
            bias_ref[h, 1, :, BLK:] = cur
            if n_keys > BLK:
                bias_ref[h, 0, :, :BLK] = jnp.where(col >= row, -sd * d_prev, NEG)
                bias_ref[h, 1, :, :BLK] = jnp.full((BLK, BLK), NEG, F32)
        stage(dil)

        def keys(g, n_keys=n_keys):
            return pl.ds(pl.multiple_of((g + 2) * BLK - n_keys, BLK), n_keys)

        def scores(g, carry, n_blocks=n_blocks, n_keys=n_keys, keys=keys):
            first = jnp.where(jnp.asarray(g, I32) % n_blocks == 0, 1, 0)
            kw = stage_ref[2, keys(g), :]
            for h in range(2):
                s = _dot_nt(stage_ref[h, staged(g), :], kw)
                s_ref[h, g, :, :n_keys] = s + bias_ref[h, first, :, 2 * BLK - n_keys:]
            return carry

        def softmax(g, carry, p=p, dil=dil, n_keys=n_keys):
            for h in range(2):
                s = s_ref[h, g, :, :n_keys]
                m = jnp.max(s, axis=1, keepdims=True)
                p_ref[h, g, :, :n_keys] = jnp.exp(s - m).astype(BF16)
                put(m_ref, h, p, g, dil, jnp.broadcast_to(m, (BLK, LANES)))
            return carry

        def values(g, carry, p=p, dil=dil, n_keys=n_keys, keys=keys):
            for h in range(2):
                put(acc_ref, h, p, g, dil, _dot(p_ref[h, g, :, :n_keys], stage_ref[3 + h, keys(g), :]))
            return carry

        lax.fori_loop(0, n_all, scores, 0, unroll=unroll)
        lax.fori_loop(0, n_all, softmax, 0, unroll=unroll)
        lax.fori_loop(0, n_all, values, 0, unroll=unroll)

    n_pat = len(DILATIONS)
    rows_per_step = 2 * BLK

    def natural_rows(ref, h, p, i):
        dil = DILATIONS[p]
        if not pitch(dil):
            return ref[h, p, pl.ds(pl.multiple_of(i * rows_per_step, rows_per_step), rows_per_step), :]
        per = rows_per_step // dil
        return jnp.concatenate(
            [ref[h, p, pl.ds(i * per + u, dil, stride=pitch(dil)), :] for u in range(per)], axis=0)

    def merge(i, carry):
        sl = pl.ds(pl.multiple_of(i * rows_per_step, rows_per_step), rows_per_step)
        nums = []
        for h in range(2):
            ms = [natural_rows(m_ref, h, p, i) for p in range(n_pat)]
            mx = functools.reduce(jnp.maximum, ms)
            nums.append(sum(jnp.exp(ms[p] - mx) * natural_rows(acc_ref, h, p, i) for p in range(n_pat)))
        acc = jnp.where(head_lanes[0], nums[0], nums[1])
        den = pltpu.roll(jnp.where(head_lanes[0], nums[1], nums[0]), HEAD_DIM, axis=1)
        o_ref[sl, :] = acc / den
        return carry

    lax.fori_loop(0, seq // rows_per_step, merge, 0)


def _attn_a(slopes, proj_a, unroll=8):
    B, T, _ = proj_a.shape
    n_pairs = N_HEADS_A // 2
    n_pat = len(DILATIONS)

    def spec(off):
        return pl.BlockSpec((None, T, LANES), lambda b, h: (b, 0, off + h))

    return pl.pallas_call(
        functools.partial(_attn_a_kernel, seq=T, unroll=unroll),
        grid=(B, n_pairs),
        in_specs=[pl.BlockSpec(memory_space=pltpu.SMEM), spec(0), spec(n_pairs), spec(2 * n_pairs)],
        out_specs=pl.BlockSpec((None, T, LANES), lambda b, h: (b, 0, h)),
        out_shape=jax.ShapeDtypeStruct((B, T, W_A), F32),
        scratch_shapes=[pltpu.VMEM((2, n_pat, T + max(DILATIONS), LANES), F32),
                        pltpu.VMEM((2, n_pat, T + max(DILATIONS), LANES), F32),
                        pltpu.VMEM((2, 2, BLK, 2 * BLK), F32),
                        pltpu.VMEM((5, T + BLK, LANES), BF16),
                        pltpu.VMEM((2, T // BLK, BLK, 2 * BLK), F32),
                        pltpu.VMEM((2, T // BLK, BLK, 2 * BLK), BF16)],
        compiler_params=pltpu.CompilerParams(
            dimension_semantics=("parallel", "parallel"), vmem_limit_bytes=VMEM_LIMIT),
        name="attn_a",
    )(slopes, proj_a, proj_a, proj_a)


def _fold(x, op):
    return functools.reduce(op, [x[r:r + 8] for r in range(0, x.shape[0], 8)])


def _dsa_kernel(slopes_ref, kk_ref, tq_ref, vt_ref, wt_ref, o_ref, sc_ref, s_ref, p_ref, acc_ref,
                *, seq, k_sel, snap_unroll):
    j = pl.program_id(1)
    n_pairs = j + 1
    t_lane = j * DSA_Q + lax.broadcasted_iota(I32, (1, DSA_Q), 1)
    row = lax.broadcasted_iota(I32, (BLK, DSA_Q), 0)
    ws = wt_ref[0:N_IDX_HEADS, :] * (N_IDX_HEADS ** -0.5)
    inf = float("inf")

    def rows(c):
        return pl.ds(pl.multiple_of(c * BLK, BLK), BLK)

    def each_chunk(fn, init):
        def pair(i, carry):
            return fn(2 * i + 1, fn(2 * i, carry))
        return lax.fori_loop(0, n_pairs, pair, init)

    def sub_reduce(x, op):
        return op(x, axis=0, keepdims=True)

    def score_chunk(c, carry):
        mn, mx = carry
        ki = kk_ref[rows(c), R_KI - R_KB:R_KI - R_KB + IDX_DIM]
        lgs = [_dot(ki, tq_ref[T_QI + h * IDX_DIM:T_QI + (h + 1) * IDX_DIM, :])
               for h in range(N_IDX_HEADS)]
        sc = functools.reduce(
            lambda a, b: a + b, [jnp.maximum(lg, 0.0) * ws[h:h + 1, :] for h, lg in enumerate(lgs)])
        causal = (c * BLK + row) <= t_lane
        sc_ref[rows(c), :] = jnp.where(causal, sc, -inf)
        mn = jnp.minimum(mn, _fold(jnp.where(causal, sc, inf), jnp.minimum))
        mx = jnp.maximum(mx, _fold(jnp.where(causal, sc, -inf), jnp.maximum))
        return mn, mx

    mn, mx = each_chunk(score_chunk, (jnp.full((8, DSA_Q), inf, F32), jnp.full((8, DSA_Q), -inf, F32)))
    few = t_lane < k_sel
    lo = jnp.where(few, -inf, sub_reduce(mn, jnp.min))
    hi = jnp.where(few, -inf, sub_reduce(mx, jnp.max))

    def count(pred):
        def body(c, cnt):
            return cnt + _fold(pred(sc_ref[rows(c), :], c).astype(I32), jnp.add)
        return sub_reduce(each_chunk(body, jnp.zeros((8, DSA_Q), I32)), jnp.sum)

    def halve(_, bounds):
        lo, hi, n_lo, n_hi = bounds
        mid = 0.5 * lo + 0.5 * hi
        cnt = count(lambda x, c: x >= mid)
        active = lo < hi
        up = active & (cnt >= k_sel)
        down = active & (cnt <= k_sel)
        return (jnp.where(up, mid, lo), jnp.where(down, mid, hi),
                jnp.where(up, cnt, n_lo), jnp.where(down, cnt, n_hi))

    bounds = lax.fori_loop(0, DSA_HALVINGS, halve, (lo, hi, t_lane + 1, jnp.zeros((1, DSA_Q), I32)))

    def snap(bounds):
        lo, hi, n_lo, n_hi = bounds
        mid = 0.5 * lo + 0.5 * hi
        mid = jnp.where(mid > lo, mid, hi)

        def body(c, carry):
            cnt, above, below = carry
            x = sc_ref[rows(c), :]
            ge = x >= mid
            return (cnt + _fold(ge.astype(I32), jnp.add),
                    jnp.minimum(above, _fold(jnp.where(ge, x, inf), jnp.minimum)),
                    jnp.maximum(below, _fold(jnp.where(ge, -inf, x), jnp.maximum)))

        cnt, above, below = each_chunk(body, (jnp.zeros((8, DSA_Q), I32),
                                              jnp.full((8, DSA_Q), inf, F32),
                                              jnp.full((8, DSA_Q), -inf, F32)))
        cnt = sub_reduce(cnt, jnp.sum)
        above = sub_reduce(above, jnp.min)
        below = sub_reduce(below, jnp.max)
        active = lo < hi
        enough = cnt >= k_sel
        up = active & enough
        down = active & (cnt <= k_sel)
        return (jnp.where(up, above, lo), jnp.where(down, jnp.where(enough, above, below), hi),
                jnp.where(up, cnt, n_lo), jnp.where(down, cnt, n_hi))

    def snaps(bounds):
        for _ in range(snap_unroll):
            bounds = snap(bounds)
        return bounds

    def unsettled(bounds):
        return jnp.max((bounds[0] < bounds[1]).astype(I32)) > 0

    tau, _, n_ge, n_gt = lax.while_loop(unsettled, snaps, bounds)

    need = k_sel - n_gt
    n_idx_bits = seq.bit_length() - 1
    assert 1 << n_idx_bits == seq

    def tie_break():
        def idx_step(it, jp):
            cand = jp + lax.shift_left(jnp.int32(1), n_idx_bits - 1 - it)
            below = count(lambda x, c: (x == tau) & ((c * BLK + row) < cand))
            return jnp.where(below < need, cand, jp)
        last = lax.fori_loop(0, n_idx_bits, idx_step, jnp.zeros((1, DSA_Q), I32))
        return jnp.where(tied, last, seq)

    tied = (n_ge > k_sel) & jnp.logical_not(few)
    surplus = jnp.max(tied.astype(I32)) > 0
    tie_last = lax.cond(surplus, tie_break, lambda: jnp.full((1, DSA_Q), seq, I32))

    slopes = [slopes_ref[h] for h in range(N_HEADS_B)]

    def qk_chunk(c, ms):
        kb = kk_ref[rows(c), 0:HEAD_DIM]
        ss = [_dot(kb, tq_ref[T_QB + h * HEAD_DIM:T_QB + (h + 1) * HEAD_DIM, :])
              for h in range(N_HEADS_B)]
        s_idx = c * BLK + row
        rel = (s_idx - t_lane).astype(F32)
        x = sc_ref[rows(c), :]
        tie = jnp.where(x == tau, jnp.where(s_idx <= tie_last, 0.0, NEG), NEG)
        sb = jnp.where(s_idx <= t_lane, jnp.where(x > tau, 0.0, tie), NEG)
        out = []
        for h in range(N_HEADS_B):
            s = ss[h] + (slopes[h] * rel + sb)
            s_ref[h, rows(c), :] = s
            out.append(jnp.maximum(ms[h], _fold(s, jnp.maximum)))
        return tuple(out)

    ms = each_chunk(qk_chunk, tuple(jnp.full((8, DSA_Q), NEG, F32) for _ in range(N_HEADS_B)))
    ms = [sub_reduce(m, jnp.max) for m in ms]

    def exp_chunk(c, ls):
        out = []
        for h in range(N_HEADS_B):
            p = jnp.exp(s_ref[h, rows(c), :] - ms[h])
            p_ref[h, rows(c), :] = p.astype(BF16)
            out.append(ls[h] + _fold(p, jnp.add))
        return tuple(out)

    ls = each_chunk(exp_chunk, tuple(jnp.zeros((8, DSA_Q), F32) for _ in range(N_HEADS_B)))

    acc_ref[...] = jnp.zeros((W_B, DSA_Q), F32)

    def pv_pair(i, carry):
        sl = pl.ds(pl.multiple_of(i * 2 * BLK, 2 * BLK), 2 * BLK)
        vt = vt_ref[:, sl]
        pvs = [_dot(vt, p_ref[h, sl, :]) for h in range(N_HEADS_B)]
        for h in range(N_HEADS_B):
            acc_ref[h * HEAD_DIM:(h + 1) * HEAD_DIM, :] += pvs[h]
        return carry

    lax.fori_loop(0, n_pairs, pv_pair, 0)
    l_all = jnp.concatenate(
        [jnp.broadcast_to(sub_reduce(ls[h], jnp.sum), (HEAD_DIM, DSA_Q)) for h in range(N_HEADS_B)],
        axis=0)
    o_ref[...] = (acc_ref[...] / l_all).T


def _dsa(slopes, proj_r, proj_t, w_t):
    B, T, _ = proj_r.shape
    k_sel = min(TOPK_MAX, T // 4)
    return pl.pallas_call(
        functools.partial(_dsa_kernel, seq=T, k_sel=k_sel, snap_unroll=4),
        grid=(B, T // DSA_Q),
        in_specs=[
            pl.BlockSpec(memory_space=pltpu.SMEM),
            pl.BlockSpec((None, T, 2 * LANES), lambda b, q: (b, 0, R_KB // (2 * LANES))),
            pl.BlockSpec((None, T_VB, DSA_Q), lambda b, q: (b, 0, q)),
            pl.BlockSpec((None, HEAD_DIM, T), lambda b, q: (b, T_VB // HEAD_DIM, 0)),
            pl.BlockSpec((None, WI_ROWS, DSA_Q), lambda b, q: (b, 0, q)),
        ],
        out_specs=pl.BlockSpec((None, DSA_Q, W_B), lambda b, q: (b, q, 0)),
        out_shape=jax.ShapeDtypeStruct((B, T, W_B), F32),
        scratch_shapes=[pltpu.VMEM((T, DSA_Q), F32),
                        pltpu.VMEM((N_HEADS_B, T, DSA_Q), F32),
                        pltpu.VMEM((N_HEADS_B, T, DSA_Q), BF16),
                        pltpu.VMEM((W_B, DSA_Q), F32)],
        compiler_params=pltpu.CompilerParams(
            dimension_semantics=("parallel", "arbitrary"), vmem_limit_bytes=VMEM_LIMIT),
        name="dsa",
    )(slopes, proj_r, proj_t, proj_t, w_t)


def _mem_kv_kernel(mem_ref, g_ref, w_ref, kv_ref):
    kv_ref[0] = _dot(_rms(mem_ref[0], g_ref[...]).astype(BF16), w_ref[...]).astype(BF16)


def _mem_kv(mem, g, w):
    B, M, D = mem.shape
    return pl.pallas_call(
        _mem_kv_kernel,
        grid=(B,),
        in_specs=[
            pl.BlockSpec((1, M, D), lambda b: (b, 0, 0)),
            pl.BlockSpec((1, D), lambda b: (0, 0)),
            pl.BlockSpec(w.shape, lambda b: (0, 0)),
        ],
        out_specs=pl.BlockSpec((1, M, 2 * W_M), lambda b: (b, 0, 0)),
        out_shape=jax.ShapeDtypeStruct((B, M, 2 * W_M), BF16),
        compiler_params=pltpu.CompilerParams(
            dimension_semantics=("parallel",), vmem_limit_bytes=VMEM_LIMIT),
        name="mem_kv",
    )(mem, g, w)


def _out_kernel(x_ref, oa_ref, ob_ref, gate_ref, qm_ref, kv_ref, wo_ref, g_ref, y_ref):
    gate = gate_ref[...].astype(F32)
    sg = gate * (1.0 / (1.0 + jnp.exp(-gate)))

    om = []
    for h in range(N_HEADS_MEM):
        lo = h * HEAD_DIM
        s = _dot_nt(qm_ref[:, lo:lo + HEAD_DIM], kv_ref[:, lo:lo + HEAD_DIM])
        m = jnp.max(s, axis=1, keepdims=True)
        p = jnp.exp(s - m)
        l = jnp.sum(p, axis=1, keepdims=True)
        om.append(_dot(p.astype(BF16), kv_ref[:, W_M + lo:W_M + lo + HEAD_DIM]) / l)
    om = jnp.concatenate(om, axis=1)

    mix = (oa_ref[...] * sg[:, :W_A]).astype(BF16)
    y = _dot(mix, wo_ref[0:W_A, :])
    mix = (ob_ref[...] * sg[:, W_A:W_A + W_B]).astype(BF16)
    y = y + _dot(mix, wo_ref[W_A:W_A + W_B, :])
    mix = (om * sg[:, W_A + W_B:]).astype(BF16)
    y = y + _dot(mix, wo_ref[W_A + W_B:, :])
    y_ref[...] = _rms(x_ref[...] + y, g_ref[...])


def _out(x, o_a, o_b, proj_r, kv_m, w_out, g, tm=256):
    B, T, D = x.shape
    M = kv_m.shape[1]
    row = lambda b, i: (b, i, 0)
    return pl.pallas_call(
        _out_kernel,
        grid=(B, T // tm),
        in_specs=[
            pl.BlockSpec((None, tm, D), row),
            pl.BlockSpec((None, tm, W_A), row),
            pl.BlockSpec((None, tm, W_B), row),
            pl.BlockSpec((None, tm, MIX_WIDTH), lambda b, i: (b, i, R_GATE // MIX_WIDTH)),
            pl.BlockSpec((None, tm, W_M), lambda b, i: (b, i, R_QM // W_M)),
            pl.BlockSpec((None, M, 2 * W_M), lambda b, i: (b, 0, 0)),
            pl.BlockSpec(w_out.shape, lambda b, i: (0, 0)),
            pl.BlockSpec((1, D), lambda b, i: (0, 0)),
        ],
        out_specs=pl.BlockSpec((None, tm, D), row),
        out_shape=jax.ShapeDtypeStruct((B, T, D), F32),
        compiler_params=pltpu.CompilerParams(
            dimension_semantics=("parallel", "parallel"), vmem_limit_bytes=VMEM_LIMIT),
        name="out",
    )(x, o_a, o_b, proj_r, proj_r, kv_m, w_out, g)


def _alibi_slopes(n):
    return 2.0 ** (-8.0 * jnp.arange(1, n + 1, dtype=F32) / n)


def _split_weights(w):
    bounds = [0]
    for s in SPLIT_SIZES:
        bounds.append(bounds[-1] + s)
    q_a, k_a, v_a, q_b, k_b, v_b, q_m, gate, q_i, k_i, w_i = (
        w[:, bounds[i]:bounds[i + 1]] for i in range(len(SPLIT_SIZES)))
    scale = HEAD_DIM ** -0.5
    idx_scale = IDX_DIM ** -0.5
    zeros = jnp.zeros((w.shape[0], LANES - HEAD_DIM), w.dtype)
    wa = jnp.concatenate([q_a * scale, k_a, v_a], axis=1)
    wr = jnp.concatenate([gate, q_m * scale, k_b, zeros, k_i, zeros], axis=1)
    wt = jnp.concatenate([q_i * idx_scale, q_b * scale, v_b], axis=1).T
    ww = jnp.concatenate([w_i.T, jnp.zeros((WI_ROWS - N_IDX_HEADS, w.shape[0]), w.dtype)], axis=0)
    return wa.astype(BF16), wr.astype(BF16), wt.astype(BF16), ww.astype(BF16)


def kernel(x, mem, g_in, g_mem, w_in, w_mem_kv, w_out, g_final):
    assert g_in.shape[0] == 1, "single-layer block: the final RMSNorm is fused into the output kernel"
    wa, wr, wt, ww = _split_weights(w_in[0])
    proj_a, proj_r, proj_t, w_t = _proj(x, g_in, wa, wr, wt, ww)
    o_a = _attn_a(_alibi_slopes(N_HEADS_A), proj_a)
    o_b = _dsa(_alibi_slopes(N_HEADS_B), proj_r, proj_t, w_t)
    kv_m = _mem_kv(mem, g_mem, w_mem_kv[0].astype(BF16))
    return _out(x, o_a, o_b, proj_r, kv_m, w_out[0].astype(BF16), g_final[None, :])
```

```python
import functools

import jax
import jax.numpy as jnp
from jax import lax
from jax.experimental import pallas as pl
from jax.experimental.pallas import tpu as pltpu

F32 = jnp.float32
BF16 = jnp.bfloat16
I32 = jnp.int32

D_MODEL = 1024
HEAD_DIM = 64
N_HEADS_A = 8
N_HEADS_B = 4
N_HEADS_MEM = 4
W_A = N_HEADS_A * HEAD_DIM
W_B = N_HEADS_B * HEAD_DIM
W_M = N_HEADS_MEM * HEAD_DIM
MIX_WIDTH = W_A + W_B + W_M
DILATIONS = (1, 4, 16)
BAND = 128
N_IDX_HEADS = 8
IDX_DIM = 64
TOPK_MAX = 256
RMS_EPS = 1e-6
SPLIT_SIZES = (W_A, W_A, W_A, W_B, HEAD_DIM, HEAD_DIM, W_M, MIX_WIDTH,
               N_IDX_HEADS * IDX_DIM, IDX_DIM, N_IDX_HEADS)

LANES = 128
BLK = 128
NEG = -1e30
VMEM_LIMIT = 48 * 1024 * 1024
ATTN_A_VMEM_LIMIT = 56 * 1024 * 1024

R_GATE = 0
R_QM = MIX_WIDTH
R_KB = R_QM + W_M
R_KI = R_KB + LANES
R_COLS = R_KI + LANES
T_QI = 0
T_QB = N_IDX_HEADS * IDX_DIM
T_VB = T_QB + W_B
T_ROWS = T_VB + HEAD_DIM
WI_ROWS = 16
DSA_Q = 256
DSA_HALVINGS = 12


def _dot(a, b):
    return jnp.dot(a, b, preferred_element_type=F32)


def _dot_nt(a, b):
    return lax.dot_general(a, b, (((1,), (1,)), ((), ())), preferred_element_type=F32)


def _rms(x, g):
    return x * lax.rsqrt(jnp.mean(x * x, axis=-1, keepdims=True) + RMS_EPS) * g


def _proj_kernel(x_ref, g_ref, wa_ref, wr_ref, wt_ref, ww_ref, oa_ref, or_ref, ot_ref, ow_ref):
    hb = _rms(x_ref[0], g_ref[...]).astype(BF16)
    oa_ref[0] = _dot(hb, wa_ref[...])
    or_ref[0] = _dot(hb, wr_ref[...]).astype(BF16)
    ot_ref[0] = _dot_nt(wt_ref[...], hb).astype(BF16)
    ow_ref[0] = _dot_nt(ww_ref[...], hb)


def _proj(x, g, wa, wr, wt, ww, tm=512):
    B, T, D = x.shape
    const = lambda b, i: (0, 0)
    return pl.pallas_call(
        _proj_kernel,
        grid=(B, T // tm),
        in_specs=[
            pl.BlockSpec((1, tm, D), lambda b, i: (b, i, 0)),
            pl.BlockSpec((1, D), const),
            pl.BlockSpec(wa.shape, const),
            pl.BlockSpec(wr.shape, const),
            pl.BlockSpec(wt.shape, const),
            pl.BlockSpec(ww.shape, const),
        ],
        out_specs=[
            pl.BlockSpec((1, tm, 3 * W_A), lambda b, i: (b, i, 0)),
            pl.BlockSpec((1, tm, R_COLS), lambda b, i: (b, i, 0)),
            pl.BlockSpec((1, T_ROWS, tm), lambda b, i: (b, 0, i)),
            pl.BlockSpec((1, WI_ROWS, tm), lambda b, i: (b, 0, i)),
        ],
        out_shape=[
            jax.ShapeDtypeStruct((B, T, 3 * W_A), F32),
            jax.ShapeDtypeStruct((B, T, R_COLS), BF16),
            jax.ShapeDtypeStruct((B, T_ROWS, T), BF16),
            jax.ShapeDtypeStruct((B, WI_ROWS, T), F32),
        ],
        compiler_params=pltpu.CompilerParams(
            dimension_semantics=("parallel", "parallel"), vmem_limit_bytes=VMEM_LIMIT),
        name="proj",
    )(x, g, wa, wr, wt, ww)


def _attn_a_kernel(slopes_ref, q_ref, k_ref, v_ref, o_ref, acc_ref, m_ref, bias_ref, stage_ref,
                   s_ref, p_ref, cm_ref, *, seq, unroll):
    hp = pl.program_id(1)
    lane = lax.broadcasted_iota(I32, (1, LANES), 1)
    head_lanes = (lane < HEAD_DIM, lane >= HEAD_DIM)
    row = lax.broadcasted_iota(I32, (BLK, BLK), 0)
    col = lax.broadcasted_iota(I32, (BLK, BLK), 1)
    d_cur = (row - col).astype(F32)
    d_prev = d_cur + float(BAND)
    n_all = seq // BLK
    n_pat = len(DILATIONS)

    def n_blocks(p):
        return n_all // DILATIONS[p]

    def n_keys(p):
        return BLK if n_blocks(p) == 1 else 2 * BLK

    def staged(g):
        return pl.ds(pl.multiple_of((g + 1) * BLK, BLK), BLK)

    def keys(p, g):
        return pl.ds(pl.multiple_of((g + 2) * BLK - n_keys(p), BLK), n_keys(p))

    def pitch(dil):
        return seq // dil + 1 if seq // dil == BLK and dil > 1 else None

    def put(ref, h, p, g, val):
        dil = DILATIONS[p]
        start = g // n_blocks(p) + (g % n_blocks(p)) * (dil * BLK)
        if dil == 1:
            ref[h, p, pl.ds(pl.multiple_of(start, BLK), BLK), :] = val
        elif pitch(dil):
            ref[h, p, pl.ds(start * pitch(dil), BLK), :] = val
        else:
            ref[h, p, pl.ds(start, BLK, stride=dil), :] = val

    def stage(p):
        dil = DILATIONS[p]
        n = seq // dil
        src_dil = DILATIONS[p - 1] if p >= 2 else 1
        step = dil // src_dil
        srcs = (cm_ref.at[0], cm_ref.at[1], cm_ref.at[2]) if p >= 2 else (q_ref, k_ref, v_ref)
        keep_copy = p + 1 < n_pat and p >= 1

        def body(g):
            first = g * BLK
            r, i0 = first // n, first % n
            start = (r % src_dil) * (seq // src_dil) + r // src_dil + step * i0
            sl = pl.ds(start, BLK) if step == 1 else pl.ds(start, BLK, stride=step)
            q, k, v = (src[sl, :] for src in srcs)
            if keep_copy:
                for a, val in enumerate((q, k, v)):
                    cm_ref[a, pl.ds(pl.multiple_of(first, BLK), BLK), :] = val
            for h in range(2):
                stage_ref[p, h, staged(g), :] = jnp.where(head_lanes[h], q, 0.0).astype(BF16)
                stage_ref[p, 3 + h, staged(g), :] = jnp.where(head_lanes[h], v, 1.0).astype(BF16)
            stage_ref[p, 2, staged(g), :] = k.astype(BF16)
        return body

    def scores(p):
        def body(g):
            first = jnp.where(jnp.asarray(g, I32) % n_blocks(p) == 0, 1, 0)
            kw = stage_ref[p, 2, keys(p, g), :]
            for h in range(2):
                s = _dot_nt(stage_ref[p, h, staged(g), :], kw)
                s_ref[p % 2, h, g, :, :n_keys(p)] = s + bias_ref[p, h, first, :, 2 * BLK - n_keys(p):]
        return body

    def softmax(p):
        def body(g):
            for h in range(2):
                s = s_ref[p % 2, h, g, :, :n_keys(p)]
                m = jnp.max(s, axis=1, keepdims=True)
                p_ref[p % 2, h, g, :, :n_keys(p)] = jnp.exp(s - m).astype(BF16)
                put(m_ref, h, p, g, jnp.broadcast_to(m, (BLK, LANES)))
        return body

    def values(p):
        def body(g):
            for h in range(2):
                put(acc_ref, h, p, g,
                    _dot(p_ref[p % 2, h, g, :, :n_keys(p)], stage_ref[p, 3 + h, keys(p, g), :]))
        return body

    def run(*bodies, unroll=unroll):
        def step(g, carry):
            for body in bodies:
                body(g)
            return carry
        lax.fori_loop(0, n_all, step, 0, unroll=unroll)

    for p, dil in enumerate(DILATIONS):
        stage_ref[p, :, 0:BLK, :] = jnp.zeros((5, BLK, LANES), BF16)
        for h in range(2):
            sd = slopes_ref[hp * 2 + h] * float(dil)
            cur = jnp.where(row >= col, -sd * d_cur, NEG)
            bias_ref[p, h, 0, :, BLK:] = cur
            bias_ref[p, h, 1, :, BLK:] = cur
            if n_keys(p) > BLK:
                bias_ref[p, h, 0, :, :BLK] = jnp.where(col >= row, -sd * d_prev, NEG)
                bias_ref[p, h, 1, :, :BLK] = jnp.full((BLK, BLK), NEG, F32)

    assert n_pat == 3
    run(stage(0), unroll=2)
    run(scores(0), stage(1))
    run(softmax(0), scores(1))
    run(values(0), softmax(1), stage(2))
    run(values(1), scores(2))
    run(softmax(2))
    run(values(2))

    rows_per_step = 2 * BLK

    def natural_rows(ref, h, p, i):
        dil = DILATIONS[p]
        if not pitch(dil):
            return ref[h, p, pl.ds(pl.multiple_of(i * rows_per_step, rows_per_step), rows_per_step), :]
        per = rows_per_step // dil
        return jnp.concatenate(
            [ref[h, p, pl.ds(i * per + u, dil, stride=pitch(dil)), :] for u in range(per)], axis=0)

    def merge(i, carry):
        sl = pl.ds(pl.multiple_of(i * rows_per_step, rows_per_step), rows_per_step)
        nums = []
        for h in range(2):
            ms = [natural_rows(m_ref, h, p, i) for p in range(n_pat)]
            mx = functools.reduce(jnp.maximum, ms)
            nums.append(sum(jnp.exp(ms[p] - mx) * natural_rows(acc_ref, h, p, i) for p in range(n_pat)))
        acc = jnp.where(head_lanes[0], nums[0], nums[1])
        den = pltpu.roll(jnp.where(head_lanes[0], nums[1], nums[0]), HEAD_DIM, axis=1)
        o_ref[sl, :] = acc / den
        return carry

    lax.fori_loop(0, seq // rows_per_step, merge, 0)


def _attn_a(slopes, proj_a, unroll=8):
    B, T, _ = proj_a.shape
    n_pairs = N_HEADS_A // 2
    n_pat = len(DILATIONS)

    def spec(off):
        return pl.BlockSpec((None, T, LANES), lambda b, h: (b, 0, off + h))

    return pl.pallas_call(
        functools.partial(_attn_a_kernel, seq=T, unroll=unroll),
        grid=(B, n_pairs),
        in_specs=[pl.BlockSpec(memory_space=pltpu.SMEM), spec(0), spec(n_pairs), spec(2 * n_pairs)],
        out_specs=pl.BlockSpec((None, T, LANES), lambda b, h: (b, 0, h)),
        out_shape=jax.ShapeDtypeStruct((B, T, W_A), F32),
        scratch_shapes=[pltpu.VMEM((2, n_pat, T + max(DILATIONS), LANES), F32),
                        pltpu.VMEM((2, n_pat, T + max(DILATIONS), LANES), F32),
                        pltpu.VMEM((n_pat, 2, 2, BLK, 2 * BLK), F32),
                        pltpu.VMEM((n_pat, 5, T + BLK, LANES), BF16),
                        pltpu.VMEM((2, 2, T // BLK, BLK, 2 * BLK), F32),
                        pltpu.VMEM((2, 2, T // BLK, BLK, 2 * BLK), BF16),
                        pltpu.VMEM((3, T, LANES), F32)],
        compiler_params=pltpu.CompilerParams(
            dimension_semantics=("parallel", "parallel"), vmem_limit_bytes=ATTN_A_VMEM_LIMIT),
        name="attn_a",
    )(slopes, proj_a, proj_a, proj_a)


def _fold(x, op):
    return functools.reduce(op, [x[r:r + 8] for r in range(0, x.shape[0], 8)])


def _dsa_kernel(slopes_ref, kk_ref, tq_ref, vt_ref, wt_ref, o_ref, sc_ref, s_ref, p_ref, acc_ref,
                *, seq, k_sel, snap_unroll):
    j = pl.program_id(1)
    n_pairs = j + 1
    t_lane = j * DSA_Q + lax.broadcasted_iota(I32, (1, DSA_Q), 1)
    row = lax.broadcasted_iota(I32, (BLK, DSA_Q), 0)
    ws = wt_ref[0:N_IDX_HEADS, :] * (N_IDX_HEADS ** -0.5)
    inf = float("inf")

    def rows(c):
        return pl.ds(pl.multiple_of(c * BLK, BLK), BLK)

    def each_chunk(fn, init):
        def pair(i, carry):
            return fn(2 * i + 1, fn(2 * i, carry))
        return lax.fori_loop(0, n_pairs, pair, init)

    def sub_reduce(x, op):
        return op(x, axis=0, keepdims=True)

    def score_chunk(c, carry):
        mn, mx = carry
        ki = kk_ref[rows(c), R_KI - R_KB:R_KI - R_KB + IDX_DIM]
        lgs = [_dot(ki, tq_ref[T_QI + h * IDX_DIM:T_QI + (h + 1) * IDX_DIM, :])
               for h in range(N_IDX_HEADS)]
        sc = functools.reduce(
            lambda a, b: a + b, [jnp.maximum(lg, 0.0) * ws[h:h + 1, :] for h, lg in enumerate(lgs)])
        causal = (c * BLK + row) <= t_lane
        sc_ref[rows(c), :] = jnp.where(causal, sc, -inf)
        mn = jnp.minimum(mn, _fold(jnp.where(causal, sc, inf), jnp.minimum))
        mx = jnp.maximum(mx, _fold(jnp.where(causal, sc, -inf), jnp.maximum))
        return mn, mx

    mn, mx = each_chunk(score_chunk, (jnp.full((8, DSA_Q), inf, F32), jnp.full((8, DSA_Q), -inf, F32)))
    few = t_lane < k_sel
    lo = jnp.where(few, -inf, sub_reduce(mn, jnp.min))
    hi = jnp.where(few, -inf, sub_reduce(mx, jnp.max))

    def count(pred):
        def body(c, cnt):
            return cnt + _fold(pred(sc_ref[rows(c), :], c).astype(I32), jnp.add)
        return sub_reduce(each_chunk(body, jnp.zeros((8, DSA_Q), I32)), jnp.sum)

    def halve(_, bounds):
        lo, hi, n_lo, n_hi = bounds
        mid = 0.5 * lo + 0.5 * hi
        cnt = count(lambda x, c: x >= mid)
        active = lo < hi
        up = active & (cnt >= k_sel)
        down = active & (cnt <= k_sel)
        return (jnp.where(up, mid, lo), jnp.where(down, mid, hi),
                jnp.where(up, cnt, n_lo), jnp.where(down, cnt, n_hi))

    bounds = lax.fori_loop(0, DSA_HALVINGS, halve, (lo, hi, t_lane + 1, jnp.zeros((1, DSA_Q), I32)))

    def snap(bounds):
        lo, hi, n_lo, n_hi = bounds
        mid = 0.5 * lo + 0.5 * hi
        mid = jnp.where(mid > lo, mid, hi)

        def body(c, carry):
            cnt, above, below = carry
            x = sc_ref[rows(c), :]
            ge = x >= mid
            return (cnt + _fold(ge.astype(I32), jnp.add),
                    jnp.minimum(above, _fold(jnp.where(ge, x, inf), jnp.minimum)),
                    jnp.maximum(below, _fold(jnp.where(ge, -inf, x), jnp.maximum)))

        cnt, above, below = each_chunk(body, (jnp.zeros((8, DSA_Q), I32),
                                              jnp.full((8, DSA_Q), inf, F32),
                                              jnp.full((8, DSA_Q), -inf, F32)))
        cnt = sub_reduce(cnt, jnp.sum)
        above = sub_reduce(above, jnp.min)
        below = sub_reduce(below, jnp.max)
        active = lo < hi
        enough = cnt >= k_sel
        up = active & enough
        down = active & (cnt <= k_sel)
        return (jnp.where(up, above, lo), jnp.where(down, jnp.where(enough, above, below), hi),
                jnp.where(up, cnt, n_lo), jnp.where(down, cnt, n_hi))

    def snaps(bounds):
        for _ in range(snap_unroll):
            bounds = snap(bounds)
        return bounds

    def unsettled(bounds):
        return jnp.max((bounds[0] < bounds[1]).astype(I32)) > 0

    tau, _, n_ge, n_gt = lax.while_loop(unsettled, snaps, bounds)

    need = k_sel - n_gt
    n_idx_bits = seq.bit_length() - 1
    assert 1 << n_idx_bits == seq

    def tie_break():
        def idx_step(it, jp):
            cand = jp + lax.shift_left(jnp.int32(1), n_idx_bits - 1 - it)
            below = count(lambda x, c: (x == tau) & ((c * BLK + row) < cand))
            return jnp.where(below < need, cand, jp)
        last = lax.fori_loop(0, n_idx_bits, idx_step, jnp.zeros((1, DSA_Q), I32))
        return jnp.where(tied, last, seq)

    tied = (n_ge > k_sel) & jnp.logical_not(few)
    surplus = jnp.max(tied.astype(I32)) > 0
    tie_last = lax.cond(surplus, tie_break, lambda: jnp.full((1, DSA_Q), seq, I32))

    slopes = [slopes_ref[h] for h in range(N_HEADS_B)]

    def qk_chunk(c, ms):
        kb = kk_ref[rows(c), 0:HEAD_DIM]
        ss = [_dot(kb, tq_ref[T_QB + h * HEAD_DIM:T_QB + (h + 1) * HEAD_DIM, :])
              for h in range(N_HEADS_B)]
        s_idx = c * BLK + row
        rel = (s_idx - t_lane).astype(F32)
        x = sc_ref[rows(c), :]
        tie = jnp.where(x == tau, jnp.where(s_idx <= tie_last, 0.0, NEG), NEG)
        sb = jnp.where(s_idx <= t_lane, jnp.where(x > tau, 0.0, tie), NEG)
        out = []
        for h in range(N_HEADS_B):
            s = ss[h] + (slopes[h] * rel + sb)
            s_ref[h, rows(c), :] = s
            out.append(jnp.maximum(ms[h], _fold(s, jnp.maximum)))
        return tuple(out)

    ms = each_chunk(qk_chunk, tuple(jnp.full((8, DSA_Q), NEG, F32) for _ in range(N_HEADS_B)))
    ms = [sub_reduce(m, jnp.max) for m in ms]

    def exp_chunk(c, ls):
        out = []
        for h in range(N_HEADS_B):
            p = jnp.exp(s_ref[h, rows(c), :] - ms[h])
            p_ref[h, rows(c), :] = p.astype(BF16)
            out.append(ls[h] + _fold(p, jnp.add))
        return tuple(out)

    ls = each_chunk(exp_chunk, tuple(jnp.zeros((8, DSA_Q), F32) for _ in range(N_HEADS_B)))

    acc_ref[...] = jnp.zeros((W_B, DSA_Q), F32)

    def pv_pair(i, carry):
        sl = pl.ds(pl.multiple_of(i * 2 * BLK, 2 * BLK), 2 * BLK)
        vt = vt_ref[:, sl]
        pvs = [_dot(vt, p_ref[h, sl, :]) for h in range(N_HEADS_B)]
        for h in range(N_HEADS_B):
            acc_ref[h * HEAD_DIM:(h + 1) * HEAD_DIM, :] += pvs[h]
        return carry

    lax.fori_loop(0, n_pairs, pv_pair, 0)
    l_all = jnp.concatenate(
        [jnp.broadcast_to(sub_reduce(ls[h], jnp.sum), (HEAD_DIM, DSA_Q)) for h in range(N_HEADS_B)],
        axis=0)
    o_ref[...] = (acc_ref[...] / l_all).T


def _dsa(slopes, proj_r, proj_t, w_t):
    B, T, _ = proj_r.shape
    k_sel = min(TOPK_MAX, T // 4)
    return pl.pallas_call(
        functools.partial(_dsa_kernel, seq=T, k_sel=k_sel, snap_unroll=4),
        grid=(B, T // DSA_Q),
        in_specs=[
            pl.BlockSpec(memory_space=pltpu.SMEM),
            pl.BlockSpec((None, T, 2 * LANES), lambda b, q: (b, 0, R_KB // (2 * LANES))),
            pl.BlockSpec((None, T_VB, DSA_Q), lambda b, q: (b, 0, q)),
            pl.BlockSpec((None, HEAD_DIM, T), lambda b, q: (b, T_VB // HEAD_DIM, 0)),
            pl.BlockSpec((None, WI_ROWS, DSA_Q), lambda b, q: (b, 0, q)),
        ],
        out_specs=pl.BlockSpec((None, DSA_Q, W_B), lambda b, q: (b, q, 0)),
        out_shape=jax.ShapeDtypeStruct((B, T, W_B), F32),
        scratch_shapes=[pltpu.VMEM((T, DSA_Q), F32),
                        pltpu.VMEM((N_HEADS_B, T, DSA_Q), F32),
                        pltpu.VMEM((N_HEADS_B, T, DSA_Q), BF16),
                        pltpu.VMEM((W_B, DSA_Q), F32)],
        compiler_params=pltpu.CompilerParams(
            dimension_semantics=("parallel", "arbitrary"), vmem_limit_bytes=VMEM_LIMIT),
        name="dsa",
    )(slopes, proj_r, proj_t, proj_t, w_t)


def _mem_kv_kernel(mem_ref, g_ref, w_ref, kv_ref):
    kv_ref[0] = _dot(_rms(mem_ref[0], g_ref[...]).astype(BF16), w_ref[...]).astype(BF16)


def _mem_kv(mem, g, w):
    B, M, D = mem.shape
    return pl.pallas_call(
        _mem_kv_kernel,
        grid=(B,),
        in_specs=[
            pl.BlockSpec((1, M, D), lambda b: (b, 0, 0)),
            pl.BlockSpec((1, D), lambda b: (0, 0)),
            pl.BlockSpec(w.shape, lambda b: (0, 0)),
        ],
        out_specs=pl.BlockSpec((1, M, 2 * W_M), lambda b: (b, 0, 0)),
        out_shape=jax.ShapeDtypeStruct((B, M, 2 * W_M), BF16),
        compiler_params=pltpu.CompilerParams(
            dimension_semantics=("parallel",), vmem_limit_bytes=VMEM_LIMIT),
        name="mem_kv",
    )(mem, g, w)


def _out_kernel(x_ref, oa_ref, ob_ref, gate_ref, qm_ref, kv_ref, wo_ref, g_ref, y_ref):
    gate = gate_ref[...].astype(F32)
    sg = gate * (1.0 / (1.0 + jnp.exp(-gate)))

    om = []
    for h in range(N_HEADS_MEM):
        lo = h * HEAD_DIM
        s = _dot_nt(qm_ref[:, lo:lo + HEAD_DIM], kv_ref[:, lo:lo + HEAD_DIM])
        m = jnp.max(s, axis=1, keepdims=True)
        p = jnp.exp(s - m)
        l = jnp.sum(p, axis=1, keepdims=True)
        om.append(_dot(p.astype(BF16), kv_ref[:, W_M + lo:W_M + lo + HEAD_DIM]) / l)
    om = jnp.concatenate(om, axis=1)

    mix = (oa_ref[...] * sg[:, :W_A]).astype(BF16)
    y = _dot(mix, wo_ref[0:W_A, :])
    mix = (ob_ref[...] * sg[:, W_A:W_A + W_B]).astype(BF16)
    y = y + _dot(mix, wo_ref[W_A:W_A + W_B, :])
    mix = (om * sg[:, W_A + W_B:]).astype(BF16)
    y = y + _dot(mix, wo_ref[W_A + W_B:, :])
    y_ref[...] = _rms(x_ref[...] + y, g_ref[...])


def _out(x, o_a, o_b, proj_r, kv_m, w_out, g, tm=256):
    B, T, D = x.shape
    M = kv_m.shape[1]
    row = lambda b, i: (b, i, 0)
    return pl.pallas_call(
        _out_kernel,
        grid=(B, T // tm),
        in_specs=[
            pl.BlockSpec((None, tm, D), row),
            pl.BlockSpec((None, tm, W_A), row),
            pl.BlockSpec((None, tm, W_B), row),
            pl.BlockSpec((None, tm, MIX_WIDTH), lambda b, i: (b, i, R_GATE // MIX_WIDTH)),
            pl.BlockSpec((None, tm, W_M), lambda b, i: (b, i, R_QM // W_M)),
            pl.BlockSpec((None, M, 2 * W_M), lambda b, i: (b, 0, 0)),
            pl.BlockSpec(w_out.shape, lambda b, i: (0, 0)),
            pl.BlockSpec((1, D), lambda b, i: (0, 0)),
        ],
        out_specs=pl.BlockSpec((None, tm, D), row),
        out_shape=jax.ShapeDtypeStruct((B, T, D), F32),
        compiler_params=pltpu.CompilerParams(
            dimension_semantics=("parallel", "parallel"), vmem_limit_bytes=VMEM_LIMIT),
        name="out",
    )(x, o_a, o_b, proj_r, proj_r, kv_m, w_out, g)


def _alibi_slopes(n):
    return 2.0 ** (-8.0 * jnp.arange(1, n + 1, dtype=F32) / n)


def _split_weights(w):
    bounds = [0]
    for s in SPLIT_SIZES:
        bounds.append(bounds[-1] + s)
    q_a, k_a, v_a, q_b, k_b, v_b, q_m, gate, q_i, k_i, w_i = (
        w[:, bounds[i]:bounds[i + 1]] for i in range(len(SPLIT_SIZES)))
    scale = HEAD_DIM ** -0.5
    idx_scale = IDX_DIM ** -0.5
    zeros = jnp.zeros((w.shape[0], LANES - HEAD_DIM), w.dtype)
    wa = jnp.concatenate([q_a * scale, k_a, v_a], axis=1)
    wr = jnp.concatenate([gate, q_m * scale, k_b, zeros, k_i, zeros], axis=1)
    wt = jnp.concatenate([q_i * idx_scale, q_b * scale, v_b], axis=1).T
    ww = jnp.concatenate([w_i.T, jnp.zeros((WI_ROWS - N_IDX_HEADS, w.shape[0]), w.dtype)], axis=0)
    return wa.astype(BF16), wr.astype(BF16), wt.astype(BF16), ww.astype(BF16)


def kernel(x, mem, g_in, g_mem, w_in, w_mem_kv, w_out, g_final):
    assert g_in.shape[0] == 1, "single-layer block: the final RMSNorm is fused into the output kernel"
    wa, wr, wt, ww = _split_weights(w_in[0])
    proj_a, proj_r, proj_t, w_t = _proj(x, g_in, wa, wr, wt, ww)
    o_a = _attn_a(_alibi_slopes(N_HEADS_A), proj_a)
    o_b = _dsa(_alibi_slopes(N_HEADS_B), proj_r, proj_t, w_t)
    kv_m = _mem_kv(mem, g_mem, w_mem_kv[0].astype(BF16))
    return _out(x, o_a, o_b, proj_r, kv_m, w_out[0].astype(BF16), g_final[None, :])
```

```python
import functools

import jax
import jax.numpy as jnp
from jax import lax
from jax.experimental import pallas as pl
from jax.experimental.pallas import tpu as pltpu

F32 = jnp.float32
BF16 = jnp.bfloat16
I32 = jnp.int32

D_MODEL = 1024
HEAD_DIM = 64
N_HEADS_A = 8
N_HEADS_B = 4
N_HEADS_MEM = 4
W_A = N_HEADS_A * HEAD_DIM
W_B = N_HEADS_B * HEAD_DIM
W_M = N_HEADS_MEM * HEAD_DIM
MIX_WIDTH = W_A + W_B + W_M
DILATIONS = (1, 4, 16)
BAND = 128
N_IDX_HEADS = 8
IDX_DIM = 64
TOPK_MAX = 256
RMS_EPS = 1e-6
SPLIT_SIZES = (W_A, W_A, W_A, W_B, HEAD_DIM, HEAD_DIM, W_M, MIX_WIDTH,
               N_IDX_HEADS * IDX_DIM, IDX_DIM, N_IDX_HEADS)

LANES = 128
BLK = 128
NEG = -1e30
VMEM_LIMIT = 48 * 1024 * 1024
ATTN_A_VMEM_LIMIT = 56 * 1024 * 1024

R_GATE = 0
R_QM = MIX_WIDTH
R_KB = R_QM + W_M
R_KI = R_KB + LANES
R_COLS = R_KI + LANES
T_QI = 0
T_QB = N_IDX_HEADS * IDX_DIM
T_VB = T_QB + W_B
T_ROWS = T_VB + HEAD_DIM
WI_ROWS = 16
DSA_Q = 256
DSA_HALVINGS = 12


def _dot(a, b):
    return jnp.dot(a, b, preferred_element_type=F32)


def _dot_nt(a, b):
    return lax.dot_general(a, b, (((1,), (1,)), ((), ())), preferred_element_type=F32)


def _rms(x, g):
    return x * lax.rsqrt(jnp.mean(x * x, axis=-1, keepdims=True) + RMS_EPS) * g


def _proj_kernel(x_ref, g_ref, wa_ref, wr_ref, wt_ref, ww_ref, oa_ref, or_ref, ot_ref, ow_ref):
    hb = _rms(x_ref[0], g_ref[...]).astype(BF16)
    oa_ref[0] = _dot(hb, wa_ref[...])
    or_ref[0] = _dot(hb, wr_ref[...]).astype(BF16)
    ot_ref[0] = _dot_nt(wt_ref[...], hb).astype(BF16)
    ow_ref[0] = _dot_nt(ww_ref[...], hb)


def _proj(x, g, wa, wr, wt, ww, tm=512):
    B, T, D = x.shape
    const = lambda b, i: (0, 0)
    return pl.pallas_call(
        _proj_kernel,
        grid=(B, T // tm),
        in_specs=[
            pl.BlockSpec((1, tm, D), lambda b, i: (b, i, 0)),
            pl.BlockSpec((1, D), const),
            pl.BlockSpec(wa.shape, const),
            pl.BlockSpec(wr.shape, const),
            pl.BlockSpec(wt.shape, const),
            pl.BlockSpec(ww.shape, const),
        ],
        out_specs=[
            pl.BlockSpec((1, tm, 3 * W_A), lambda b, i: (b, i, 0)),
            pl.BlockSpec((1, tm, R_COLS), lambda b, i: (b, i, 0)),
            pl.BlockSpec((1, T_ROWS, tm), lambda b, i: (b, 0, i)),
            pl.BlockSpec((1, WI_ROWS, tm), lambda b, i: (b, 0, i)),
        ],
        out_shape=[
            jax.ShapeDtypeStruct((B, T, 3 * W_A), F32),
            jax.ShapeDtypeStruct((B, T, R_COLS), BF16),
            jax.ShapeDtypeStruct((B, T_ROWS, T), BF16),
            jax.ShapeDtypeStruct((B, WI_ROWS, T), F32),
        ],
        compiler_params=pltpu.CompilerParams(
            dimension_semantics=("parallel", "parallel"), vmem_limit_bytes=VMEM_LIMIT),
        name="proj",
    )(x, g, wa, wr, wt, ww)


def _attn_a_kernel(slopes_ref, q_ref, k_ref, v_ref, o_ref, acc_ref, m_ref, bias_ref, stage_ref,
                   s_ref, p_ref, cm_ref, *, seq, unroll):
    hp = pl.program_id(1)
    lane = lax.broadcasted_iota(I32, (1, LANES), 1)
    head_lanes = (lane < HEAD_DIM, lane >= HEAD_DIM)
    row = lax.broadcasted_iota(I32, (BLK, BLK), 0)
    col = lax.broadcasted_iota(I32, (BLK, BLK), 1)
    d_cur = (row - col).astype(F32)
    d_prev = d_cur + float(BAND)
    n_all = seq // BLK
    n_pat = len(DILATIONS)

    def n_blocks(p):
        return n_all // DILATIONS[p]

    def n_keys(p):
        return BLK if n_blocks(p) == 1 else 2 * BLK

    def staged(g):
        return pl.ds(pl.multiple_of((g + 1) * BLK, BLK), BLK)

    def keys(p, g):
        return pl.ds(pl.multiple_of((g + 2) * BLK - n_keys(p), BLK), n_keys(p))

    def pitch(dil):
        return seq // dil + 1 if seq // dil == BLK and dil > 1 else None

    def put(ref, h, p, g, val):
        dil = DILATIONS[p]
        start = g // n_blocks(p) + (g % n_blocks(p)) * (dil * BLK)
        if dil == 1:
            ref[h, p, pl.ds(pl.multiple_of(start, BLK), BLK), :] = val
        elif pitch(dil):
            ref[h, p, pl.ds(start * pitch(dil), BLK), :] = val
        else:
            ref[h, p, pl.ds(start, BLK, stride=dil), :] = val

    def stage(p):
        dil = DILATIONS[p]
        n = seq // dil
        src_dil = DILATIONS[p - 1] if p >= 2 else 1
        step = dil // src_dil
        srcs = (cm_ref.at[0], cm_ref.at[1], cm_ref.at[2]) if p >= 2 else (q_ref, k_ref, v_ref)
        keep_copy = p + 1 < n_pat and p >= 1

        def body(g):
            first = g * BLK
            r, i0 = first // n, first % n
            start = (r % src_dil) * (seq // src_dil) + r // src_dil + step * i0
            sl = pl.ds(start, BLK) if step == 1 else pl.ds(start, BLK, stride=step)
            q, k, v = (src[sl, :] for src in srcs)
            if keep_copy:
                for a, val in enumerate((q, k, v)):
                    cm_ref[a, pl.ds(pl.multiple_of(first, BLK), BLK), :] = val
            for h in range(2):
                stage_ref[p, h, staged(g), :] = jnp.where(head_lanes[h], q, 0.0).astype(BF16)
                stage_ref[p, 3 + h, staged(g), :] = jnp.where(head_lanes[h], v, 1.0).astype(BF16)
            stage_ref[p, 2, staged(g), :] = k.astype(BF16)
        return body

    def scores(p):
        def body(g):
            first = jnp.where(jnp.asarray(g, I32) % n_blocks(p) == 0, 1, 0)
            kw = stage_ref[p, 2, keys(p, g), :]
            for h in range(2):
                s = _dot_nt(stage_ref[p, h, staged(g), :], kw)
                s_ref[p % 2, h, g, :, :n_keys(p)] = s + bias_ref[p, h, first, :, 2 * BLK - n_keys(p):]
        return body

    def softmax(p):
        def body(g):
            for h in range(2):
                s = s_ref[p % 2, h, g, :, :n_keys(p)]
                m = jnp.max(s, axis=1, keepdims=True)
                p_ref[p % 2, h, g, :, :n_keys(p)] = jnp.exp(s - m).astype(BF16)
                put(m_ref, h, p, g, jnp.broadcast_to(m, (BLK, LANES)))
        return body

    def values(p):
        def body(g):
            for h in range(2):
                put(acc_ref, h, p, g,
                    _dot(p_ref[p % 2, h, g, :, :n_keys(p)], stage_ref[p, 3 + h, keys(p, g), :]))
        return body

    def run(*bodies, unroll=unroll):
        def step(g, carry):
            for body in bodies:
                body(g)
            return carry
        lax.fori_loop(0, n_all, step, 0, unroll=unroll)

    for p, dil in enumerate(DILATIONS):
        stage_ref[p, :, 0:BLK, :] = jnp.zeros((5, BLK, LANES), BF16)
        for h in range(2):
            sd = slopes_ref[hp * 2 + h] * float(dil)
            cur = jnp.where(row >= col, -sd * d_cur, NEG)
            bias_ref[p, h, 0, :, BLK:] = cur
            bias_ref[p, h, 1, :, BLK:] = cur
            if n_keys(p) > BLK:
                bias_ref[p, h, 0, :, :BLK] = jnp.where(col >= row, -sd * d_prev, NEG)
                bias_ref[p, h, 1, :, :BLK] = jnp.full((BLK, BLK), NEG, F32)

    assert n_pat == 3
    run(stage(0), unroll=2)
    run(scores(0), stage(1))
    run(softmax(0), scores(1))
    run(values(0), softmax(1), stage(2))
    run(values(1), scores(2))
    run(softmax(2))
    run(values(2))

    rows_per_step = 2 * BLK

    def natural_rows(ref, h, p, i):
        dil = DILATIONS[p]
        if not pitch(dil):
            return ref[h, p, pl.ds(pl.multiple_of(i * rows_per_step, rows_per_step), rows_per_step), :]
        per = rows_per_step // dil
        return jnp.concatenate(
            [ref[h, p, pl.ds(i * per + u, dil, stride=pitch(dil)), :] for u in range(per)], axis=0)

    def merge(i, carry):
        sl = pl.ds(pl.multiple_of(i * rows_per_step, rows_per_step), rows_per_step)
        nums = []
        for h in range(2):
            ms = [natural_rows(m_ref, h, p, i) for p in range(n_pat)]
            mx = functools.reduce(jnp.maximum, ms)
            nums.append(sum(jnp.exp(ms[p] - mx) * natural_rows(acc_ref, h, p, i) for p in range(n_pat)))
        acc = jnp.where(head_lanes[0], nums[0], nums[1])
        den = pltpu.roll(jnp.where(head_lanes[0], nums[1], nums[0]), HEAD_DIM, axis=1)
        o_ref[sl, :] = acc / den
        return carry

    lax.fori_loop(0, seq // rows_per_step, merge, 0)


def _attn_a(slopes, proj_a, unroll=8):
    B, T, _ = proj_a.shape
    n_pairs = N_HEADS_A // 2
    n_pat = len(DILATIONS)

    def spec(off):
        return pl.BlockSpec((None, T, LANES), lambda b, h: (b, 0, off + h))

    return pl.pallas_call(
        functools.partial(_attn_a_kernel, seq=T, unroll=unroll),
        grid=(B, n_pairs),
        in_specs=[pl.BlockSpec(memory_space=pltpu.SMEM), spec(0), spec(n_pairs), spec(2 * n_pairs)],
        out_specs=pl.BlockSpec((None, T, LANES), lambda b, h: (b, 0, h)),
        out_shape=jax.ShapeDtypeStruct((B, T, W_A), F32),
        scratch_shapes=[pltpu.VMEM((2, n_pat, T + max(DILATIONS), LANES), F32),
                        pltpu.VMEM((2, n_pat, T + max(DILATIONS), LANES), F32),
                        pltpu.VMEM((n_pat, 2, 2, BLK, 2 * BLK), F32),
                        pltpu.VMEM((n_pat, 5, T + BLK, LANES), BF16),
                        pltpu.VMEM((2, 2, T // BLK, BLK, 2 * BLK), F32),
                        pltpu.VMEM((2, 2, T // BLK, BLK, 2 * BLK), BF16),
                        pltpu.VMEM((3, T, LANES), F32)],
        compiler_params=pltpu.CompilerParams(
            dimension_semantics=("parallel", "parallel"), vmem_limit_bytes=ATTN_A_VMEM_LIMIT),
        name="attn_a",
    )(slopes, proj_a, proj_a, proj_a)


def _fold(x, op):
    return functools.reduce(op, [x[r:r + 8] for r in range(0, x.shape[0], 8)])


def _dsa_kernel(slopes_ref, kk_ref, tq_ref, vt_ref, wt_ref, o_ref, sc_ref, s_ref, p_ref, acc_ref,
                *, seq, k_sel, snap_unroll):
    j = pl.program_id(1)
    n_pairs = j + 1
    t_lane = j * DSA_Q + lax.broadcasted_iota(I32, (1, DSA_Q), 1)
    row = lax.broadcasted_iota(I32, (BLK, DSA_Q), 0)
    ws = wt_ref[0:N_IDX_HEADS, :] * (N_IDX_HEADS ** -0.5)
    inf = float("inf")

    def rows(c):
        return pl.ds(pl.multiple_of(c * BLK, BLK), BLK)

    def each_chunk(fn, init):
        def group(first, n, carry):
            for u in range(n):
                carry = fn(first + u, carry)
            return carry
        carry = lax.fori_loop(0, n_pairs // 2, lambda i, c: group(4 * i, 4, c), init)
        return lax.fori_loop(0, n_pairs % 2, lambda i, c: group(4 * (n_pairs // 2), 2, c), carry)

    def sub_reduce(x, op):
        return op(x, axis=0, keepdims=True)

    def score_chunk(c, carry):
        mn, mx = carry
        ki = kk_ref[rows(c), R_KI - R_KB:R_KI - R_KB + IDX_DIM]
        lgs = [_dot(ki, tq_ref[T_QI + h * IDX_DIM:T_QI + (h + 1) * IDX_DIM, :])
               for h in range(N_IDX_HEADS)]
        sc = functools.reduce(
            lambda a, b: a + b, [jnp.maximum(lg, 0.0) * ws[h:h + 1, :] for h, lg in enumerate(lgs)])
        causal = (c * BLK + row) <= t_lane
        sc_ref[rows(c), :] = jnp.where(causal, sc, -inf)
        mn = jnp.minimum(mn, _fold(jnp.where(causal, sc, inf), jnp.minimum))
        mx = jnp.maximum(mx, _fold(jnp.where(causal, sc, -inf), jnp.maximum))
        return mn, mx

    mn, mx = each_chunk(score_chunk, (jnp.full((8, DSA_Q), inf, F32), jnp.full((8, DSA_Q), -inf, F32)))
    few = t_lane < k_sel
    lo = jnp.where(few, -inf, sub_reduce(mn, jnp.min))
    hi = jnp.where(few, -inf, sub_reduce(mx, jnp.max))

    def count(pred):
        def body(c, cnt):
            return cnt + _fold(pred(sc_ref[rows(c), :], c).astype(I32), jnp.add)
        return sub_reduce(each_chunk(body, jnp.zeros((8, DSA_Q), I32)), jnp.sum)

    def halve(_, bounds):
        lo, hi, n_lo, n_hi = bounds
        mid = 0.5 * lo + 0.5 * hi
        cnt = count(lambda x, c: x >= mid)
        active = lo < hi
        up = active & (cnt >= k_sel)
        down = active & (cnt <= k_sel)
        return (jnp.where(up, mid, lo), jnp.where(down, mid, hi),
                jnp.where(up, cnt, n_lo), jnp.where(down, cnt, n_hi))

    bounds = lax.fori_loop(0, DSA_HALVINGS, halve, (lo, hi, t_lane + 1, jnp.zeros((1, DSA_Q), I32)))

    def snap(bounds):
        lo, hi, n_lo, n_hi = bounds
        mid = 0.5 * lo + 0.5 * hi
        mid = jnp.where(mid > lo, mid, hi)

        def body(c, carry):
            cnt, above, below = carry
            x = sc_ref[rows(c), :]
            ge = x >= mid
            return (cnt + _fold(ge.astype(I32), jnp.add),
                    jnp.minimum(above, _fold(jnp.where(ge, x, inf), jnp.minimum)),
                    jnp.maximum(below, _fold(jnp.where(ge, -inf, x), jnp.maximum)))

        cnt, above, below = each_chunk(body, (jnp.zeros((8, DSA_Q), I32),
                                              jnp.full((8, DSA_Q), inf, F32),
                                              jnp.full((8, DSA_Q), -inf, F32)))
        cnt = sub_reduce(cnt, jnp.sum)
        above = sub_reduce(above, jnp.min)
        below = sub_reduce(below, jnp.max)
        active = lo < hi
        enough = cnt >= k_sel
        up = active & enough
        down = active & (cnt <= k_sel)
        return (jnp.where(up, above, lo), jnp.where(down, jnp.where(enough, above, below), hi),
                jnp.where(up, cnt, n_lo), jnp.where(down, cnt, n_hi))

    def snaps(bounds):
        for _ in range(snap_unroll):
            bounds = snap(bounds)
        return bounds

    def unsettled(bounds):
        return jnp.max((bounds[0] < bounds[1]).astype(I32)) > 0

    tau, _, n_ge, n_gt = lax.while_loop(unsettled, snaps, bounds)

    need = k_sel - n_gt
    n_idx_bits = seq.bit_length() - 1
    assert 1 << n_idx_bits == seq

    def tie_break():
        def idx_step(it, jp):
            cand = jp + lax.shift_left(jnp.int32(1), n_idx_bits - 1 - it)
            below = count(lambda x, c: (x == tau) & ((c * BLK + row) < cand))
            return jnp.where(below < need, cand, jp)
        last = lax.fori_loop(0, n_idx_bits, idx_step, jnp.zeros((1, DSA_Q), I32))
        return jnp.where(tied, last, seq)

    tied = (n_ge > k_sel) & jnp.logical_not(few)
    surplus = jnp.max(tied.astype(I32)) > 0
    tie_last = lax.cond(surplus, tie_break, lambda: jnp.full((1, DSA_Q), seq, I32))

    slopes = [slopes_ref[h] for h in range(N_HEADS_B)]

    def qk_chunk(c, ms):
        kb = kk_ref[rows(c), 0:HEAD_DIM]
        ss = [_dot(kb, tq_ref[T_QB + h * HEAD_DIM:T_QB + (h + 1) * HEAD_DIM, :])
              for h in range(N_HEADS_B)]
        s_idx = c * BLK + row
        rel = (s_idx - t_lane).astype(F32)
        x = sc_ref[rows(c), :]
        tie = jnp.where(x == tau, jnp.where(s_idx <= tie_last, 0.0, NEG), NEG)
        sb = jnp.where(s_idx <= t_lane, jnp.where(x > tau, 0.0, tie), NEG)
        out = []
        for h in range(N_HEADS_B):
            s = ss[h] + (slopes[h] * rel + sb)
            s_ref[h, rows(c), :] = s
            out.append(jnp.maximum(ms[h], _fold(s, jnp.maximum)))
        return tuple(out)

    ms = each_chunk(qk_chunk, tuple(jnp.full((8, DSA_Q), NEG, F32) for _ in range(N_HEADS_B)))
    ms = [sub_reduce(m, jnp.max) for m in ms]

    def exp_chunk(c, ls):
        out = []
        for h in range(N_HEADS_B):
            p = jnp.exp(s_ref[h, rows(c), :] - ms[h])
            p_ref[h, rows(c), :] = p.astype(BF16)
            out.append(ls[h] + _fold(p, jnp.add))
        return tuple(out)

    ls = each_chunk(exp_chunk, tuple(jnp.zeros((8, DSA_Q), F32) for _ in range(N_HEADS_B)))

    acc_ref[...] = jnp.zeros((W_B, DSA_Q), F32)

    def pv_pairs(first_pair, n, carry):
        pvs = []
        for u in range(n):
            sl = pl.ds(pl.multiple_of((first_pair + u) * 2 * BLK, 2 * BLK), 2 * BLK)
            vt = vt_ref[:, sl]
            pvs.append([_dot(vt, p_ref[h, sl, :]) for h in range(N_HEADS_B)])
        for h in range(N_HEADS_B):
            acc_ref[h * HEAD_DIM:(h + 1) * HEAD_DIM, :] += functools.reduce(
                lambda a, b: a + b, [pv[h] for pv in pvs])
        return carry

    lax.fori_loop(0, n_pairs // 2, lambda i, c: pv_pairs(2 * i, 2, c), 0)
    lax.fori_loop(0, n_pairs % 2, lambda i, c: pv_pairs(n_pairs - 1, 1, c), 0)
    l_all = jnp.concatenate(
        [jnp.broadcast_to(sub_reduce(ls[h], jnp.sum), (HEAD_DIM, DSA_Q)) for h in range(N_HEADS_B)],
        axis=0)
    o_ref[...] = (acc_ref[...] / l_all).T


def _dsa(slopes, proj_r, proj_t, w_t):
    B, T, _ = proj_r.shape
    k_sel = min(TOPK_MAX, T // 4)
    return pl.pallas_call(
        functools.partial(_dsa_kernel, seq=T, k_sel=k_sel, snap_unroll=4),
        grid=(B, T // DSA_Q),
        in_specs=[
            pl.BlockSpec(memory_space=pltpu.SMEM),
            pl.BlockSpec((None, T, 2 * LANES), lambda b, q: (b, 0, R_KB // (2 * LANES))),
            pl.BlockSpec((None, T_VB, DSA_Q), lambda b, q: (b, 0, q)),
            pl.BlockSpec((None, HEAD_DIM, T), lambda b, q: (b, T_VB // HEAD_DIM, 0)),
            pl.BlockSpec((None, WI_ROWS, DSA_Q), lambda b, q: (b, 0, q)),
        ],
        out_specs=pl.BlockSpec((None, DSA_Q, W_B), lambda b, q: (b, q, 0)),
        out_shape=jax.ShapeDtypeStruct((B, T, W_B), F32),
        scratch_shapes=[pltpu.VMEM((T, DSA_Q), F32),
                        pltpu.VMEM((N_HEADS_B, T, DSA_Q), F32),
                        pltpu.VMEM((N_HEADS_B, T, DSA_Q), BF16),
                        pltpu.VMEM((W_B, DSA_Q), F32)],
        compiler_params=pltpu.CompilerParams(
            dimension_semantics=("parallel", "arbitrary"), vmem_limit_bytes=VMEM_LIMIT),
        name="dsa",
    )(slopes, proj_r, proj_t, proj_t, w_t)


def _mem_kv_kernel(mem_ref, g_ref, w_ref, kv_ref):
    kv_ref[0] = _dot(_rms(mem_ref[0], g_ref[...]).astype(BF16), w_ref[...]).astype(BF16)


def _mem_kv(mem, g, w):
    B, M, D = mem.shape
    return pl.pallas_call(
        _mem_kv_kernel,
        grid=(B,),
        in_specs=[
            pl.BlockSpec((1, M, D), lambda b: (b, 0, 0)),
            pl.BlockSpec((1, D), lambda b: (0, 0)),
            pl.BlockSpec(w.shape, lambda b: (0, 0)),
        ],
        out_specs=pl.BlockSpec((1, M, 2 * W_M), lambda b: (b, 0, 0)),
        out_shape=jax.ShapeDtypeStruct((B, M, 2 * W_M), BF16),
        compiler_params=pltpu.CompilerParams(
            dimension_semantics=("parallel",), vmem_limit_bytes=VMEM_LIMIT),
        name="mem_kv",
    )(mem, g, w)


def _out_kernel(x_ref, oa_ref, ob_ref, gate_ref, qm_ref, kv_ref, wo_ref, g_ref, y_ref):
    gate = gate_ref[...].astype(F32)
    sg = gate * (1.0 / (1.0 + jnp.exp(-gate)))

    om = []
    for h in range(N_HEADS_MEM):
        lo = h * HEAD_DIM
        s = _dot_nt(qm_ref[:, lo:lo + HEAD_DIM], kv_ref[:, lo:lo + HEAD_DIM])
        m = jnp.max(s, axis=1, keepdims=True)
        p = jnp.exp(s - m)
        l = jnp.sum(p, axis=1, keepdims=True)
        om.append(_dot(p.astype(BF16), kv_ref[:, W_M + lo:W_M + lo + HEAD_DIM]) / l)
    om = jnp.concatenate(om, axis=1)

    mix = (oa_ref[...] * sg[:, :W_A]).astype(BF16)
    y = _dot(mix, wo_ref[0:W_A, :])
    mix = (ob_ref[...] * sg[:, W_A:W_A + W_B]).astype(BF16)
    y = y + _dot(mix, wo_ref[W_A:W_A + W_B, :])
    mix = (om * sg[:, W_A + W_B:]).astype(BF16)
    y = y + _dot(mix, wo_ref[W_A + W_B:, :])
    y_ref[...] = _rms(x_ref[...] + y, g_ref[...])


def _out(x, o_a, o_b, proj_r, kv_m, w_out, g, tm=256):
    B, T, D = x.shape
    M = kv_m.shape[1]
    row = lambda b, i: (b, i, 0)
    return pl.pallas_call(
        _out_kernel,
        grid=(B, T // tm),
        in_specs=[
            pl.BlockSpec((None, tm, D), row),
            pl.BlockSpec((None, tm, W_A), row),
            pl.BlockSpec((None, tm, W_B), row),
            pl.BlockSpec((None, tm, MIX_WIDTH), lambda b, i: (b, i, R_GATE // MIX_WIDTH)),
            pl.BlockSpec((None, tm, W_M), lambda b, i: (b, i, R_QM // W_M)),
            pl.BlockSpec((None, M, 2 * W_M), lambda b, i: (b, 0, 0)),
            pl.BlockSpec(w_out.shape, lambda b, i: (0, 0)),
            pl.BlockSpec((1, D), lambda b, i: (0, 0)),
        ],
        out_specs=pl.BlockSpec((None, tm, D), row),
        out_shape=jax.ShapeDtypeStruct((B, T, D), F32),
        compiler_params=pltpu.CompilerParams(
            dimension_semantics=("parallel", "parallel"), vmem_limit_bytes=VMEM_LIMIT),
        name="out",
    )(x, o_a, o_b, proj_r, proj_r, kv_m, w_out, g)


def _alibi_slopes(n):
    return 2.0 ** (-8.0 * jnp.arange(1, n + 1, dtype=F32) / n)


def _split_weights(w):
    bounds = [0]
    for s in SPLIT_SIZES:
        bounds.append(bounds[-1] + s)
    q_a, k_a, v_a, q_b, k_b, v_b, q_m, gate, q_i, k_i, w_i = (
        w[:, bounds[i]:bounds[i + 1]] for i in range(len(SPLIT_SIZES)))
    scale = HEAD_DIM ** -0.5
    idx_scale = IDX_DIM ** -0.5
    zeros = jnp.zeros((w.shape[0], LANES - HEAD_DIM), w.dtype)
    wa = jnp.concatenate([q_a * scale, k_a, v_a], axis=1)
    wr = jnp.concatenate([gate, q_m * scale, k_b, zeros, k_i, zeros], axis=1)
    wt = jnp.concatenate([q_i * idx_scale, q_b * scale, v_b], axis=1).T
    ww = jnp.concatenate([w_i.T, jnp.zeros((WI_ROWS - N_IDX_HEADS, w.shape[0]), w.dtype)], axis=0)
    return wa.astype(BF16), wr.astype(BF16), wt.astype(BF16), ww.astype(BF16)


def kernel(x, mem, g_in, g_mem, w_in, w_mem_kv, w_out, g_final):
    assert g_in.shape[0] == 1, "single-layer block: the final RMSNorm is fused into the output kernel"
    wa, wr, wt, ww = _split_weights(w_in[0])
    proj_a, proj_r, proj_t, w_t = _proj(x, g_in, wa, wr, wt, ww)
    o_a = _attn_a(_alibi_slopes(N_HEADS_A), proj_a)
    o_b = _dsa(_alibi_slopes(N_HEADS_B), proj_r, proj_t, w_t)
    kv_m = _mem_kv(mem, g_mem, w_mem_kv[0].astype(BF16))
    return _out(x, o_a, o_b, proj_r, kv_m, w_out[0].astype(BF16), g_final[None, :])
```

```python
import functools
import math

import jax
import jax.numpy as jnp
from jax import lax
from jax.experimental import pallas as pl
from jax.experimental.pallas import tpu as pltpu

F32 = jnp.float32
BF16 = jnp.bfloat16
I32 = jnp.int32

D_MODEL = 1024
HEAD_DIM = 64
N_HEADS_A = 8
N_HEADS_B = 4
N_HEADS_MEM = 4
W_A = N_HEADS_A * HEAD_DIM
W_B = N_HEADS_B * HEAD_DIM
W_M = N_HEADS_MEM * HEAD_DIM
MIX_WIDTH = W_A + W_B + W_M
DILATIONS = (1, 4, 16)
BAND = 128
N_IDX_HEADS = 8
IDX_DIM = 64
TOPK_MAX = 256
RMS_EPS = 1e-6
SPLIT_SIZES = (W_A, W_A, W_A, W_B, HEAD_DIM, HEAD_DIM, W_M, MIX_WIDTH,
               N_IDX_HEADS * IDX_DIM, IDX_DIM, N_IDX_HEADS)

LANES = 128
BLK = 128
NEG = -1e30
VMEM_LIMIT = 48 * 1024 * 1024
ATTN_A_VMEM_LIMIT = 56 * 1024 * 1024

R_GATE = 0
R_QM = MIX_WIDTH
R_KB = R_QM + W_M
R_KI = R_KB + LANES
R_COLS = R_KI + LANES
T_QI = 0
T_QB = N_IDX_HEADS * IDX_DIM
T_VB = T_QB + W_B
T_ROWS = T_VB + HEAD_DIM
WI_ROWS = 16
DSA_Q = 256
DSA_HALVINGS = 12
POS_SPLIT = 64


def _dot(a, b):
    return jnp.dot(a, b, preferred_element_type=F32)


def _dot_nt(a, b):
    return lax.dot_general(a, b, (((1,), (1,)), ((), ())), preferred_element_type=F32)


def _rms(x, g):
    return x * lax.rsqrt(jnp.mean(x * x, axis=-1, keepdims=True) + RMS_EPS) * g


def _proj_kernel(x_ref, g_ref, wa_ref, wr_ref, wt_ref, ww_ref, oa_ref, or_ref, ot_ref, ow_ref):
    hb = _rms(x_ref[0], g_ref[...]).astype(BF16)
    oa_ref[0] = _dot(hb, wa_ref[...])
    or_ref[0] = _dot(hb, wr_ref[...]).astype(BF16)
    ot_ref[0] = _dot_nt(wt_ref[...], hb).astype(BF16)
    ow_ref[0] = _dot_nt(ww_ref[...], hb)


def _proj(x, g, wa, wr, wt, ww, tm=512):
    B, T, D = x.shape
    const = lambda b, i: (0, 0)
    return pl.pallas_call(
        _proj_kernel,
        grid=(B, T // tm),
        in_specs=[
            pl.BlockSpec((1, tm, D), lambda b, i: (b, i, 0)),
            pl.BlockSpec((1, D), const),
            pl.BlockSpec(wa.shape, const),
            pl.BlockSpec(wr.shape, const),
            pl.BlockSpec(wt.shape, const),
            pl.BlockSpec(ww.shape, const),
        ],
        out_specs=[
            pl.BlockSpec((1, tm, 3 * W_A), lambda b, i: (b, i, 0)),
            pl.BlockSpec((1, tm, R_COLS), lambda b, i: (b, i, 0)),
            pl.BlockSpec((1, T_ROWS, tm), lambda b, i: (b, 0, i)),
            pl.BlockSpec((1, WI_ROWS, tm), lambda b, i: (b, 0, i)),
        ],
        out_shape=[
            jax.ShapeDtypeStruct((B, T, 3 * W_A), F32),
            jax.ShapeDtypeStruct((B, T, R_COLS), BF16),
            jax.ShapeDtypeStruct((B, T_ROWS, T), BF16),
            jax.ShapeDtypeStruct((B, WI_ROWS, T), F32),
        ],
        compiler_params=pltpu.CompilerParams(
            dimension_semantics=("parallel", "parallel"), vmem_limit_bytes=VMEM_LIMIT),
        name="proj",
    )(x, g, wa, wr, wt, ww)


def _attn_a_kernel(slopes_ref, q_ref, k_ref, v_ref, o_ref, acc_ref, m_ref, bias_ref, stage_ref,
                   s_ref, p_ref, cm_ref, *, seq, unroll):
    hp = pl.program_id(1)
    lane = lax.broadcasted_iota(I32, (1, LANES), 1)
    head_lanes = (lane < HEAD_DIM, lane >= HEAD_DIM)
    row = lax.broadcasted_iota(I32, (BLK, BLK), 0)
    col = lax.broadcasted_iota(I32, (BLK, BLK), 1)
    d_cur = (row - col).astype(F32)
    d_prev = d_cur + float(BAND)
    n_all = seq // BLK
    n_pat = len(DILATIONS)

    def n_blocks(p):
        return n_all // DILATIONS[p]

    def n_keys(p):
        return BLK if n_blocks(p) == 1 else 2 * BLK

    def staged(g):
        return pl.ds(pl.multiple_of((g + 1) * BLK, BLK), BLK)

    def keys(p, g):
        return pl.ds(pl.multiple_of((g + 2) * BLK - n_keys(p), BLK), n_keys(p))

    def pitch(dil):
        return seq // dil + 1 if seq // dil == BLK and dil > 1 else None

    def put(ref, h, p, g, val):
        dil = DILATIONS[p]
        start = g // n_blocks(p) + (g % n_blocks(p)) * (dil * BLK)
        if dil == 1:
            ref[h, p, pl.ds(pl.multiple_of(start, BLK), BLK), :] = val
        elif pitch(dil):
            ref[h, p, pl.ds(start * pitch(dil), BLK), :] = val
        else:
            ref[h, p, pl.ds(start, BLK, stride=dil), :] = val

    def stage(p):
        dil = DILATIONS[p]
        n = seq // dil
        src_dil = DILATIONS[p - 1] if p >= 2 else 1
        step = dil // src_dil
        srcs = (cm_ref.at[0], cm_ref.at[1], cm_ref.at[2]) if p >= 2 else (q_ref, k_ref, v_ref)
        keep_copy = p + 1 < n_pat and p >= 1

        def body(g):
            first = g * BLK
            r, i0 = first // n, first % n
            start = (r % src_dil) * (seq // src_dil) + r // src_dil + step * i0
            sl = pl.ds(start, BLK) if step == 1 else pl.ds(start, BLK, stride=step)
            q, k, v = (src[sl, :] for src in srcs)
            if keep_copy:
                for a, val in enumerate((q, k, v)):
                    cm_ref[a, pl.ds(pl.multiple_of(first, BLK), BLK), :] = val
            for h in range(2):
                stage_ref[p, h, staged(g), :] = jnp.where(head_lanes[h], q, 0.0).astype(BF16)
                stage_ref[p, 3 + h, staged(g), :] = jnp.where(head_lanes[h], v, 1.0).astype(BF16)
            stage_ref[p, 2, staged(g), :] = k.astype(BF16)
        return body

    def scores(p):
        def body(g):
            first = jnp.where(jnp.asarray(g, I32) % n_blocks(p) == 0, 1, 0)
            kw = stage_ref[p, 2, keys(p, g), :]
            for h in range(2):
                s = _dot_nt(stage_ref[p, h, staged(g), :], kw)
                s_ref[p % 2, h, g, :, :n_keys(p)] = s + bias_ref[p, h, first, :, 2 * BLK - n_keys(p):]
        return body

    def softmax(p):
        def body(g):
            for h in range(2):
                s = s_ref[p % 2, h, g, :, :n_keys(p)]
                m = jnp.max(s, axis=1, keepdims=True)
                p_ref[p % 2, h, g, :, :n_keys(p)] = jnp.exp(s - m).astype(BF16)
                put(m_ref, h, p, g, jnp.broadcast_to(m, (BLK, LANES)))
        return body

    def values(p):
        def body(g):
            for h in range(2):
                put(acc_ref, h, p, g,
                    _dot(p_ref[p % 2, h, g, :, :n_keys(p)], stage_ref[p, 3 + h, keys(p, g), :]))
        return body

    def run(*bodies, unroll=unroll):
        def step(g, carry):
            for body in bodies:
                body(g)
            return carry
        lax.fori_loop(0, n_all, step, 0, unroll=unroll)

    for p, dil in enumerate(DILATIONS):
        stage_ref[p, :, 0:BLK, :] = jnp.zeros((5, BLK, LANES), BF16)
        for h in range(2):
            sd = slopes_ref[hp * 2 + h] * float(dil)
            cur = jnp.where(row >= col, -sd * d_cur, NEG)
            bias_ref[p, h, 0, :, BLK:] = cur
            bias_ref[p, h, 1, :, BLK:] = cur
            if n_keys(p) > BLK:
                bias_ref[p, h, 0, :, :BLK] = jnp.where(col >= row, -sd * d_prev, NEG)
                bias_ref[p, h, 1, :, :BLK] = jnp.full((BLK, BLK), NEG, F32)

    assert n_pat == 3
    run(stage(0), unroll=2)
    run(scores(0), stage(1))
    run(softmax(0), scores(1))
    run(values(0), softmax(1), stage(2))
    run(values(1), scores(2))
    run(softmax(2))
    run(values(2))

    rows_per_step = 2 * BLK

    def natural_rows(ref, h, p, i):
        dil = DILATIONS[p]
        if not pitch(dil):
            return ref[h, p, pl.ds(pl.multiple_of(i * rows_per_step, rows_per_step), rows_per_step), :]
        per = rows_per_step // dil
        return jnp.concatenate(
            [ref[h, p, pl.ds(i * per + u, dil, stride=pitch(dil)), :] for u in range(per)], axis=0)

    def merge(i, carry):
        sl = pl.ds(pl.multiple_of(i * rows_per_step, rows_per_step), rows_per_step)
        nums = []
        for h in range(2):
            ms = [natural_rows(m_ref, h, p, i) for p in range(n_pat)]
            mx = functools.reduce(jnp.maximum, ms)
            nums.append(sum(jnp.exp(ms[p] - mx) * natural_rows(acc_ref, h, p, i) for p in range(n_pat)))
        acc = jnp.where(head_lanes[0], nums[0], nums[1])
        den = pltpu.roll(jnp.where(head_lanes[0], nums[1], nums[0]), HEAD_DIM, axis=1)
        o_ref[sl, :] = acc / den
        return carry

    lax.fori_loop(0, seq // rows_per_step, merge, 0)


def _attn_a(slopes, proj_a, unroll=8):
    B, T, _ = proj_a.shape
    n_pairs = N_HEADS_A // 2
    n_pat = len(DILATIONS)

    def spec(off):
        return pl.BlockSpec((None, T, LANES), lambda b, h: (b, 0, off + h))

    return pl.pallas_call(
        functools.partial(_attn_a_kernel, seq=T, unroll=unroll),
        grid=(B, n_pairs),
        in_specs=[pl.BlockSpec(memory_space=pltpu.SMEM), spec(0), spec(n_pairs), spec(2 * n_pairs)],
        out_specs=pl.BlockSpec((None, T, LANES), lambda b, h: (b, 0, h)),
        out_shape=jax.ShapeDtypeStruct((B, T, W_A), F32),
        scratch_shapes=[pltpu.VMEM((2, n_pat, T + max(DILATIONS), LANES), F32),
                        pltpu.VMEM((2, n_pat, T + max(DILATIONS), LANES), F32),
                        pltpu.VMEM((n_pat, 2, 2, BLK, 2 * BLK), F32),
                        pltpu.VMEM((n_pat, 5, T + BLK, LANES), BF16),
                        pltpu.VMEM((2, 2, T // BLK, BLK, 2 * BLK), F32),
                        pltpu.VMEM((2, 2, T // BLK, BLK, 2 * BLK), BF16),
                        pltpu.VMEM((3, T, LANES), F32)],
        compiler_params=pltpu.CompilerParams(
            dimension_semantics=("parallel", "parallel"), vmem_limit_bytes=ATTN_A_VMEM_LIMIT),
        name="attn_a",
    )(slopes, proj_a, proj_a, proj_a)


def _fold(x, op):
    return functools.reduce(op, [x[r:r + 8] for r in range(0, x.shape[0], 8)])


def _dsa_kernel(kk_ref, tq_ref, vt_ref, wt_ref, o_ref, sc_ref, s_ref, p_ref, acc_ref, kpos_ref,
                *, seq, k_sel, snap_unroll, slopes):
    j = pl.program_id(1)
    n_pairs = j + 1
    t_lane = j * DSA_Q + lax.broadcasted_iota(I32, (1, DSA_Q), 1)
    row = lax.broadcasted_iota(I32, (BLK, DSA_Q), 0)
    ws = wt_ref[0:N_IDX_HEADS, :] * (N_IDX_HEADS ** -0.5)
    inf = float("inf")

    def rows(c):
        return pl.ds(pl.multiple_of(c * BLK, BLK), BLK)

    def each_chunk(fn, init):
        def group(first, n, carry):
            for u in range(n):
                carry = fn(first + u, carry)
            return carry
        carry = lax.fori_loop(0, n_pairs // 2, lambda i, c: group(4 * i, 4, c), init)
        return lax.fori_loop(0, n_pairs % 2, lambda i, c: group(4 * (n_pairs // 2), 2, c), carry)

    def sub_reduce(x, op):
        return op(x, axis=0, keepdims=True)

    def score_chunk(c, carry):
        mn, mx = carry
        ki = kk_ref[rows(c), R_KI - R_KB:R_KI - R_KB + IDX_DIM]
        lgs = [_dot(ki, tq_ref[T_QI + h * IDX_DIM:T_QI + (h + 1) * IDX_DIM, :])
               for h in range(N_IDX_HEADS)]
        sc = functools.reduce(
            lambda a, b: a + b, [jnp.maximum(lg, 0.0) * ws[h:h + 1, :] for h, lg in enumerate(lgs)])
        causal = (c * BLK + row) <= t_lane
        sc_ref[rows(c), :] = jnp.where(causal, sc, -inf)
        mn = jnp.minimum(mn, _fold(jnp.where(causal, sc, inf), jnp.minimum))
        mx = jnp.maximum(mx, _fold(jnp.where(causal, sc, -inf), jnp.maximum))
        return mn, mx

    mn, mx = each_chunk(score_chunk, (jnp.full((8, DSA_Q), inf, F32), jnp.full((8, DSA_Q), -inf, F32)))
    few = t_lane < k_sel
    lo = jnp.where(few, -inf, sub_reduce(mn, jnp.min))
    hi = jnp.where(few, -inf, sub_reduce(mx, jnp.max))

    def count(pred):
        def body(c, cnt):
            return cnt + _fold(pred(sc_ref[rows(c), :], c).astype(I32), jnp.add)
        return sub_reduce(each_chunk(body, jnp.zeros((8, DSA_Q), I32)), jnp.sum)

    def halve(_, bounds):
        lo, hi, n_lo, n_hi = bounds
        mid = 0.5 * lo + 0.5 * hi
        cnt = count(lambda x, c: x >= mid)
        active = lo < hi
        up = active & (cnt >= k_sel)
        down = active & (cnt <= k_sel)
        return (jnp.where(up, mid, lo), jnp.where(down, mid, hi),
                jnp.where(up, cnt, n_lo), jnp.where(down, cnt, n_hi))

    bounds = lax.fori_loop(0, DSA_HALVINGS, halve, (lo, hi, t_lane + 1, jnp.zeros((1, DSA_Q), I32)))

    def snap(bounds):
        lo, hi, n_lo, n_hi = bounds
        mid = 0.5 * lo + 0.5 * hi
        mid = jnp.where(mid > lo, mid, hi)

        def body(c, carry):
            cnt, above, below = carry
            x = sc_ref[rows(c), :]
            ge = x >= mid
            return (cnt + _fold(ge.astype(I32), jnp.add),
                    jnp.minimum(above, _fold(jnp.where(ge, x, inf), jnp.minimum)),
                    jnp.maximum(below, _fold(jnp.where(ge, -inf, x), jnp.maximum)))

        cnt, above, below = each_chunk(body, (jnp.zeros((8, DSA_Q), I32),
                                              jnp.full((8, DSA_Q), inf, F32),
                                              jnp.full((8, DSA_Q), -inf, F32)))
        cnt = sub_reduce(cnt, jnp.sum)
        above = sub_reduce(above, jnp.min)
        below = sub_reduce(below, jnp.max)
        active = lo < hi
        enough = cnt >= k_sel
        up = active & enough
        down = active & (cnt <= k_sel)
        return (jnp.where(up, above, lo), jnp.where(down, jnp.where(enough, above, below), hi),
                jnp.where(up, cnt, n_lo), jnp.where(down, cnt, n_hi))

    def snaps(bounds):
        for _ in range(snap_unroll):
            bounds = snap(bounds)
        return bounds

    def unsettled(bounds):
        return jnp.max((bounds[0] < bounds[1]).astype(I32)) > 0

    tau, _, n_ge, n_gt = lax.while_loop(unsettled, snaps, bounds)

    need = k_sel - n_gt
    n_idx_bits = seq.bit_length() - 1
    assert 1 << n_idx_bits == seq

    def tie_break():
        def idx_step(it, jp):
            cand = jp + lax.shift_left(jnp.int32(1), n_idx_bits - 1 - it)
            below = count(lambda x, c: (x == tau) & ((c * BLK + row) < cand))
            return jnp.where(below < need, cand, jp)
        last = lax.fori_loop(0, n_idx_bits, idx_step, jnp.zeros((1, DSA_Q), I32))
        return jnp.where(tied, last, seq)

    tied = (n_ge > k_sel) & jnp.logical_not(few)
    surplus = jnp.max(tied.astype(I32)) > 0
    tie_last = lax.cond(surplus, tie_break, lambda: jnp.full((1, DSA_Q), seq, I32))

    @pl.when(j == 0)
    def _():
        pos = lax.broadcasted_iota(I32, (seq, LANES), 0)
        ln = lax.broadcasted_iota(I32, (seq, LANES), 1) - HEAD_DIM
        feat = jnp.where(ln == 0, pos // POS_SPLIT,
                         jnp.where(ln == 1, pos % POS_SPLIT, jnp.where((ln == 2) | (ln == 3), 1, 0)))
        kpos_ref[...] = kk_ref[:, 0:LANES] + feat.astype(BF16)

    q_row = lax.broadcasted_iota(I32, (HEAD_DIM, DSA_Q), 0)
    t_hi = (t_lane // POS_SPLIT).astype(F32)
    t_lo = (t_lane % POS_SPLIT).astype(F32)
    q_aug = []
    for h in range(N_HEADS_B):
        feat = jnp.where(q_row == 0, POS_SPLIT * slopes[h],
                         jnp.where(q_row == 1, slopes[h],
                                   jnp.where(q_row == 2, -POS_SPLIT * slopes[h] * t_hi,
                                             jnp.where(q_row == 3, -slopes[h] * t_lo, 0.0))))
        q_aug.append(jnp.concatenate(
            [tq_ref[T_QB + h * HEAD_DIM:T_QB + (h + 1) * HEAD_DIM, :], feat.astype(BF16)], axis=0))

    def qk_chunk(c, ms):
        kb = kpos_ref[rows(c), :]
        s_idx = c * BLK + row
        x = sc_ref[rows(c), :]
        tie = jnp.where(x == tau, jnp.where(s_idx <= tie_last, 0.0, NEG), NEG)
        sb = jnp.where(s_idx <= t_lane, jnp.where(x > tau, 0.0, tie), NEG)
        out = []
        for h in range(N_HEADS_B):
            s = _dot(kb, q_aug[h]) + sb
            s_ref[h, rows(c), :] = s
            out.append(jnp.maximum(ms[h], _fold(s, jnp.maximum)))
        return tuple(out)

    ms = each_chunk(qk_chunk, tuple(jnp.full((8, DSA_Q), NEG, F32) for _ in range(N_HEADS_B)))
    ms = [sub_reduce(m, jnp.max) for m in ms]

    def exp_chunk(c, ls):
        out = []
        for h in range(N_HEADS_B):
            p = jnp.exp(s_ref[h, rows(c), :] - ms[h])
            p_ref[h, rows(c), :] = p.astype(BF16)
            out.append(ls[h] + _fold(p, jnp.add))
        return tuple(out)

    ls = each_chunk(exp_chunk, tuple(jnp.zeros((8, DSA_Q), F32) for _ in range(N_HEADS_B)))

    acc_ref[...] = jnp.zeros((W_B, DSA_Q), F32)

    def pv_pairs(first_pair, n, carry):
        pvs = []
        for u in range(n):
            sl = pl.ds(pl.multiple_of((first_pair + u) * 2 * BLK, 2 * BLK), 2 * BLK)
            vt = vt_ref[:, sl]
            pvs.append([_dot(vt, p_ref[h, sl, :]) for h in range(N_HEADS_B)])
        for h in range(N_HEADS_B):
            acc_ref[h * HEAD_DIM:(h + 1) * HEAD_DIM, :] += functools.reduce(
                lambda a, b: a + b, [pv[h] for pv in pvs])
        return carry

    lax.fori_loop(0, n_pairs // 2, lambda i, c: pv_pairs(2 * i, 2, c), 0)
    lax.fori_loop(0, n_pairs % 2, lambda i, c: pv_pairs(n_pairs - 1, 1, c), 0)
    l_all = jnp.concatenate(
        [jnp.broadcast_to(sub_reduce(ls[h], jnp.sum), (HEAD_DIM, DSA_Q)) for h in range(N_HEADS_B)],
        axis=0)
    o_ref[...] = (acc_ref[...] / l_all).T


def _dsa(proj_r, proj_t, w_t):
    B, T, _ = proj_r.shape
    k_sel = min(TOPK_MAX, T // 4)
    slopes = _static_alibi_slopes(N_HEADS_B)
    assert all(math.frexp(s)[0] == 0.5 for s in slopes) and T // POS_SPLIT <= 256 and POS_SPLIT <= 256
    return pl.pallas_call(
        functools.partial(_dsa_kernel, seq=T, k_sel=k_sel, snap_unroll=4, slopes=slopes),
        grid=(B, T // DSA_Q),
        in_specs=[
            pl.BlockSpec((None, T, 2 * LANES), lambda b, q: (b, 0, R_KB // (2 * LANES))),
            pl.BlockSpec((None, T_VB, DSA_Q), lambda b, q: (b, 0, q)),
            pl.BlockSpec((None, HEAD_DIM, T), lambda b, q: (b, T_VB // HEAD_DIM, 0)),
            pl.BlockSpec((None, WI_ROWS, DSA_Q), lambda b, q: (b, 0, q)),
        ],
        out_specs=pl.BlockSpec((None, DSA_Q, W_B), lambda b, q: (b, q, 0)),
        out_shape=jax.ShapeDtypeStruct((B, T, W_B), F32),
        scratch_shapes=[pltpu.VMEM((T, DSA_Q), F32),
                        pltpu.VMEM((N_HEADS_B, T, DSA_Q), F32),
                        pltpu.VMEM((N_HEADS_B, T, DSA_Q), BF16),
                        pltpu.VMEM((W_B, DSA_Q), F32),
                        pltpu.VMEM((T, LANES), BF16)],
        compiler_params=pltpu.CompilerParams(
            dimension_semantics=("parallel", "arbitrary"), vmem_limit_bytes=VMEM_LIMIT),
        name="dsa",
    )(proj_r, proj_t, proj_t, w_t)


def _mem_kv_kernel(mem_ref, g_ref, w_ref, kv_ref):
    kv_ref[0] = _dot(_rms(mem_ref[0], g_ref[...]).astype(BF16), w_ref[...]).astype(BF16)


def _mem_kv(mem, g, w):
    B, M, D = mem.shape
    return pl.pallas_call(
        _mem_kv_kernel,
        grid=(B,),
        in_specs=[
            pl.BlockSpec((1, M, D), lambda b: (b, 0, 0)),
            pl.BlockSpec((1, D), lambda b: (0, 0)),
            pl.BlockSpec(w.shape, lambda b: (0, 0)),
        ],
        out_specs=pl.BlockSpec((1, M, 2 * W_M), lambda b: (b, 0, 0)),
        out_shape=jax.ShapeDtypeStruct((B, M, 2 * W_M), BF16),
        compiler_params=pltpu.CompilerParams(
            dimension_semantics=("parallel",), vmem_limit_bytes=VMEM_LIMIT),
        name="mem_kv",
    )(mem, g, w)


def _out_kernel(x_ref, oa_ref, ob_ref, gate_ref, qm_ref, kv_ref, wo_ref, g_ref, y_ref):
    gate = gate_ref[...].astype(F32)
    sg = gate * (1.0 / (1.0 + jnp.exp(-gate)))

    om = []
    for h in range(N_HEADS_MEM):
        lo = h * HEAD_DIM
        s = _dot_nt(qm_ref[:, lo:lo + HEAD_DIM], kv_ref[:, lo:lo + HEAD_DIM])
        m = jnp.max(s, axis=1, keepdims=True)
        p = jnp.exp(s - m)
        l = jnp.sum(p, axis=1, keepdims=True)
        om.append(_dot(p.astype(BF16), kv_ref[:, W_M + lo:W_M + lo + HEAD_DIM]) / l)
    om = jnp.concatenate(om, axis=1)

    mix = (oa_ref[...] * sg[:, :W_A]).astype(BF16)
    y = _dot(mix, wo_ref[0:W_A, :])
    mix = (ob_ref[...] * sg[:, W_A:W_A + W_B]).astype(BF16)
    y = y + _dot(mix, wo_ref[W_A:W_A + W_B, :])
    mix = (om * sg[:, W_A + W_B:]).astype(BF16)
    y = y + _dot(mix, wo_ref[W_A + W_B:, :])
    y_ref[...] = _rms(x_ref[...] + y, g_ref[...])


def _out(x, o_a, o_b, proj_r, kv_m, w_out, g, tm=256):
    B, T, D = x.shape
    M = kv_m.shape[1]
    row = lambda b, i: (b, i, 0)
    return pl.pallas_call(
        _out_kernel,
        grid=(B, T // tm),
        in_specs=[
            pl.BlockSpec((None, tm, D), row),
            pl.BlockSpec((None, tm, W_A), row),
            pl.BlockSpec((None, tm, W_B), row),
            pl.BlockSpec((None, tm, MIX_WIDTH), lambda b, i: (b, i, R_GATE // MIX_WIDTH)),
            pl.BlockSpec((None, tm, W_M), lambda b, i: (b, i, R_QM // W_M)),
            pl.BlockSpec((None, M, 2 * W_M), lambda b, i: (b, 0, 0)),
            pl.BlockSpec(w_out.shape, lambda b, i: (0, 0)),
            pl.BlockSpec((1, D), lambda b, i: (0, 0)),
        ],
        out_specs=pl.BlockSpec((None, tm, D), row),
        out_shape=jax.ShapeDtypeStruct((B, T, D), F32),
        compiler_params=pltpu.CompilerParams(
            dimension_semantics=("parallel", "parallel"), vmem_limit_bytes=VMEM_LIMIT),
        name="out",
    )(x, o_a, o_b, proj_r, proj_r, kv_m, w_out, g)


def _alibi_slopes(n):
    return 2.0 ** (-8.0 * jnp.arange(1, n + 1, dtype=F32) / n)


def _static_alibi_slopes(n):
    return tuple(2.0 ** (-8.0 * i / n) for i in range(1, n + 1))


def _split_weights(w):
    bounds = [0]
    for s in SPLIT_SIZES:
        bounds.append(bounds[-1] + s)
    q_a, k_a, v_a, q_b, k_b, v_b, q_m, gate, q_i, k_i, w_i = (
        w[:, bounds[i]:bounds[i + 1]] for i in range(len(SPLIT_SIZES)))
    scale = HEAD_DIM ** -0.5
    idx_scale = IDX_DIM ** -0.5
    zeros = jnp.zeros((w.shape[0], LANES - HEAD_DIM), w.dtype)
    wa = jnp.concatenate([q_a * scale, k_a, v_a], axis=1)
    wr = jnp.concatenate([gate, q_m * scale, k_b, zeros, k_i, zeros], axis=1)
    wt = jnp.concatenate([q_i * idx_scale, q_b * scale, v_b], axis=1).T
    ww = jnp.concatenate([w_i.T, jnp.zeros((WI_ROWS - N_IDX_HEADS, w.shape[0]), w.dtype)], axis=0)
    return wa.astype(BF16), wr.astype(BF16), wt.astype(BF16), ww.astype(BF16)


def kernel(x, mem, g_in, g_mem, w_in, w_mem_kv, w_out, g_final):
    assert g_in.shape[0] == 1, "single-layer block: the final RMSNorm is fused into the output kernel"
    wa, wr, wt, ww = _split_weights(w_in[0])
    proj_a, proj_r, proj_t, w_t = _proj(x, g_in, wa, wr, wt, ww)
    o_a = _attn_a(_alibi_slopes(N_HEADS_A), proj_a)
    o_b = _dsa(proj_r, proj_t, w_t)
    kv_m = _mem_kv(mem, g_mem, w_mem_kv[0].astype(BF16))
    return _out(x, o_a, o_b, proj_r, kv_m, w_out[0].astype(BF16), g_final[None, :])
```

```python
import functools
import math

import jax
import jax.numpy as jnp
from jax import lax
from jax.experimental import pallas as pl
from jax.experimental.pallas import tpu as pltpu

F32 = jnp.float32
BF16 = jnp.bfloat16
I32 = jnp.int32

D_MODEL = 1024
HEAD_DIM = 64
N_HEADS_A = 8
N_HEADS_B = 4
N_HEADS_MEM = 4
W_A = N_HEADS_A * HEAD_DIM
W_B = N_HEADS_B * HEAD_DIM
W_M = N_HEADS_MEM * HEAD_DIM
MIX_WIDTH = W_A + W_B + W_M
DILATIONS = (1, 4, 16)
BAND = 128
N_IDX_HEADS = 8
IDX_DIM = 64
TOPK_MAX = 256
RMS_EPS = 1e-6
SPLIT_SIZES = (W_A, W_A, W_A, W_B, HEAD_DIM, HEAD_DIM, W_M, MIX_WIDTH,
               N_IDX_HEADS * IDX_DIM, IDX_DIM, N_IDX_HEADS)

LANES = 128
BLK = 128
NEG = -1e30
VMEM_LIMIT = 48 * 1024 * 1024
ATTN_A_VMEM_LIMIT = 56 * 1024 * 1024

R_GATE = 0
R_QM = MIX_WIDTH
R_KB = R_QM + W_M
R_KI = R_KB + LANES
R_COLS = R_KI + LANES
T_QI = 0
T_QB = N_IDX_HEADS * IDX_DIM
T_VB = T_QB + W_B
T_ROWS = T_VB + HEAD_DIM
WI_ROWS = 16
DSA_Q = 256
DSA_HALVINGS = 12
POS_SPLIT = 64


def _dot(a, b):
    return jnp.dot(a, b, preferred_element_type=F32)


def _dot_nt(a, b):
    return lax.dot_general(a, b, (((1,), (1,)), ((), ())), preferred_element_type=F32)


def _rms(x, g):
    return x * lax.rsqrt(jnp.mean(x * x, axis=-1, keepdims=True) + RMS_EPS) * g


def _proj_kernel(x_ref, g_ref, wa_ref, wr_ref, wt_ref, ww_ref, oa_ref, or_ref, ot_ref, ow_ref):
    hb = _rms(x_ref[0], g_ref[...]).astype(BF16)
    oa_ref[0] = _dot(hb, wa_ref[...])
    or_ref[0] = _dot(hb, wr_ref[...]).astype(BF16)
    ot_ref[0] = _dot_nt(wt_ref[...], hb).astype(BF16)
    ow_ref[0] = _dot_nt(ww_ref[...], hb)


def _proj(x, g, wa, wr, wt, ww, tm=512):
    B, T, D = x.shape
    const = lambda b, i: (0, 0)
    return pl.pallas_call(
        _proj_kernel,
        grid=(B, T // tm),
        in_specs=[
            pl.BlockSpec((1, tm, D), lambda b, i: (b, i, 0)),
            pl.BlockSpec((1, D), const),
            pl.BlockSpec(wa.shape, const),
            pl.BlockSpec(wr.shape, const),
            pl.BlockSpec(wt.shape, const),
            pl.BlockSpec(ww.shape, const),
        ],
        out_specs=[
            pl.BlockSpec((1, tm, 3 * W_A), lambda b, i: (b, i, 0)),
            pl.BlockSpec((1, tm, R_COLS), lambda b, i: (b, i, 0)),
            pl.BlockSpec((1, T_ROWS, tm), lambda b, i: (b, 0, i)),
            pl.BlockSpec((1, WI_ROWS, tm), lambda b, i: (b, 0, i)),
        ],
        out_shape=[
            jax.ShapeDtypeStruct((B, T, 3 * W_A), F32),
            jax.ShapeDtypeStruct((B, T, R_COLS), BF16),
            jax.ShapeDtypeStruct((B, T_ROWS, T), BF16),
            jax.ShapeDtypeStruct((B, WI_ROWS, T), F32),
        ],
        compiler_params=pltpu.CompilerParams(
            dimension_semantics=("parallel", "parallel"), vmem_limit_bytes=VMEM_LIMIT),
        name="proj",
    )(x, g, wa, wr, wt, ww)


def _attn_a_kernel(slopes_ref, q_ref, k_ref, v_ref, o_ref, acc_ref, m_ref, bias_ref, stage_ref,
                   s_ref, p_ref, cm_ref, *, seq, unroll):
    hp = pl.program_id(1)
    lane = lax.broadcasted_iota(I32, (1, LANES), 1)
    head_lanes = (lane < HEAD_DIM, lane >= HEAD_DIM)
    row = lax.broadcasted_iota(I32, (BLK, BLK), 0)
    col = lax.broadcasted_iota(I32, (BLK, BLK), 1)
    d_cur = (row - col).astype(F32)
    d_prev = d_cur + float(BAND)
    n_all = seq // BLK
    n_pat = len(DILATIONS)

    def n_blocks(p):
        return n_all // DILATIONS[p]

    def n_keys(p):
        return BLK if n_blocks(p) == 1 else 2 * BLK

    def staged(g):
        return pl.ds(pl.multiple_of((g + 1) * BLK, BLK), BLK)

    def keys(p, g):
        return pl.ds(pl.multiple_of((g + 2) * BLK - n_keys(p), BLK), n_keys(p))

    def pitch(dil):
        return seq // dil + 1 if seq // dil == BLK and dil > 1 else None

    def put(ref, h, p, g, val):
        dil = DILATIONS[p]
        start = g // n_blocks(p) + (g % n_blocks(p)) * (dil * BLK)
        if dil == 1:
            ref[h, p, pl.ds(pl.multiple_of(start, BLK), BLK), :] = val
        elif pitch(dil):
            ref[h, p, pl.ds(start * pitch(dil), BLK), :] = val
        else:
            ref[h, p, pl.ds(start, BLK, stride=dil), :] = val

    def stage(p):
        dil = DILATIONS[p]
        n = seq // dil
        src_dil = DILATIONS[p - 1] if p >= 2 else 1
        step = dil // src_dil
        srcs = (cm_ref.at[0], cm_ref.at[1], cm_ref.at[2]) if p >= 2 else (q_ref, k_ref, v_ref)
        keep_copy = p + 1 < n_pat and p >= 1

        def body(g):
            first = g * BLK
            r, i0 = first // n, first % n
            start = (r % src_dil) * (seq // src_dil) + r // src_dil + step * i0
            sl = pl.ds(start, BLK) if step == 1 else pl.ds(start, BLK, stride=step)
            q, k, v = (src[sl, :] for src in srcs)
            if keep_copy:
                for a, val in enumerate((q, k, v)):
                    cm_ref[a, pl.ds(pl.multiple_of(first, BLK), BLK), :] = val
            for h in range(2):
                stage_ref[p, h, staged(g), :] = jnp.where(head_lanes[h], q, 0.0).astype(BF16)
                stage_ref[p, 3 + h, staged(g), :] = jnp.where(head_lanes[h], v, 1.0).astype(BF16)
            stage_ref[p, 2, staged(g), :] = k.astype(BF16)
        return body

    def scores(p):
        def body(g):
            first = jnp.where(jnp.asarray(g, I32) % n_blocks(p) == 0, 1, 0)
            kw = stage_ref[p, 2, keys(p, g), :]
            for h in range(2):
                s = _dot_nt(stage_ref[p, h, staged(g), :], kw)
                s_ref[p % 2, h, g, :, :n_keys(p)] = s + bias_ref[p, h, first, :, 2 * BLK - n_keys(p):]
        return body

    def softmax(p):
        def body(g):
            for h in range(2):
                s = s_ref[p % 2, h, g, :, :n_keys(p)]
                m = jnp.max(s, axis=1, keepdims=True)
                p_ref[p % 2, h, g, :, :n_keys(p)] = jnp.exp(s - m).astype(BF16)
                put(m_ref, h, p, g, jnp.broadcast_to(m, (BLK, LANES)))
        return body

    def values(p):
        def body(g):
            for h in range(2):
                put(acc_ref, h, p, g,
                    _dot(p_ref[p % 2, h, g, :, :n_keys(p)], stage_ref[p, 3 + h, keys(p, g), :]))
        return body

    def run(*bodies, unroll=unroll):
        def step(g, carry):
            for body in bodies:
                body(g)
            return carry
        lax.fori_loop(0, n_all, step, 0, unroll=unroll)

    for p, dil in enumerate(DILATIONS):
        stage_ref[p, :, 0:BLK, :] = jnp.zeros((5, BLK, LANES), BF16)
        for h in range(2):
            sd = slopes_ref[hp * 2 + h] * float(dil)
            cur = jnp.where(row >= col, -sd * d_cur, NEG)
            bias_ref[p, h, 0, :, BLK:] = cur
            bias_ref[p, h, 1, :, BLK:] = cur
            if n_keys(p) > BLK:
                bias_ref[p, h, 0, :, :BLK] = jnp.where(col >= row, -sd * d_prev, NEG)
                bias_ref[p, h, 1, :, :BLK] = jnp.full((BLK, BLK), NEG, F32)

    assert n_pat == 3
    run(stage(0), unroll=2)
    run(scores(0), stage(1))
    run(softmax(0), scores(1))
    run(values(0), softmax(1), stage(2))
    run(values(1), scores(2))
    run(softmax(2))
    run(values(2))

    rows_per_step = 2 * BLK

    def natural_rows(ref, h, p, i):
        dil = DILATIONS[p]
        if not pitch(dil):
            return ref[h, p, pl.ds(pl.multiple_of(i * rows_per_step, rows_per_step), rows_per_step), :]
        per = rows_per_step // dil
        return jnp.concatenate(
            [ref[h, p, pl.ds(i * per + u, dil, stride=pitch(dil)), :] for u in range(per)], axis=0)

    def merge(i, carry):
        sl = pl.ds(pl.multiple_of(i * rows_per_step, rows_per_step), rows_per_step)
        nums = []
        for h in range(2):
            ms = [natural_rows(m_ref, h, p, i) for p in range(n_pat)]
            mx = functools.reduce(jnp.maximum, ms)
            nums.append(sum(jnp.exp(ms[p] - mx) * natural_rows(acc_ref, h, p, i) for p in range(n_pat)))
        acc = jnp.where(head_lanes[0], nums[0], nums[1])
        den = pltpu.roll(jnp.where(head_lanes[0], nums[1], nums[0]), HEAD_DIM, axis=1)
        o_ref[sl, :] = acc / den
        return carry

    lax.fori_loop(0, seq // rows_per_step, merge, 0)


def _attn_a(slopes, proj_a, unroll=8):
    B, T, _ = proj_a.shape
    n_pairs = N_HEADS_A // 2
    n_pat = len(DILATIONS)

    def spec(off):
        return pl.BlockSpec((None, T, LANES), lambda b, h: (b, 0, off + h))

    return pl.pallas_call(
        functools.partial(_attn_a_kernel, seq=T, unroll=unroll),
        grid=(B, n_pairs),
        in_specs=[pl.BlockSpec(memory_space=pltpu.SMEM), spec(0), spec(n_pairs), spec(2 * n_pairs)],
        out_specs=pl.BlockSpec((None, T, LANES), lambda b, h: (b, 0, h)),
        out_shape=jax.ShapeDtypeStruct((B, T, W_A), F32),
        scratch_shapes=[pltpu.VMEM((2, n_pat, T + max(DILATIONS), LANES), F32),
                        pltpu.VMEM((2, n_pat, T + max(DILATIONS), LANES), F32),
                        pltpu.VMEM((n_pat, 2, 2, BLK, 2 * BLK), F32),
                        pltpu.VMEM((n_pat, 5, T + BLK, LANES), BF16),
                        pltpu.VMEM((2, 2, T // BLK, BLK, 2 * BLK), F32),
                        pltpu.VMEM((2, 2, T // BLK, BLK, 2 * BLK), BF16),
                        pltpu.VMEM((3, T, LANES), F32)],
        compiler_params=pltpu.CompilerParams(
            dimension_semantics=("parallel", "parallel"), vmem_limit_bytes=ATTN_A_VMEM_LIMIT),
        name="attn_a",
    )(slopes, proj_a, proj_a, proj_a)


def _fold(x, op):
    return functools.reduce(op, [x[r:r + 8] for r in range(0, x.shape[0], 8)])


def _dsa_kernel(kk_ref, tq_ref, vt_ref, wt_ref, o_ref, sc_ref, s_ref, p_ref, acc_ref, kpos_ref,
                *, seq, k_sel, snap_unroll, slopes):
    j = pl.program_id(1)
    n_pairs = j + 1
    t_lane = j * DSA_Q + lax.broadcasted_iota(I32, (1, DSA_Q), 1)
    row = lax.broadcasted_iota(I32, (BLK, DSA_Q), 0)
    ws = wt_ref[0:N_IDX_HEADS, :] * (N_IDX_HEADS ** -0.5)
    inf = float("inf")

    def rows(c):
        return pl.ds(pl.multiple_of(c * BLK, BLK), BLK)

    def each_chunk(fn, init):
        def group(first, n, carry):
            for u in range(n):
                carry = fn(first + u, carry)
            return carry
        carry = lax.fori_loop(0, n_pairs // 2, lambda i, c: group(4 * i, 4, c), init)
        return lax.fori_loop(0, n_pairs % 2, lambda i, c: group(4 * (n_pairs // 2), 2, c), carry)

    def sub_reduce(x, op):
        return op(x, axis=0, keepdims=True)

    def score_chunk(c, carry):
        mn, mx = carry
        ki = kk_ref[rows(c), R_KI - R_KB:R_KI - R_KB + IDX_DIM]
        lgs = [_dot(ki, tq_ref[T_QI + h * IDX_DIM:T_QI + (h + 1) * IDX_DIM, :])
               for h in range(N_IDX_HEADS)]
        sc = functools.reduce(
            lambda a, b: a + b, [jnp.maximum(lg, 0.0) * ws[h:h + 1, :] for h, lg in enumerate(lgs)])
        causal = (c * BLK + row) <= t_lane
        sc_ref[rows(c), :] = jnp.where(causal, sc, -inf)
        mn = jnp.minimum(mn, _fold(jnp.where(causal, sc, inf), jnp.minimum))
        mx = jnp.maximum(mx, _fold(jnp.where(causal, sc, -inf), jnp.maximum))
        return mn, mx

    mn, mx = each_chunk(score_chunk, (jnp.full((8, DSA_Q), inf, F32), jnp.full((8, DSA_Q), -inf, F32)))
    few = t_lane < k_sel
    lo = jnp.where(few, -inf, sub_reduce(mn, jnp.min))
    hi = jnp.where(few, -inf, sub_reduce(mx, jnp.max))

    def count(pred):
        def body(c, cnt):
            return cnt + _fold(pred(sc_ref[rows(c), :], c).astype(I32), jnp.add)
        return sub_reduce(each_chunk(body, jnp.zeros((8, DSA_Q), I32)), jnp.sum)

    def halve(_, bounds):
        lo, hi, n_lo, n_hi = bounds
        mid = 0.5 * lo + 0.5 * hi
        cnt = count(lambda x, c: x >= mid)
        active = lo < hi
        up = active & (cnt >= k_sel)
        down = active & (cnt <= k_sel)
        return (jnp.where(up, mid, lo), jnp.where(down, mid, hi),
                jnp.where(up, cnt, n_lo), jnp.where(down, cnt, n_hi))

    bounds = lax.fori_loop(0, DSA_HALVINGS, halve, (lo, hi, t_lane + 1, jnp.zeros((1, DSA_Q), I32)))

    def snap(bounds):
        lo, hi, n_lo, n_hi = bounds
        mid = 0.5 * lo + 0.5 * hi
        mid = jnp.where(mid > lo, mid, hi)

        def body(c, carry):
            cnt, above, below = carry
            x = sc_ref[rows(c), :]
            ge = x >= mid
            return (cnt + _fold(ge.astype(I32), jnp.add),
                    jnp.minimum(above, _fold(jnp.where(ge, x, inf), jnp.minimum)),
                    jnp.maximum(below, _fold(jnp.where(ge, -inf, x), jnp.maximum)))

        cnt, above, below = each_chunk(body, (jnp.zeros((8, DSA_Q), I32),
                                              jnp.full((8, DSA_Q), inf, F32),
                                              jnp.full((8, DSA_Q), -inf, F32)))
        cnt = sub_reduce(cnt, jnp.sum)
        above = sub_reduce(above, jnp.min)
        below = sub_reduce(below, jnp.max)
        active = lo < hi
        enough = cnt >= k_sel
        up = active & enough
        down = active & (cnt <= k_sel)
        return (jnp.where(up, above, lo), jnp.where(down, jnp.where(enough, above, below), hi),
                jnp.where(up, cnt, n_lo), jnp.where(down, cnt, n_hi))

    def snaps(bounds):
        for _ in range(snap_unroll):
            bounds = snap(bounds)
        return bounds

    def unsettled(bounds):
        return jnp.max((bounds[0] < bounds[1]).astype(I32)) > 0

    tau, _, n_ge, n_gt = lax.while_loop(unsettled, snaps, bounds)

    need = k_sel - n_gt
    n_idx_bits = seq.bit_length() - 1
    assert 1 << n_idx_bits == seq

    def tie_break():
        def idx_step(it, jp):
            cand = jp + lax.shift_left(jnp.int32(1), n_idx_bits - 1 - it)
            below = count(lambda x, c: (x == tau) & ((c * BLK + row) < cand))
            return jnp.where(below < need, cand, jp)
        last = lax.fori_loop(0, n_idx_bits, idx_step, jnp.zeros((1, DSA_Q), I32))
        return jnp.where(tied, last, seq)

    tied = (n_ge > k_sel) & jnp.logical_not(few)
    surplus = jnp.max(tied.astype(I32)) > 0
    tie_last = lax.cond(surplus, tie_break, lambda: jnp.full((1, DSA_Q), seq, I32))

    @pl.when(j == 0)
    def _():
        pos = lax.broadcasted_iota(I32, (seq, LANES), 0)
        ln = lax.broadcasted_iota(I32, (seq, LANES), 1) - HEAD_DIM
        feat = jnp.where(ln == 0, pos // POS_SPLIT,
                         jnp.where(ln == 1, pos % POS_SPLIT, jnp.where((ln == 2) | (ln == 3), 1, 0)))
        kpos_ref[...] = kk_ref[:, 0:LANES] + feat.astype(BF16)

    q_row = lax.broadcasted_iota(I32, (HEAD_DIM, DSA_Q), 0)
    t_hi = (t_lane // POS_SPLIT).astype(F32)
    t_lo = (t_lane % POS_SPLIT).astype(F32)
    q_aug = []
    for h in range(N_HEADS_B):
        feat = jnp.where(q_row == 0, POS_SPLIT * slopes[h],
                         jnp.where(q_row == 1, slopes[h],
                                   jnp.where(q_row == 2, -POS_SPLIT * slopes[h] * t_hi,
                                             jnp.where(q_row == 3, -slopes[h] * t_lo, 0.0))))
        q_aug.append(jnp.concatenate(
            [tq_ref[T_QB + h * HEAD_DIM:T_QB + (h + 1) * HEAD_DIM, :], feat.astype(BF16)], axis=0))

    def qk_chunk(c, ms):
        kb = kpos_ref[rows(c), :]
        s_idx = c * BLK + row
        x = sc_ref[rows(c), :]
        tie = jnp.where(x == tau, jnp.where(s_idx <= tie_last, 0.0, NEG), NEG)
        sb = jnp.where(s_idx <= t_lane, jnp.where(x > tau, 0.0, tie), NEG)
        out = []
        for h in range(N_HEADS_B):
            s = _dot(kb, q_aug[h]) + sb
            s_ref[h, rows(c), :] = s
            out.append(jnp.maximum(ms[h], _fold(s, jnp.maximum)))
        return tuple(out)

    ms = each_chunk(qk_chunk, tuple(jnp.full((8, DSA_Q), NEG, F32) for _ in range(N_HEADS_B)))
    ms = [sub_reduce(m, jnp.max) for m in ms]

    def exp_chunk(c, ls):
        out = []
        for h in range(N_HEADS_B):
            p = jnp.exp(s_ref[h, rows(c), :] - ms[h])
            p_ref[h, rows(c), :] = p.astype(BF16)
            out.append(ls[h] + _fold(p, jnp.add))
        return tuple(out)

    ls = each_chunk(exp_chunk, tuple(jnp.zeros((8, DSA_Q), F32) for _ in range(N_HEADS_B)))

    acc_ref[...] = jnp.zeros((W_B, DSA_Q), F32)

    def pv_pairs(first_pair, n, carry):
        pvs = []
        for u in range(n):
            sl = pl.ds(pl.multiple_of((first_pair + u) * 2 * BLK, 2 * BLK), 2 * BLK)
            vt = vt_ref[:, sl]
            pvs.append([_dot(vt, p_ref[h, sl, :]) for h in range(N_HEADS_B)])
        for h in range(N_HEADS_B):
            acc_ref[h * HEAD_DIM:(h + 1) * HEAD_DIM, :] += functools.reduce(
                lambda a, b: a + b, [pv[h] for pv in pvs])
        return carry

    lax.fori_loop(0, n_pairs // 2, lambda i, c: pv_pairs(2 * i, 2, c), 0)
    lax.fori_loop(0, n_pairs % 2, lambda i, c: pv_pairs(n_pairs - 1, 1, c), 0)
    l_all = jnp.concatenate(
        [jnp.broadcast_to(sub_reduce(ls[h], jnp.sum), (HEAD_DIM, DSA_Q)) for h in range(N_HEADS_B)],
        axis=0)
    o_ref[...] = (acc_ref[...] / l_all).T


def _dsa(proj_r, proj_t, w_t):
    B, T, _ = proj_r.shape
    k_sel = min(TOPK_MAX, T // 4)
    slopes = _static_alibi_slopes(N_HEADS_B)
    assert all(math.frexp(s)[0] == 0.5 for s in slopes) and T // POS_SPLIT <= 256 and POS_SPLIT <= 256
    return pl.pallas_call(
        functools.partial(_dsa_kernel, seq=T, k_sel=k_sel, snap_unroll=4, slopes=slopes),
        grid=(B, T // DSA_Q),
        in_specs=[
            pl.BlockSpec((None, T, 2 * LANES), lambda b, q: (b, 0, R_KB // (2 * LANES))),
            pl.BlockSpec((None, T_VB, DSA_Q), lambda b, q: (b, 0, q)),
            pl.BlockSpec((None, HEAD_DIM, T), lambda b, q: (b, T_VB // HEAD_DIM, 0)),
            pl.BlockSpec((None, WI_ROWS, DSA_Q), lambda b, q: (b, 0, q)),
        ],
        out_specs=pl.BlockSpec((None, DSA_Q, W_B), lambda b, q: (b, q, 0)),
        out_shape=jax.ShapeDtypeStruct((B, T, W_B), F32),
        scratch_shapes=[pltpu.VMEM((T, DSA_Q), F32),
                        pltpu.VMEM((N_HEADS_B, T, DSA_Q), F32),
                        pltpu.VMEM((N_HEADS_B, T, DSA_Q), BF16),
                        pltpu.VMEM((W_B, DSA_Q), F32),
                        pltpu.VMEM((T, LANES), BF16)],
        compiler_params=pltpu.CompilerParams(
            dimension_semantics=("parallel", "arbitrary"), vmem_limit_bytes=VMEM_LIMIT),
        name="dsa",
    )(proj_r, proj_t, proj_t, w_t)


def _mem_kv_kernel(mem_ref, g_ref, w_ref, kv_ref):
    kv_ref[0] = _dot(_rms(mem_ref[0], g_ref[...]).astype(BF16), w_ref[...]).astype(BF16)


def _mem_kv(mem, g, w):
    B, M, D = mem.shape
    return pl.pallas_call(
        _mem_kv_kernel,
        grid=(B,),
        in_specs=[
            pl.BlockSpec((1, M, D), lambda b: (b, 0, 0)),
            pl.BlockSpec((1, D), lambda b: (0, 0)),
            pl.BlockSpec(w.shape, lambda b: (0, 0)),
        ],
        out_specs=pl.BlockSpec((1, M, 2 * W_M), lambda b: (b, 0, 0)),
        out_shape=jax.ShapeDtypeStruct((B, M, 2 * W_M), BF16),
        compiler_params=pltpu.CompilerParams(
            dimension_semantics=("parallel",), vmem_limit_bytes=VMEM_LIMIT),
        name="mem_kv",
    )(mem, g, w)


def _out_kernel(x_ref, oa_ref, ob_ref, gate_ref, qm_ref, kv_ref, wo_ref, g_ref, y_ref, mix_ref, *, sub):
    tm = x_ref.shape[0]
    subs = [slice(r0, r0 + sub) for r0 in range(0, tm, sub)]
    lane = lax.broadcasted_iota(I32, (1, W_M), 1)
    head_lanes = [(lane >= h * HEAD_DIM) & (lane < (h + 1) * HEAD_DIM) for h in range(N_HEADS_MEM)]
    km = kv_ref[:, 0:W_M]
    vm = kv_ref[:, W_M:2 * W_M]
    zero = jnp.zeros((), BF16)
    vm_heads = [jnp.where(head_lanes[h], vm, zero) for h in range(N_HEADS_MEM)]

    def silu(g):
        return g * (1.0 / (1.0 + jnp.exp(-g)))

    def gated(o, rows, lo, width):
        return (o * silu(gate_ref[rows, lo:lo + width].astype(F32))).astype(BF16)

    for rows in subs:
        qm = qm_ref[rows, :]
        om = None
        for h in range(N_HEADS_MEM):
            s = _dot_nt(jnp.where(head_lanes[h], qm, zero), km)
            p = jnp.exp(s - jnp.max(s, axis=1, keepdims=True))
            p = p / jnp.sum(p, axis=1, keepdims=True)
            o = _dot(p.astype(BF16), vm_heads[h])
            om = o if om is None else om + o
        mix_ref[rows, W_A + W_B:] = gated(om, rows, W_A + W_B, W_M)

    for rows in subs:
        half = W_A // 2
        mix_ref[rows, 0:half] = gated(oa_ref[rows, 0:half], rows, 0, half)
        mix_ref[rows, half:W_A] = gated(oa_ref[rows, half:W_A], rows, half, half)
        mix_ref[rows, W_A:W_A + W_B] = gated(ob_ref[rows, :], rows, W_A, W_B)
        y_ref[rows, :] = _dot(mix_ref[rows, :], wo_ref[...])

    for rows in subs:
        y_ref[rows, :] = _rms(x_ref[rows, :] + y_ref[rows, :], g_ref[...])


def _out(x, o_a, o_b, proj_r, kv_m, w_out, g, tm=512, sub=256):
    B, T, D = x.shape
    M = kv_m.shape[1]
    row = lambda b, i: (b, i, 0)
    return pl.pallas_call(
        functools.partial(_out_kernel, sub=sub),
        grid=(B, T // tm),
        in_specs=[
            pl.BlockSpec((None, tm, D), row),
            pl.BlockSpec((None, tm, W_A), row),
            pl.BlockSpec((None, tm, W_B), row),
            pl.BlockSpec((None, tm, MIX_WIDTH), lambda b, i: (b, i, R_GATE // MIX_WIDTH)),
            pl.BlockSpec((None, tm, W_M), lambda b, i: (b, i, R_QM // W_M)),
            pl.BlockSpec((None, M, 2 * W_M), lambda b, i: (b, 0, 0)),
            pl.BlockSpec(w_out.shape, lambda b, i: (0, 0)),
            pl.BlockSpec((1, D), lambda b, i: (0, 0)),
        ],
        out_specs=pl.BlockSpec((None, tm, D), row),
        out_shape=jax.ShapeDtypeStruct((B, T, D), F32),
        scratch_shapes=[pltpu.VMEM((tm, MIX_WIDTH), BF16)],
        compiler_params=pltpu.CompilerParams(
            dimension_semantics=("parallel", "parallel"), vmem_limit_bytes=VMEM_LIMIT),
        name="out",
    )(x, o_a, o_b, proj_r, proj_r, kv_m, w_out, g)


def _alibi_slopes(n):
    return 2.0 ** (-8.0 * jnp.arange(1, n + 1, dtype=F32) / n)


def _static_alibi_slopes(n):
    return tuple(2.0 ** (-8.0 * i / n) for i in range(1, n + 1))


def _split_weights(w):
    bounds = [0]
    for s in SPLIT_SIZES:
        bounds.append(bounds[-1] + s)
    q_a, k_a, v_a, q_b, k_b, v_b, q_m, gate, q_i, k_i, w_i = (
        w[:, bounds[i]:bounds[i + 1]] for i in range(len(SPLIT_SIZES)))
    scale = HEAD_DIM ** -0.5
    idx_scale = IDX_DIM ** -0.5
    zeros = jnp.zeros((w.shape[0], LANES - HEAD_DIM), w.dtype)
    wa = jnp.concatenate([q_a * scale, k_a, v_a], axis=1)
    wr = jnp.concatenate([gate, q_m * scale, k_b, zeros, k_i, zeros], axis=1)
    wt = jnp.concatenate([q_i * idx_scale, q_b * scale, v_b], axis=1).T
    ww = jnp.concatenate([w_i.T, jnp.zeros((WI_ROWS - N_IDX_HEADS, w.shape[0]), w.dtype)], axis=0)
    return wa.astype(BF16), wr.astype(BF16), wt.astype(BF16), ww.astype(BF16)


def kernel(x, mem, g_in, g_mem, w_in, w_mem_kv, w_out, g_final):
    assert g_in.shape[0] == 1, "single-layer block: the final RMSNorm is fused into the output kernel"
    wa, wr, wt, ww = _split_weights(w_in[0])
    proj_a, proj_r, proj_t, w_t = _proj(x, g_in, wa, wr, wt, ww)
    o_a = _attn_a(_alibi_slopes(N_HEADS_A), proj_a)
    o_b = _dsa(proj_r, proj_t, w_t)
    kv_m = _mem_kv(mem, g_mem, w_mem_kv[0].astype(BF16))
    return _out(x, o_a, o_b, proj_r, kv_m, w_out[0].astype(BF16), g_final[None, :])
```

```python
import functools
import math

import jax
import jax.numpy as jnp
from jax import lax
from jax.experimental import pallas as pl
from jax.experimental.pallas import tpu as pltpu

F32 = jnp.float32
BF16 = jnp.bfloat16
I32 = jnp.int32

D_MODEL = 1024
HEAD_DIM = 64
N_HEADS_A = 8
N_HEADS_B = 4
N_HEADS_MEM = 4
W_A = N_HEADS_A * HEAD_DIM
W_B = N_HEADS_B * HEAD_DIM
W_M = N_HEADS_MEM * HEAD_DIM
MIX_WIDTH = W_A + W_B + W_M
DILATIONS = (1, 4, 16)
BAND = 128
N_IDX_HEADS = 8
IDX_DIM = 64
TOPK_MAX = 256
RMS_EPS = 1e-6
SPLIT_SIZES = (W_A, W_A, W_A, W_B, HEAD_DIM, HEAD_DIM, W_M, MIX_WIDTH,
               N_IDX_HEADS * IDX_DIM, IDX_DIM, N_IDX_HEADS)

LANES = 128
BLK = 128
NEG = -1e30
VMEM_LIMIT = 48 * 1024 * 1024
ATTN_A_VMEM_LIMIT = 56 * 1024 * 1024

R_GATE = 0
R_QM = MIX_WIDTH
R_KB = R_QM + W_M
R_KI = R_KB + LANES
R_COLS = R_KI + LANES
T_QI = 0
T_QB = N_IDX_HEADS * IDX_DIM
T_VB = T_QB + W_B
T_ROWS = T_VB + HEAD_DIM
WI_ROWS = 16
DSA_Q = 256
DSA_HALVINGS = 12
POS_SPLIT = 64


def _dot(a, b):
    return jnp.dot(a, b, preferred_element_type=F32)


def _dot_nt(a, b):
    return lax.dot_general(a, b, (((1,), (1,)), ((), ())), preferred_element_type=F32)


def _rms(x, g):
    return x * lax.rsqrt(jnp.mean(x * x, axis=-1, keepdims=True) + RMS_EPS) * g


def _proj_kernel(x_ref, g_ref, wa_ref, wr_ref, wt_ref, ww_ref, oa_ref, or_ref, ot_ref, ow_ref):
    hb = _rms(x_ref[0], g_ref[...]).astype(BF16)
    oa_ref[0] = _dot(hb, wa_ref[...])
    or_ref[0] = _dot(hb, wr_ref[...]).astype(BF16)
    ot_ref[0] = _dot_nt(wt_ref[...], hb).astype(BF16)
    ow_ref[0] = _dot_nt(ww_ref[...], hb)


def _proj(x, g, wa, wr, wt, ww, tm=512):
    B, T, D = x.shape
    const = lambda b, i: (0, 0)
    return pl.pallas_call(
        _proj_kernel,
        grid=(B, T // tm),
        in_specs=[
            pl.BlockSpec((1, tm, D), lambda b, i: (b, i, 0)),
            pl.BlockSpec((1, D), const),
            pl.BlockSpec(wa.shape, const),
            pl.BlockSpec(wr.shape, const),
            pl.BlockSpec(wt.shape, const),
            pl.BlockSpec(ww.shape, const),
        ],
        out_specs=[
            pl.BlockSpec((1, tm, 3 * W_A), lambda b, i: (b, i, 0)),
            pl.BlockSpec((1, tm, R_COLS), lambda b, i: (b, i, 0)),
            pl.BlockSpec((1, T_ROWS, tm), lambda b, i: (b, 0, i)),
            pl.BlockSpec((1, WI_ROWS, tm), lambda b, i: (b, 0, i)),
        ],
        out_shape=[
            jax.ShapeDtypeStruct((B, T, 3 * W_A), F32),
            jax.ShapeDtypeStruct((B, T, R_COLS), BF16),
            jax.ShapeDtypeStruct((B, T_ROWS, T), BF16),
            jax.ShapeDtypeStruct((B, WI_ROWS, T), F32),
        ],
        compiler_params=pltpu.CompilerParams(
            dimension_semantics=("parallel", "parallel"), vmem_limit_bytes=VMEM_LIMIT),
        name="proj",
    )(x, g, wa, wr, wt, ww)


def _attn_a_kernel(slopes_ref, q_ref, k_ref, v_ref, o_ref, acc_ref, m_ref, bias_ref, stage_ref,
                   p_ref, cm_ref, *, seq, unroll):
    hp = pl.program_id(1)
    lane = lax.broadcasted_iota(I32, (1, LANES), 1)
    head_lanes = (lane < HEAD_DIM, lane >= HEAD_DIM)
    row = lax.broadcasted_iota(I32, (BLK, BLK), 0)
    col = lax.broadcasted_iota(I32, (BLK, BLK), 1)
    d_cur = (row - col).astype(F32)
    d_prev = d_cur + float(BAND)
    n_all = seq // BLK
    n_pat = len(DILATIONS)

    def n_blocks(p):
        return n_all // DILATIONS[p]

    def n_keys(p):
        return BLK if n_blocks(p) == 1 else 2 * BLK

    def staged(g):
        return pl.ds(pl.multiple_of((g + 1) * BLK, BLK), BLK)

    def keys(p, g):
        return pl.ds(pl.multiple_of((g + 2) * BLK - n_keys(p), BLK), n_keys(p))

    def pitch(dil):
        return seq // dil + 1 if seq // dil == BLK and dil > 1 else None

    def put(ref, h, p, g, val):
        dil = DILATIONS[p]
        start = g // n_blocks(p) + (g % n_blocks(p)) * (dil * BLK)
        if dil == 1:
            ref[h, p, pl.ds(pl.multiple_of(start, BLK), BLK), :] = val
        elif pitch(dil):
            ref[h, p, pl.ds(start * pitch(dil), BLK), :] = val
        else:
            ref[h, p, pl.ds(start, BLK, stride=dil), :] = val

    def stage(p):
        dil = DILATIONS[p]
        n = seq // dil
        src_dil = DILATIONS[p - 1] if p >= 2 else 1
        step = dil // src_dil
        srcs = (cm_ref.at[0], cm_ref.at[1], cm_ref.at[2]) if p >= 2 else (q_ref, k_ref, v_ref)
        keep_copy = p + 1 < n_pat and p >= 1

        def body(g):
            first = g * BLK
            r, i0 = first // n, first % n
            start = (r % src_dil) * (seq // src_dil) + r // src_dil + step * i0
            sl = pl.ds(start, BLK) if step == 1 else pl.ds(start, BLK, stride=step)
            q, k, v = (src[sl, :] for src in srcs)
            if keep_copy:
                for a, val in enumerate((q, k, v)):
                    cm_ref[a, pl.ds(pl.multiple_of(first, BLK), BLK), :] = val
            for h in range(2):
                stage_ref[p, h, staged(g), :] = jnp.where(head_lanes[h], q, 0.0).astype(BF16)
                stage_ref[p, 3 + h, staged(g), :] = jnp.where(head_lanes[h], v, 1.0).astype(BF16)
            stage_ref[p, 2, staged(g), :] = k.astype(BF16)
        return body

    def probs(p):
        def body(g):
            first = jnp.where(jnp.asarray(g, I32) % n_blocks(p) == 0, 1, 0)
            kw = stage_ref[p, 2, keys(p, g), :]
            for h in range(2):
                s = _dot_nt(stage_ref[p, h, staged(g), :], kw)
                s = s + bias_ref[p, h, first, :, 2 * BLK - n_keys(p):]
                m = jnp.max(s, axis=1, keepdims=True)
                p_ref[p % 2, h, g, :, :n_keys(p)] = jnp.exp(s - m).astype(BF16)
                put(m_ref, h, p, g, jnp.broadcast_to(m, (BLK, LANES)))
        return body

    def values(p):
        def body(g):
            for h in range(2):
                put(acc_ref, h, p, g,
                    _dot(p_ref[p % 2, h, g, :, :n_keys(p)], stage_ref[p, 3 + h, keys(p, g), :]))
        return body

    def run(*bodies, unroll=unroll):
        def step(g, carry):
            for body in bodies:
                body(g)
            return carry
        lax.fori_loop(0, n_all, step, 0, unroll=unroll)

    for p, dil in enumerate(DILATIONS):
        stage_ref[p, :, 0:BLK, :] = jnp.zeros((5, BLK, LANES), BF16)
        for h in range(2):
            sd = slopes_ref[hp * 2 + h] * float(dil)
            cur = jnp.where(row >= col, -sd * d_cur, NEG)
            bias_ref[p, h, 0, :, BLK:] = cur
            bias_ref[p, h, 1, :, BLK:] = cur
            if n_keys(p) > BLK:
                bias_ref[p, h, 0, :, :BLK] = jnp.where(col >= row, -sd * d_prev, NEG)
                bias_ref[p, h, 1, :, :BLK] = jnp.full((BLK, BLK), NEG, F32)

    assert n_pat == 3
    run(stage(0), unroll=2)
    run(probs(0), stage(1))
    run(values(0), probs(1), stage(2))
    run(values(1), probs(2))
    run(values(2))

    rows_per_step = 2 * BLK

    def natural_rows(ref, h, p, i):
        dil = DILATIONS[p]
        if not pitch(dil):
            return ref[h, p, pl.ds(pl.multiple_of(i * rows_per_step, rows_per_step), rows_per_step), :]
        per = rows_per_step // dil
        return jnp.concatenate(
            [ref[h, p, pl.ds(i * per + u, dil, stride=pitch(dil)), :] for u in range(per)], axis=0)

    def merge(i, carry):
        sl = pl.ds(pl.multiple_of(i * rows_per_step, rows_per_step), rows_per_step)
        nums = []
        for h in range(2):
            ms = [natural_rows(m_ref, h, p, i) for p in range(n_pat)]
            mx = functools.reduce(jnp.maximum, ms)
            nums.append(sum(jnp.exp(ms[p] - mx) * natural_rows(acc_ref, h, p, i) for p in range(n_pat)))
        acc = jnp.where(head_lanes[0], nums[0], nums[1])
        den = pltpu.roll(jnp.where(head_lanes[0], nums[1], nums[0]), HEAD_DIM, axis=1)
        o_ref[sl, :] = acc / den
        return carry

    lax.fori_loop(0, seq // rows_per_step, merge, 0)


def _attn_a(slopes, proj_a, unroll=8):
    B, T, _ = proj_a.shape
    n_pairs = N_HEADS_A // 2
    n_pat = len(DILATIONS)

    def spec(off):
        return pl.BlockSpec((None, T, LANES), lambda b, h: (b, 0, off + h))

    return pl.pallas_call(
        functools.partial(_attn_a_kernel, seq=T, unroll=unroll),
        grid=(B, n_pairs),
        in_specs=[pl.BlockSpec(memory_space=pltpu.SMEM), spec(0), spec(n_pairs), spec(2 * n_pairs)],
        out_specs=pl.BlockSpec((None, T, LANES), lambda b, h: (b, 0, h)),
        out_shape=jax.ShapeDtypeStruct((B, T, W_A), F32),
        scratch_shapes=[pltpu.VMEM((2, n_pat, T + max(DILATIONS), LANES), F32),
                        pltpu.VMEM((2, n_pat, T + max(DILATIONS), LANES), F32),
                        pltpu.VMEM((n_pat, 2, 2, BLK, 2 * BLK), F32),
                        pltpu.VMEM((n_pat, 5, T + BLK, LANES), BF16),
                        pltpu.VMEM((2, 2, T // BLK, BLK, 2 * BLK), BF16),
                        pltpu.VMEM((3, T, LANES), F32)],
        compiler_params=pltpu.CompilerParams(
            dimension_semantics=("parallel", "parallel"), vmem_limit_bytes=ATTN_A_VMEM_LIMIT),
        name="attn_a",
    )(slopes, proj_a, proj_a, proj_a)


def _fold(x, op):
    return functools.reduce(op, [x[r:r + 8] for r in range(0, x.shape[0], 8)])


def _dsa_kernel(kk_ref, tq_ref, vt_ref, wt_ref, o_ref, sc_ref, s_ref, p_ref, acc_ref, kpos_ref,
                *, seq, k_sel, snap_unroll, slopes):
    j = pl.program_id(1)
    n_pairs = j + 1
    t_lane = j * DSA_Q + lax.broadcasted_iota(I32, (1, DSA_Q), 1)
    row = lax.broadcasted_iota(I32, (BLK, DSA_Q), 0)
    ws = wt_ref[0:N_IDX_HEADS, :] * (N_IDX_HEADS ** -0.5)
    inf = float("inf")

    def rows(c):
        return pl.ds(pl.multiple_of(c * BLK, BLK), BLK)

    def each_chunk(fn, init):
        def group(first, n, carry):
            for u in range(n):
                carry = fn(first + u, carry)
            return carry
        carry = lax.fori_loop(0, n_pairs // 2, lambda i, c: group(4 * i, 4, c), init)
        return lax.fori_loop(0, n_pairs % 2, lambda i, c: group(4 * (n_pairs // 2), 2, c), carry)

    def sub_reduce(x, op):
        return op(x, axis=0, keepdims=True)

    def score_chunk(c, carry):
        mn, mx = carry
        ki = kk_ref[rows(c), R_KI - R_KB:R_KI - R_KB + IDX_DIM]
        lgs = [_dot(ki, tq_ref[T_QI + h * IDX_DIM:T_QI + (h + 1) * IDX_DIM, :])
               for h in range(N_IDX_HEADS)]
        sc = functools.reduce(
            lambda a, b: a + b, [jnp.maximum(lg, 0.0) * ws[h:h + 1, :] for h, lg in enumerate(lgs)])
        causal = (c * BLK + row) <= t_lane
        sc_ref[rows(c), :] = jnp.where(causal, sc, -inf)
        mn = jnp.minimum(mn, _fold(jnp.where(causal, sc, inf), jnp.minimum))
        mx = jnp.maximum(mx, _fold(jnp.where(causal, sc, -inf), jnp.maximum))
        return mn, mx

    mn, mx = each_chunk(score_chunk, (jnp.full((8, DSA_Q), inf, F32), jnp.full((8, DSA_Q), -inf, F32)))
    few = t_lane < k_sel
    lo = jnp.where(few, -inf, sub_reduce(mn, jnp.min))
    hi = jnp.where(few, -inf, sub_reduce(mx, jnp.max))

    def count(pred):
        def body(c, cnt):
            return cnt + _fold(pred(sc_ref[rows(c), :], c).astype(I32), jnp.add)
        return sub_reduce(each_chunk(body, jnp.zeros((8, DSA_Q), I32)), jnp.sum)

    def halve(_, bounds):
        lo, hi, n_lo, n_hi = bounds
        mid = 0.5 * lo + 0.5 * hi
        cnt = count(lambda x, c: x >= mid)
        active = lo < hi
        up = active & (cnt >= k_sel)
        down = active & (cnt <= k_sel)
        return (jnp.where(up, mid, lo), jnp.where(down, mid, hi),
                jnp.where(up, cnt, n_lo), jnp.where(down, cnt, n_hi))

    bounds = lax.fori_loop(0, DSA_HALVINGS, halve, (lo, hi, t_lane + 1, jnp.zeros((1, DSA_Q), I32)))

    def snap(bounds):
        lo, hi, n_lo, n_hi = bounds
        mid = 0.5 * lo + 0.5 * hi
        mid = jnp.where(mid > lo, mid, hi)

        def body(c, carry):
            cnt, above, below = carry
            x = sc_ref[rows(c), :]
            ge = x >= mid
            return (cnt + _fold(ge.astype(I32), jnp.add),
                    jnp.minimum(above, _fold(jnp.where(ge, x, inf), jnp.minimum)),
                    jnp.maximum(below, _fold(jnp.where(ge, -inf, x), jnp.maximum)))

        cnt, above, below = each_chunk(body, (jnp.zeros((8, DSA_Q), I32),
                                              jnp.full((8, DSA_Q), inf, F32),
                                              jnp.full((8, DSA_Q), -inf, F32)))
        cnt = sub_reduce(cnt, jnp.sum)
        above = sub_reduce(above, jnp.min)
        below = sub_reduce(below, jnp.max)
        active = lo < hi
        enough = cnt >= k_sel
        up = active & enough
        down = active & (cnt <= k_sel)
        return (jnp.where(up, above, lo), jnp.where(down, jnp.where(enough, above, below), hi),
                jnp.where(up, cnt, n_lo), jnp.where(down, cnt, n_hi))

    def snaps(bounds):
        for _ in range(snap_unroll):
            bounds = snap(bounds)
        return bounds

    def unsettled(bounds):
        return jnp.max((bounds[0] < bounds[1]).astype(I32)) > 0

    tau, _, n_ge, n_gt = lax.while_loop(unsettled, snaps, snaps(bounds))

    need = k_sel - n_gt
    n_idx_bits = seq.bit_length() - 1
    assert 1 << n_idx_bits == seq

    def tie_break():
        def idx_step(it, jp):
            cand = jp + lax.shift_left(jnp.int32(1), n_idx_bits - 1 - it)
            below = count(lambda x, c: (x == tau) & ((c * BLK + row) < cand))
            return jnp.where(below < need, cand, jp)
        last = lax.fori_loop(0, n_idx_bits, idx_step, jnp.zeros((1, DSA_Q), I32))
        return jnp.where(tied, last, seq)

    tied = (n_ge > k_sel) & jnp.logical_not(few)
    surplus = jnp.max(tied.astype(I32)) > 0
    tie_last = lax.cond(surplus, tie_break, lambda: jnp.full((1, DSA_Q), seq, I32))

    @pl.when(j == 0)
    def _():
        pos = lax.broadcasted_iota(I32, (seq, LANES), 0)
        ln = lax.broadcasted_iota(I32, (seq, LANES), 1) - HEAD_DIM
        feat = jnp.where(ln == 0, pos // POS_SPLIT,
                         jnp.where(ln == 1, pos % POS_SPLIT, jnp.where((ln == 2) | (ln == 3), 1, 0)))
        kpos_ref[...] = kk_ref[:, 0:LANES] + feat.astype(BF16)

    q_row = lax.broadcasted_iota(I32, (HEAD_DIM, DSA_Q), 0)
    t_hi = (t_lane // POS_SPLIT).astype(F32)
    t_lo = (t_lane % POS_SPLIT).astype(F32)
    q_aug = []
    for h in range(N_HEADS_B):
        feat = jnp.where(q_row == 0, POS_SPLIT * slopes[h],
                         jnp.where(q_row == 1, slopes[h],
                                   jnp.where(q_row == 2, -POS_SPLIT * slopes[h] * t_hi,
                                             jnp.where(q_row == 3, -slopes[h] * t_lo, 0.0))))
        q_aug.append(jnp.concatenate(
            [tq_ref[T_QB + h * HEAD_DIM:T_QB + (h + 1) * HEAD_DIM, :], feat.astype(BF16)], axis=0))

    def qk_chunk(c, ms):
        kb = kpos_ref[rows(c), :]
        s_idx = c * BLK + row
        x = sc_ref[rows(c), :]
        tie = jnp.where(x == tau, jnp.where(s_idx <= tie_last, 0.0, NEG), NEG)
        sb = jnp.where(s_idx <= t_lane, jnp.where(x > tau, 0.0, tie), NEG)
        out = []
        for h in range(N_HEADS_B):
            s = _dot(kb, q_aug[h]) + sb
            s_ref[h, rows(c), :] = s
            out.append(jnp.maximum(ms[h], _fold(s, jnp.maximum)))
        return tuple(out)

    ms = each_chunk(qk_chunk, tuple(jnp.full((8, DSA_Q), NEG, F32) for _ in range(N_HEADS_B)))
    ms = [sub_reduce(m, jnp.max) for m in ms]

    def exp_chunk(c, ls):
        out = []
        for h in range(N_HEADS_B):
            p = jnp.exp(s_ref[h, rows(c), :] - ms[h])
            p_ref[h, rows(c), :] = p.astype(BF16)
            out.append(ls[h] + _fold(p, jnp.add))
        return tuple(out)

    ls = each_chunk(exp_chunk, tuple(jnp.zeros((8, DSA_Q), F32) for _ in range(N_HEADS_B)))

    acc_ref[...] = jnp.zeros((W_B, DSA_Q), F32)

    def pv_pairs(first_pair, n, carry):
        pvs = []
        for u in range(n):
            sl = pl.ds(pl.multiple_of((first_pair + u) * 2 * BLK, 2 * BLK), 2 * BLK)
            vt = vt_ref[:, sl]
            pvs.append([_dot(vt, p_ref[h, sl, :]) for h in range(N_HEADS_B)])
        for h in range(N_HEADS_B):
            acc_ref[h * HEAD_DIM:(h + 1) * HEAD_DIM, :] += functools.reduce(
                lambda a, b: a + b, [pv[h] for pv in pvs])
        return carry

    lax.fori_loop(0, n_pairs // 2, lambda i, c: pv_pairs(2 * i, 2, c), 0)
    lax.fori_loop(0, n_pairs % 2, lambda i, c: pv_pairs(n_pairs - 1, 1, c), 0)
    l_all = jnp.concatenate(
        [jnp.broadcast_to(sub_reduce(ls[h], jnp.sum), (HEAD_DIM, DSA_Q)) for h in range(N_HEADS_B)],
        axis=0)
    o_ref[...] = (acc_ref[...] / l_all).T


def _dsa(proj_r, proj_t, w_t):
    B, T, _ = proj_r.shape
    k_sel = min(TOPK_MAX, T // 4)
    slopes = _static_alibi_slopes(N_HEADS_B)
    assert all(math.frexp(s)[0] == 0.5 for s in slopes) and T // POS_SPLIT <= 256 and POS_SPLIT <= 256
    return pl.pallas_call(
        functools.partial(_dsa_kernel, seq=T, k_sel=k_sel, snap_unroll=4, slopes=slopes),
        grid=(B, T // DSA_Q),
        in_specs=[
            pl.BlockSpec((None, T, 2 * LANES), lambda b, q: (b, 0, R_KB // (2 * LANES))),
            pl.BlockSpec((None, T_VB, DSA_Q), lambda b, q: (b, 0, q)),
            pl.BlockSpec((None, HEAD_DIM, T), lambda b, q: (b, T_VB // HEAD_DIM, 0)),
            pl.BlockSpec((None, WI_ROWS, DSA_Q), lambda b, q: (b, 0, q)),
        ],
        out_specs=pl.BlockSpec((None, DSA_Q, W_B), lambda b, q: (b, q, 0)),
        out_shape=jax.ShapeDtypeStruct((B, T, W_B), F32),
        scratch_shapes=[pltpu.VMEM((T, DSA_Q), F32),
                        pltpu.VMEM((N_HEADS_B, T, DSA_Q), F32),
                        pltpu.VMEM((N_HEADS_B, T, DSA_Q), BF16),
                        pltpu.VMEM((W_B, DSA_Q), F32),
                        pltpu.VMEM((T, LANES), BF16)],
        compiler_params=pltpu.CompilerParams(
            dimension_semantics=("parallel", "arbitrary"), vmem_limit_bytes=VMEM_LIMIT),
        name="dsa",
    )(proj_r, proj_t, proj_t, w_t)


def _mem_kv_kernel(mem_ref, g_ref, w_ref, kv_ref):
    kv_ref[0] = _dot(_rms(mem_ref[0], g_ref[...]).astype(BF16), w_ref[...]).astype(BF16)


def _mem_kv(mem, g, w):
    B, M, D = mem.shape
    return pl.pallas_call(
        _mem_kv_kernel,
        grid=(B,),
        in_specs=[
            pl.BlockSpec((1, M, D), lambda b: (b, 0, 0)),
            pl.BlockSpec((1, D), lambda b: (0, 0)),
            pl.BlockSpec(w.shape, lambda b: (0, 0)),
        ],
        out_specs=pl.BlockSpec((1, M, 2 * W_M), lambda b: (b, 0, 0)),
        out_shape=jax.ShapeDtypeStruct((B, M, 2 * W_M), BF16),
        compiler_params=pltpu.CompilerParams(
            dimension_semantics=("parallel",), vmem_limit_bytes=VMEM_LIMIT),
        name="mem_kv",
    )(mem, g, w)


def _out_kernel(x_ref, oa_ref, ob_ref, gate_ref, qm_ref, kv_ref, wo_ref, g_ref, y_ref, mix_ref, *, sub):
    tm = x_ref.shape[0]
    subs = [slice(r0, r0 + sub) for r0 in range(0, tm, sub)]
    lane = lax.broadcasted_iota(I32, (1, W_M), 1)
    head_lanes = [(lane >= h * HEAD_DIM) & (lane < (h + 1) * HEAD_DIM) for h in range(N_HEADS_MEM)]
    km = kv_ref[:, 0:W_M]
    vm = kv_ref[:, W_M:2 * W_M]
    zero = jnp.zeros((), BF16)
    vm_heads = [jnp.where(head_lanes[h], vm, zero) for h in range(N_HEADS_MEM)]

    def silu(g):
        return g * (1.0 / (1.0 + jnp.exp(-g)))

    def gated(o, rows, lo, width):
        return (o * silu(gate_ref[rows, lo:lo + width].astype(F32))).astype(BF16)

    for rows in subs:
        qm = qm_ref[rows, :]
        om = None
        for h in range(N_HEADS_MEM):
            s = _dot_nt(jnp.where(head_lanes[h], qm, zero), km)
            p = jnp.exp(s - jnp.max(s, axis=1, keepdims=True))
            p = p / jnp.sum(p, axis=1, keepdims=True)
            o = _dot(p.astype(BF16), vm_heads[h])
            om = o if om is None else om + o
        mix_ref[rows, W_A + W_B:] = gated(om, rows, W_A + W_B, W_M)

    for rows in subs:
        half = W_A // 2
        mix_ref[rows, 0:half] = gated(oa_ref[rows, 0:half], rows, 0, half)
        mix_ref[rows, half:W_A] = gated(oa_ref[rows, half:W_A], rows, half, half)
        mix_ref[rows, W_A:W_A + W_B] = gated(ob_ref[rows, :], rows, W_A, W_B)
        y_ref[rows, :] = _dot(mix_ref[rows, :], wo_ref[...])

    for rows in subs:
        y_ref[rows, :] = _rms(x_ref[rows, :] + y_ref[rows, :], g_ref[...])


def _out(x, o_a, o_b, proj_r, kv_m, w_out, g, tm=512, sub=256):
    B, T, D = x.shape
    M = kv_m.shape[1]
    row = lambda b, i: (b, i, 0)
    return pl.pallas_call(
        functools.partial(_out_kernel, sub=sub),
        grid=(B, T // tm),
        in_specs=[
            pl.BlockSpec((None, tm, D), row),
            pl.BlockSpec((None, tm, W_A), row),
            pl.BlockSpec((None, tm, W_B), row),
            pl.BlockSpec((None, tm, MIX_WIDTH), lambda b, i: (b, i, R_GATE // MIX_WIDTH)),
            pl.BlockSpec((None, tm, W_M), lambda b, i: (b, i, R_QM // W_M)),
            pl.BlockSpec((None, M, 2 * W_M), lambda b, i: (b, 0, 0)),
            pl.BlockSpec(w_out.shape, lambda b, i: (0, 0)),
            pl.BlockSpec((1, D), lambda b, i: (0, 0)),
        ],
        out_specs=pl.BlockSpec((None, tm, D), row),
        out_shape=jax.ShapeDtypeStruct((B, T, D), F32),
        scratch_shapes=[pltpu.VMEM((tm, MIX_WIDTH), BF16)],
        compiler_params=pltpu.CompilerParams(
            dimension_semantics=("parallel", "parallel"), vmem_limit_bytes=VMEM_LIMIT),
        name="out",
    )(x, o_a, o_b, proj_r, proj_r, kv_m, w_out, g)


def _alibi_slopes(n):
    return 2.0 ** (-8.0 * jnp.arange(1, n + 1, dtype=F32) / n)


def _static_alibi_slopes(n):
    return tuple(2.0 ** (-8.0 * i / n) for i in range(1, n + 1))


def _split_weights(w):
    bounds = [0]
    for s in SPLIT_SIZES:
        bounds.append(bounds[-1] + s)
    q_a, k_a, v_a, q_b, k_b, v_b, q_m, gate, q_i, k_i, w_i = (
        w[:, bounds[i]:bounds[i + 1]] for i in range(len(SPLIT_SIZES)))
    scale = HEAD_DIM ** -0.5
    idx_scale = IDX_DIM ** -0.5
    zeros = jnp.zeros((w.shape[0], LANES - HEAD_DIM), w.dtype)
    wa = jnp.concatenate([q_a * scale, k_a, v_a], axis=1)
    wr = jnp.concatenate([gate, q_m * scale, k_b, zeros, k_i, zeros], axis=1)
    wt = jnp.concatenate([q_i * idx_scale, q_b * scale, v_b], axis=1).T
    ww = jnp.concatenate([w_i.T, jnp.zeros((WI_ROWS - N_IDX_HEADS, w.shape[0]), w.dtype)], axis=0)
    return wa.astype(BF16), wr.astype(BF16), wt.astype(BF16), ww.astype(BF16)


def kernel(x, mem, g_in, g_mem, w_in, w_mem_kv, w_out, g_final):
    assert g_in.shape[0] == 1, "single-layer block: the final RMSNorm is fused into the output kernel"
    wa, wr, wt, ww = _split_weights(w_in[0])
    proj_a, proj_r, proj_t, w_t = _proj(x, g_in, wa, wr, wt, ww)
    o_a = _attn_a(_alibi_slopes(N_HEADS_A), proj_a)
    o_b = _dsa(proj_r, proj_t, w_t)
    kv_m = _mem_kv(mem, g_mem, w_mem_kv[0].astype(BF16))
    return _out(x, o_a, o_b, proj_r, kv_m, w_out[0].astype(BF16), g_final[None, :])
```

```python
import functools
import math

import jax
import jax.numpy as jnp
from jax import lax
from jax.experimental import pallas as pl
from jax.experimental.pallas import tpu as pltpu

F32 = jnp.float32
BF16 = jnp.bfloat16
I32 = jnp.int32

D_MODEL = 1024
HEAD_DIM = 64
N_HEADS_A = 8
N_HEADS_B = 4
N_HEADS_MEM = 4
W_A = N_HEADS_A * HEAD_DIM
W_B = N_HEADS_B * HEAD_DIM
W_M = N_HEADS_MEM * HEAD_DIM
MIX_WIDTH = W_A + W_B + W_M
DILATIONS = (1, 4, 16)
BAND = 128
N_IDX_HEADS = 8
IDX_DIM = 64
TOPK_MAX = 256
RMS_EPS = 1e-6
SPLIT_SIZES = (W_A, W_A, W_A, W_B, HEAD_DIM, HEAD_DIM, W_M, MIX_WIDTH,
               N_IDX_HEADS * IDX_DIM, IDX_DIM, N_IDX_HEADS)

LANES = 128
BLK = 128
NEG = -1e30
VMEM_LIMIT = 48 * 1024 * 1024
ATTN_A_VMEM_LIMIT = 56 * 1024 * 1024

R_GATE = 0
R_QM = MIX_WIDTH
R_KB = R_QM + W_M
R_KI = R_KB + LANES
R_COLS = R_KI + LANES
T_QI = 0
T_QB = N_IDX_HEADS * IDX_DIM
T_VB = T_QB + W_B
T_ROWS = T_VB + HEAD_DIM
WI_ROWS = 16
DSA_Q = 256
DSA_HALVINGS = 12
POS_SPLIT = 64


def _dot(a, b):
    return jnp.dot(a, b, preferred_element_type=F32)


def _dot_nt(a, b):
    return lax.dot_general(a, b, (((1,), (1,)), ((), ())), preferred_element_type=F32)


def _rms(x, g):
    return x * lax.rsqrt(jnp.mean(x * x, axis=-1, keepdims=True) + RMS_EPS) * g


def _proj_kernel(x_ref, g_ref, wa_ref, wr_ref, wt_ref, ww_ref, oa_ref, or_ref, ot_ref, ow_ref):
    hb = _rms(x_ref[0], g_ref[...]).astype(BF16)
    oa_ref[0] = _dot(hb, wa_ref[...])
    or_ref[0] = _dot(hb, wr_ref[...]).astype(BF16)
    ot_ref[0] = _dot_nt(wt_ref[...], hb).astype(BF16)
    ow_ref[0] = _dot_nt(ww_ref[...], hb)


def _proj(x, g, wa, wr, wt, ww, tm=512):
    B, T, D = x.shape
    const = lambda b, i: (0, 0)
    return pl.pallas_call(
        _proj_kernel,
        grid=(B, T // tm),
        in_specs=[
            pl.BlockSpec((1, tm, D), lambda b, i: (b, i, 0)),
            pl.BlockSpec((1, D), const),
            pl.BlockSpec(wa.shape, const),
            pl.BlockSpec(wr.shape, const),
            pl.BlockSpec(wt.shape, const),
            pl.BlockSpec(ww.shape, const),
        ],
        out_specs=[
            pl.BlockSpec((1, tm, 3 * W_A), lambda b, i: (b, i, 0)),
            pl.BlockSpec((1, tm, R_COLS), lambda b, i: (b, i, 0)),
            pl.BlockSpec((1, T_ROWS, tm), lambda b, i: (b, 0, i)),
            pl.BlockSpec((1, WI_ROWS, tm), lambda b, i: (b, 0, i)),
        ],
        out_shape=[
            jax.ShapeDtypeStruct((B, T, 3 * W_A), F32),
            jax.ShapeDtypeStruct((B, T, R_COLS), BF16),
            jax.ShapeDtypeStruct((B, T_ROWS, T), BF16),
            jax.ShapeDtypeStruct((B, WI_ROWS, T), F32),
        ],
        compiler_params=pltpu.CompilerParams(
            dimension_semantics=("parallel", "parallel"), vmem_limit_bytes=VMEM_LIMIT),
        name="proj",
    )(x, g, wa, wr, wt, ww)


def _attn_a_kernel(slopes_ref, q_ref, k_ref, v_ref, o_ref, acc_ref, m_ref, bias_ref, stage_ref,
                   p_ref, cm_ref, *, seq, unroll):
    hp = pl.program_id(1)
    lane = lax.broadcasted_iota(I32, (1, LANES), 1)
    head_lanes = (lane < HEAD_DIM, lane >= HEAD_DIM)
    row = lax.broadcasted_iota(I32, (BLK, BLK), 0)
    col = lax.broadcasted_iota(I32, (BLK, BLK), 1)
    d_cur = (row - col).astype(F32)
    d_prev = d_cur + float(BAND)
    n_all = seq // BLK
    n_pat = len(DILATIONS)

    def n_blocks(p):
        return n_all // DILATIONS[p]

    def n_keys(p):
        return BLK if n_blocks(p) == 1 else 2 * BLK

    def staged(g):
        return pl.ds(pl.multiple_of((g + 1) * BLK, BLK), BLK)

    def keys(p, g):
        return pl.ds(pl.multiple_of((g + 2) * BLK - n_keys(p), BLK), n_keys(p))

    def pitch(dil):
        return seq // dil + 1 if seq // dil == BLK and dil > 1 else None

    def put(ref, h, p, g, val):
        dil = DILATIONS[p]
        start = g // n_blocks(p) + (g % n_blocks(p)) * (dil * BLK)
        if dil == 1:
            ref[h, p, pl.ds(pl.multiple_of(start, BLK), BLK), :] = val
        elif pitch(dil):
            ref[h, p, pl.ds(start * pitch(dil), BLK), :] = val
        else:
            ref[h, p, pl.ds(start, BLK, stride=dil), :] = val

    def stage(p):
        dil = DILATIONS[p]
        n = seq // dil
        src_dil = DILATIONS[p - 1] if p >= 2 else 1
        step = dil // src_dil
        srcs = (cm_ref.at[0], cm_ref.at[1], cm_ref.at[2]) if p >= 2 else (q_ref, k_ref, v_ref)
        keep_copy = p + 1 < n_pat and p >= 1

        def body(g):
            first = g * BLK
            r, i0 = first // n, first % n
            start = (r % src_dil) * (seq // src_dil) + r // src_dil + step * i0
            sl = pl.ds(start, BLK) if step == 1 else pl.ds(start, BLK, stride=step)
            q, k, v = (src[sl, :] for src in srcs)
            if keep_copy:
                for a, val in enumerate((q, k, v)):
                    cm_ref[a, pl.ds(pl.multiple_of(first, BLK), BLK), :] = val
            for h in range(2):
                stage_ref[p, h, staged(g), :] = jnp.where(head_lanes[h], q, 0.0).astype(BF16)
                stage_ref[p, 3 + h, staged(g), :] = jnp.where(head_lanes[h], v, 1.0).astype(BF16)
            stage_ref[p, 2, staged(g), :] = k.astype(BF16)
        return body

    def probs(p):
        def body(g):
            first = jnp.where(jnp.asarray(g, I32) % n_blocks(p) == 0, 1, 0)
            kw = stage_ref[p, 2, keys(p, g), :]
            for h in range(2):
                s = _dot_nt(stage_ref[p, h, staged(g), :], kw)
                s = s + bias_ref[p, h, first, :, 2 * BLK - n_keys(p):]
                m = jnp.max(s, axis=1, keepdims=True)
                p_ref[p % 2, h, g, :, :n_keys(p)] = jnp.exp(s - m).astype(BF16)
                put(m_ref, h, p, g, jnp.broadcast_to(m, (BLK, LANES)))
        return body

    def values(p):
        def body(g):
            for h in range(2):
                put(acc_ref, h, p, g,
                    _dot(p_ref[p % 2, h, g, :, :n_keys(p)], stage_ref[p, 3 + h, keys(p, g), :]))
        return body

    def run(*bodies, unroll=unroll):
        def step(g, carry):
            for body in bodies:
                body(g)
            return carry
        lax.fori_loop(0, n_all, step, 0, unroll=unroll)

    for p, dil in enumerate(DILATIONS):
        stage_ref[p, :, 0:BLK, :] = jnp.zeros((5, BLK, LANES), BF16)
        for h in range(2):
            sd = slopes_ref[hp * 2 + h] * float(dil)
            cur = jnp.where(row >= col, -sd * d_cur, NEG)
            bias_ref[p, h, 0, :, BLK:] = cur
            bias_ref[p, h, 1, :, BLK:] = cur
            if n_keys(p) > BLK:
                bias_ref[p, h, 0, :, :BLK] = jnp.where(col >= row, -sd * d_prev, NEG)
                bias_ref[p, h, 1, :, :BLK] = jnp.full((BLK, BLK), NEG, F32)

    assert n_pat == 3
    run(stage(0), unroll=2)
    run(probs(0), stage(1))
    run(values(0), probs(1), stage(2))
    run(values(1), probs(2))
    run(values(2))

    rows_per_step = 2 * BLK

    def natural_rows(ref, h, p, i):
        dil = DILATIONS[p]
        if not pitch(dil):
            return ref[h, p, pl.ds(pl.multiple_of(i * rows_per_step, rows_per_step), rows_per_step), :]
        per = rows_per_step // dil
        return jnp.concatenate(
            [ref[h, p, pl.ds(i * per + u, dil, stride=pitch(dil)), :] for u in range(per)], axis=0)

    def merge(i, carry):
        sl = pl.ds(pl.multiple_of(i * rows_per_step, rows_per_step), rows_per_step)
        nums = []
        for h in range(2):
            ms = [natural_rows(m_ref, h, p, i) for p in range(n_pat)]
            mx = functools.reduce(jnp.maximum, ms)
            nums.append(sum(jnp.exp(ms[p] - mx) * natural_rows(acc_ref, h, p, i) for p in range(n_pat)))
        acc = jnp.where(head_lanes[0], nums[0], nums[1])
        den = pltpu.roll(jnp.where(head_lanes[0], nums[1], nums[0]), HEAD_DIM, axis=1)
        o_ref[sl, :] = acc / den
        return carry

    lax.fori_loop(0, seq // rows_per_step, merge, 0)


def _attn_a(slopes, proj_a, unroll=8):
    B, T, _ = proj_a.shape
    n_pairs = N_HEADS_A // 2
    n_pat = len(DILATIONS)

    def spec(off):
        return pl.BlockSpec((None, T, LANES), lambda b, h: (b, 0, off + h))

    return pl.pallas_call(
        functools.partial(_attn_a_kernel, seq=T, unroll=unroll),
        grid=(B, n_pairs),
        in_specs=[pl.BlockSpec(memory_space=pltpu.SMEM), spec(0), spec(n_pairs), spec(2 * n_pairs)],
        out_specs=pl.BlockSpec((None, T, LANES), lambda b, h: (b, 0, h)),
        out_shape=jax.ShapeDtypeStruct((B, T, W_A), F32),
        scratch_shapes=[pltpu.VMEM((2, n_pat, T + max(DILATIONS), LANES), F32),
                        pltpu.VMEM((2, n_pat, T + max(DILATIONS), LANES), F32),
                        pltpu.VMEM((n_pat, 2, 2, BLK, 2 * BLK), F32),
                        pltpu.VMEM((n_pat, 5, T + BLK, LANES), BF16),
                        pltpu.VMEM((2, 2, T // BLK, BLK, 2 * BLK), BF16),
                        pltpu.VMEM((3, T, LANES), F32)],
        compiler_params=pltpu.CompilerParams(
            dimension_semantics=("parallel", "parallel"), vmem_limit_bytes=ATTN_A_VMEM_LIMIT),
        name="attn_a",
    )(slopes, proj_a, proj_a, proj_a)


def _fold(x, op):
    return functools.reduce(op, [x[r:r + 8] for r in range(0, x.shape[0], 8)])


def _dsa_kernel(kk_ref, tq_ref, vt_ref, wt_ref, o_ref, sc_ref, s_ref, acc_ref, kpos_ref,
                *, seq, k_sel, snap_unroll, slopes):
    j = pl.program_id(1)
    n_pairs = j + 1
    t_lane = j * DSA_Q + lax.broadcasted_iota(I32, (1, DSA_Q), 1)
    row = lax.broadcasted_iota(I32, (BLK, DSA_Q), 0)
    ws = wt_ref[0:N_IDX_HEADS, :] * (N_IDX_HEADS ** -0.5)
    inf = float("inf")

    def rows(c):
        return pl.ds(pl.multiple_of(c * BLK, BLK), BLK)

    def each_chunk(fn, init):
        def group(first, n, carry):
            for u in range(n):
                carry = fn(first + u, carry)
            return carry
        carry = lax.fori_loop(0, n_pairs // 2, lambda i, c: group(4 * i, 4, c), init)
        return lax.fori_loop(0, n_pairs % 2, lambda i, c: group(4 * (n_pairs // 2), 2, c), carry)

    def sub_reduce(x, op):
        return op(x, axis=0, keepdims=True)

    def score_chunk(c, carry):
        mn, mx = carry
        ki = kk_ref[rows(c), R_KI - R_KB:R_KI - R_KB + IDX_DIM]
        lgs = [_dot(ki, tq_ref[T_QI + h * IDX_DIM:T_QI + (h + 1) * IDX_DIM, :])
               for h in range(N_IDX_HEADS)]
        sc = functools.reduce(
            lambda a, b: a + b, [jnp.maximum(lg, 0.0) * ws[h:h + 1, :] for h, lg in enumerate(lgs)])
        causal = (c * BLK + row) <= t_lane
        sc_ref[rows(c), :] = jnp.where(causal, sc, -inf)
        mn = jnp.minimum(mn, _fold(jnp.where(causal, sc, inf), jnp.minimum))
        mx = jnp.maximum(mx, _fold(jnp.where(causal, sc, -inf), jnp.maximum))
        return mn, mx

    mn, mx = each_chunk(score_chunk, (jnp.full((8, DSA_Q), inf, F32), jnp.full((8, DSA_Q), -inf, F32)))
    few = t_lane < k_sel
    lo = jnp.where(few, -inf, sub_reduce(mn, jnp.min))
    hi = jnp.where(few, -inf, sub_reduce(mx, jnp.max))

    def count(pred):
        def body(c, cnt):
            return cnt + _fold(pred(sc_ref[rows(c), :], c).astype(I32), jnp.add)
        return sub_reduce(each_chunk(body, jnp.zeros((8, DSA_Q), I32)), jnp.sum)

    def halve(_, bounds):
        lo, hi, n_lo, n_hi = bounds
        mid = 0.5 * lo + 0.5 * hi
        cnt = count(lambda x, c: x >= mid)
        active = lo < hi
        up = active & (cnt >= k_sel)
        down = active & (cnt <= k_sel)
        return (jnp.where(up, mid, lo), jnp.where(down, mid, hi),
                jnp.where(up, cnt, n_lo), jnp.where(down, cnt, n_hi))

    bounds = lax.fori_loop(0, DSA_HALVINGS, halve, (lo, hi, t_lane + 1, jnp.zeros((1, DSA_Q), I32)))

    def snap(bounds):
        lo, hi, n_lo, n_hi = bounds
        mid = 0.5 * lo + 0.5 * hi
        mid = jnp.where(mid > lo, mid, hi)

        def body(c, carry):
            cnt, above, below = carry
            x = sc_ref[rows(c), :]
            ge = x >= mid
            return (cnt + _fold(ge.astype(I32), jnp.add),
                    jnp.minimum(above, _fold(jnp.where(ge, x, inf), jnp.minimum)),
                    jnp.maximum(below, _fold(jnp.where(ge, -inf, x), jnp.maximum)))

        cnt, above, below = each_chunk(body, (jnp.zeros((8, DSA_Q), I32),
                                              jnp.full((8, DSA_Q), inf, F32),
                                              jnp.full((8, DSA_Q), -inf, F32)))
        cnt = sub_reduce(cnt, jnp.sum)
        above = sub_reduce(above, jnp.min)
        below = sub_reduce(below, jnp.max)
        active = lo < hi
        enough = cnt >= k_sel
        up = active & enough
        down = active & (cnt <= k_sel)
        return (jnp.where(up, above, lo), jnp.where(down, jnp.where(enough, above, below), hi),
                jnp.where(up, cnt, n_lo), jnp.where(down, cnt, n_hi))

    def snaps(bounds):
        for _ in range(snap_unroll):
            bounds = snap(bounds)
        return bounds

    def unsettled(bounds):
        return jnp.max((bounds[0] < bounds[1]).astype(I32)) > 0

    tau, _, n_ge, n_gt = lax.while_loop(unsettled, snaps, snaps(bounds))

    need = k_sel - n_gt
    n_idx_bits = seq.bit_length() - 1
    assert 1 << n_idx_bits == seq

    def tie_break():
        def idx_step(it, jp):
            cand = jp + lax.shift_left(jnp.int32(1), n_idx_bits - 1 - it)
            below = count(lambda x, c: (x == tau) & ((c * BLK + row) < cand))
            return jnp.where(below < need, cand, jp)
        last = lax.fori_loop(0, n_idx_bits, idx_step, jnp.zeros((1, DSA_Q), I32))
        return jnp.where(tied, last, seq)

    tied = (n_ge > k_sel) & jnp.logical_not(few)
    surplus = jnp.max(tied.astype(I32)) > 0
    tie_last = lax.cond(surplus, tie_break, lambda: jnp.full((1, DSA_Q), seq, I32))

    @pl.when(j == 0)
    def _():
        pos = lax.broadcasted_iota(I32, (seq, LANES), 0)
        ln = lax.broadcasted_iota(I32, (seq, LANES), 1) - HEAD_DIM
        feat = jnp.where(ln == 0, pos // POS_SPLIT,
                         jnp.where(ln == 1, pos % POS_SPLIT, jnp.where((ln == 2) | (ln == 3), 1, 0)))
        kpos_ref[...] = kk_ref[:, 0:LANES] + feat.astype(BF16)

    q_row = lax.broadcasted_iota(I32, (HEAD_DIM, DSA_Q), 0)
    t_hi = (t_lane // POS_SPLIT).astype(F32)
    t_lo = (t_lane % POS_SPLIT).astype(F32)
    q_aug = []
    for h in range(N_HEADS_B):
        feat = jnp.where(q_row == 0, POS_SPLIT * slopes[h],
                         jnp.where(q_row == 1, slopes[h],
                                   jnp.where(q_row == 2, -POS_SPLIT * slopes[h] * t_hi,
                                             jnp.where(q_row == 3, -slopes[h] * t_lo, 0.0))))
        q_aug.append(jnp.concatenate(
            [tq_ref[T_QB + h * HEAD_DIM:T_QB + (h + 1) * HEAD_DIM, :], feat.astype(BF16)], axis=0))

    def qk_chunk(c, ms):
        kb = kpos_ref[rows(c), :]
        s_idx = c * BLK + row
        x = sc_ref[rows(c), :]
        tie = jnp.where(x == tau, jnp.where(s_idx <= tie_last, 0.0, NEG), NEG)
        sb = jnp.where(s_idx <= t_lane, jnp.where(x > tau, 0.0, tie), NEG)
        out = []
        for h in range(N_HEADS_B):
            s = _dot(kb, q_aug[h]) + sb
            s_ref[h, rows(c), :] = s
            out.append(jnp.maximum(ms[h], _fold(s, jnp.maximum)))
        return tuple(out)

    ms = each_chunk(qk_chunk, tuple(jnp.full((8, DSA_Q), NEG, F32) for _ in range(N_HEADS_B)))
    ms = [sub_reduce(m, jnp.max) for m in ms]

    acc_ref[...] = jnp.zeros((W_B, DSA_Q), F32)

    def exp_pv(first_pair, n, ls):
        ls = list(ls)
        pvs = []
        for u in range(n):
            pair = first_pair + u
            ps = []
            for h in range(N_HEADS_B):
                halves = []
                for c in (2 * pair, 2 * pair + 1):
                    p = jnp.exp(s_ref[h, rows(c), :] - ms[h])
                    ls[h] = ls[h] + _fold(p, jnp.add)
                    halves.append(p.astype(BF16))
                ps.append(jnp.concatenate(halves, axis=0))
            vt = vt_ref[:, pl.ds(pl.multiple_of(pair * 2 * BLK, 2 * BLK), 2 * BLK)]
            pvs.append([_dot(vt, ps[h]) for h in range(N_HEADS_B)])
        for h in range(N_HEADS_B):
            acc_ref[h * HEAD_DIM:(h + 1) * HEAD_DIM, :] += functools.reduce(
                lambda a, b: a + b, [pv[h] for pv in pvs])
        return tuple(ls)

    ls = tuple(jnp.zeros((8, DSA_Q), F32) for _ in range(N_HEADS_B))
    ls = lax.fori_loop(0, n_pairs // 2, lambda i, c: exp_pv(2 * i, 2, c), ls)
    ls = lax.fori_loop(0, n_pairs % 2, lambda i, c: exp_pv(n_pairs - 1, 1, c), ls)
    l_all = jnp.concatenate(
        [jnp.broadcast_to(sub_reduce(ls[h], jnp.sum), (HEAD_DIM, DSA_Q)) for h in range(N_HEADS_B)],
        axis=0)
    o_ref[...] = (acc_ref[...] / l_all).T


def _dsa(proj_r, proj_t, w_t):
    B, T, _ = proj_r.shape
    k_sel = min(TOPK_MAX, T // 4)
    slopes = _static_alibi_slopes(N_HEADS_B)
    assert all(math.frexp(s)[0] == 0.5 for s in slopes) and T // POS_SPLIT <= 256 and POS_SPLIT <= 256
    return pl.pallas_call(
        functools.partial(_dsa_kernel, seq=T, k_sel=k_sel, snap_unroll=4, slopes=slopes),
        grid=(B, T // DSA_Q),
        in_specs=[
            pl.BlockSpec((None, T, 2 * LANES), lambda b, q: (b, 0, R_KB // (2 * LANES))),
            pl.BlockSpec((None, T_VB, DSA_Q), lambda b, q: (b, 0, q)),
            pl.BlockSpec((None, HEAD_DIM, T), lambda b, q: (b, T_VB // HEAD_DIM, 0)),
            pl.BlockSpec((None, WI_ROWS, DSA_Q), lambda b, q: (b, 0, q)),
        ],
        out_specs=pl.BlockSpec((None, DSA_Q, W_B), lambda b, q: (b, q, 0)),
        out_shape=jax.ShapeDtypeStruct((B, T, W_B), F32),
        scratch_shapes=[pltpu.VMEM((T, DSA_Q), F32),
                        pltpu.VMEM((N_HEADS_B, T, DSA_Q), F32),
                        pltpu.VMEM((W_B, DSA_Q), F32),
                        pltpu.VMEM((T, LANES), BF16)],
        compiler_params=pltpu.CompilerParams(
            dimension_semantics=("parallel", "arbitrary"), vmem_limit_bytes=VMEM_LIMIT),
        name="dsa",
    )(proj_r, proj_t, proj_t, w_t)


def _mem_kv_kernel(mem_ref, g_ref, w_ref, kv_ref):
    kv_ref[0] = _dot(_rms(mem_ref[0], g_ref[...]).astype(BF16), w_ref[...]).astype(BF16)


def _mem_kv(mem, g, w):
    B, M, D = mem.shape
    return pl.pallas_call(
        _mem_kv_kernel,
        grid=(B,),
        in_specs=[
            pl.BlockSpec((1, M, D), lambda b: (b, 0, 0)),
            pl.BlockSpec((1, D), lambda b: (0, 0)),
            pl.BlockSpec(w.shape, lambda b: (0, 0)),
        ],
        out_specs=pl.BlockSpec((1, M, 2 * W_M), lambda b: (b, 0, 0)),
        out_shape=jax.ShapeDtypeStruct((B, M, 2 * W_M), BF16),
        compiler_params=pltpu.CompilerParams(
            dimension_semantics=("parallel",), vmem_limit_bytes=VMEM_LIMIT),
        name="mem_kv",
    )(mem, g, w)


def _out_kernel(x_ref, oa_ref, ob_ref, gate_ref, qm_ref, kv_ref, wo_ref, g_ref, y_ref, mix_ref, *, sub):
    tm = x_ref.shape[0]
    subs = [slice(r0, r0 + sub) for r0 in range(0, tm, sub)]
    lane = lax.broadcasted_iota(I32, (1, W_M), 1)
    head_lanes = [(lane >= h * HEAD_DIM) & (lane < (h + 1) * HEAD_DIM) for h in range(N_HEADS_MEM)]
    km = kv_ref[:, 0:W_M]
    vm = kv_ref[:, W_M:2 * W_M]
    zero = jnp.zeros((), BF16)
    vm_heads = [jnp.where(head_lanes[h], vm, zero) for h in range(N_HEADS_MEM)]

    def silu(g):
        return g * (1.0 / (1.0 + jnp.exp(-g)))

    def gated(o, rows, lo, width):
        return (o * silu(gate_ref[rows, lo:lo + width].astype(F32))).astype(BF16)

    for rows in subs:
        qm = qm_ref[rows, :]
        om = None
        for h in range(N_HEADS_MEM):
            s = _dot_nt(jnp.where(head_lanes[h], qm, zero), km)
            p = jnp.exp(s - jnp.max(s, axis=1, keepdims=True))
            p = p / jnp.sum(p, axis=1, keepdims=True)
            o = _dot(p.astype(BF16), vm_heads[h])
            om = o if om is None else om + o
        mix_ref[rows, W_A + W_B:] = gated(om, rows, W_A + W_B, W_M)

    for rows in subs:
        half = W_A // 2
        mix_ref[rows, 0:half] = gated(oa_ref[rows, 0:half], rows, 0, half)
        mix_ref[rows, half:W_A] = gated(oa_ref[rows, half:W_A], rows, half, half)
        mix_ref[rows, W_A:W_A + W_B] = gated(ob_ref[rows, :], rows, W_A, W_B)
        y_ref[rows, :] = _dot(mix_ref[rows, :], wo_ref[...])

    for rows in subs:
        y_ref[rows, :] = _rms(x_ref[rows, :] + y_ref[rows, :], g_ref[...])


def _out(x, o_a, o_b, proj_r, kv_m, w_out, g, tm=512, sub=256):
    B, T, D = x.shape
    M = kv_m.shape[1]
    row = lambda b, i: (b, i, 0)
    return pl.pallas_call(
        functools.partial(_out_kernel, sub=sub),
        grid=(B, T // tm),
        in_specs=[
            pl.BlockSpec((None, tm, D), row),
            pl.BlockSpec((None, tm, W_A), row),
            pl.BlockSpec((None, tm, W_B), row),
            pl.BlockSpec((None, tm, MIX_WIDTH), lambda b, i: (b, i, R_GATE // MIX_WIDTH)),
            pl.BlockSpec((None, tm, W_M), lambda b, i: (b, i, R_QM // W_M)),
            pl.BlockSpec((None, M, 2 * W_M), lambda b, i: (b, 0, 0)),
            pl.BlockSpec(w_out.shape, lambda b, i: (0, 0)),
            pl.BlockSpec((1, D), lambda b, i: (0, 0)),
        ],
        out_specs=pl.BlockSpec((None, tm, D), row),
        out_shape=jax.ShapeDtypeStruct((B, T, D), F32),
        scratch_shapes=[pltpu.VMEM((tm, MIX_WIDTH), BF16)],
        compiler_params=pltpu.CompilerParams(
            dimension_semantics=("parallel", "parallel"), vmem_limit_bytes=VMEM_LIMIT),
        name="out",
    )(x, o_a, o_b, proj_r, proj_r, kv_m, w_out, g)


def _alibi_slopes(n):
    return 2.0 ** (-8.0 * jnp.arange(1, n + 1, dtype=F32) / n)


def _static_alibi_slopes(n):
    return tuple(2.0 ** (-8.0 * i / n) for i in range(1, n + 1))


def _split_weights(w):
    bounds = [0]
    for s in SPLIT_SIZES:
        bounds.append(bounds[-1] + s)
    q_a, k_a, v_a, q_b, k_b, v_b, q_m, gate, q_i, k_i, w_i = (
        w[:, bounds[i]:bounds[i + 1]] for i in range(len(SPLIT_SIZES)))
    scale = HEAD_DIM ** -0.5
    idx_scale = IDX_DIM ** -0.5
    zeros = jnp.zeros((w.shape[0], LANES - HEAD_DIM), w.dtype)
    wa = jnp.concatenate([q_a * scale, k_a, v_a], axis=1)
    wr = jnp.concatenate([gate, q_m * scale, k_b, zeros, k_i, zeros], axis=1)
    wt = jnp.concatenate([q_i * idx_scale, q_b * scale, v_b], axis=1).T
    ww = jnp.concatenate([w_i.T, jnp.zeros((WI_ROWS - N_IDX_HEADS, w.shape[0]), w.dtype)], axis=0)
    return wa.astype(BF16), wr.astype(BF16), wt.astype(BF16), ww.astype(BF16)


def kernel(x, mem, g_in, g_mem, w_in, w_mem_kv, w_out, g_final):
    assert g_in.shape[0] == 1, "single-layer block: the final RMSNorm is fused into the output kernel"
    wa, wr, wt, ww = _split_weights(w_in[0])
    proj_a, proj_r, proj_t, w_t = _proj(x, g_in, wa, wr, wt, ww)
    o_a = _attn_a(_alibi_slopes(N_HEADS_A), proj_a)
    o_b = _dsa(proj_r, proj_t, w_t)
    kv_m = _mem_kv(mem, g_mem, w_mem_kv[0].astype(BF16))
    return _out(x, o_a, o_b, proj_r, kv_m, w_out[0].astype(BF16), g_final[None, :])
```

```python
import functools
import math

import jax
import jax.numpy as jnp
from jax import lax
from jax.experimental import pallas as pl
from jax.experimental.pallas import tpu as pltpu

F32 = jnp.float32
BF16 = jnp.bfloat16
I32 = jnp.int32

D_MODEL = 1024
HEAD_DIM = 64
N_HEADS_A = 8
N_HEADS_B = 4
N_HEADS_MEM = 4
W_A = N_HEADS_A * HEAD_DIM
W_B = N_HEADS_B * HEAD_DIM
W_M = N_HEADS_MEM * HEAD_DIM
MIX_WIDTH = W_A + W_B + W_M
DILATIONS = (1, 4, 16)
BAND = 128
N_IDX_HEADS = 8
IDX_DIM = 64
TOPK_MAX = 256
RMS_EPS = 1e-6
SPLIT_SIZES = (W_A, W_A, W_A, W_B, HEAD_DIM, HEAD_DIM, W_M, MIX_WIDTH,
               N_IDX_HEADS * IDX_DIM, IDX_DIM, N_IDX_HEADS)

LANES = 128
BLK = 128
NEG = -1e30
VMEM_LIMIT = 48 * 1024 * 1024

R_GATE = 0
R_QM = MIX_WIDTH
R_KB = R_QM + W_M
R_KI = R_KB + LANES
R_COLS = R_KI + LANES
T_QI = 0
T_QB = N_IDX_HEADS * IDX_DIM
T_VB = T_QB + W_B
T_ROWS = T_VB + HEAD_DIM
WI_ROWS = 16
DSA_Q = 256
DSA_HALVINGS = 12
POS_SPLIT = 64


def _dot(a, b):
    return jnp.dot(a, b, preferred_element_type=F32)


def _dot_nt(a, b):
    return lax.dot_general(a, b, (((1,), (1,)), ((), ())), preferred_element_type=F32)


def _rms(x, g):
    return x * lax.rsqrt(jnp.mean(x * x, axis=-1, keepdims=True) + RMS_EPS) * g


def _proj_kernel(x_ref, g_ref, wa_ref, wr_ref, wt_ref, ww_ref,
                 oa_ref, oa4_ref, oa16_ref, or_ref, ot_ref, ow_ref, res_ref, res4_ref):
    hb = _rms(x_ref[0], g_ref[...]).astype(BF16)
    tm = hb.shape[0]
    d4, d16 = DILATIONS[1], DILATIONS[2]
    step = d16 // d4
    res = _dot(hb, wa_ref[...])
    oa_ref[0] = res.astype(BF16)
    for grp in range(res.shape[1] // LANES):
        lanes = slice(grp * LANES, (grp + 1) * LANES)
        res_ref[grp] = res[:, lanes]
        for r in range(d4):
            rows4 = res_ref[grp, pl.ds(r, tm // d4, stride=d4), :]
            oa4_ref[0, r, :, lanes] = rows4.astype(BF16)
            res4_ref[grp, r * (tm // d4):(r + 1) * (tm // d4), :] = rows4
        for r in range(d16):
            start = (r % d4) * (tm // d4) + r // d4
            oa16_ref[0, r, :, lanes] = res4_ref[grp, pl.ds(start, tm // d16, stride=step), :].astype(BF16)
    or_ref[0] = _dot(hb, wr_ref[...]).astype(BF16)
    ot_ref[0] = _dot_nt(wt_ref[...], hb).astype(BF16)
    ow_ref[0] = _dot_nt(ww_ref[...], hb)


def _proj(x, g, wa, wr, wt, ww, tm=512):
    B, T, D = x.shape
    const = lambda b, i: (0, 0)
    d4, d16 = DILATIONS[1], DILATIONS[2]
    nat, cm4, cm16, proj_r, proj_t, w_t = pl.pallas_call(
        _proj_kernel,
        grid=(B, T // tm),
        in_specs=[
            pl.BlockSpec((1, tm, D), lambda b, i: (b, i, 0)),
            pl.BlockSpec((1, D), const),
            pl.BlockSpec(wa.shape, const),
            pl.BlockSpec(wr.shape, const),
            pl.BlockSpec(wt.shape, const),
            pl.BlockSpec(ww.shape, const),
        ],
        out_specs=[
            pl.BlockSpec((1, tm, 3 * W_A), lambda b, i: (b, i, 0)),
            pl.BlockSpec((1, d4, tm // d4, 3 * W_A), lambda b, i: (b, 0, i, 0)),
            pl.BlockSpec((1, d16, tm // d16, 3 * W_A), lambda b, i: (b, 0, i, 0)),
            pl.BlockSpec((1, tm, R_COLS), lambda b, i: (b, i, 0)),
            pl.BlockSpec((1, T_ROWS, tm), lambda b, i: (b, 0, i)),
            pl.BlockSpec((1, WI_ROWS, tm), lambda b, i: (b, 0, i)),
        ],
        out_shape=[
            jax.ShapeDtypeStruct((B, T, 3 * W_A), BF16),
            jax.ShapeDtypeStruct((B, d4, T // d4, 3 * W_A), BF16),
            jax.ShapeDtypeStruct((B, d16, T // d16, 3 * W_A), BF16),
            jax.ShapeDtypeStruct((B, T, R_COLS), BF16),
            jax.ShapeDtypeStruct((B, T_ROWS, T), BF16),
            jax.ShapeDtypeStruct((B, WI_ROWS, T), F32),
        ],
        scratch_shapes=[pltpu.VMEM((3 * W_A // LANES, tm, LANES), F32)] * 2,
        compiler_params=pltpu.CompilerParams(
            dimension_semantics=("parallel", "parallel"), vmem_limit_bytes=VMEM_LIMIT),
        name="proj",
    )(x, g, wa, wr, wt, ww)
    return (nat, cm4.reshape(nat.shape), cm16.reshape(nat.shape)), proj_r, proj_t, w_t


def _attn_a_kernel(slopes_ref, *refs, seq, unroll):
    n_pat = len(DILATIONS)
    qkv = [refs[3 * p:3 * p + 3] for p in range(n_pat)]
    o_ref, acc_ref, m_ref, bias_ref, p_ref = refs[3 * n_pat:]
    hp = pl.program_id(1)
    lane = lax.broadcasted_iota(I32, (1, LANES), 1)
    head_lanes = (lane < HEAD_DIM, lane >= HEAD_DIM)
    row = lax.broadcasted_iota(I32, (BLK, BLK), 0)
    col = lax.broadcasted_iota(I32, (BLK, BLK), 1)
    d_cur = (row - col).astype(F32)
    d_prev = d_cur + float(BAND)
    n_all = seq // BLK
    zero = jnp.zeros((), BF16)
    one = jnp.ones((), BF16)

    def n_blocks(p):
        return n_all // DILATIONS[p]

    def n_keys(p):
        return BLK if n_blocks(p) == 1 else 2 * BLK

    def block(g):
        return pl.ds(pl.multiple_of(g * BLK, BLK), BLK)

    def keys(p, g):
        return pl.ds(pl.multiple_of(jnp.maximum((g + 1) * BLK - n_keys(p), 0), BLK), n_keys(p))

    def variant(p, g):
        g = jnp.asarray(g, I32)
        return jnp.where(g == 0, 2, jnp.where(g % n_blocks(p) == 0, 1, 0))

    def pitch(dil):
        return seq // dil + 1 if seq // dil == BLK and dil > 1 else None

    def put(ref, h, p, g, val):
        dil = DILATIONS[p]
        start = g // n_blocks(p) + (g % n_blocks(p)) * (dil * BLK)
        if dil == 1:
            ref[h, p, pl.ds(pl.multiple_of(start, BLK), BLK), :] = val
        elif pitch(dil):
            ref[h, p, pl.ds(start * pitch(dil), BLK), :] = val
        else:
            ref[h, p, pl.ds(start, BLK, stride=dil), :] = val

    def probs(p):
        q_ref, k_ref, _ = qkv[p]

        def body(g):
            kw = k_ref[keys(p, g), :]
            q = q_ref[block(g), :]
            for h in range(2):
                s = _dot_nt(jnp.where(head_lanes[h], q, zero), kw)
                s = s + bias_ref[p, h, variant(p, g), :, 2 * BLK - n_keys(p):]
                m = jnp.max(s, axis=1, keepdims=True)
                p_ref[p % 2, h, g, :, :n_keys(p)] = jnp.exp(s - m).astype(BF16)
                put(m_ref, h, p, g, jnp.broadcast_to(m, (BLK, LANES)))
        return body

    def values(p):
        v_ref = qkv[p][2]

        def body(g):
            vw = v_ref[keys(p, g), :]
            for h in range(2):
                put(acc_ref, h, p, g,
                    _dot(p_ref[p % 2, h, g, :, :n_keys(p)], jnp.where(head_lanes[h], vw, one)))
        return body

    def run(*bodies):
        def step(g, carry):
            for body in bodies:
                body(g)
            return carry
        lax.fori_loop(0, n_all, step, 0, unroll=unroll)

    for p, dil in enumerate(DILATIONS):
        for h in range(2):
            sd = slopes_ref[hp * 2 + h] * float(dil)
            cur = jnp.where(row >= col, -sd * d_cur, NEG)
            masked = jnp.full((BLK, BLK), NEG, F32)
            bias_ref[p, h, 0, :, BLK:] = cur
            bias_ref[p, h, 1, :, BLK:] = cur
            bias_ref[p, h, 2, :, BLK:] = cur if n_keys(p) == BLK else masked
            if n_keys(p) > BLK:
                bias_ref[p, h, 0, :, :BLK] = jnp.where(col >= row, -sd * d_prev, NEG)
                bias_ref[p, h, 1, :, :BLK] = masked
                bias_ref[p, h, 2, :, :BLK] = cur

    assert n_pat == 3
    run(probs(0))
    run(values(0), probs(1))
    run(values(1), probs(2))
    run(values(2))

    rows_per_step = 2 * BLK

    def natural_rows(ref, h, p, i):
        dil = DILATIONS[p]
        if not pitch(dil):
            return ref[h, p, pl.ds(pl.multiple_of(i * rows_per_step, rows_per_step), rows_per_step), :]
        per = rows_per_step // dil
        return jnp.concatenate(
            [ref[h, p, pl.ds(i * per + u, dil, stride=pitch(dil)), :] for u in range(per)], axis=0)

    def merge(i, carry):
        sl = pl.ds(pl.multiple_of(i * rows_per_step, rows_per_step), rows_per_step)
        nums = []
        for h in range(2):
            ms = [natural_rows(m_ref, h, p, i) for p in range(n_pat)]
            mx = functools.reduce(jnp.maximum, ms)
            nums.append(sum(jnp.exp(ms[p] - mx) * natural_rows(acc_ref, h, p, i) for p in range(n_pat)))
        acc = jnp.where(head_lanes[0], nums[0], nums[1])
        den = pltpu.roll(jnp.where(head_lanes[0], nums[1], nums[0]), HEAD_DIM, axis=1)
        o_ref[sl, :] = acc / den
        return carry

    lax.fori_loop(0, seq // rows_per_step, merge, 0)


def _attn_a(slopes, qkv_layouts, unroll=8):
    B, T, _ = qkv_layouts[0].shape
    n_pairs = N_HEADS_A // 2
    n_pat = len(DILATIONS)

    def spec(off):
        return pl.BlockSpec((None, T, LANES), lambda b, h: (b, 0, off + h))

    return pl.pallas_call(
        functools.partial(_attn_a_kernel, seq=T, unroll=unroll),
        grid=(B, n_pairs),
        in_specs=[pl.BlockSpec(memory_space=pltpu.SMEM)]
        + [spec(part * n_pairs) for _ in range(n_pat) for part in range(3)],
        out_specs=pl.BlockSpec((None, T, LANES), lambda b, h: (b, 0, h)),
        out_shape=jax.ShapeDtypeStruct((B, T, W_A), F32),
        scratch_shapes=[pltpu.VMEM((2, n_pat, T + max(DILATIONS), LANES), F32),
                        pltpu.VMEM((2, n_pat, T + max(DILATIONS), LANES), F32),
                        pltpu.VMEM((n_pat, 2, 3, BLK, 2 * BLK), F32),
                        pltpu.VMEM((2, 2, T // BLK, BLK, 2 * BLK), BF16)],
        compiler_params=pltpu.CompilerParams(
            dimension_semantics=("parallel", "parallel"), vmem_limit_bytes=VMEM_LIMIT),
        name="attn_a",
    )(slopes, *[a for a in qkv_layouts for _ in range(3)])


def _fold(x, op):
    return functools.reduce(op, [x[r:r + 8] for r in range(0, x.shape[0], 8)])


def _dsa_kernel(kk_ref, tq_ref, vt_ref, wt_ref, o_ref, sc_ref, s_ref, acc_ref, kpos_ref,
                *, seq, k_sel, snap_unroll, slopes):
    j = pl.program_id(1)
    n_pairs = j + 1
    t_lane = j * DSA_Q + lax.broadcasted_iota(I32, (1, DSA_Q), 1)
    row = lax.broadcasted_iota(I32, (BLK, DSA_Q), 0)
    ws = wt_ref[0:N_IDX_HEADS, :] * (N_IDX_HEADS ** -0.5)
    inf = float("inf")

    def rows(c):
        return pl.ds(pl.multiple_of(c * BLK, BLK), BLK)

    def each_chunk(fn, init):
        def group(first, n, carry):
            for u in range(n):
                carry = fn(first + u, carry)
            return carry
        carry = lax.fori_loop(0, n_pairs // 2, lambda i, c: group(4 * i, 4, c), init)
        return lax.fori_loop(0, n_pairs % 2, lambda i, c: group(4 * (n_pairs // 2), 2, c), carry)

    def sub_reduce(x, op):
        return op(x, axis=0, keepdims=True)

    def score_chunk(c, carry):
        mn, mx = carry
        ki = kk_ref[rows(c), R_KI - R_KB:R_KI - R_KB + IDX_DIM]
        lgs = [_dot(ki, tq_ref[T_QI + h * IDX_DIM:T_QI + (h + 1) * IDX_DIM, :])
               for h in range(N_IDX_HEADS)]
        sc = functools.reduce(
            lambda a, b: a + b, [jnp.maximum(lg, 0.0) * ws[h:h + 1, :] for h, lg in enumerate(lgs)])
        causal = (c * BLK + row) <= t_lane
        sc_ref[rows(c), :] = jnp.where(causal, sc, -inf)
        mn = jnp.minimum(mn, _fold(jnp.where(causal, sc, inf), jnp.minimum))
        mx = jnp.maximum(mx, _fold(jnp.where(causal, sc, -inf), jnp.maximum))
        return mn, mx

    mn, mx = each_chunk(score_chunk, (jnp.full((8, DSA_Q), inf, F32), jnp.full((8, DSA_Q), -inf, F32)))
    few = t_lane < k_sel
    lo = jnp.where(few, -inf, sub_reduce(mn, jnp.min))
    hi = jnp.where(few, -inf, sub_reduce(mx, jnp.max))

    def count(pred):
        def body(c, cnt):
            return cnt + _fold(pred(sc_ref[rows(c), :], c).astype(I32), jnp.add)
        return sub_reduce(each_chunk(body, jnp.zeros((8, DSA_Q), I32)), jnp.sum)

    def halve(_, bounds):
        lo, hi, n_lo, n_hi = bounds
        mid = 0.5 * lo + 0.5 * hi
        cnt = count(lambda x, c: x >= mid)
        active = lo < hi
        up = active & (cnt >= k_sel)
        down = active & (cnt <= k_sel)
        return (jnp.where(up, mid, lo), jnp.where(down, mid, hi),
                jnp.where(up, cnt, n_lo), jnp.where(down, cnt, n_hi))

    bounds = lax.fori_loop(0, DSA_HALVINGS, halve, (lo, hi, t_lane + 1, jnp.zeros((1, DSA_Q), I32)))

    def snap(bounds):
        lo, hi, n_lo, n_hi = bounds
        mid = 0.5 * lo + 0.5 * hi
        mid = jnp.where(mid > lo, mid, hi)

        def body(c, carry):
            cnt, above, below = carry
            x = sc_ref[rows(c), :]
            ge = x >= mid
            return (cnt + _fold(ge.astype(I32), jnp.add),
                    jnp.minimum(above, _fold(jnp.where(ge, x, inf), jnp.minimum)),
                    jnp.maximum(below, _fold(jnp.where(ge, -inf, x), jnp.maximum)))

        cnt, above, below = each_chunk(body, (jnp.zeros((8, DSA_Q), I32),
                                              jnp.full((8, DSA_Q), inf, F32),
                                              jnp.full((8, DSA_Q), -inf, F32)))
        cnt = sub_reduce(cnt, jnp.sum)
        above = sub_reduce(above, jnp.min)
        below = sub_reduce(below, jnp.max)
        active = lo < hi
        enough = cnt >= k_sel
        up = active & enough
        down = active & (cnt <= k_sel)
        return (jnp.where(up, above, lo), jnp.where(down, jnp.where(enough, above, below), hi),
                jnp.where(up, cnt, n_lo), jnp.where(down, cnt, n_hi))

    def snaps(bounds):
        for _ in range(snap_unroll):
            bounds = snap(bounds)
        return bounds

    def unsettled(bounds):
        return jnp.max((bounds[0] < bounds[1]).astype(I32)) > 0

    tau, _, n_ge, n_gt = lax.while_loop(unsettled, snaps, snaps(bounds))

    need = k_sel - n_gt
    n_idx_bits = seq.bit_length() - 1
    assert 1 << n_idx_bits == seq

    def tie_break():
        def idx_step(it, jp):
            cand = jp + lax.shift_left(jnp.int32(1), n_idx_bits - 1 - it)
            below = count(lambda x, c: (x == tau) & ((c * BLK + row) < cand))
            return jnp.where(below < need, cand, jp)
        last = lax.fori_loop(0, n_idx_bits, idx_step, jnp.zeros((1, DSA_Q), I32))
        return jnp.where(tied, last, seq)

    tied = (n_ge > k_sel) & jnp.logical_not(few)
    surplus = jnp.max(tied.astype(I32)) > 0
    tie_last = lax.cond(surplus, tie_break, lambda: jnp.full((1, DSA_Q), seq, I32))

    @pl.when(j == 0)
    def _():
        pos = lax.broadcasted_iota(I32, (seq, LANES), 0)
        ln = lax.broadcasted_iota(I32, (seq, LANES), 1) - HEAD_DIM
        feat = jnp.where(ln == 0, pos // POS_SPLIT,
                         jnp.where(ln == 1, pos % POS_SPLIT, jnp.where((ln == 2) | (ln == 3), 1, 0)))
        kpos_ref[...] = kk_ref[:, 0:LANES] + feat.astype(BF16)

    q_row = lax.broadcasted_iota(I32, (HEAD_DIM, DSA_Q), 0)
    t_hi = (t_lane // POS_SPLIT).astype(F32)
    t_lo = (t_lane % POS_SPLIT).astype(F32)
    q_aug = []
    for h in range(N_HEADS_B):
        feat = jnp.where(q_row == 0, POS_SPLIT * slopes[h],
                         jnp.where(q_row == 1, slopes[h],
                                   jnp.where(q_row == 2, -POS_SPLIT * slopes[h] * t_hi,
                                             jnp.where(q_row == 3, -slopes[h] * t_lo, 0.0))))
        q_aug.append(jnp.concatenate(
            [tq_ref[T_QB + h * HEAD_DIM:T_QB + (h + 1) * HEAD_DIM, :], feat.astype(BF16)], axis=0))

    def qk_chunk(c, ms):
        kb = kpos_ref[rows(c), :]
        s_idx = c * BLK + row
        x = sc_ref[rows(c), :]
        tie = jnp.where(x == tau, jnp.where(s_idx <= tie_last, 0.0, NEG), NEG)
        sb = jnp.where(s_idx <= t_lane, jnp.where(x > tau, 0.0, tie), NEG)
        out = []
        for h in range(N_HEADS_B):
            s = _dot(kb, q_aug[h]) + sb
            s_ref[h, rows(c), :] = s
            out.append(jnp.maximum(ms[h], _fold(s, jnp.maximum)))
        return tuple(out)

    ms = each_chunk(qk_chunk, tuple(jnp.full((8, DSA_Q), NEG, F32) for _ in range(N_HEADS_B)))
    ms = [sub_reduce(m, jnp.max) for m in ms]

    acc_ref[...] = jnp.zeros((W_B, DSA_Q), F32)

    def exp_pv(first_pair, n, ls):
        ls = list(ls)
        pvs = []
        for u in range(n):
            pair = first_pair + u
            ps = []
            for h in range(N_HEADS_B):
                halves = []
                for c in (2 * pair, 2 * pair + 1):
                    p = jnp.exp(s_ref[h, rows(c), :] - ms[h])
                    ls[h] = ls[h] + _fold(p, jnp.add)
                    halves.append(p.astype(BF16))
                ps.append(jnp.concatenate(halves, axis=0))
            vt = vt_ref[:, pl.ds(pl.multiple_of(pair * 2 * BLK, 2 * BLK), 2 * BLK)]
            pvs.append([_dot(vt, ps[h]) for h in range(N_HEADS_B)])
        for h in range(N_HEADS_B):
            acc_ref[h * HEAD_DIM:(h + 1) * HEAD_DIM, :] += functools.reduce(
                lambda a, b: a + b, [pv[h] for pv in pvs])
        return tuple(ls)

    ls = tuple(jnp.zeros((8, DSA_Q), F32) for _ in range(N_HEADS_B))
    ls = lax.fori_loop(0, n_pairs // 2, lambda i, c: exp_pv(2 * i, 2, c), ls)
    ls = lax.fori_loop(0, n_pairs % 2, lambda i, c: exp_pv(n_pairs - 1, 1, c), ls)
    l_all = jnp.concatenate(
        [jnp.broadcast_to(sub_reduce(ls[h], jnp.sum), (HEAD_DIM, DSA_Q)) for h in range(N_HEADS_B)],
        axis=0)
    o_ref[...] = (acc_ref[...] / l_all).T


def _dsa(proj_r, proj_t, w_t):
    B, T, _ = proj_r.shape
    k_sel = min(TOPK_MAX, T // 4)
    slopes = _static_alibi_slopes(N_HEADS_B)
    assert all(math.frexp(s)[0] == 0.5 for s in slopes) and T // POS_SPLIT <= 256 and POS_SPLIT <= 256
    return pl.pallas_call(
        functools.partial(_dsa_kernel, seq=T, k_sel=k_sel, snap_unroll=4, slopes=slopes),
        grid=(B, T // DSA_Q),
        in_specs=[
            pl.BlockSpec((None, T, 2 * LANES), lambda b, q: (b, 0, R_KB // (2 * LANES))),
            pl.BlockSpec((None, T_VB, DSA_Q), lambda b, q: (b, 0, q)),
            pl.BlockSpec((None, HEAD_DIM, T), lambda b, q: (b, T_VB // HEAD_DIM, 0)),
            pl.BlockSpec((None, WI_ROWS, DSA_Q), lambda b, q: (b, 0, q)),
        ],
        out_specs=pl.BlockSpec((None, DSA_Q, W_B), lambda b, q: (b, q, 0)),
        out_shape=jax.ShapeDtypeStruct((B, T, W_B), F32),
        scratch_shapes=[pltpu.VMEM((T, DSA_Q), F32),
                        pltpu.VMEM((N_HEADS_B, T, DSA_Q), F32),
                        pltpu.VMEM((W_B, DSA_Q), F32),
                        pltpu.VMEM((T, LANES), BF16)],
        compiler_params=pltpu.CompilerParams(
            dimension_semantics=("parallel", "arbitrary"), vmem_limit_bytes=VMEM_LIMIT),
        name="dsa",
    )(proj_r, proj_t, proj_t, w_t)


def _mem_kv_kernel(mem_ref, g_ref, w_ref, kv_ref):
    kv_ref[0] = _dot(_rms(mem_ref[0], g_ref[...]).astype(BF16), w_ref[...]).astype(BF16)


def _mem_kv(mem, g, w):
    B, M, D = mem.shape
    return pl.pallas_call(
        _mem_kv_kernel,
        grid=(B,),
        in_specs=[
            pl.BlockSpec((1, M, D), lambda b: (b, 0, 0)),
            pl.BlockSpec((1, D), lambda b: (0, 0)),
            pl.BlockSpec(w.shape, lambda b: (0, 0)),
        ],
        out_specs=pl.BlockSpec((1, M, 2 * W_M), lambda b: (b, 0, 0)),
        out_shape=jax.ShapeDtypeStruct((B, M, 2 * W_M), BF16),
        compiler_params=pltpu.CompilerParams(
            dimension_semantics=("parallel",), vmem_limit_bytes=VMEM_LIMIT),
        name="mem_kv",
    )(mem, g, w)


def _out_kernel(x_ref, oa_ref, ob_ref, gate_ref, qm_ref, kv_ref, wo_ref, g_ref, y_ref, mix_ref, *, sub):
    tm = x_ref.shape[0]
    subs = [slice(r0, r0 + sub) for r0 in range(0, tm, sub)]
    lane = lax.broadcasted_iota(I32, (1, W_M), 1)
    head_lanes = [(lane >= h * HEAD_DIM) & (lane < (h + 1) * HEAD_DIM) for h in range(N_HEADS_MEM)]
    km = kv_ref[:, 0:W_M]
    vm = kv_ref[:, W_M:2 * W_M]
    zero = jnp.zeros((), BF16)
    vm_heads = [jnp.where(head_lanes[h], vm, zero) for h in range(N_HEADS_MEM)]

    def silu(g):
        return g * (1.0 / (1.0 + jnp.exp(-g)))

    def gated(o, rows, lo, width):
        return (o * silu(gate_ref[rows, lo:lo + width].astype(F32))).astype(BF16)

    for rows in subs:
        qm = qm_ref[rows, :]
        om = None
        for h in range(N_HEADS_MEM):
            s = _dot_nt(jnp.where(head_lanes[h], qm, zero), km)
            p = jnp.exp(s - jnp.max(s, axis=1, keepdims=True))
            p = p / jnp.sum(p, axis=1, keepdims=True)
            o = _dot(p.astype(BF16), vm_heads[h])
            om = o if om is None else om + o
        mix_ref[rows, W_A + W_B:] = gated(om, rows, W_A + W_B, W_M)

    for rows in subs:
        half = W_A // 2
        mix_ref[rows, 0:half] = gated(oa_ref[rows, 0:half], rows, 0, half)
        mix_ref[rows, half:W_A] = gated(oa_ref[rows, half:W_A], rows, half, half)
        mix_ref[rows, W_A:W_A + W_B] = gated(ob_ref[rows, :], rows, W_A, W_B)
        y_ref[rows, :] = _dot(mix_ref[rows, :], wo_ref[...])

    for rows in subs:
        y_ref[rows, :] = _rms(x_ref[rows, :] + y_ref[rows, :], g_ref[...])


def _out(x, o_a, o_b, proj_r, kv_m, w_out, g, tm=512, sub=256):
    B, T, D = x.shape
    M = kv_m.shape[1]
    row = lambda b, i: (b, i, 0)
    return pl.pallas_call(
        functools.partial(_out_kernel, sub=sub),
        grid=(B, T // tm),
        in_specs=[
            pl.BlockSpec((None, tm, D), row),
            pl.BlockSpec((None, tm, W_A), row),
            pl.BlockSpec((None, tm, W_B), row),
            pl.BlockSpec((None, tm, MIX_WIDTH), lambda b, i: (b, i, R_GATE // MIX_WIDTH)),
            pl.BlockSpec((None, tm, W_M), lambda b, i: (b, i, R_QM // W_M)),
            pl.BlockSpec((None, M, 2 * W_M), lambda b, i: (b, 0, 0)),
            pl.BlockSpec(w_out.shape, lambda b, i: (0, 0)),
            pl.BlockSpec((1, D), lambda b, i: (0, 0)),
        ],
        out_specs=pl.BlockSpec((None, tm, D), row),
        out_shape=jax.ShapeDtypeStruct((B, T, D), F32),
        scratch_shapes=[pltpu.VMEM((tm, MIX_WIDTH), BF16)],
        compiler_params=pltpu.CompilerParams(
            dimension_semantics=("parallel", "parallel"), vmem_limit_bytes=VMEM_LIMIT),
        name="out",
    )(x, o_a, o_b, proj_r, proj_r, kv_m, w_out, g)


def _alibi_slopes(n):
    return 2.0 ** (-8.0 * jnp.arange(1, n + 1, dtype=F32) / n)


def _static_alibi_slopes(n):
    return tuple(2.0 ** (-8.0 * i / n) for i in range(1, n + 1))


def _split_weights(w):
    bounds = [0]
    for s in SPLIT_SIZES:
        bounds.append(bounds[-1] + s)
    q_a, k_a, v_a, q_b, k_b, v_b, q_m, gate, q_i, k_i, w_i = (
        w[:, bounds[i]:bounds[i + 1]] for i in range(len(SPLIT_SIZES)))
    scale = HEAD_DIM ** -0.5
    idx_scale = IDX_DIM ** -0.5
    zeros = jnp.zeros((w.shape[0], LANES - HEAD_DIM), w.dtype)
    wa = jnp.concatenate([q_a * scale, k_a, v_a], axis=1)
    wr = jnp.concatenate([gate, q_m * scale, k_b, zeros, k_i, zeros], axis=1)
    wt = jnp.concatenate([q_i * idx_scale, q_b * scale, v_b], axis=1).T
    ww = jnp.concatenate([w_i.T, jnp.zeros((WI_ROWS - N_IDX_HEADS, w.shape[0]), w.dtype)], axis=0)
    return wa.astype(BF16), wr.astype(BF16), wt.astype(BF16), ww.astype(BF16)


def kernel(x, mem, g_in, g_mem, w_in, w_mem_kv, w_out, g_final):
    assert g_in.shape[0] == 1, "single-layer block: the final RMSNorm is fused into the output kernel"
    wa, wr, wt, ww = _split_weights(w_in[0])
    qkv_a, proj_r, proj_t, w_t = _proj(x, g_in, wa, wr, wt, ww)
    o_a = _attn_a(_alibi_slopes(N_HEADS_A), qkv_a)
    o_b = _dsa(proj_r, proj_t, w_t)
    kv_m = _mem_kv(mem, g_mem, w_mem_kv[0].astype(BF16))
    return _out(x, o_a, o_b, proj_r, kv_m, w_out[0].astype(BF16), g_final[None, :])
```

```python
import functools
import math

import numpy as np

import jax
import jax.numpy as jnp
from jax import lax
from jax.experimental import pallas as pl
from jax.experimental.pallas import tpu as pltpu

F32 = jnp.float32
BF16 = jnp.bfloat16
I32 = jnp.int32

D_MODEL = 1024
HEAD_DIM = 64
N_HEADS_A = 8
N_HEADS_B = 4
N_HEADS_MEM = 4
W_A = N_HEADS_A * HEAD_DIM
W_B = N_HEADS_B * HEAD_DIM
W_M = N_HEADS_MEM * HEAD_DIM
MIX_WIDTH = W_A + W_B + W_M
DILATIONS = (1, 4, 16)
BAND = 128
N_IDX_HEADS = 8
IDX_DIM = 64
TOPK_MAX = 256
RMS_EPS = 1e-6
SPLIT_SIZES = (W_A, W_A, W_A, W_B, HEAD_DIM, HEAD_DIM, W_M, MIX_WIDTH,
               N_IDX_HEADS * IDX_DIM, IDX_DIM, N_IDX_HEADS)

LANES = 128
BLK = 128
NEG = -1e30
VMEM_LIMIT = 48 * 1024 * 1024

R_GATE = 0
R_QM = MIX_WIDTH
R_KB = R_QM + W_M
R_KI = R_KB + LANES
R_COLS = R_KI + LANES
T_QI = 0
T_QB = N_IDX_HEADS * IDX_DIM
T_VB = T_QB + W_B
T_ROWS = T_VB + HEAD_DIM
WI_ROWS = 16
DSA_Q = 256
DSA_HALVINGS = 12
POS_SPLIT = 64
LOG2E = math.log2(math.e)


def _dot(a, b):
    return jnp.dot(a, b, preferred_element_type=F32)


def _dot_nt(a, b):
    return lax.dot_general(a, b, (((1,), (1,)), ((), ())), preferred_element_type=F32)


def _rms(x, g):
    return x * lax.rsqrt(jnp.mean(x * x, axis=-1, keepdims=True) + RMS_EPS) * g


def _proj_kernel(x_ref, g_ref, wa_ref, wr_ref, wt_ref, ww_ref,
                 oa_ref, oa4_ref, oa16_ref, or_ref, ot_ref, ow_ref, res_ref, res4_ref):
    hb = _rms(x_ref[0], g_ref[...]).astype(BF16)
    tm = hb.shape[0]
    d4, d16 = DILATIONS[1], DILATIONS[2]
    step = d16 // d4
    res = _dot(hb, wa_ref[...])
    oa_ref[0] = res.astype(BF16)
    for grp in range(res.shape[1] // LANES):
        lanes = slice(grp * LANES, (grp + 1) * LANES)
        res_ref[grp] = res[:, lanes]
        for r in range(d4):
            rows4 = res_ref[grp, pl.ds(r, tm // d4, stride=d4), :]
            oa4_ref[0, r, :, lanes] = rows4.astype(BF16)
            res4_ref[grp, r * (tm // d4):(r + 1) * (tm // d4), :] = rows4
        for r in range(d16):
            start = (r % d4) * (tm // d4) + r // d4
            oa16_ref[0, r, :, lanes] = res4_ref[grp, pl.ds(start, tm // d16, stride=step), :].astype(BF16)
    or_ref[0] = _dot(hb, wr_ref[...]).astype(BF16)
    ot_ref[0] = _dot_nt(wt_ref[...], hb).astype(BF16)
    ow_ref[0] = _dot_nt(ww_ref[...], hb)


def _proj(x, g, wa, wr, wt, ww, tm=512):
    B, T, D = x.shape
    const = lambda b, i: (0, 0)
    d4, d16 = DILATIONS[1], DILATIONS[2]
    nat, cm4, cm16, proj_r, proj_t, w_t = pl.pallas_call(
        _proj_kernel,
        grid=(B, T // tm),
        in_specs=[
            pl.BlockSpec((1, tm, D), lambda b, i: (b, i, 0)),
            pl.BlockSpec((1, D), const),
            pl.BlockSpec(wa.shape, const),
            pl.BlockSpec(wr.shape, const),
            pl.BlockSpec(wt.shape, const),
            pl.BlockSpec(ww.shape, const),
        ],
        out_specs=[
            pl.BlockSpec((1, tm, 3 * W_A), lambda b, i: (b, i, 0)),
            pl.BlockSpec((1, d4, tm // d4, 3 * W_A), lambda b, i: (b, 0, i, 0)),
            pl.BlockSpec((1, d16, tm // d16, 3 * W_A), lambda b, i: (b, 0, i, 0)),
            pl.BlockSpec((1, tm, R_COLS), lambda b, i: (b, i, 0)),
            pl.BlockSpec((1, T_ROWS, tm), lambda b, i: (b, 0, i)),
            pl.BlockSpec((1, WI_ROWS, tm), lambda b, i: (b, 0, i)),
        ],
        out_shape=[
            jax.ShapeDtypeStruct((B, T, 3 * W_A), BF16),
            jax.ShapeDtypeStruct((B, d4, T // d4, 3 * W_A), BF16),
            jax.ShapeDtypeStruct((B, d16, T // d16, 3 * W_A), BF16),
            jax.ShapeDtypeStruct((B, T, R_COLS), BF16),
            jax.ShapeDtypeStruct((B, T_ROWS, T), BF16),
            jax.ShapeDtypeStruct((B, WI_ROWS, T), F32),
        ],
        scratch_shapes=[pltpu.VMEM((3 * W_A // LANES, tm, LANES), F32)] * 2,
        compiler_params=pltpu.CompilerParams(
            dimension_semantics=("parallel", "parallel"), vmem_limit_bytes=VMEM_LIMIT),
        name="proj",
    )(x, g, wa, wr, wt, ww)
    return (nat, cm4.reshape(nat.shape), cm16.reshape(nat.shape)), proj_r, proj_t, w_t


def _attn_a_kernel(slopes_ref, *refs, seq, unroll):
    n_pat = len(DILATIONS)
    qkv = [refs[3 * p:3 * p + 3] for p in range(n_pat)]
    o_ref, acc_ref, m_ref, bias_ref, p_ref = refs[3 * n_pat:]
    hp = pl.program_id(1)
    lane = lax.broadcasted_iota(I32, (1, LANES), 1)
    head_lanes = (lane < HEAD_DIM, lane >= HEAD_DIM)
    row = lax.broadcasted_iota(I32, (BLK, BLK), 0)
    col = lax.broadcasted_iota(I32, (BLK, BLK), 1)
    d_cur = (row - col).astype(F32)
    d_prev = d_cur + float(BAND)
    n_all = seq // BLK
    zero = jnp.zeros((), BF16)
    one = jnp.ones((), BF16)

    def n_blocks(p):
        return n_all // DILATIONS[p]

    def n_keys(p):
        return BLK if n_blocks(p) == 1 else 2 * BLK

    def block(g):
        return pl.ds(pl.multiple_of(g * BLK, BLK), BLK)

    def keys(p, g):
        return pl.ds(pl.multiple_of(jnp.maximum((g + 1) * BLK - n_keys(p), 0), BLK), n_keys(p))

    def variant(p, g):
        g = jnp.asarray(g, I32)
        return jnp.where(g == 0, 2, jnp.where(g % n_blocks(p) == 0, 1, 0))

    def pitch(dil):
        return seq // dil + 1 if seq // dil == BLK and dil > 1 else None

    def put(ref, h, p, g, val):
        dil = DILATIONS[p]
        start = g // n_blocks(p) + (g % n_blocks(p)) * (dil * BLK)
        if dil == 1:
            ref[h, p, pl.ds(pl.multiple_of(start, BLK), BLK), :] = val
        elif pitch(dil):
            ref[h, p, pl.ds(start * pitch(dil), BLK), :] = val
        else:
            ref[h, p, pl.ds(start, BLK, stride=dil), :] = val

    def probs(p):
        q_ref, k_ref, _ = qkv[p]

        def body(g):
            kw = k_ref[keys(p, g), :]
            q = q_ref[block(g), :]
            for h in range(2):
                s = _dot_nt(jnp.where(head_lanes[h], q, zero), kw)
                s = s + bias_ref[p, h, variant(p, g), :, 2 * BLK - n_keys(p):]
                m = jnp.max(s, axis=1, keepdims=True)
                p_ref[p % 2, h, g, :, :n_keys(p)] = jnp.exp2(s - m).astype(BF16)
                put(m_ref, h, p, g, jnp.broadcast_to(m, (BLK, LANES)))
        return body

    def values(p):
        v_ref = qkv[p][2]

        def body(g):
            vw = v_ref[keys(p, g), :]
            for h in range(2):
                put(acc_ref, h, p, g,
                    _dot(p_ref[p % 2, h, g, :, :n_keys(p)], jnp.where(head_lanes[h], vw, one)))
        return body

    def run(*bodies):
        def step(g, carry):
            for body in bodies:
                body(g)
            return carry
        lax.fori_loop(0, n_all, step, 0, unroll=unroll)

    for p, dil in enumerate(DILATIONS):
        for h in range(2):
            sd = slopes_ref[hp * 2 + h] * (float(dil) * LOG2E)
            cur = jnp.where(row >= col, -sd * d_cur, NEG)
            masked = jnp.full((BLK, BLK), NEG, F32)
            bias_ref[p, h, 0, :, BLK:] = cur
            bias_ref[p, h, 1, :, BLK:] = cur
            bias_ref[p, h, 2, :, BLK:] = cur if n_keys(p) == BLK else masked
            if n_keys(p) > BLK:
                bias_ref[p, h, 0, :, :BLK] = jnp.where(col >= row, -sd * d_prev, NEG)
                bias_ref[p, h, 1, :, :BLK] = masked
                bias_ref[p, h, 2, :, :BLK] = cur

    assert n_pat == 3
    run(probs(0))
    run(values(0), probs(1))
    run(values(1), probs(2))
    run(values(2))

    rows_per_step = 2 * BLK

    def natural_rows(ref, h, p, i):
        dil = DILATIONS[p]
        if not pitch(dil):
            return ref[h, p, pl.ds(pl.multiple_of(i * rows_per_step, rows_per_step), rows_per_step), :]
        per = rows_per_step // dil
        return jnp.concatenate(
            [ref[h, p, pl.ds(i * per + u, dil, stride=pitch(dil)), :] for u in range(per)], axis=0)

    def merge(i, carry):
        sl = pl.ds(pl.multiple_of(i * rows_per_step, rows_per_step), rows_per_step)
        nums = []
        for h in range(2):
            ms = [natural_rows(m_ref, h, p, i) for p in range(n_pat)]
            mx = functools.reduce(jnp.maximum, ms)
            nums.append(sum(jnp.exp2(ms[p] - mx) * natural_rows(acc_ref, h, p, i) for p in range(n_pat)))
        acc = jnp.where(head_lanes[0], nums[0], nums[1])
        den = pltpu.roll(jnp.where(head_lanes[0], nums[1], nums[0]), HEAD_DIM, axis=1)
        o_ref[sl, :] = acc / den
        return carry

    lax.fori_loop(0, seq // rows_per_step, merge, 0)


def _attn_a(slopes, qkv_layouts, unroll=8):
    B, T, _ = qkv_layouts[0].shape
    n_pairs = N_HEADS_A // 2
    n_pat = len(DILATIONS)

    def spec(off):
        return pl.BlockSpec((None, T, LANES), lambda b, h: (b, 0, off + h))

    return pl.pallas_call(
        functools.partial(_attn_a_kernel, seq=T, unroll=unroll),
        grid=(B, n_pairs),
        in_specs=[pl.BlockSpec(memory_space=pltpu.SMEM)]
        + [spec(part * n_pairs) for _ in range(n_pat) for part in range(3)],
        out_specs=pl.BlockSpec((None, T, LANES), lambda b, h: (b, 0, h)),
        out_shape=jax.ShapeDtypeStruct((B, T, W_A), F32),
        scratch_shapes=[pltpu.VMEM((2, n_pat, T + max(DILATIONS), LANES), F32),
                        pltpu.VMEM((2, n_pat, T + max(DILATIONS), LANES), F32),
                        pltpu.VMEM((n_pat, 2, 3, BLK, 2 * BLK), F32),
                        pltpu.VMEM((2, 2, T // BLK, BLK, 2 * BLK), BF16)],
        compiler_params=pltpu.CompilerParams(
            dimension_semantics=("parallel", "parallel"), vmem_limit_bytes=VMEM_LIMIT),
        name="attn_a",
    )(slopes, *[a for a in qkv_layouts for _ in range(3)])


def _bf16_pair(c):
    hi = float(np.asarray(c, np.float32).astype(BF16).astype(np.float32))
    lo = float(np.asarray(c - hi, np.float32).astype(BF16).astype(np.float32))
    return hi, lo


def _fold(x, op):
    return functools.reduce(op, [x[r:r + 8] for r in range(0, x.shape[0], 8)])


def _dsa_kernel(kk_ref, tq_ref, vt_ref, wt_ref, o_ref, sc_ref, s_ref, acc_ref, kpos_ref,
                *, seq, k_sel, snap_unroll, slopes):
    j = pl.program_id(1)
    n_pairs = j + 1
    t_lane = j * DSA_Q + lax.broadcasted_iota(I32, (1, DSA_Q), 1)
    row = lax.broadcasted_iota(I32, (BLK, DSA_Q), 0)
    ws = wt_ref[0:N_IDX_HEADS, :] * (N_IDX_HEADS ** -0.5)
    inf = float("inf")

    def rows(c):
        return pl.ds(pl.multiple_of(c * BLK, BLK), BLK)

    def each_chunk(fn, init):
        def group(first, n, carry):
            for u in range(n):
                carry = fn(first + u, carry)
            return carry
        carry = lax.fori_loop(0, n_pairs // 2, lambda i, c: group(4 * i, 4, c), init)
        return lax.fori_loop(0, n_pairs % 2, lambda i, c: group(4 * (n_pairs // 2), 2, c), carry)

    def sub_reduce(x, op):
        return op(x, axis=0, keepdims=True)

    def score_chunk(c, carry):
        mn, mx = carry
        ki = kk_ref[rows(c), R_KI - R_KB:R_KI - R_KB + IDX_DIM]
        lgs = [_dot(ki, tq_ref[T_QI + h * IDX_DIM:T_QI + (h + 1) * IDX_DIM, :])
               for h in range(N_IDX_HEADS)]
        sc = functools.reduce(
            lambda a, b: a + b, [jnp.maximum(lg, 0.0) * ws[h:h + 1, :] for h, lg in enumerate(lgs)])
        causal = (c * BLK + row) <= t_lane
        sc_ref[rows(c), :] = jnp.where(causal, sc, -inf)
        mn = jnp.minimum(mn, _fold(jnp.where(causal, sc, inf), jnp.minimum))
        mx = jnp.maximum(mx, _fold(jnp.where(causal, sc, -inf), jnp.maximum))
        return mn, mx

    mn, mx = each_chunk(score_chunk, (jnp.full((8, DSA_Q), inf, F32), jnp.full((8, DSA_Q), -inf, F32)))
    few = t_lane < k_sel
    lo = jnp.where(few, -inf, sub_reduce(mn, jnp.min))
    hi = jnp.where(few, -inf, sub_reduce(mx, jnp.max))

    def count(pred):
        def body(c, cnt):
            return cnt + _fold(pred(sc_ref[rows(c), :], c).astype(I32), jnp.add)
        return sub_reduce(each_chunk(body, jnp.zeros((8, DSA_Q), I32)), jnp.sum)

    def halve(_, bounds):
        lo, hi, n_lo, n_hi = bounds
        mid = 0.5 * lo + 0.5 * hi
        cnt = count(lambda x, c: x >= mid)
        active = lo < hi
        up = active & (cnt >= k_sel)
        down = active & (cnt <= k_sel)
        return (jnp.where(up, mid, lo), jnp.where(down, mid, hi),
                jnp.where(up, cnt, n_lo), jnp.where(down, cnt, n_hi))

    bounds = lax.fori_loop(0, DSA_HALVINGS, halve, (lo, hi, t_lane + 1, jnp.zeros((1, DSA_Q), I32)))

    def snap(bounds):
        lo, hi, n_lo, n_hi = bounds
        mid = 0.5 * lo + 0.5 * hi
        mid = jnp.where(mid > lo, mid, hi)

        def body(c, carry):
            cnt, above, below = carry
            x = sc_ref[rows(c), :]
            ge = x >= mid
            return (cnt + _fold(ge.astype(I32), jnp.add),
                    jnp.minimum(above, _fold(jnp.where(ge, x, inf), jnp.minimum)),
                    jnp.maximum(below, _fold(jnp.where(ge, -inf, x), jnp.maximum)))

        cnt, above, below = each_chunk(body, (jnp.zeros((8, DSA_Q), I32),
                                              jnp.full((8, DSA_Q), inf, F32),
                                              jnp.full((8, DSA_Q), -inf, F32)))
        cnt = sub_reduce(cnt, jnp.sum)
        above = sub_reduce(above, jnp.min)
        below = sub_reduce(below, jnp.max)
        active = lo < hi
        enough = cnt >= k_sel
        up = active & enough
        down = active & (cnt <= k_sel)
        return (jnp.where(up, above, lo), jnp.where(down, jnp.where(enough, above, below), hi),
                jnp.where(up, cnt, n_lo), jnp.where(down, cnt, n_hi))

    def snaps(bounds):
        for _ in range(snap_unroll):
            bounds = snap(bounds)
        return bounds

    def unsettled(bounds):
        return jnp.max((bounds[0] < bounds[1]).astype(I32)) > 0

    tau, _, n_ge, n_gt = lax.while_loop(unsettled, snaps, snaps(bounds))

    need = k_sel - n_gt
    n_idx_bits = seq.bit_length() - 1
    assert 1 << n_idx_bits == seq

    def tie_break():
        def idx_step(it, jp):
            cand = jp + lax.shift_left(jnp.int32(1), n_idx_bits - 1 - it)
            below = count(lambda x, c: (x == tau) & ((c * BLK + row) < cand))
            return jnp.where(below < need, cand, jp)
        last = lax.fori_loop(0, n_idx_bits, idx_step, jnp.zeros((1, DSA_Q), I32))
        return jnp.where(tied, last, seq)

    tied = (n_ge > k_sel) & jnp.logical_not(few)
    surplus = jnp.max(tied.astype(I32)) > 0
    tie_last = lax.cond(surplus, tie_break, lambda: jnp.full((1, DSA_Q), seq, I32))

    @pl.when(j == 0)
    def _():
        pos = lax.broadcasted_iota(I32, (seq, LANES), 0)
        ln = lax.broadcasted_iota(I32, (seq, LANES), 1) - HEAD_DIM
        feat = jnp.where(ln == 0, pos // POS_SPLIT,
                         jnp.where(ln == 1, pos % POS_SPLIT,
                                   jnp.where(ln == 2, pos // POS_SPLIT, jnp.where(ln == 3, pos % POS_SPLIT, 0))))
        kpos_ref[...] = kk_ref[:, 0:LANES] + feat.astype(BF16)

    q_row = lax.broadcasted_iota(I32, (HEAD_DIM, DSA_Q), 0)
    q_aug = []
    for h in range(N_HEADS_B):
        c_hi, c_lo = _bf16_pair(slopes[h] * LOG2E)
        feat = jnp.where(q_row == 0, POS_SPLIT * c_hi,
                         jnp.where(q_row == 1, c_hi,
                                   jnp.where(q_row == 2, POS_SPLIT * c_lo,
                                             jnp.where(q_row == 3, c_lo, 0.0))))
        q_aug.append(jnp.concatenate(
            [tq_ref[T_QB + h * HEAD_DIM:T_QB + (h + 1) * HEAD_DIM, :], feat.astype(BF16)], axis=0))

    def qk_chunk(c, ms):
        kb = kpos_ref[rows(c), :]
        s_idx = c * BLK + row
        x = sc_ref[rows(c), :]
        tie = jnp.where(x == tau, jnp.where(s_idx <= tie_last, 0.0, NEG), NEG)
        sb = jnp.where(s_idx <= t_lane, jnp.where(x > tau, 0.0, tie), NEG)
        out = []
        for h in range(N_HEADS_B):
            s = _dot(kb, q_aug[h]) + sb
            s_ref[h, rows(c), :] = s
            out.append(jnp.maximum(ms[h], _fold(s, jnp.maximum)))
        return tuple(out)

    ms = each_chunk(qk_chunk, tuple(jnp.full((8, DSA_Q), NEG, F32) for _ in range(N_HEADS_B)))
    ms = [sub_reduce(m, jnp.max) for m in ms]

    acc_ref[...] = jnp.zeros((W_B, DSA_Q), F32)

    def exp_pv(first_pair, n, ls):
        ls = list(ls)
        pvs = []
        for u in range(n):
            pair = first_pair + u
            ps = []
            for h in range(N_HEADS_B):
                halves = []
                for c in (2 * pair, 2 * pair + 1):
                    p = jnp.exp2(s_ref[h, rows(c), :] - ms[h])
                    ls[h] = ls[h] + _fold(p, jnp.add)
                    halves.append(p.astype(BF16))
                ps.append(jnp.concatenate(halves, axis=0))
            vt = vt_ref[:, pl.ds(pl.multiple_of(pair * 2 * BLK, 2 * BLK), 2 * BLK)]
            pvs.append([_dot(vt, ps[h]) for h in range(N_HEADS_B)])
        for h in range(N_HEADS_B):
            acc_ref[h * HEAD_DIM:(h + 1) * HEAD_DIM, :] += functools.reduce(
                lambda a, b: a + b, [pv[h] for pv in pvs])
        return tuple(ls)

    ls = tuple(jnp.zeros((8, DSA_Q), F32) for _ in range(N_HEADS_B))
    ls = lax.fori_loop(0, n_pairs // 2, lambda i, c: exp_pv(2 * i, 2, c), ls)
    ls = lax.fori_loop(0, n_pairs % 2, lambda i, c: exp_pv(n_pairs - 1, 1, c), ls)
    l_all = jnp.concatenate(
        [jnp.broadcast_to(sub_reduce(ls[h], jnp.sum), (HEAD_DIM, DSA_Q)) for h in range(N_HEADS_B)],
        axis=0)
    o_ref[...] = (acc_ref[...] / l_all).T


def _dsa(proj_r, proj_t, w_t):
    B, T, _ = proj_r.shape
    k_sel = min(TOPK_MAX, T // 4)
    slopes = _static_alibi_slopes(N_HEADS_B)
    assert T // POS_SPLIT <= 256 and POS_SPLIT <= 256
    return pl.pallas_call(
        functools.partial(_dsa_kernel, seq=T, k_sel=k_sel, snap_unroll=4, slopes=slopes),
        grid=(B, T // DSA_Q),
        in_specs=[
            pl.BlockSpec((None, T, 2 * LANES), lambda b, q: (b, 0, R_KB // (2 * LANES))),
            pl.BlockSpec((None, T_VB, DSA_Q), lambda b, q: (b, 0, q)),
            pl.BlockSpec((None, HEAD_DIM, T), lambda b, q: (b, T_VB // HEAD_DIM, 0)),
            pl.BlockSpec((None, WI_ROWS, DSA_Q), lambda b, q: (b, 0, q)),
        ],
        out_specs=pl.BlockSpec((None, DSA_Q, W_B), lambda b, q: (b, q, 0)),
        out_shape=jax.ShapeDtypeStruct((B, T, W_B), F32),
        scratch_shapes=[pltpu.VMEM((T, DSA_Q), F32),
                        pltpu.VMEM((N_HEADS_B, T, DSA_Q), F32),
                        pltpu.VMEM((W_B, DSA_Q), F32),
                        pltpu.VMEM((T, LANES), BF16)],
        compiler_params=pltpu.CompilerParams(
            dimension_semantics=("parallel", "arbitrary"), vmem_limit_bytes=VMEM_LIMIT),
        name="dsa",
    )(proj_r, proj_t, proj_t, w_t)


def _mem_kv_kernel(mem_ref, g_ref, w_ref, kv_ref):
    kv_ref[0] = _dot(_rms(mem_ref[0], g_ref[...]).astype(BF16), w_ref[...]).astype(BF16)


def _mem_kv(mem, g, w):
    B, M, D = mem.shape
    return pl.pallas_call(
        _mem_kv_kernel,
        grid=(B,),
        in_specs=[
            pl.BlockSpec((1, M, D), lambda b: (b, 0, 0)),
            pl.BlockSpec((1, D), lambda b: (0, 0)),
            pl.BlockSpec(w.shape, lambda b: (0, 0)),
        ],
        out_specs=pl.BlockSpec((1, M, 2 * W_M), lambda b: (b, 0, 0)),
        out_shape=jax.ShapeDtypeStruct((B, M, 2 * W_M), BF16),
        compiler_params=pltpu.CompilerParams(
            dimension_semantics=("parallel",), vmem_limit_bytes=VMEM_LIMIT),
        name="mem_kv",
    )(mem, g, w)


def _out_kernel(x_ref, oa_ref, ob_ref, gate_ref, qm_ref, kv_ref, wo_ref, g_ref, y_ref, mix_ref, *, sub):
    tm = x_ref.shape[0]
    subs = [slice(r0, r0 + sub) for r0 in range(0, tm, sub)]
    lane = lax.broadcasted_iota(I32, (1, W_M), 1)
    head_lanes = [(lane >= h * HEAD_DIM) & (lane < (h + 1) * HEAD_DIM) for h in range(N_HEADS_MEM)]
    km = kv_ref[:, 0:W_M]
    vm = kv_ref[:, W_M:2 * W_M]
    zero = jnp.zeros((), BF16)
    vm_heads = [jnp.where(head_lanes[h], vm, zero) for h in range(N_HEADS_MEM)]

    def silu(g):
        return g * (1.0 / (1.0 + jnp.exp(-g)))

    def gated(o, rows, lo, width):
        return (o * silu(gate_ref[rows, lo:lo + width].astype(F32))).astype(BF16)

    for rows in subs:
        qm = qm_ref[rows, :]
        om = None
        for h in range(N_HEADS_MEM):
            s = _dot_nt(jnp.where(head_lanes[h], qm, zero), km)
            p = jnp.exp2(s - jnp.max(s, axis=1, keepdims=True))
            p = p / jnp.sum(p, axis=1, keepdims=True)
            o = _dot(p.astype(BF16), vm_heads[h])
            om = o if om is None else om + o
        mix_ref[rows, W_A + W_B:] = gated(om, rows, W_A + W_B, W_M)

    for rows in subs:
        half = W_A // 2
        mix_ref[rows, 0:half] = gated(oa_ref[rows, 0:half], rows, 0, half)
        mix_ref[rows, half:W_A] = gated(oa_ref[rows, half:W_A], rows, half, half)
        mix_ref[rows, W_A:W_A + W_B] = gated(ob_ref[rows, :], rows, W_A, W_B)
        y_ref[rows, :] = _dot(mix_ref[rows, :], wo_ref[...])

    for rows in subs:
        y_ref[rows, :] = _rms(x_ref[rows, :] + y_ref[rows, :], g_ref[...])


def _out(x, o_a, o_b, proj_r, kv_m, w_out, g, tm=512, sub=256):
    B, T, D = x.shape
    M = kv_m.shape[1]
    row = lambda b, i: (b, i, 0)
    return pl.pallas_call(
        functools.partial(_out_kernel, sub=sub),
        grid=(B, T // tm),
        in_specs=[
            pl.BlockSpec((None, tm, D), row),
            pl.BlockSpec((None, tm, W_A), row),
            pl.BlockSpec((None, tm, W_B), row),
            pl.BlockSpec((None, tm, MIX_WIDTH), lambda b, i: (b, i, R_GATE // MIX_WIDTH)),
            pl.BlockSpec((None, tm, W_M), lambda b, i: (b, i, R_QM // W_M)),
            pl.BlockSpec((None, M, 2 * W_M), lambda b, i: (b, 0, 0)),
            pl.BlockSpec(w_out.shape, lambda b, i: (0, 0)),
            pl.BlockSpec((1, D), lambda b, i: (0, 0)),
        ],
        out_specs=pl.BlockSpec((None, tm, D), row),
        out_shape=jax.ShapeDtypeStruct((B, T, D), F32),
        scratch_shapes=[pltpu.VMEM((tm, MIX_WIDTH), BF16)],
        compiler_params=pltpu.CompilerParams(
            dimension_semantics=("parallel", "parallel"), vmem_limit_bytes=VMEM_LIMIT),
        name="out",
    )(x, o_a, o_b, proj_r, proj_r, kv_m, w_out, g)


def _alibi_slopes(n):
    return 2.0 ** (-8.0 * jnp.arange(1, n + 1, dtype=F32) / n)


def _static_alibi_slopes(n):
    return tuple(2.0 ** (-8.0 * i / n) for i in range(1, n + 1))


def _split_weights(w):
    bounds = [0]
    for s in SPLIT_SIZES:
        bounds.append(bounds[-1] + s)
    q_a, k_a, v_a, q_b, k_b, v_b, q_m, gate, q_i, k_i, w_i = (
        w[:, bounds[i]:bounds[i + 1]] for i in range(len(SPLIT_SIZES)))
    scale = HEAD_DIM ** -0.5 * LOG2E
    idx_scale = IDX_DIM ** -0.5
    zeros = jnp.zeros((w.shape[0], LANES - HEAD_DIM), w.dtype)
    wa = jnp.concatenate([q_a * scale, k_a, v_a], axis=1)
    wr = jnp.concatenate([gate, q_m * scale, k_b, zeros, k_i, zeros], axis=1)
    wt = jnp.concatenate([q_i * idx_scale, q_b * scale, v_b], axis=1).T
    ww = jnp.concatenate([w_i.T, jnp.zeros((WI_ROWS - N_IDX_HEADS, w.shape[0]), w.dtype)], axis=0)
    return wa.astype(BF16), wr.astype(BF16), wt.astype(BF16), ww.astype(BF16)


def kernel(x, mem, g_in, g_mem, w_in, w_mem_kv, w_out, g_final):
    assert g_in.shape[0] == 1, "single-layer block: the final RMSNorm is fused into the output kernel"
    wa, wr, wt, ww = _split_weights(w_in[0])
    qkv_a, proj_r, proj_t, w_t = _proj(x, g_in, wa, wr, wt, ww)
    o_a = _attn_a(_alibi_slopes(N_HEADS_A), qkv_a)
    o_b = _dsa(proj_r, proj_t, w_t)
    kv_m = _mem_kv(mem, g_mem, w_mem_kv[0].astype(BF16))
    return _out(x, o_a, o_b, proj_r, kv_m, w_out[0].astype(BF16), g_final[None, :])
```

```python
import functools
import math

import numpy as np

import jax
import jax.numpy as jnp
from jax import lax
from jax.experimental import pallas as pl
from jax.experimental.pallas import tpu as pltpu

F32 = jnp.float32
BF16 = jnp.bfloat16
I32 = jnp.int32

D_MODEL = 1024
HEAD_DIM = 64
N_HEADS_A = 8
N_HEADS_B = 4
N_HEADS_MEM = 4
W_A = N_HEADS_A * HEAD_DIM
W_B = N_HEADS_B * HEAD_DIM
W_M = N_HEADS_MEM * HEAD_DIM
MIX_WIDTH = W_A + W_B + W_M
DILATIONS = (1, 4, 16)
BAND = 128
N_IDX_HEADS = 8
IDX_DIM = 64
TOPK_MAX = 256
RMS_EPS = 1e-6
SPLIT_SIZES = (W_A, W_A, W_A, W_B, HEAD_DIM, HEAD_DIM, W_M, MIX_WIDTH,
               N_IDX_HEADS * IDX_DIM, IDX_DIM, N_IDX_HEADS)

LANES = 128
BLK = 128
NEG = -1e30
VMEM_LIMIT = 48 * 1024 * 1024

R_GATE = 0
R_QM = MIX_WIDTH
R_KB = R_QM + W_M
R_KI = R_KB + LANES
R_COLS = R_KI + LANES
T_QI = 0
T_QB = N_IDX_HEADS * IDX_DIM
T_VB = T_QB + W_B
T_ROWS = T_VB + HEAD_DIM
WI_ROWS = 16
DSA_Q = 256
DSA_HALVINGS = 12
POS_SPLIT = 64
LOG2E = math.log2(math.e)


def _dot(a, b):
    return jnp.dot(a, b, preferred_element_type=F32)


def _dot_nt(a, b):
    return lax.dot_general(a, b, (((1,), (1,)), ((), ())), preferred_element_type=F32)


def _rms(x, g):
    return x * lax.rsqrt(jnp.mean(x * x, axis=-1, keepdims=True) + RMS_EPS) * g


def _proj_kernel(x_ref, g_ref, wa_ref, wr_ref, wt_ref, ww_ref,
                 oa_ref, oa4_ref, oa16_ref, or_ref, ot_ref, ow_ref, res_ref, res4_ref):
    hb = _rms(x_ref[0], g_ref[...]).astype(BF16)
    tm = hb.shape[0]
    d4, d16 = DILATIONS[1], DILATIONS[2]
    step = d16 // d4
    res = _dot(hb, wa_ref[...])
    oa_ref[0] = res.astype(BF16)
    for grp in range(res.shape[1] // LANES):
        lanes = slice(grp * LANES, (grp + 1) * LANES)
        res_ref[grp] = res[:, lanes]
        for r in range(d4):
            rows4 = res_ref[grp, pl.ds(r, tm // d4, stride=d4), :]
            oa4_ref[0, r, :, lanes] = rows4.astype(BF16)
            res4_ref[grp, r * (tm // d4):(r + 1) * (tm // d4), :] = rows4
        for r in range(d16):
            start = (r % d4) * (tm // d4) + r // d4
            oa16_ref[0, r, :, lanes] = res4_ref[grp, pl.ds(start, tm // d16, stride=step), :].astype(BF16)
    or_ref[0] = _dot(hb, wr_ref[...]).astype(BF16)
    ot_ref[0] = _dot_nt(wt_ref[...], hb).astype(BF16)
    ow_ref[0] = _dot_nt(ww_ref[...], hb)


def _proj(x, g, wa, wr, wt, ww, tm=512):
    B, T, D = x.shape
    const = lambda b, i: (0, 0)
    d4, d16 = DILATIONS[1], DILATIONS[2]
    nat, cm4, cm16, proj_r, proj_t, w_t = pl.pallas_call(
        _proj_kernel,
        grid=(B, T // tm),
        in_specs=[
            pl.BlockSpec((1, tm, D), lambda b, i: (b, i, 0)),
            pl.BlockSpec((1, D), const),
            pl.BlockSpec(wa.shape, const),
            pl.BlockSpec(wr.shape, const),
            pl.BlockSpec(wt.shape, const),
            pl.BlockSpec(ww.shape, const),
        ],
        out_specs=[
            pl.BlockSpec((1, tm, 3 * W_A), lambda b, i: (b, i, 0)),
            pl.BlockSpec((1, d4, tm // d4, 3 * W_A), lambda b, i: (b, 0, i, 0)),
            pl.BlockSpec((1, d16, tm // d16, 3 * W_A), lambda b, i: (b, 0, i, 0)),
            pl.BlockSpec((1, tm, R_COLS), lambda b, i: (b, i, 0)),
            pl.BlockSpec((1, T_ROWS, tm), lambda b, i: (b, 0, i)),
            pl.BlockSpec((1, WI_ROWS, tm), lambda b, i: (b, 0, i)),
        ],
        out_shape=[
            jax.ShapeDtypeStruct((B, T, 3 * W_A), BF16),
            jax.ShapeDtypeStruct((B, d4, T // d4, 3 * W_A), BF16),
            jax.ShapeDtypeStruct((B, d16, T // d16, 3 * W_A), BF16),
            jax.ShapeDtypeStruct((B, T, R_COLS), BF16),
            jax.ShapeDtypeStruct((B, T_ROWS, T), BF16),
            jax.ShapeDtypeStruct((B, WI_ROWS, T), F32),
        ],
        scratch_shapes=[pltpu.VMEM((3 * W_A // LANES, tm, LANES), F32)] * 2,
        compiler_params=pltpu.CompilerParams(
            dimension_semantics=("parallel", "parallel"), vmem_limit_bytes=VMEM_LIMIT),
        name="proj",
    )(x, g, wa, wr, wt, ww)
    return (nat, cm4.reshape(nat.shape), cm16.reshape(nat.shape)), proj_r, proj_t, w_t


def _attn_a_kernel(slopes_ref, *refs, seq, unroll):
    n_pat = len(DILATIONS)
    qkv = [refs[3 * p:3 * p + 3] for p in range(n_pat)]
    o_ref, acc_ref, m_ref, bias_ref, p_ref = refs[3 * n_pat:]
    hp = pl.program_id(1)
    lane = lax.broadcasted_iota(I32, (1, LANES), 1)
    head_lanes = (lane < HEAD_DIM, lane >= HEAD_DIM)
    row = lax.broadcasted_iota(I32, (BLK, BLK), 0)
    col = lax.broadcasted_iota(I32, (BLK, BLK), 1)
    d_cur = (row - col).astype(F32)
    d_prev = d_cur + float(BAND)
    n_all = seq // BLK
    zero = jnp.zeros((), BF16)
    one = jnp.ones((), BF16)

    def n_blocks(p):
        return n_all // DILATIONS[p]

    def n_keys(p):
        return BLK if n_blocks(p) == 1 else 2 * BLK

    def block(g):
        return pl.ds(pl.multiple_of(g * BLK, BLK), BLK)

    def keys(p, g):
        return pl.ds(pl.multiple_of(jnp.maximum((g + 1) * BLK - n_keys(p), 0), BLK), n_keys(p))

    def variant(p, g):
        g = jnp.asarray(g, I32)
        return jnp.where(g == 0, 2, jnp.where(g % n_blocks(p) == 0, 1, 0))

    def pitch(dil):
        return seq // dil + 1 if seq // dil == BLK and dil > 1 else None

    def put(ref, h, p, g, val):
        dil = DILATIONS[p]
        start = g // n_blocks(p) + (g % n_blocks(p)) * (dil * BLK)
        if dil == 1:
            ref[h, p, pl.ds(pl.multiple_of(start, BLK), BLK), :] = val
        elif pitch(dil):
            ref[h, p, pl.ds(start * pitch(dil), BLK), :] = val
        else:
            ref[h, p, pl.ds(start, BLK, stride=dil), :] = val

    def probs(p):
        q_ref, k_ref, _ = qkv[p]

        def body(g):
            kw = k_ref[keys(p, g), :]
            q = q_ref[block(g), :]
            for h in range(2):
                s = _dot_nt(jnp.where(head_lanes[h], q, zero), kw)
                s = s + bias_ref[p, h, variant(p, g), :, 2 * BLK - n_keys(p):]
                m = jnp.max(s, axis=1, keepdims=True)
                p_ref[p % 2, h, g, :, :n_keys(p)] = jnp.exp2(s - m).astype(BF16)
                put(m_ref, h, p, g, jnp.broadcast_to(m, (BLK, LANES)))
        return body

    def values(p):
        v_ref = qkv[p][2]

        def body(g):
            vw = v_ref[keys(p, g), :]
            for h in range(2):
                put(acc_ref, h, p, g,
                    _dot(p_ref[p % 2, h, g, :, :n_keys(p)], jnp.where(head_lanes[h], vw, one)))
        return body

    def run(*bodies):
        def step(g, carry):
            for body in bodies:
                body(g)
            return carry
        lax.fori_loop(0, n_all, step, 0, unroll=unroll)

    for p, dil in enumerate(DILATIONS):
        for h in range(2):
            sd = slopes_ref[hp * 2 + h] * (float(dil) * LOG2E)
            cur = jnp.where(row >= col, -sd * d_cur, NEG)
            masked = jnp.full((BLK, BLK), NEG, F32)
            bias_ref[p, h, 0, :, BLK:] = cur
            bias_ref[p, h, 1, :, BLK:] = cur
            bias_ref[p, h, 2, :, BLK:] = cur if n_keys(p) == BLK else masked
            if n_keys(p) > BLK:
                bias_ref[p, h, 0, :, :BLK] = jnp.where(col >= row, -sd * d_prev, NEG)
                bias_ref[p, h, 1, :, :BLK] = masked
                bias_ref[p, h, 2, :, :BLK] = cur

    assert n_pat == 3
    run(probs(0))
    run(values(0), probs(1))
    run(values(1), probs(2))
    run(values(2))

    rows_per_step = 2 * BLK

    def natural_rows(ref, h, p, i):
        dil = DILATIONS[p]
        if not pitch(dil):
            return ref[h, p, pl.ds(pl.multiple_of(i * rows_per_step, rows_per_step), rows_per_step), :]
        per = rows_per_step // dil
        return jnp.concatenate(
            [ref[h, p, pl.ds(i * per + u, dil, stride=pitch(dil)), :] for u in range(per)], axis=0)

    def merge(i, carry):
        sl = pl.ds(pl.multiple_of(i * rows_per_step, rows_per_step), rows_per_step)
        nums = []
        for h in range(2):
            ms = [natural_rows(m_ref, h, p, i) for p in range(n_pat)]
            mx = functools.reduce(jnp.maximum, ms)
            nums.append(sum(jnp.exp2(ms[p] - mx) * natural_rows(acc_ref, h, p, i) for p in range(n_pat)))
        acc = jnp.where(head_lanes[0], nums[0], nums[1])
        den = pltpu.roll(jnp.where(head_lanes[0], nums[1], nums[0]), HEAD_DIM, axis=1)
        o_ref[sl, :] = acc / den
        return carry

    lax.fori_loop(0, seq // rows_per_step, merge, 0, unroll=4)


def _attn_a(slopes, qkv_layouts, unroll=16):
    B, T, _ = qkv_layouts[0].shape
    n_pairs = N_HEADS_A // 2
    n_pat = len(DILATIONS)

    def spec(off):
        return pl.BlockSpec((None, T, LANES), lambda b, h: (b, 0, off + h))

    return pl.pallas_call(
        functools.partial(_attn_a_kernel, seq=T, unroll=unroll),
        grid=(B, n_pairs),
        in_specs=[pl.BlockSpec(memory_space=pltpu.SMEM)]
        + [spec(part * n_pairs) for _ in range(n_pat) for part in range(3)],
        out_specs=pl.BlockSpec((None, T, LANES), lambda b, h: (b, 0, h)),
        out_shape=jax.ShapeDtypeStruct((B, T, W_A), F32),
        scratch_shapes=[pltpu.VMEM((2, n_pat, T + max(DILATIONS), LANES), F32),
                        pltpu.VMEM((2, n_pat, T + max(DILATIONS), LANES), F32),
                        pltpu.VMEM((n_pat, 2, 3, BLK, 2 * BLK), F32),
                        pltpu.VMEM((2, 2, T // BLK, BLK, 2 * BLK), BF16)],
        compiler_params=pltpu.CompilerParams(
            dimension_semantics=("parallel", "parallel"), vmem_limit_bytes=VMEM_LIMIT),
        name="attn_a",
    )(slopes, *[a for a in qkv_layouts for _ in range(3)])


def _bf16_pair(c):
    hi = float(np.asarray(c, np.float32).astype(BF16).astype(np.float32))
    lo = float(np.asarray(c - hi, np.float32).astype(BF16).astype(np.float32))
    return hi, lo


def _fold(x, op):
    return functools.reduce(op, [x[r:r + 8] for r in range(0, x.shape[0], 8)])


def _dsa_kernel(kk_ref, tq_ref, vt_ref, wt_ref, o_ref, sc_ref, s_ref, acc_ref, kpos_ref,
                *, seq, k_sel, snap_unroll, slopes):
    j = pl.program_id(1)
    n_pairs = j + 1
    t_lane = j * DSA_Q + lax.broadcasted_iota(I32, (1, DSA_Q), 1)
    row = lax.broadcasted_iota(I32, (BLK, DSA_Q), 0)
    ws = wt_ref[0:N_IDX_HEADS, :] * (N_IDX_HEADS ** -0.5)
    inf = float("inf")

    def rows(c):
        return pl.ds(pl.multiple_of(c * BLK, BLK), BLK)

    def each_chunk(fn, init):
        def group(first, n, carry):
            for u in range(n):
                carry = fn(first + u, carry)
            return carry
        carry = lax.fori_loop(0, n_pairs // 2, lambda i, c: group(4 * i, 4, c), init)
        return lax.fori_loop(0, n_pairs % 2, lambda i, c: group(4 * (n_pairs // 2), 2, c), carry)

    def sub_reduce(x, op):
        return op(x, axis=0, keepdims=True)

    def score_chunk(c, carry):
        mn, mx = carry
        ki = kk_ref[rows(c), R_KI - R_KB:R_KI - R_KB + IDX_DIM]
        lgs = [_dot(ki, tq_ref[T_QI + h * IDX_DIM:T_QI + (h + 1) * IDX_DIM, :])
               for h in range(N_IDX_HEADS)]
        sc = functools.reduce(
            lambda a, b: a + b, [jnp.maximum(lg, 0.0) * ws[h:h + 1, :] for h, lg in enumerate(lgs)])
        causal = (c * BLK + row) <= t_lane
        sc_ref[rows(c), :] = jnp.where(causal, sc, -inf)
        mn = jnp.minimum(mn, _fold(jnp.where(causal, sc, inf), jnp.minimum))
        mx = jnp.maximum(mx, _fold(jnp.where(causal, sc, -inf), jnp.maximum))
        return mn, mx

    mn, mx = each_chunk(score_chunk, (jnp.full((8, DSA_Q), inf, F32), jnp.full((8, DSA_Q), -inf, F32)))
    few = t_lane < k_sel
    lo = jnp.where(few, -inf, sub_reduce(mn, jnp.min))
    hi = jnp.where(few, -inf, sub_reduce(mx, jnp.max))

    def count(pred):
        def body(c, cnt):
            return cnt + _fold(pred(sc_ref[rows(c), :], c).astype(I32), jnp.add)
        return sub_reduce(each_chunk(body, jnp.zeros((8, DSA_Q), I32)), jnp.sum)

    def halve(_, bounds):
        lo, hi, n_lo, n_hi = bounds
        mid = 0.5 * lo + 0.5 * hi
        cnt = count(lambda x, c: x >= mid)
        active = lo < hi
        up = active & (cnt >= k_sel)
        down = active & (cnt <= k_sel)
        return (jnp.where(up, mid, lo), jnp.where(down, mid, hi),
                jnp.where(up, cnt, n_lo), jnp.where(down, cnt, n_hi))

    bounds = lax.fori_loop(0, DSA_HALVINGS, halve, (lo, hi, t_lane + 1, jnp.zeros((1, DSA_Q), I32)))

    def snap(bounds):
        lo, hi, n_lo, n_hi = bounds
        mid = 0.5 * lo + 0.5 * hi
        mid = jnp.where(mid > lo, mid, hi)

        def body(c, carry):
            cnt, above, below = carry
            x = sc_ref[rows(c), :]
            ge = x >= mid
            return (cnt + _fold(ge.astype(I32), jnp.add),
                    jnp.minimum(above, _fold(jnp.where(ge, x, inf), jnp.minimum)),
                    jnp.maximum(below, _fold(jnp.where(ge, -inf, x), jnp.maximum)))

        cnt, above, below = each_chunk(body, (jnp.zeros((8, DSA_Q), I32),
                                              jnp.full((8, DSA_Q), inf, F32),
                                              jnp.full((8, DSA_Q), -inf, F32)))
        cnt = sub_reduce(cnt, jnp.sum)
        above = sub_reduce(above, jnp.min)
        below = sub_reduce(below, jnp.max)
        active = lo < hi
        enough = cnt >= k_sel
        up = active & enough
        down = active & (cnt <= k_sel)
        return (jnp.where(up, above, lo), jnp.where(down, jnp.where(enough, above, below), hi),
                jnp.where(up, cnt, n_lo), jnp.where(down, cnt, n_hi))

    def snaps(bounds):
        for _ in range(snap_unroll):
            bounds = snap(bounds)
        return bounds

    def unsettled(bounds):
        return jnp.max((bounds[0] < bounds[1]).astype(I32)) > 0

    tau, _, n_ge, n_gt = lax.while_loop(unsettled, snaps, snaps(bounds))

    need = k_sel - n_gt
    n_idx_bits = seq.bit_length() - 1
    assert 1 << n_idx_bits == seq

    def tie_break():
        def idx_step(it, jp):
            cand = jp + lax.shift_left(jnp.int32(1), n_idx_bits - 1 - it)
            below = count(lambda x, c: (x == tau) & ((c * BLK + row) < cand))
            return jnp.where(below < need, cand, jp)
        last = lax.fori_loop(0, n_idx_bits, idx_step, jnp.zeros((1, DSA_Q), I32))
        return jnp.where(tied, last, seq)

    tied = (n_ge > k_sel) & jnp.logical_not(few)
    surplus = jnp.max(tied.astype(I32)) > 0
    tie_last = lax.cond(surplus, tie_break, lambda: jnp.full((1, DSA_Q), seq, I32))

    @pl.when(j == 0)
    def _():
        pos = lax.broadcasted_iota(I32, (seq, LANES), 0)
        ln = lax.broadcasted_iota(I32, (seq, LANES), 1) - HEAD_DIM
        feat = jnp.where(ln == 0, pos // POS_SPLIT,
                         jnp.where(ln == 1, pos % POS_SPLIT,
                                   jnp.where(ln == 2, pos // POS_SPLIT, jnp.where(ln == 3, pos % POS_SPLIT, 0))))
        kpos_ref[...] = kk_ref[:, 0:LANES] + feat.astype(BF16)

    q_row = lax.broadcasted_iota(I32, (HEAD_DIM, DSA_Q), 0)
    q_aug = []
    for h in range(N_HEADS_B):
        c_hi, c_lo = _bf16_pair(slopes[h] * LOG2E)
        feat = jnp.where(q_row == 0, POS_SPLIT * c_hi,
                         jnp.where(q_row == 1, c_hi,
                                   jnp.where(q_row == 2, POS_SPLIT * c_lo,
                                             jnp.where(q_row == 3, c_lo, 0.0))))
        q_aug.append(jnp.concatenate(
            [tq_ref[T_QB + h * HEAD_DIM:T_QB + (h + 1) * HEAD_DIM, :], feat.astype(BF16)], axis=0))

    def qk_chunk(c, ms):
        kb = kpos_ref[rows(c), :]
        s_idx = c * BLK + row
        x = sc_ref[rows(c), :]
        tie = jnp.where(x == tau, jnp.where(s_idx <= tie_last, 0.0, NEG), NEG)
        sb = jnp.where(s_idx <= t_lane, jnp.where(x > tau, 0.0, tie), NEG)
        out = []
        for h in range(N_HEADS_B):
            s = _dot(kb, q_aug[h]) + sb
            s_ref[h, rows(c), :] = s
            out.append(jnp.maximum(ms[h], _fold(s, jnp.maximum)))
        return tuple(out)

    ms = each_chunk(qk_chunk, tuple(jnp.full((8, DSA_Q), NEG, F32) for _ in range(N_HEADS_B)))
    ms = [sub_reduce(m, jnp.max) for m in ms]

    acc_ref[...] = jnp.zeros((W_B, DSA_Q), F32)

    def exp_pv(first_pair, n, ls):
        ls = list(ls)
        pvs = []
        for u in range(n):
            pair = first_pair + u
            ps = []
            for h in range(N_HEADS_B):
                halves = []
                for c in (2 * pair, 2 * pair + 1):
                    p = jnp.exp2(s_ref[h, rows(c), :] - ms[h])
                    ls[h] = ls[h] + _fold(p, jnp.add)
                    halves.append(p.astype(BF16))
                ps.append(jnp.concatenate(halves, axis=0))
            vt = vt_ref[:, pl.ds(pl.multiple_of(pair * 2 * BLK, 2 * BLK), 2 * BLK)]
            pvs.append([_dot(vt, ps[h]) for h in range(N_HEADS_B)])
        for h in range(N_HEADS_B):
            acc_ref[h * HEAD_DIM:(h + 1) * HEAD_DIM, :] += functools.reduce(
                lambda a, b: a + b, [pv[h] for pv in pvs])
        return tuple(ls)

    ls = tuple(jnp.zeros((8, DSA_Q), F32) for _ in range(N_HEADS_B))
    ls = lax.fori_loop(0, n_pairs // 2, lambda i, c: exp_pv(2 * i, 2, c), ls)
    ls = lax.fori_loop(0, n_pairs % 2, lambda i, c: exp_pv(n_pairs - 1, 1, c), ls)
    l_all = jnp.concatenate(
        [jnp.broadcast_to(sub_reduce(ls[h], jnp.sum), (HEAD_DIM, DSA_Q)) for h in range(N_HEADS_B)],
        axis=0)
    o_ref[...] = (acc_ref[...] / l_all).T


def _dsa(proj_r, proj_t, w_t):
    B, T, _ = proj_r.shape
    k_sel = min(TOPK_MAX, T // 4)
    slopes = _static_alibi_slopes(N_HEADS_B)
    assert T // POS_SPLIT <= 256 and POS_SPLIT <= 256
    return pl.pallas_call(
        functools.partial(_dsa_kernel, seq=T, k_sel=k_sel, snap_unroll=4, slopes=slopes),
        grid=(B, T // DSA_Q),
        in_specs=[
            pl.BlockSpec((None, T, 2 * LANES), lambda b, q: (b, 0, R_KB // (2 * LANES))),
            pl.BlockSpec((None, T_VB, DSA_Q), lambda b, q: (b, 0, q)),
            pl.BlockSpec((None, HEAD_DIM, T), lambda b, q: (b, T_VB // HEAD_DIM, 0)),
            pl.BlockSpec((None, WI_ROWS, DSA_Q), lambda b, q: (b, 0, q)),
        ],
        out_specs=pl.BlockSpec((None, DSA_Q, W_B), lambda b, q: (b, q, 0)),
        out_shape=jax.ShapeDtypeStruct((B, T, W_B), F32),
        scratch_shapes=[pltpu.VMEM((T, DSA_Q), F32),
                        pltpu.VMEM((N_HEADS_B, T, DSA_Q), F32),
                        pltpu.VMEM((W_B, DSA_Q), F32),
                        pltpu.VMEM((T, LANES), BF16)],
        compiler_params=pltpu.CompilerParams(
            dimension_semantics=("parallel", "arbitrary"), vmem_limit_bytes=VMEM_LIMIT),
        name="dsa",
    )(proj_r, proj_t, proj_t, w_t)


def _mem_kv_kernel(mem_ref, g_ref, w_ref, kv_ref):
    kv_ref[0] = _dot(_rms(mem_ref[0], g_ref[...]).astype(BF16), w_ref[...]).astype(BF16)


def _mem_kv(mem, g, w):
    B, M, D = mem.shape
    return pl.pallas_call(
        _mem_kv_kernel,
        grid=(B,),
        in_specs=[
            pl.BlockSpec((1, M, D), lambda b: (b, 0, 0)),
            pl.BlockSpec((1, D), lambda b: (0, 0)),
            pl.BlockSpec(w.shape, lambda b: (0, 0)),
        ],
        out_specs=pl.BlockSpec((1, M, 2 * W_M), lambda b: (b, 0, 0)),
        out_shape=jax.ShapeDtypeStruct((B, M, 2 * W_M), BF16),
        compiler_params=pltpu.CompilerParams(
            dimension_semantics=("parallel",), vmem_limit_bytes=VMEM_LIMIT),
        name="mem_kv",
    )(mem, g, w)


def _out_kernel(x_ref, oa_ref, ob_ref, gate_ref, qm_ref, kv_ref, wo_ref, g_ref, y_ref, mix_ref, *, sub):
    tm = x_ref.shape[0]
    subs = [slice(r0, r0 + sub) for r0 in range(0, tm, sub)]
    lane = lax.broadcasted_iota(I32, (1, W_M), 1)
    head_lanes = [(lane >= h * HEAD_DIM) & (lane < (h + 1) * HEAD_DIM) for h in range(N_HEADS_MEM)]
    km = kv_ref[:, 0:W_M]
    vm = kv_ref[:, W_M:2 * W_M]
    zero = jnp.zeros((), BF16)
    vm_heads = [jnp.where(head_lanes[h], vm, zero) for h in range(N_HEADS_MEM)]

    def silu(g):
        return g * (1.0 / (1.0 + jnp.exp(-g)))

    def gated(o, rows, lo, width):
        return (o * silu(gate_ref[rows, lo:lo + width].astype(F32))).astype(BF16)

    for rows in subs:
        qm = qm_ref[rows, :]
        om = None
        for h in range(N_HEADS_MEM):
            s = _dot_nt(jnp.where(head_lanes[h], qm, zero), km)
            p = jnp.exp2(s - jnp.max(s, axis=1, keepdims=True))
            p = p / jnp.sum(p, axis=1, keepdims=True)
            o = _dot(p.astype(BF16), vm_heads[h])
            om = o if om is None else om + o
        mix_ref[rows, W_A + W_B:] = gated(om, rows, W_A + W_B, W_M)

    for rows in subs:
        half = W_A // 2
        mix_ref[rows, 0:half] = gated(oa_ref[rows, 0:half], rows, 0, half)
        mix_ref[rows, half:W_A] = gated(oa_ref[rows, half:W_A], rows, half, half)
        mix_ref[rows, W_A:W_A + W_B] = gated(ob_ref[rows, :], rows, W_A, W_B)
        y_ref[rows, :] = _dot(mix_ref[rows, :], wo_ref[...])

    for rows in subs:
        y_ref[rows, :] = _rms(x_ref[rows, :] + y_ref[rows, :], g_ref[...])


def _out(x, o_a, o_b, proj_r, kv_m, w_out, g, tm=512, sub=256):
    B, T, D = x.shape
    M = kv_m.shape[1]
    row = lambda b, i: (b, i, 0)
    return pl.pallas_call(
        functools.partial(_out_kernel, sub=sub),
        grid=(B, T // tm),
        in_specs=[
            pl.BlockSpec((None, tm, D), row),
            pl.BlockSpec((None, tm, W_A), row),
            pl.BlockSpec((None, tm, W_B), row),
            pl.BlockSpec((None, tm, MIX_WIDTH), lambda b, i: (b, i, R_GATE // MIX_WIDTH)),
            pl.BlockSpec((None, tm, W_M), lambda b, i: (b, i, R_QM // W_M)),
            pl.BlockSpec((None, M, 2 * W_M), lambda b, i: (b, 0, 0)),
            pl.BlockSpec(w_out.shape, lambda b, i: (0, 0)),
            pl.BlockSpec((1, D), lambda b, i: (0, 0)),
        ],
        out_specs=pl.BlockSpec((None, tm, D), row),
        out_shape=jax.ShapeDtypeStruct((B, T, D), F32),
        scratch_shapes=[pltpu.VMEM((tm, MIX_WIDTH), BF16)],
        compiler_params=pltpu.CompilerParams(
            dimension_semantics=("parallel", "parallel"), vmem_limit_bytes=VMEM_LIMIT),
        name="out",
    )(x, o_a, o_b, proj_r, proj_r, kv_m, w_out, g)


def _alibi_slopes(n):
    return 2.0 ** (-8.0 * jnp.arange(1, n + 1, dtype=F32) / n)


def _static_alibi_slopes(n):
    return tuple(2.0 ** (-8.0 * i / n) for i in range(1, n + 1))


def _split_weights(w):
    bounds = [0]
    for s in SPLIT_SIZES:
        bounds.append(bounds[-1] + s)
    q_a, k_a, v_a, q_b, k_b, v_b, q_m, gate, q_i, k_i, w_i = (
        w[:, bounds[i]:bounds[i + 1]] for i in range(len(SPLIT_SIZES)))
    scale = HEAD_DIM ** -0.5 * LOG2E
    idx_scale = IDX_DIM ** -0.5
    zeros = jnp.zeros((w.shape[0], LANES - HEAD_DIM), w.dtype)
    wa = jnp.concatenate([q_a * scale, k_a, v_a], axis=1)
    wr = jnp.concatenate([gate, q_m * scale, k_b, zeros, k_i, zeros], axis=1)
    wt = jnp.concatenate([q_i * idx_scale, q_b * scale, v_b], axis=1).T
    ww = jnp.concatenate([w_i.T, jnp.zeros((WI_ROWS - N_IDX_HEADS, w.shape[0]), w.dtype)], axis=0)
    return wa.astype(BF16), wr.astype(BF16), wt.astype(BF16), ww.astype(BF16)


def kernel(x, mem, g_in, g_mem, w_in, w_mem_kv, w_out, g_final):
    assert g_in.shape[0] == 1, "single-layer block: the final RMSNorm is fused into the output kernel"
    wa, wr, wt, ww = _split_weights(w_in[0])
    qkv_a, proj_r, proj_t, w_t = _proj(x, g_in, wa, wr, wt, ww)
    o_a = _attn_a(_alibi_slopes(N_HEADS_A), qkv_a)
    o_b = _dsa(proj_r, proj_t, w_t)
    kv_m = _mem_kv(mem, g_mem, w_mem_kv[0].astype(BF16))
    return _out(x, o_a, o_b, proj_r, kv_m, w_out[0].astype(BF16), g_final[None, :])
```

```python
import functools
import math

import numpy as np

import jax
import jax.numpy as jnp
from jax import lax
from jax.experimental import pallas as pl
from jax.experimental.pallas import tpu as pltpu

F32 = jnp.float32
BF16 = jnp.bfloat16
I32 = jnp.int32

D_MODEL = 1024
HEAD_DIM = 64
N_HEADS_A = 8
N_HEADS_B = 4
N_HEADS_MEM = 4
W_A = N_HEADS_A * HEAD_DIM
W_B = N_HEADS_B * HEAD_DIM
W_M = N_HEADS_MEM * HEAD_DIM
MIX_WIDTH = W_A + W_B + W_M
DILATIONS = (1, 4, 16)
BAND = 128
N_IDX_HEADS = 8
IDX_DIM = 64
TOPK_MAX = 256
RMS_EPS = 1e-6
SPLIT_SIZES = (W_A, W_A, W_A, W_B, HEAD_DIM, HEAD_DIM, W_M, MIX_WIDTH,
               N_IDX_HEADS * IDX_DIM, IDX_DIM, N_IDX_HEADS)

LANES = 128
BLK = 128
NEG = -1e30
VMEM_LIMIT = 48 * 1024 * 1024

R_GATE = 0
R_QM = MIX_WIDTH
R_KB = R_QM + W_M
R_KI = R_KB + LANES
R_COLS = R_KI + LANES
T_QI = 0
T_QB = N_IDX_HEADS * IDX_DIM
T_VB = T_QB + W_B
T_ROWS = T_VB + HEAD_DIM
WI_ROWS = 16
DSA_Q = 256
DSA_HALVINGS = 12
POS_SPLIT = 64
LOG2E = math.log2(math.e)


def _dot(a, b):
    return jnp.dot(a, b, preferred_element_type=F32)


def _dot_nt(a, b):
    return lax.dot_general(a, b, (((1,), (1,)), ((), ())), preferred_element_type=F32)


def _rms(x, g):
    return x * lax.rsqrt(jnp.mean(x * x, axis=-1, keepdims=True) + RMS_EPS) * g


def _proj_kernel(x_ref, g_ref, wa_ref, wr_ref, wt_ref, ww_ref,
                 oa_ref, oa4_ref, oa16_ref, or_ref, ot_ref, ow_ref, res_ref, res4_ref):
    hb = _rms(x_ref[0], g_ref[...]).astype(BF16)
    tm = hb.shape[0]
    d4, d16 = DILATIONS[1], DILATIONS[2]
    step = d16 // d4
    res = _dot(hb, wa_ref[...])
    oa_ref[0] = res.astype(BF16)
    for grp in range(res.shape[1] // LANES):
        lanes = slice(grp * LANES, (grp + 1) * LANES)
        res_ref[grp] = res[:, lanes]
        for r in range(d4):
            rows4 = res_ref[grp, pl.ds(r, tm // d4, stride=d4), :]
            oa4_ref[0, r, :, lanes] = rows4.astype(BF16)
            res4_ref[grp, r * (tm // d4):(r + 1) * (tm // d4), :] = rows4
        for r in range(d16):
            start = (r % d4) * (tm // d4) + r // d4
            oa16_ref[0, r, :, lanes] = res4_ref[grp, pl.ds(start, tm // d16, stride=step), :].astype(BF16)
    or_ref[0] = _dot(hb, wr_ref[...]).astype(BF16)
    ot_ref[0] = _dot_nt(wt_ref[...], hb).astype(BF16)
    ow_ref[0] = _dot_nt(ww_ref[...], hb)


def _proj(x, g, wa, wr, wt, ww, tm=512):
    B, T, D = x.shape
    const = lambda b, i: (0, 0)
    d4, d16 = DILATIONS[1], DILATIONS[2]
    nat, cm4, cm16, proj_r, proj_t, w_t = pl.pallas_call(
        _proj_kernel,
        grid=(B, T // tm),
        in_specs=[
            pl.BlockSpec((1, tm, D), lambda b, i: (b, i, 0)),
            pl.BlockSpec((1, D), const),
            pl.BlockSpec(wa.shape, const),
            pl.BlockSpec(wr.shape, const),
            pl.BlockSpec(wt.shape, const),
            pl.BlockSpec(ww.shape, const),
        ],
        out_specs=[
            pl.BlockSpec((1, tm, 3 * W_A), lambda b, i: (b, i, 0)),
            pl.BlockSpec((1, d4, tm // d4, 3 * W_A), lambda b, i: (b, 0, i, 0)),
            pl.BlockSpec((1, d16, tm // d16, 3 * W_A), lambda b, i: (b, 0, i, 0)),
            pl.BlockSpec((1, tm, R_COLS), lambda b, i: (b, i, 0)),
            pl.BlockSpec((1, T_ROWS, tm), lambda b, i: (b, 0, i)),
            pl.BlockSpec((1, WI_ROWS, tm), lambda b, i: (b, 0, i)),
        ],
        out_shape=[
            jax.ShapeDtypeStruct((B, T, 3 * W_A), BF16),
            jax.ShapeDtypeStruct((B, d4, T // d4, 3 * W_A), BF16),
            jax.ShapeDtypeStruct((B, d16, T // d16, 3 * W_A), BF16),
            jax.ShapeDtypeStruct((B, T, R_COLS), BF16),
            jax.ShapeDtypeStruct((B, T_ROWS, T), BF16),
            jax.ShapeDtypeStruct((B, WI_ROWS, T), F32),
        ],
        scratch_shapes=[pltpu.VMEM((3 * W_A // LANES, tm, LANES), F32)] * 2,
        compiler_params=pltpu.CompilerParams(
            dimension_semantics=("parallel", "parallel"), vmem_limit_bytes=VMEM_LIMIT),
        name="proj",
    )(x, g, wa, wr, wt, ww)
    return (nat, cm4.reshape(nat.shape), cm16.reshape(nat.shape)), proj_r, proj_t, w_t


def _attn_a_kernel(slopes_ref, *refs, seq, unroll):
    n_pat = len(DILATIONS)
    qkv = [refs[3 * p:3 * p + 3] for p in range(n_pat)]
    o_ref, acc_ref, m_ref, bias_ref, p_ref = refs[3 * n_pat:]
    hp = pl.program_id(1)
    lane = lax.broadcasted_iota(I32, (1, LANES), 1)
    head_lanes = (lane < HEAD_DIM, lane >= HEAD_DIM)
    row = lax.broadcasted_iota(I32, (BLK, BLK), 0)
    col = lax.broadcasted_iota(I32, (BLK, BLK), 1)
    d_cur = (row - col).astype(F32)
    d_prev = d_cur + float(BAND)
    n_all = seq // BLK
    zero = jnp.zeros((), BF16)
    one = jnp.ones((), BF16)

    def n_blocks(p):
        return n_all // DILATIONS[p]

    def n_keys(p):
        return BLK if n_blocks(p) == 1 else 2 * BLK

    def block(g):
        return pl.ds(pl.multiple_of(g * BLK, BLK), BLK)

    def keys(p, g):
        return pl.ds(pl.multiple_of(jnp.maximum((g + 1) * BLK - n_keys(p), 0), BLK), n_keys(p))

    def variant(p, g):
        g = jnp.asarray(g, I32)
        return jnp.where(g == 0, 2, jnp.where(g % n_blocks(p) == 0, 1, 0))

    def pitch(dil):
        return seq // dil + 1 if seq // dil == BLK and dil > 1 else None

    def put(ref, h, p, g, val):
        dil = DILATIONS[p]
        start = g // n_blocks(p) + (g % n_blocks(p)) * (dil * BLK)
        if dil == 1:
            ref[h, p, pl.ds(pl.multiple_of(start, BLK), BLK), :] = val
        elif pitch(dil):
            ref[h, p, pl.ds(start * pitch(dil), BLK), :] = val
        else:
            ref[h, p, pl.ds(start, BLK, stride=dil), :] = val

    def probs(p):
        q_ref, k_ref, _ = qkv[p]

        def body(g):
            kw = k_ref[keys(p, g), :]
            q = q_ref[block(g), :]
            for h in range(2):
                s = _dot_nt(jnp.where(head_lanes[h], q, zero), kw)
                s = s + bias_ref[p, h, variant(p, g), :, 2 * BLK - n_keys(p):]
                m = jnp.max(s, axis=1, keepdims=True)
                p_ref[p % 2, h, g, :, :n_keys(p)] = jnp.exp2(s - m).astype(BF16)
                put(m_ref, h, p, g, jnp.broadcast_to(m, (BLK, LANES)))
        return body

    def values(p):
        v_ref = qkv[p][2]

        def body(g):
            vw = v_ref[keys(p, g), :]
            for h in range(2):
                put(acc_ref, h, p, g,
                    _dot(p_ref[p % 2, h, g, :, :n_keys(p)], jnp.where(head_lanes[h], vw, one)))
        return body

    def run(*bodies):
        def step(g, carry):
            for body in bodies:
                body(g)
            return carry
        lax.fori_loop(0, n_all, step, 0, unroll=unroll)

    for p, dil in enumerate(DILATIONS):
        for h in range(2):
            sd = slopes_ref[hp * 2 + h] * (float(dil) * LOG2E)
            cur = jnp.where(row >= col, -sd * d_cur, NEG)
            masked = jnp.full((BLK, BLK), NEG, F32)
            bias_ref[p, h, 0, :, BLK:] = cur
            bias_ref[p, h, 1, :, BLK:] = cur
            bias_ref[p, h, 2, :, BLK:] = cur if n_keys(p) == BLK else masked
            if n_keys(p) > BLK:
                bias_ref[p, h, 0, :, :BLK] = jnp.where(col >= row, -sd * d_prev, NEG)
                bias_ref[p, h, 1, :, :BLK] = masked
                bias_ref[p, h, 2, :, :BLK] = cur

    assert n_pat == 3
    run(probs(0))
    run(values(0), probs(1))
    run(values(1), probs(2))
    run(values(2))

    rows_per_step = 2 * BLK

    def natural_rows(ref, h, p, i):
        dil = DILATIONS[p]
        if not pitch(dil):
            return ref[h, p, pl.ds(pl.multiple_of(i * rows_per_step, rows_per_step), rows_per_step), :]
        per = rows_per_step // dil
        return jnp.concatenate(
            [ref[h, p, pl.ds(i * per + u, dil, stride=pitch(dil)), :] for u in range(per)], axis=0)

    def merge(i, carry):
        sl = pl.ds(pl.multiple_of(i * rows_per_step, rows_per_step), rows_per_step)
        nums = []
        for h in range(2):
            ms = [natural_rows(m_ref, h, p, i) for p in range(n_pat)]
            mx = functools.reduce(jnp.maximum, ms)
            nums.append(sum(jnp.exp2(ms[p] - mx) * natural_rows(acc_ref, h, p, i) for p in range(n_pat)))
        acc = jnp.where(head_lanes[0], nums[0], nums[1])
        den = pltpu.roll(jnp.where(head_lanes[0], nums[1], nums[0]), HEAD_DIM, axis=1)
        o_ref[sl, :] = acc / den
        return carry

    lax.fori_loop(0, seq // rows_per_step, merge, 0, unroll=4)


def _attn_a(slopes, qkv_layouts, unroll=16):
    B, T, _ = qkv_layouts[0].shape
    n_pairs = N_HEADS_A // 2
    n_pat = len(DILATIONS)

    def spec(off):
        return pl.BlockSpec((None, T, LANES), lambda b, h: (b, 0, off + h))

    return pl.pallas_call(
        functools.partial(_attn_a_kernel, seq=T, unroll=unroll),
        grid=(B, n_pairs),
        in_specs=[pl.BlockSpec(memory_space=pltpu.SMEM)]
        + [spec(part * n_pairs) for _ in range(n_pat) for part in range(3)],
        out_specs=pl.BlockSpec((None, T, LANES), lambda b, h: (b, 0, h)),
        out_shape=jax.ShapeDtypeStruct((B, T, W_A), F32),
        scratch_shapes=[pltpu.VMEM((2, n_pat, T + max(DILATIONS), LANES), F32),
                        pltpu.VMEM((2, n_pat, T + max(DILATIONS), LANES), F32),
                        pltpu.VMEM((n_pat, 2, 3, BLK, 2 * BLK), F32),
                        pltpu.VMEM((2, 2, T // BLK, BLK, 2 * BLK), BF16)],
        compiler_params=pltpu.CompilerParams(
            dimension_semantics=("parallel", "parallel"), vmem_limit_bytes=VMEM_LIMIT),
        name="attn_a",
    )(slopes, *[a for a in qkv_layouts for _ in range(3)])


def _bf16_pair(c):
    hi = float(np.asarray(c, np.float32).astype(BF16).astype(np.float32))
    lo = float(np.asarray(c - hi, np.float32).astype(BF16).astype(np.float32))
    return hi, lo


def _fold(x, op):
    return functools.reduce(op, [x[r:r + 8] for r in range(0, x.shape[0], 8)])


def _dsa_kernel(kk_ref, tq_ref, vt_ref, wt_ref, o_ref, sc_ref, s_ref, acc_ref, kpos_ref,
                *, seq, k_sel, snap_unroll, slopes):
    j = pl.program_id(1)
    branches = [functools.partial(_dsa_step, n, kk_ref, tq_ref, vt_ref, wt_ref, o_ref, sc_ref, s_ref,
                                  acc_ref, kpos_ref, seq=seq, k_sel=k_sel, snap_unroll=snap_unroll,
                                  slopes=slopes) for n in range(1, seq // DSA_Q + 1)]
    lax.switch(j, branches)


def _dsa_step(n_pairs, kk_ref, tq_ref, vt_ref, wt_ref, o_ref, sc_ref, s_ref, acc_ref, kpos_ref,
              *, seq, k_sel, snap_unroll, slopes):
    j = n_pairs - 1
    t_lane = j * DSA_Q + lax.broadcasted_iota(I32, (1, DSA_Q), 1)
    row = lax.broadcasted_iota(I32, (BLK, DSA_Q), 0)
    ws = wt_ref[0:N_IDX_HEADS, :] * (N_IDX_HEADS ** -0.5)
    inf = float("inf")

    def rows(c):
        return pl.ds(c * BLK, BLK)

    def each_chunk(fn, carry):
        for c in range(2 * n_pairs):
            carry = fn(c, carry)
        return carry

    def sub_reduce(x, op):
        return op(x, axis=0, keepdims=True)

    def score_chunk(c, carry):
        mn, mx = carry
        ki = kk_ref[rows(c), R_KI - R_KB:R_KI - R_KB + IDX_DIM]
        lgs = [_dot(ki, tq_ref[T_QI + h * IDX_DIM:T_QI + (h + 1) * IDX_DIM, :])
               for h in range(N_IDX_HEADS)]
        sc = functools.reduce(
            lambda a, b: a + b, [jnp.maximum(lg, 0.0) * ws[h:h + 1, :] for h, lg in enumerate(lgs)])
        causal = (c * BLK + row) <= t_lane
        sc_ref[rows(c), :] = jnp.where(causal, sc, -inf)
        mn = jnp.minimum(mn, _fold(jnp.where(causal, sc, inf), jnp.minimum))
        mx = jnp.maximum(mx, _fold(jnp.where(causal, sc, -inf), jnp.maximum))
        return mn, mx

    mn, mx = each_chunk(score_chunk, (jnp.full((8, DSA_Q), inf, F32), jnp.full((8, DSA_Q), -inf, F32)))
    few = t_lane < k_sel
    lo = jnp.where(few, -inf, sub_reduce(mn, jnp.min))
    hi = jnp.where(few, -inf, sub_reduce(mx, jnp.max))

    def count(pred):
        def body(c, cnt):
            return cnt + _fold(pred(sc_ref[rows(c), :], c).astype(I32), jnp.add)
        return sub_reduce(each_chunk(body, jnp.zeros((8, DSA_Q), I32)), jnp.sum)

    def halve(_, bounds):
        lo, hi, n_lo, n_hi = bounds
        mid = 0.5 * lo + 0.5 * hi
        cnt = count(lambda x, c: x >= mid)
        active = lo < hi
        up = active & (cnt >= k_sel)
        down = active & (cnt <= k_sel)
        return (jnp.where(up, mid, lo), jnp.where(down, mid, hi),
                jnp.where(up, cnt, n_lo), jnp.where(down, cnt, n_hi))

    bounds = lax.fori_loop(0, DSA_HALVINGS, halve, (lo, hi, t_lane + 1, jnp.zeros((1, DSA_Q), I32)))

    def snap(bounds):
        lo, hi, n_lo, n_hi = bounds
        mid = 0.5 * lo + 0.5 * hi
        mid = jnp.where(mid > lo, mid, hi)

        def body(c, carry):
            cnt, above, below = carry
            x = sc_ref[rows(c), :]
            ge = x >= mid
            return (cnt + _fold(ge.astype(I32), jnp.add),
                    jnp.minimum(above, _fold(jnp.where(ge, x, inf), jnp.minimum)),
                    jnp.maximum(below, _fold(jnp.where(ge, -inf, x), jnp.maximum)))

        cnt, above, below = each_chunk(body, (jnp.zeros((8, DSA_Q), I32),
                                              jnp.full((8, DSA_Q), inf, F32),
                                              jnp.full((8, DSA_Q), -inf, F32)))
        cnt = sub_reduce(cnt, jnp.sum)
        above = sub_reduce(above, jnp.min)
        below = sub_reduce(below, jnp.max)
        active = lo < hi
        enough = cnt >= k_sel
        up = active & enough
        down = active & (cnt <= k_sel)
        return (jnp.where(up, above, lo), jnp.where(down, jnp.where(enough, above, below), hi),
                jnp.where(up, cnt, n_lo), jnp.where(down, cnt, n_hi))

    def snaps(bounds):
        for _ in range(snap_unroll):
            bounds = snap(bounds)
        return bounds

    def unsettled(bounds):
        return jnp.max((bounds[0] < bounds[1]).astype(I32)) > 0

    tau, _, n_ge, n_gt = lax.while_loop(unsettled, snaps, snaps(bounds))

    need = k_sel - n_gt
    n_idx_bits = seq.bit_length() - 1
    assert 1 << n_idx_bits == seq

    def tie_break():
        def idx_step(it, jp):
            cand = jp + lax.shift_left(jnp.int32(1), n_idx_bits - 1 - it)
            below = count(lambda x, c: (x == tau) & ((c * BLK + row) < cand))
            return jnp.where(below < need, cand, jp)
        last = lax.fori_loop(0, n_idx_bits, idx_step, jnp.zeros((1, DSA_Q), I32))
        return jnp.where(tied, last, seq)

    tied = (n_ge > k_sel) & jnp.logical_not(few)
    surplus = jnp.max(tied.astype(I32)) > 0
    tie_last = lax.cond(surplus, tie_break, lambda: jnp.full((1, DSA_Q), seq, I32))

    if j == 0:
        pos = lax.broadcasted_iota(I32, (seq, LANES), 0)
        ln = lax.broadcasted_iota(I32, (seq, LANES), 1) - HEAD_DIM
        feat = jnp.where(ln == 0, pos // POS_SPLIT,
                         jnp.where(ln == 1, pos % POS_SPLIT,
                                   jnp.where(ln == 2, pos // POS_SPLIT, jnp.where(ln == 3, pos % POS_SPLIT, 0))))
        kpos_ref[...] = kk_ref[:, 0:LANES] + feat.astype(BF16)

    q_row = lax.broadcasted_iota(I32, (HEAD_DIM, DSA_Q), 0)
    q_aug = []
    for h in range(N_HEADS_B):
        c_hi, c_lo = _bf16_pair(slopes[h] * LOG2E)
        feat = jnp.where(q_row == 0, POS_SPLIT * c_hi,
                         jnp.where(q_row == 1, c_hi,
                                   jnp.where(q_row == 2, POS_SPLIT * c_lo,
                                             jnp.where(q_row == 3, c_lo, 0.0))))
        q_aug.append(jnp.concatenate(
            [tq_ref[T_QB + h * HEAD_DIM:T_QB + (h + 1) * HEAD_DIM, :], feat.astype(BF16)], axis=0))

    def qk_chunk(c, ms):
        kb = kpos_ref[rows(c), :]
        s_idx = c * BLK + row
        x = sc_ref[rows(c), :]
        tie = jnp.where(x == tau, jnp.where(s_idx <= tie_last, 0.0, NEG), NEG)
        sb = jnp.where(s_idx <= t_lane, jnp.where(x > tau, 0.0, tie), NEG)
        out = []
        for h in range(N_HEADS_B):
            s = _dot(kb, q_aug[h]) + sb
            s_ref[h, rows(c), :] = s
            out.append(jnp.maximum(ms[h], _fold(s, jnp.maximum)))
        return tuple(out)

    ms = each_chunk(qk_chunk, tuple(jnp.full((8, DSA_Q), NEG, F32) for _ in range(N_HEADS_B)))
    ms = [sub_reduce(m, jnp.max) for m in ms]

    acc_ref[...] = jnp.zeros((W_B, DSA_Q), F32)

    def exp_pv(first_pair, n, ls):
        ls = list(ls)
        pvs = []
        for u in range(n):
            pair = first_pair + u
            ps = []
            for h in range(N_HEADS_B):
                halves = []
                for c in (2 * pair, 2 * pair + 1):
                    p = jnp.exp2(s_ref[h, rows(c), :] - ms[h])
                    ls[h] = ls[h] + _fold(p, jnp.add)
                    halves.append(p.astype(BF16))
                ps.append(jnp.concatenate(halves, axis=0))
            vt = vt_ref[:, pl.ds(pair * 2 * BLK, 2 * BLK)]
            pvs.append([_dot(vt, ps[h]) for h in range(N_HEADS_B)])
        for h in range(N_HEADS_B):
            acc_ref[h * HEAD_DIM:(h + 1) * HEAD_DIM, :] += functools.reduce(
                lambda a, b: a + b, [pv[h] for pv in pvs])
        return tuple(ls)

    ls = tuple(jnp.zeros((8, DSA_Q), F32) for _ in range(N_HEADS_B))
    for first_pair in range(0, n_pairs, 2):
        ls = exp_pv(first_pair, min(2, n_pairs - first_pair), ls)
    l_all = jnp.concatenate(
        [jnp.broadcast_to(sub_reduce(ls[h], jnp.sum), (HEAD_DIM, DSA_Q)) for h in range(N_HEADS_B)],
        axis=0)
    o_ref[...] = (acc_ref[...] / l_all).T


def _dsa(proj_r, proj_t, w_t):
    B, T, _ = proj_r.shape
    k_sel = min(TOPK_MAX, T // 4)
    slopes = _static_alibi_slopes(N_HEADS_B)
    assert T // POS_SPLIT <= 256 and POS_SPLIT <= 256
    return pl.pallas_call(
        functools.partial(_dsa_kernel, seq=T, k_sel=k_sel, snap_unroll=4, slopes=slopes),
        grid=(B, T // DSA_Q),
        in_specs=[
            pl.BlockSpec((None, T, 2 * LANES), lambda b, q: (b, 0, R_KB // (2 * LANES))),
            pl.BlockSpec((None, T_VB, DSA_Q), lambda b, q: (b, 0, q)),
            pl.BlockSpec((None, HEAD_DIM, T), lambda b, q: (b, T_VB // HEAD_DIM, 0)),
            pl.BlockSpec((None, WI_ROWS, DSA_Q), lambda b, q: (b, 0, q)),
        ],
        out_specs=pl.BlockSpec((None, DSA_Q, W_B), lambda b, q: (b, q, 0)),
        out_shape=jax.ShapeDtypeStruct((B, T, W_B), F32),
        scratch_shapes=[pltpu.VMEM((T, DSA_Q), F32),
                        pltpu.VMEM((N_HEADS_B, T, DSA_Q), F32),
                        pltpu.VMEM((W_B, DSA_Q), F32),
                        pltpu.VMEM((T, LANES), BF16)],
        compiler_params=pltpu.CompilerParams(
            dimension_semantics=("parallel", "arbitrary"), vmem_limit_bytes=VMEM_LIMIT),
        name="dsa",
    )(proj_r, proj_t, proj_t, w_t)


def _mem_kv_kernel(mem_ref, g_ref, w_ref, kv_ref):
    kv_ref[0] = _dot(_rms(mem_ref[0], g_ref[...]).astype(BF16), w_ref[...]).astype(BF16)


def _mem_kv(mem, g, w):
    B, M, D = mem.shape
    return pl.pallas_call(
        _mem_kv_kernel,
        grid=(B,),
        in_specs=[
            pl.BlockSpec((1, M, D), lambda b: (b, 0, 0)),
            pl.BlockSpec((1, D), lambda b: (0, 0)),
            pl.BlockSpec(w.shape, lambda b: (0, 0)),
        ],
        out_specs=pl.BlockSpec((1, M, 2 * W_M), lambda b: (b, 0, 0)),
        out_shape=jax.ShapeDtypeStruct((B, M, 2 * W_M), BF16),
        compiler_params=pltpu.CompilerParams(
            dimension_semantics=("parallel",), vmem_limit_bytes=VMEM_LIMIT),
        name="mem_kv",
    )(mem, g, w)


def _out_kernel(x_ref, oa_ref, ob_ref, gate_ref, qm_ref, kv_ref, wo_ref, g_ref, y_ref, mix_ref, *, sub):
    tm = x_ref.shape[0]
    subs = [slice(r0, r0 + sub) for r0 in range(0, tm, sub)]
    lane = lax.broadcasted_iota(I32, (1, W_M), 1)
    head_lanes = [(lane >= h * HEAD_DIM) & (lane < (h + 1) * HEAD_DIM) for h in range(N_HEADS_MEM)]
    km = kv_ref[:, 0:W_M]
    vm = kv_ref[:, W_M:2 * W_M]
    zero = jnp.zeros((), BF16)
    vm_heads = [jnp.where(head_lanes[h], vm, zero) for h in range(N_HEADS_MEM)]

    def silu(g):
        return g * (1.0 / (1.0 + jnp.exp(-g)))

    def gated(o, rows, lo, width):
        return (o * silu(gate_ref[rows, lo:lo + width].astype(F32))).astype(BF16)

    for rows in subs:
        qm = qm_ref[rows, :]
        om = None
        for h in range(N_HEADS_MEM):
            s = _dot_nt(jnp.where(head_lanes[h], qm, zero), km)
            p = jnp.exp2(s - jnp.max(s, axis=1, keepdims=True))
            p = p / jnp.sum(p, axis=1, keepdims=True)
            o = _dot(p.astype(BF16), vm_heads[h])
            om = o if om is None else om + o
        mix_ref[rows, W_A + W_B:] = gated(om, rows, W_A + W_B, W_M)

    for rows in subs:
        half = W_A // 2
        mix_ref[rows, 0:half] = gated(oa_ref[rows, 0:half], rows, 0, half)
        mix_ref[rows, half:W_A] = gated(oa_ref[rows, half:W_A], rows, half, half)
        mix_ref[rows, W_A:W_A + W_B] = gated(ob_ref[rows, :], rows, W_A, W_B)
        y_ref[rows, :] = _dot(mix_ref[rows, :], wo_ref[...])

    for rows in subs:
        y_ref[rows, :] = _rms(x_ref[rows, :] + y_ref[rows, :], g_ref[...])


def _out(x, o_a, o_b, proj_r, kv_m, w_out, g, tm=512, sub=256):
    B, T, D = x.shape
    M = kv_m.shape[1]
    row = lambda b, i: (b, i, 0)
    return pl.pallas_call(
        functools.partial(_out_kernel, sub=sub),
        grid=(B, T // tm),
        in_specs=[
            pl.BlockSpec((None, tm, D), row),
            pl.BlockSpec((None, tm, W_A), row),
            pl.BlockSpec((None, tm, W_B), row),
            pl.BlockSpec((None, tm, MIX_WIDTH), lambda b, i: (b, i, R_GATE // MIX_WIDTH)),
            pl.BlockSpec((None, tm, W_M), lambda b, i: (b, i, R_QM // W_M)),
            pl.BlockSpec((None, M, 2 * W_M), lambda b, i: (b, 0, 0)),
            pl.BlockSpec(w_out.shape, lambda b, i: (0, 0)),
            pl.BlockSpec((1, D), lambda b, i: (0, 0)),
        ],
        out_specs=pl.BlockSpec((None, tm, D), row),
        out_shape=jax.ShapeDtypeStruct((B, T, D), F32),
        scratch_shapes=[pltpu.VMEM((tm, MIX_WIDTH), BF16)],
        compiler_params=pltpu.CompilerParams(
            dimension_semantics=("parallel", "parallel"), vmem_limit_bytes=VMEM_LIMIT),
        name="out",
    )(x, o_a, o_b, proj_r, proj_r, kv_m, w_out, g)


def _alibi_slopes(n):
    return 2.0 ** (-8.0 * jnp.arange(1, n + 1, dtype=F32) / n)


def _static_alibi_slopes(n):
    return tuple(2.0 ** (-8.0 * i / n) for i in range(1, n + 1))


def _split_weights(w):
    bounds = [0]
    for s in SPLIT_SIZES:
        bounds.append(bounds[-1] + s)
    q_a, k_a, v_a, q_b, k_b, v_b, q_m, gate, q_i, k_i, w_i = (
        w[:, bounds[i]:bounds[i + 1]] for i in range(len(SPLIT_SIZES)))
    scale = HEAD_DIM ** -0.5 * LOG2E
    idx_scale = IDX_DIM ** -0.5
    zeros = jnp.zeros((w.shape[0], LANES - HEAD_DIM), w.dtype)
    wa = jnp.concatenate([q_a * scale, k_a, v_a], axis=1)
    wr = jnp.concatenate([gate, q_m * scale, k_b, zeros, k_i, zeros], axis=1)
    wt = jnp.concatenate([q_i * idx_scale, q_b * scale, v_b], axis=1).T
    ww = jnp.concatenate([w_i.T, jnp.zeros((WI_ROWS - N_IDX_HEADS, w.shape[0]), w.dtype)], axis=0)
    return wa.astype(BF16), wr.astype(BF16), wt.astype(BF16), ww.astype(BF16)


def kernel(x, mem, g_in, g_mem, w_in, w_mem_kv, w_out, g_final):
    assert g_in.shape[0] == 1, "single-layer block: the final RMSNorm is fused into the output kernel"
    wa, wr, wt, ww = _split_weights(w_in[0])
    qkv_a, proj_r, proj_t, w_t = _proj(x, g_in, wa, wr, wt, ww)
    o_a = _attn_a(_alibi_slopes(N_HEADS_A), qkv_a)
    o_b = _dsa(proj_r, proj_t, w_t)
    kv_m = _mem_kv(mem, g_mem, w_mem_kv[0].astype(BF16))
    return _out(x, o_a, o_b, proj_r, kv_m, w_out[0].astype(BF16), g_final[None, :])
```

```python
import functools
import math

import numpy as np

import jax
import jax.numpy as jnp
from jax import lax
from jax.experimental import pallas as pl
from jax.experimental.pallas import tpu as pltpu

F32 = jnp.float32
BF16 = jnp.bfloat16
I32 = jnp.int32

D_MODEL = 1024
HEAD_DIM = 64
N_HEADS_A = 8
N_HEADS_B = 4
N_HEADS_MEM = 4
W_A = N_HEADS_A * HEAD_DIM
W_B = N_HEADS_B * HEAD_DIM
W_M = N_HEADS_MEM * HEAD_DIM
MIX_WIDTH = W_A + W_B + W_M
DILATIONS = (1, 4, 16)
BAND = 128
N_IDX_HEADS = 8
IDX_DIM = 64
TOPK_MAX = 256
RMS_EPS = 1e-6
SPLIT_SIZES = (W_A, W_A, W_A, W_B, HEAD_DIM, HEAD_DIM, W_M, MIX_WIDTH,
               N_IDX_HEADS * IDX_DIM, IDX_DIM, N_IDX_HEADS)

LANES = 128
BLK = 128
NEG = -1e30
VMEM_LIMIT = 48 * 1024 * 1024

R_GATE = 0
R_QM = MIX_WIDTH
R_KB = R_QM + W_M
R_KI = R_KB + LANES
R_COLS = R_KI + LANES
T_QI = 0
T_QB = N_IDX_HEADS * IDX_DIM
T_VB = T_QB + W_B
T_ROWS = T_VB + HEAD_DIM
WI_ROWS = 16
DSA_Q = 256
DSA_HALVINGS = 12
POS_SPLIT = 64
LOG2E = math.log2(math.e)


def _dot(a, b):
    return jnp.dot(a, b, preferred_element_type=F32)


def _dot_nt(a, b):
    return lax.dot_general(a, b, (((1,), (1,)), ((), ())), preferred_element_type=F32)


def _rms(x, g):
    return x * lax.rsqrt(jnp.mean(x * x, axis=-1, keepdims=True) + RMS_EPS) * g


def _proj_kernel(x_ref, g_ref, wa_ref, wr_ref, wt_ref, ww_ref,
                 oa_ref, oa4_ref, oa16_ref, or_ref, ot_ref, ow_ref, res_ref, res4_ref):
    hb = _rms(x_ref[0], g_ref[...]).astype(BF16)
    tm = hb.shape[0]
    d4, d16 = DILATIONS[1], DILATIONS[2]
    step = d16 // d4
    res = _dot(hb, wa_ref[...])
    oa_ref[0] = res.astype(BF16)
    for grp in range(res.shape[1] // LANES):
        lanes = slice(grp * LANES, (grp + 1) * LANES)
        res_ref[grp] = res[:, lanes]
        for r in range(d4):
            rows4 = res_ref[grp, pl.ds(r, tm // d4, stride=d4), :]
            oa4_ref[0, r, :, lanes] = rows4.astype(BF16)
            res4_ref[grp, r * (tm // d4):(r + 1) * (tm // d4), :] = rows4
        for r in range(d16):
            start = (r % d4) * (tm // d4) + r // d4
            oa16_ref[0, r, :, lanes] = res4_ref[grp, pl.ds(start, tm // d16, stride=step), :].astype(BF16)
    or_ref[0] = _dot(hb, wr_ref[...]).astype(BF16)
    ot_ref[0] = _dot_nt(wt_ref[...], hb).astype(BF16)
    ow_ref[0] = _dot_nt(ww_ref[...], hb)


def _proj(x, g, wa, wr, wt, ww, tm=512):
    B, T, D = x.shape
    const = lambda b, i: (0, 0)
    d4, d16 = DILATIONS[1], DILATIONS[2]
    nat, cm4, cm16, proj_r, proj_t, w_t = pl.pallas_call(
        _proj_kernel,
        grid=(B, T // tm),
        in_specs=[
            pl.BlockSpec((1, tm, D), lambda b, i: (b, i, 0)),
            pl.BlockSpec((1, D), const),
            pl.BlockSpec(wa.shape, const),
            pl.BlockSpec(wr.shape, const),
            pl.BlockSpec(wt.shape, const),
            pl.BlockSpec(ww.shape, const),
        ],
        out_specs=[
            pl.BlockSpec((1, tm, 3 * W_A), lambda b, i: (b, i, 0)),
            pl.BlockSpec((1, d4, tm // d4, 3 * W_A), lambda b, i: (b, 0, i, 0)),
            pl.BlockSpec((1, d16, tm // d16, 3 * W_A), lambda b, i: (b, 0, i, 0)),
            pl.BlockSpec((1, tm, R_COLS), lambda b, i: (b, i, 0)),
            pl.BlockSpec((1, T_ROWS, tm), lambda b, i: (b, 0, i)),
            pl.BlockSpec((1, WI_ROWS, tm), lambda b, i: (b, 0, i)),
        ],
        out_shape=[
            jax.ShapeDtypeStruct((B, T, 3 * W_A), BF16),
            jax.ShapeDtypeStruct((B, d4, T // d4, 3 * W_A), BF16),
            jax.ShapeDtypeStruct((B, d16, T // d16, 3 * W_A), BF16),
            jax.ShapeDtypeStruct((B, T, R_COLS), BF16),
            jax.ShapeDtypeStruct((B, T_ROWS, T), BF16),
            jax.ShapeDtypeStruct((B, WI_ROWS, T), F32),
        ],
        scratch_shapes=[pltpu.VMEM((3 * W_A // LANES, tm, LANES), F32)] * 2,
        compiler_params=pltpu.CompilerParams(
            dimension_semantics=("parallel", "parallel"), vmem_limit_bytes=VMEM_LIMIT),
        name="proj",
    )(x, g, wa, wr, wt, ww)
    return (nat, cm4.reshape(nat.shape), cm16.reshape(nat.shape)), proj_r, proj_t, w_t


def _attn_a_kernel(slopes_ref, *refs, seq, unroll):
    n_pat = len(DILATIONS)
    qkv = [refs[3 * p:3 * p + 3] for p in range(n_pat)]
    o_ref, acc_ref, m_ref, bias_ref, p_ref = refs[3 * n_pat:]
    hp = pl.program_id(1)
    lane = lax.broadcasted_iota(I32, (1, LANES), 1)
    head_lanes = (lane < HEAD_DIM, lane >= HEAD_DIM)
    row = lax.broadcasted_iota(I32, (BLK, BLK), 0)
    col = lax.broadcasted_iota(I32, (BLK, BLK), 1)
    d_cur = (row - col).astype(F32)
    d_prev = d_cur + float(BAND)
    n_all = seq // BLK
    zero = jnp.zeros((), BF16)
    one = jnp.ones((), BF16)

    def n_blocks(p):
        return n_all // DILATIONS[p]

    def n_keys(p):
        return BLK if n_blocks(p) == 1 else 2 * BLK

    def block(g):
        return pl.ds(pl.multiple_of(g * BLK, BLK), BLK)

    def keys(p, g):
        return pl.ds(pl.multiple_of(jnp.maximum((g + 1) * BLK - n_keys(p), 0), BLK), n_keys(p))

    def variant(p, g):
        g = jnp.asarray(g, I32)
        return jnp.where(g == 0, 2, jnp.where(g % n_blocks(p) == 0, 1, 0))

    def pitch(dil):
        return seq // dil + 1 if seq // dil == BLK and dil > 1 else None

    def put(ref, h, p, g, val):
        dil = DILATIONS[p]
        start = g // n_blocks(p) + (g % n_blocks(p)) * (dil * BLK)
        if dil == 1:
            ref[h, p, pl.ds(pl.multiple_of(start, BLK), BLK), :] = val
        elif pitch(dil):
            ref[h, p, pl.ds(start * pitch(dil), BLK), :] = val
        else:
            ref[h, p, pl.ds(start, BLK, stride=dil), :] = val

    def probs(p):
        q_ref, k_ref, _ = qkv[p]

        def body(g):
            kw = k_ref[keys(p, g), :]
            q = q_ref[block(g), :]
            for h in range(2):
                s = _dot_nt(jnp.where(head_lanes[h], q, zero), kw)
                s = s + bias_ref[p, h, variant(p, g), :, 2 * BLK - n_keys(p):]
                m = jnp.max(s, axis=1, keepdims=True)
                p_ref[p % 2, h, g, :, :n_keys(p)] = jnp.exp2(s - m).astype(BF16)
                put(m_ref, h, p, g, jnp.broadcast_to(m, (BLK, LANES)))
        return body

    def values(p):
        v_ref = qkv[p][2]

        def body(g):
            vw = v_ref[keys(p, g), :]
            for h in range(2):
                put(acc_ref, h, p, g,
                    _dot(p_ref[p % 2, h, g, :, :n_keys(p)], jnp.where(head_lanes[h], vw, one)))
        return body

    def run(*bodies):
        def step(g, carry):
            for body in bodies:
                body(g)
            return carry
        lax.fori_loop(0, n_all, step, 0, unroll=unroll)

    for p, dil in enumerate(DILATIONS):
        for h in range(2):
            sd = slopes_ref[hp * 2 + h] * (float(dil) * LOG2E)
            cur = jnp.where(row >= col, -sd * d_cur, NEG)
            masked = jnp.full((BLK, BLK), NEG, F32)
            bias_ref[p, h, 0, :, BLK:] = cur
            bias_ref[p, h, 1, :, BLK:] = cur
            bias_ref[p, h, 2, :, BLK:] = cur if n_keys(p) == BLK else masked
            if n_keys(p) > BLK:
                bias_ref[p, h, 0, :, :BLK] = jnp.where(col >= row, -sd * d_prev, NEG)
                bias_ref[p, h, 1, :, :BLK] = masked
                bias_ref[p, h, 2, :, :BLK] = cur

    assert n_pat == 3
    run(probs(0))
    run(values(0), probs(1))
    run(values(1), probs(2))
    run(values(2))

    rows_per_step = 2 * BLK

    def natural_rows(ref, h, p, i):
        dil = DILATIONS[p]
        if not pitch(dil):
            return ref[h, p, pl.ds(pl.multiple_of(i * rows_per_step, rows_per_step), rows_per_step), :]
        per = rows_per_step // dil
        return jnp.concatenate(
            [ref[h, p, pl.ds(i * per + u, dil, stride=pitch(dil)), :] for u in range(per)], axis=0)

    def merge(i, carry):
        sl = pl.ds(pl.multiple_of(i * rows_per_step, rows_per_step), rows_per_step)
        nums = []
        for h in range(2):
            ms = [natural_rows(m_ref, h, p, i) for p in range(n_pat)]
            mx = functools.reduce(jnp.maximum, ms)
            nums.append(sum(jnp.exp2(ms[p] - mx) * natural_rows(acc_ref, h, p, i) for p in range(n_pat)))
        acc = jnp.where(head_lanes[0], nums[0], nums[1])
        den = pltpu.roll(jnp.where(head_lanes[0], nums[1], nums[0]), HEAD_DIM, axis=1)
        o_ref[sl, :] = acc / den
        return carry

    lax.fori_loop(0, seq // rows_per_step, merge, 0, unroll=4)


def _attn_a(slopes, qkv_layouts, unroll=16):
    B, T, _ = qkv_layouts[0].shape
    n_pairs = N_HEADS_A // 2
    n_pat = len(DILATIONS)

    def spec(off):
        return pl.BlockSpec((None, T, LANES), lambda b, h: (b, 0, off + h))

    return pl.pallas_call(
        functools.partial(_attn_a_kernel, seq=T, unroll=unroll),
        grid=(B, n_pairs),
        in_specs=[pl.BlockSpec(memory_space=pltpu.SMEM)]
        + [spec(part * n_pairs) for _ in range(n_pat) for part in range(3)],
        out_specs=pl.BlockSpec((None, T, LANES), lambda b, h: (b, 0, h)),
        out_shape=jax.ShapeDtypeStruct((B, T, W_A), F32),
        scratch_shapes=[pltpu.VMEM((2, n_pat, T + max(DILATIONS), LANES), F32),
                        pltpu.VMEM((2, n_pat, T + max(DILATIONS), LANES), F32),
                        pltpu.VMEM((n_pat, 2, 3, BLK, 2 * BLK), F32),
                        pltpu.VMEM((2, 2, T // BLK, BLK, 2 * BLK), BF16)],
        compiler_params=pltpu.CompilerParams(
            dimension_semantics=("parallel", "parallel"), vmem_limit_bytes=VMEM_LIMIT),
        name="attn_a",
    )(slopes, *[a for a in qkv_layouts for _ in range(3)])


def _bf16_pair(c):
    hi = float(np.asarray(c, np.float32).astype(BF16).astype(np.float32))
    lo = float(np.asarray(c - hi, np.float32).astype(BF16).astype(np.float32))
    return hi, lo


def _fold(x, op):
    return functools.reduce(op, [x[r:r + 8] for r in range(0, x.shape[0], 8)])


def _dsa_kernel(kk_ref, tq_ref, vt_ref, wt_ref, o_ref, sc_ref, s_ref, acc_ref, kpos_ref,
                *, seq, k_sel, snap_unroll, slopes):
    j = pl.program_id(1)
    n_pairs = j + 1
    t_lane = j * DSA_Q + lax.broadcasted_iota(I32, (1, DSA_Q), 1)
    row = lax.broadcasted_iota(I32, (BLK, DSA_Q), 0)
    ws = wt_ref[0:N_IDX_HEADS, :] * (N_IDX_HEADS ** -0.5)
    inf = float("inf")

    def rows(c):
        return pl.ds(pl.multiple_of(c * BLK, BLK), BLK)

    def each_chunk(fn, init, widest=4):
        def group(first, n, carry):
            for u in range(n):
                carry = fn(first + u, carry)
            return carry
        carry, done, width = init, 0, widest
        while width >= 2:
            steps = (2 * n_pairs - done) // width
            carry = lax.fori_loop(
                0, steps, lambda i, c, done=done, width=width: group(done + width * i, width, c), carry)
            done, width = done + steps * width, width // 2
        return carry

    def sub_reduce(x, op):
        return op(x, axis=0, keepdims=True)

    def score_chunk(c, carry):
        mn, mx = carry
        ki = kk_ref[rows(c), R_KI - R_KB:R_KI - R_KB + IDX_DIM]
        lgs = [_dot(ki, tq_ref[T_QI + h * IDX_DIM:T_QI + (h + 1) * IDX_DIM, :])
               for h in range(N_IDX_HEADS)]
        sc = functools.reduce(
            lambda a, b: a + b, [jnp.maximum(lg, 0.0) * ws[h:h + 1, :] for h, lg in enumerate(lgs)])
        causal = (c * BLK + row) <= t_lane
        sc_ref[rows(c), :] = jnp.where(causal, sc, -inf)
        mn = jnp.minimum(mn, _fold(jnp.where(causal, sc, inf), jnp.minimum))
        mx = jnp.maximum(mx, _fold(jnp.where(causal, sc, -inf), jnp.maximum))
        return mn, mx

    mn, mx = each_chunk(score_chunk, (jnp.full((8, DSA_Q), inf, F32), jnp.full((8, DSA_Q), -inf, F32)),
                        widest=8)
    few = t_lane < k_sel
    lo = jnp.where(few, -inf, sub_reduce(mn, jnp.min))
    hi = jnp.where(few, -inf, sub_reduce(mx, jnp.max))

    def count(pred):
        def body(c, cnt):
            return cnt + _fold(pred(sc_ref[rows(c), :], c).astype(I32), jnp.add)
        return sub_reduce(each_chunk(body, jnp.zeros((8, DSA_Q), I32)), jnp.sum)

    def halve(_, bounds):
        lo, hi, n_lo, n_hi = bounds
        mid = 0.5 * lo + 0.5 * hi
        cnt = count(lambda x, c: x >= mid)
        active = lo < hi
        up = active & (cnt >= k_sel)
        down = active & (cnt <= k_sel)
        return (jnp.where(up, mid, lo), jnp.where(down, mid, hi),
                jnp.where(up, cnt, n_lo), jnp.where(down, cnt, n_hi))

    bounds = lax.fori_loop(0, DSA_HALVINGS, halve, (lo, hi, t_lane + 1, jnp.zeros((1, DSA_Q), I32)))

    def snap(bounds):
        lo, hi, n_lo, n_hi = bounds
        mid = 0.5 * lo + 0.5 * hi
        mid = jnp.where(mid > lo, mid, hi)

        def body(c, carry):
            cnt, above, below = carry
            x = sc_ref[rows(c), :]
            ge = x >= mid
            return (cnt + _fold(ge.astype(I32), jnp.add),
                    jnp.minimum(above, _fold(jnp.where(ge, x, inf), jnp.minimum)),
                    jnp.maximum(below, _fold(jnp.where(ge, -inf, x), jnp.maximum)))

        cnt, above, below = each_chunk(body, (jnp.zeros((8, DSA_Q), I32),
                                              jnp.full((8, DSA_Q), inf, F32),
                                              jnp.full((8, DSA_Q), -inf, F32)))
        cnt = sub_reduce(cnt, jnp.sum)
        above = sub_reduce(above, jnp.min)
        below = sub_reduce(below, jnp.max)
        active = lo < hi
        enough = cnt >= k_sel
        up = active & enough
        down = active & (cnt <= k_sel)
        return (jnp.where(up, above, lo), jnp.where(down, jnp.where(enough, above, below), hi),
                jnp.where(up, cnt, n_lo), jnp.where(down, cnt, n_hi))

    def snaps(bounds):
        for _ in range(snap_unroll):
            bounds = snap(bounds)
        return bounds

    def unsettled(bounds):
        return jnp.max((bounds[0] < bounds[1]).astype(I32)) > 0

    tau, _, n_ge, n_gt = lax.while_loop(unsettled, snaps, snaps(bounds))

    need = k_sel - n_gt
    n_idx_bits = seq.bit_length() - 1
    assert 1 << n_idx_bits == seq

    def tie_break():
        def idx_step(it, jp):
            cand = jp + lax.shift_left(jnp.int32(1), n_idx_bits - 1 - it)
            below = count(lambda x, c: (x == tau) & ((c * BLK + row) < cand))
            return jnp.where(below < need, cand, jp)
        last = lax.fori_loop(0, n_idx_bits, idx_step, jnp.zeros((1, DSA_Q), I32))
        return jnp.where(tied, last, seq)

    tied = (n_ge > k_sel) & jnp.logical_not(few)
    surplus = jnp.max(tied.astype(I32)) > 0
    tie_last = lax.cond(surplus, tie_break, lambda: jnp.full((1, DSA_Q), seq, I32))

    @pl.when(j == 0)
    def _():
        pos = lax.broadcasted_iota(I32, (seq, LANES), 0)
        ln = lax.broadcasted_iota(I32, (seq, LANES), 1) - HEAD_DIM
        feat = jnp.where(ln == 0, pos // POS_SPLIT,
                         jnp.where(ln == 1, pos % POS_SPLIT,
                                   jnp.where(ln == 2, pos // POS_SPLIT, jnp.where(ln == 3, pos % POS_SPLIT, 0))))
        kpos_ref[...] = kk_ref[:, 0:LANES] + feat.astype(BF16)

    q_row = lax.broadcasted_iota(I32, (HEAD_DIM, DSA_Q), 0)
    q_aug = []
    for h in range(N_HEADS_B):
        c_hi, c_lo = _bf16_pair(slopes[h] * LOG2E)
        feat = jnp.where(q_row == 0, POS_SPLIT * c_hi,
                         jnp.where(q_row == 1, c_hi,
                                   jnp.where(q_row == 2, POS_SPLIT * c_lo,
                                             jnp.where(q_row == 3, c_lo, 0.0))))
        q_aug.append(jnp.concatenate(
            [tq_ref[T_QB + h * HEAD_DIM:T_QB + (h + 1) * HEAD_DIM, :], feat.astype(BF16)], axis=0))

    def qk_chunk(c, ms):
        kb = kpos_ref[rows(c), :]
        s_idx = c * BLK + row
        x = sc_ref[rows(c), :]
        tie = jnp.where(x == tau, jnp.where(s_idx <= tie_last, 0.0, NEG), NEG)
        sb = jnp.where(s_idx <= t_lane, jnp.where(x > tau, 0.0, tie), NEG)
        out = []
        for h in range(N_HEADS_B):
            s = _dot(kb, q_aug[h]) + sb
            s_ref[h, rows(c), :] = s
            out.append(jnp.maximum(ms[h], _fold(s, jnp.maximum)))
        return tuple(out)

    ms = each_chunk(qk_chunk, tuple(jnp.full((8, DSA_Q), NEG, F32) for _ in range(N_HEADS_B)), widest=8)
    ms = [sub_reduce(m, jnp.max) for m in ms]

    acc_ref[...] = jnp.zeros((W_B, DSA_Q), F32)

    def exp_pv(first_pair, n, ls):
        ls = list(ls)
        pvs = []
        for u in range(n):
            pair = first_pair + u
            ps = []
            for h in range(N_HEADS_B):
                halves = []
                for c in (2 * pair, 2 * pair + 1):
                    p = jnp.exp2(s_ref[h, rows(c), :] - ms[h])
                    ls[h] = ls[h] + _fold(p, jnp.add)
                    halves.append(p.astype(BF16))
                ps.append(jnp.concatenate(halves, axis=0))
            vt = vt_ref[:, pl.ds(pl.multiple_of(pair * 2 * BLK, 2 * BLK), 2 * BLK)]
            pvs.append([_dot(vt, ps[h]) for h in range(N_HEADS_B)])
        for h in range(N_HEADS_B):
            acc_ref[h * HEAD_DIM:(h + 1) * HEAD_DIM, :] += functools.reduce(
                lambda a, b: a + b, [pv[h] for pv in pvs])
        return tuple(ls)

    ls = tuple(jnp.zeros((8, DSA_Q), F32) for _ in range(N_HEADS_B))
    ls = lax.fori_loop(0, n_pairs // 2, lambda i, c: exp_pv(2 * i, 2, c), ls)
    ls = lax.fori_loop(0, n_pairs % 2, lambda i, c: exp_pv(n_pairs - 1, 1, c), ls)
    l_all = jnp.concatenate(
        [jnp.broadcast_to(sub_reduce(ls[h], jnp.sum), (HEAD_DIM, DSA_Q)) for h in range(N_HEADS_B)],
        axis=0)
    o_ref[...] = (acc_ref[...] / l_all).T


def _dsa(proj_r, proj_t, w_t):
    B, T, _ = proj_r.shape
    k_sel = min(TOPK_MAX, T // 4)
    slopes = _static_alibi_slopes(N_HEADS_B)
    assert T // POS_SPLIT <= 256 and POS_SPLIT <= 256
    return pl.pallas_call(
        functools.partial(_dsa_kernel, seq=T, k_sel=k_sel, snap_unroll=4, slopes=slopes),
        grid=(B, T // DSA_Q),
        in_specs=[
            pl.BlockSpec((None, T, 2 * LANES), lambda b, q: (b, 0, R_KB // (2 * LANES))),
            pl.BlockSpec((None, T_VB, DSA_Q), lambda b, q: (b, 0, q)),
            pl.BlockSpec((None, HEAD_DIM, T), lambda b, q: (b, T_VB // HEAD_DIM, 0)),
            pl.BlockSpec((None, WI_ROWS, DSA_Q), lambda b, q: (b, 0, q)),
        ],
        out_specs=pl.BlockSpec((None, DSA_Q, W_B), lambda b, q: (b, q, 0)),
        out_shape=jax.ShapeDtypeStruct((B, T, W_B), F32),
        scratch_shapes=[pltpu.VMEM((T, DSA_Q), F32),
                        pltpu.VMEM((N_HEADS_B, T, DSA_Q), F32),
                        pltpu.VMEM((W_B, DSA_Q), F32),
                        pltpu.VMEM((T, LANES), BF16)],
        compiler_params=pltpu.CompilerParams(
            dimension_semantics=("parallel", "arbitrary"), vmem_limit_bytes=VMEM_LIMIT),
        name="dsa",
    )(proj_r, proj_t, proj_t, w_t)


def _mem_kv_kernel(mem_ref, g_ref, w_ref, kv_ref):
    kv_ref[0] = _dot(_rms(mem_ref[0], g_ref[...]).astype(BF16), w_ref[...]).astype(BF16)


def _mem_kv(mem, g, w):
    B, M, D = mem.shape
    return pl.pallas_call(
        _mem_kv_kernel,
        grid=(B,),
        in_specs=[
            pl.BlockSpec((1, M, D), lambda b: (b, 0, 0)),
            pl.BlockSpec((1, D), lambda b: (0, 0)),
            pl.BlockSpec(w.shape, lambda b: (0, 0)),
        ],
        out_specs=pl.BlockSpec((1, M, 2 * W_M), lambda b: (b, 0, 0)),
        out_shape=jax.ShapeDtypeStruct((B, M, 2 * W_M), BF16),
        compiler_params=pltpu.CompilerParams(
            dimension_semantics=("parallel",), vmem_limit_bytes=VMEM_LIMIT),
        name="mem_kv",
    )(mem, g, w)


def _out_kernel(x_ref, oa_ref, ob_ref, gate_ref, qm_ref, kv_ref, wo_ref, g_ref, y_ref, mix_ref, *, sub):
    tm = x_ref.shape[0]
    subs = [slice(r0, r0 + sub) for r0 in range(0, tm, sub)]
    lane = lax.broadcasted_iota(I32, (1, W_M), 1)
    head_lanes = [(lane >= h * HEAD_DIM) & (lane < (h + 1) * HEAD_DIM) for h in range(N_HEADS_MEM)]
    km = kv_ref[:, 0:W_M]
    vm = kv_ref[:, W_M:2 * W_M]
    zero = jnp.zeros((), BF16)
    vm_heads = [jnp.where(head_lanes[h], vm, zero) for h in range(N_HEADS_MEM)]

    def silu(g):
        return g * (1.0 / (1.0 + jnp.exp(-g)))

    def gated(o, rows, lo, width):
        return (o * silu(gate_ref[rows, lo:lo + width].astype(F32))).astype(BF16)

    for rows in subs:
        qm = qm_ref[rows, :]
        om = None
        for h in range(N_HEADS_MEM):
            s = _dot_nt(jnp.where(head_lanes[h], qm, zero), km)
            p = jnp.exp2(s - jnp.max(s, axis=1, keepdims=True))
            p = p / jnp.sum(p, axis=1, keepdims=True)
            o = _dot(p.astype(BF16), vm_heads[h])
            om = o if om is None else om + o
        mix_ref[rows, W_A + W_B:] = gated(om, rows, W_A + W_B, W_M)

    for rows in subs:
        half = W_A // 2
        mix_ref[rows, 0:half] = gated(oa_ref[rows, 0:half], rows, 0, half)
        mix_ref[rows, half:W_A] = gated(oa_ref[rows, half:W_A], rows, half, half)
        mix_ref[rows, W_A:W_A + W_B] = gated(ob_ref[rows, :], rows, W_A, W_B)
        y_ref[rows, :] = _dot(mix_ref[rows, :], wo_ref[...])

    for rows in subs:
        y_ref[rows, :] = _rms(x_ref[rows, :] + y_ref[rows, :], g_ref[...])


def _out(x, o_a, o_b, proj_r, kv_m, w_out, g, tm=512, sub=256):
    B, T, D = x.shape
    M = kv_m.shape[1]
    row = lambda b, i: (b, i, 0)
    return pl.pallas_call(
        functools.partial(_out_kernel, sub=sub),
        grid=(B, T // tm),
        in_specs=[
            pl.BlockSpec((None, tm, D), row),
            pl.BlockSpec((None, tm, W_A), row),
            pl.BlockSpec((None, tm, W_B), row),
            pl.BlockSpec((None, tm, MIX_WIDTH), lambda b, i: (b, i, R_GATE // MIX_WIDTH)),
            pl.BlockSpec((None, tm, W_M), lambda b, i: (b, i, R_QM // W_M)),
            pl.BlockSpec((None, M, 2 * W_M), lambda b, i: (b, 0, 0)),
            pl.BlockSpec(w_out.shape, lambda b, i: (0, 0)),
            pl.BlockSpec((1, D), lambda b, i: (0, 0)),
        ],
        out_specs=pl.BlockSpec((None, tm, D), row),
        out_shape=jax.ShapeDtypeStruct((B, T, D), F32),
        scratch_shapes=[pltpu.VMEM((tm, MIX_WIDTH), BF16)],
        compiler_params=pltpu.CompilerParams(
            dimension_semantics=("parallel", "parallel"), vmem_limit_bytes=VMEM_LIMIT),
        name="out",
    )(x, o_a, o_b, proj_r, proj_r, kv_m, w_out, g)


def _alibi_slopes(n):
    return 2.0 ** (-8.0 * jnp.arange(1, n + 1, dtype=F32) / n)


def _static_alibi_slopes(n):
    return tuple(2.0 ** (-8.0 * i / n) for i in range(1, n + 1))


def _split_weights(w):
    bounds = [0]
    for s in SPLIT_SIZES:
        bounds.append(bounds[-1] + s)
    q_a, k_a, v_a, q_b, k_b, v_b, q_m, gate, q_i, k_i, w_i = (
        w[:, bounds[i]:bounds[i + 1]] for i in range(len(SPLIT_SIZES)))
    scale = HEAD_DIM ** -0.5 * LOG2E
    idx_scale = IDX_DIM ** -0.5
    zeros = jnp.zeros((w.shape[0], LANES - HEAD_DIM), w.dtype)
    wa = jnp.concatenate([q_a * scale, k_a, v_a], axis=1)
    wr = jnp.concatenate([gate, q_m * scale, k_b, zeros, k_i, zeros], axis=1)
    wt = jnp.concatenate([q_i * idx_scale, q_b * scale, v_b], axis=1).T
    ww = jnp.concatenate([w_i.T, jnp.zeros((WI_ROWS - N_IDX_HEADS, w.shape[0]), w.dtype)], axis=0)
    return wa.astype(BF16), wr.astype(BF16), wt.astype(BF16), ww.astype(BF16)


def kernel(x, mem, g_in, g_mem, w_in, w_mem_kv, w_out, g_final):
    assert g_in.shape[0] == 1, "single-layer block: the final RMSNorm is fused into the output kernel"
    wa, wr, wt, ww = _split_weights(w_in[0])
    qkv_a, proj_r, proj_t, w_t = _proj(x, g_in, wa, wr, wt, ww)
    o_a = _attn_a(_alibi_slopes(N_HEADS_A), qkv_a)
    o_b = _dsa(proj_r, proj_t, w_t)
    kv_m = _mem_kv(mem, g_mem, w_mem_kv[0].astype(BF16))
    return _out(x, o_a, o_b, proj_r, kv_m, w_out[0].astype(BF16), g_final[None, :])
```

```python
import functools
import math

import numpy as np

import jax
import jax.numpy as jnp
from jax import lax
from jax.experimental import pallas as pl
from jax.experimental.pallas import tpu as pltpu

F32 = jnp.float32
BF16 = jnp.bfloat16
I32 = jnp.int32

D_MODEL = 1024
HEAD_DIM = 64
N_HEADS_A = 8
N_HEADS_B = 4
N_HEADS_MEM = 4
W_A = N_HEADS_A * HEAD_DIM
W_B = N_HEADS_B * HEAD_DIM
W_M = N_HEADS_MEM * HEAD_DIM
MIX_WIDTH = W_A + W_B + W_M
DILATIONS = (1, 4, 16)
BAND = 128
N_IDX_HEADS = 8
IDX_DIM = 64
TOPK_MAX = 256
RMS_EPS = 1e-6
SPLIT_SIZES = (W_A, W_A, W_A, W_B, HEAD_DIM, HEAD_DIM, W_M, MIX_WIDTH,
               N_IDX_HEADS * IDX_DIM, IDX_DIM, N_IDX_HEADS)

LANES = 128
BLK = 128
NEG = -1e30
VMEM_LIMIT = 48 * 1024 * 1024

R_GATE = 0
R_QM = MIX_WIDTH
R_KB = R_QM + W_M
R_KI = R_KB + LANES
R_COLS = R_KI + LANES
T_QI = 0
T_QB = N_IDX_HEADS * IDX_DIM
T_VB = T_QB + W_B
T_ROWS = T_VB + HEAD_DIM
WI_ROWS = 16
DSA_Q = 256
DSA_HALVINGS = 12
POS_SPLIT = 64
LOG2E = math.log2(math.e)


def _dot(a, b):
    return jnp.dot(a, b, preferred_element_type=F32)


def _dot_nt(a, b):
    return lax.dot_general(a, b, (((1,), (1,)), ((), ())), preferred_element_type=F32)


def _rms(x, g):
    return x * lax.rsqrt(jnp.mean(x * x, axis=-1, keepdims=True) + RMS_EPS) * g


def _proj_kernel(x_ref, g_ref, wa_ref, wr_ref, wt_ref, ww_ref,
                 oa_ref, oa4_ref, oa16_ref, or_ref, ot_ref, ow_ref, res_ref, res4_ref):
    hb = _rms(x_ref[0], g_ref[...]).astype(BF16)
    tm = hb.shape[0]
    d4, d16 = DILATIONS[1], DILATIONS[2]
    step = d16 // d4
    res = _dot(hb, wa_ref[...])
    oa_ref[0] = res.astype(BF16)
    for grp in range(res.shape[1] // LANES):
        lanes = slice(grp * LANES, (grp + 1) * LANES)
        res_ref[grp] = res[:, lanes]
        for r in range(d4):
            rows4 = res_ref[grp, pl.ds(r, tm // d4, stride=d4), :]
            oa4_ref[0, r, :, lanes] = rows4.astype(BF16)
            res4_ref[grp, r * (tm // d4):(r + 1) * (tm // d4), :] = rows4
        for r in range(d16):
            start = (r % d4) * (tm // d4) + r // d4
            oa16_ref[0, r, :, lanes] = res4_ref[grp, pl.ds(start, tm // d16, stride=step), :].astype(BF16)
    or_ref[0] = _dot(hb, wr_ref[...]).astype(BF16)
    ot_ref[0] = _dot_nt(wt_ref[...], hb).astype(BF16)
    ow_ref[0] = _dot_nt(ww_ref[...], hb)


def _proj(x, g, wa, wr, wt, ww, tm=512):
    B, T, D = x.shape
    const = lambda b, i: (0, 0)
    d4, d16 = DILATIONS[1], DILATIONS[2]
    nat, cm4, cm16, proj_r, proj_t, w_t = pl.pallas_call(
        _proj_kernel,
        grid=(B, T // tm),
        in_specs=[
            pl.BlockSpec((1, tm, D), lambda b, i: (b, i, 0)),
            pl.BlockSpec((1, D), const),
            pl.BlockSpec(wa.shape, const),
            pl.BlockSpec(wr.shape, const),
            pl.BlockSpec(wt.shape, const),
            pl.BlockSpec(ww.shape, const),
        ],
        out_specs=[
            pl.BlockSpec((1, tm, 3 * W_A), lambda b, i: (b, i, 0)),
            pl.BlockSpec((1, d4, tm // d4, 3 * W_A), lambda b, i: (b, 0, i, 0)),
            pl.BlockSpec((1, d16, tm // d16, 3 * W_A), lambda b, i: (b, 0, i, 0)),
            pl.BlockSpec((1, tm, R_COLS), lambda b, i: (b, i, 0)),
            pl.BlockSpec((1, T_ROWS, tm), lambda b, i: (b, 0, i)),
            pl.BlockSpec((1, WI_ROWS, tm), lambda b, i: (b, 0, i)),
        ],
        out_shape=[
            jax.ShapeDtypeStruct((B, T, 3 * W_A), BF16),
            jax.ShapeDtypeStruct((B, d4, T // d4, 3 * W_A), BF16),
            jax.ShapeDtypeStruct((B, d16, T // d16, 3 * W_A), BF16),
            jax.ShapeDtypeStruct((B, T, R_COLS), BF16),
            jax.ShapeDtypeStruct((B, T_ROWS, T), BF16),
            jax.ShapeDtypeStruct((B, WI_ROWS, T), F32),
        ],
        scratch_shapes=[pltpu.VMEM((3 * W_A // LANES, tm, LANES), F32)] * 2,
        compiler_params=pltpu.CompilerParams(
            dimension_semantics=("parallel", "parallel"), vmem_limit_bytes=VMEM_LIMIT),
        name="proj",
    )(x, g, wa, wr, wt, ww)
    return (nat, cm4.reshape(nat.shape), cm16.reshape(nat.shape)), proj_r, proj_t, w_t


def _attn_a_kernel(slopes_ref, *refs, seq, unroll):
    n_pat = len(DILATIONS)
    qkv = [refs[3 * p:3 * p + 3] for p in range(n_pat)]
    o_ref, acc_ref, m_ref, bias_ref, p_ref = refs[3 * n_pat:]
    hp = pl.program_id(1)
    lane = lax.broadcasted_iota(I32, (1, LANES), 1)
    head_lanes = (lane < HEAD_DIM, lane >= HEAD_DIM)
    row = lax.broadcasted_iota(I32, (BLK, BLK), 0)
    col = lax.broadcasted_iota(I32, (BLK, BLK), 1)
    d_cur = (row - col).astype(F32)
    d_prev = d_cur + float(BAND)
    n_all = seq // BLK
    zero = jnp.zeros((), BF16)
    one = jnp.ones((), BF16)

    def n_blocks(p):
        return n_all // DILATIONS[p]

    def n_keys(p):
        return BLK if n_blocks(p) == 1 else 2 * BLK

    def block(g):
        return pl.ds(pl.multiple_of(g * BLK, BLK), BLK)

    def keys(p, g):
        return pl.ds(pl.multiple_of(jnp.maximum((g + 1) * BLK - n_keys(p), 0), BLK), n_keys(p))

    def variant(p, g):
        g = jnp.asarray(g, I32)
        return jnp.where(g == 0, 2, jnp.where(g % n_blocks(p) == 0, 1, 0))

    def pitch(dil):
        return seq // dil + 1 if seq // dil == BLK and dil > 1 else None

    def put(ref, h, p, g, val):
        dil = DILATIONS[p]
        start = g // n_blocks(p) + (g % n_blocks(p)) * (dil * BLK)
        if dil == 1:
            ref[h, p, pl.ds(pl.multiple_of(start, BLK), BLK), :] = val
        elif pitch(dil):
            ref[h, p, pl.ds(start * pitch(dil), BLK), :] = val
        else:
            ref[h, p, pl.ds(start, BLK, stride=dil), :] = val

    def probs(p):
        q_ref, k_ref, _ = qkv[p]

        def body(g):
            kw = k_ref[keys(p, g), :]
            q = q_ref[block(g), :]
            for h in range(2):
                s = _dot_nt(jnp.where(head_lanes[h], q, zero), kw)
                s = s + bias_ref[p, h, variant(p, g), :, 2 * BLK - n_keys(p):]
                m = jnp.max(s, axis=1, keepdims=True)
                p_ref[p % 2, h, g, :, :n_keys(p)] = jnp.exp2(s - m).astype(BF16)
                put(m_ref, h, p, g, jnp.broadcast_to(m, (BLK, LANES)))
        return body

    def values(p):
        v_ref = qkv[p][2]

        def body(g):
            vw = v_ref[keys(p, g), :]
            for h in range(2):
                put(acc_ref, h, p, g,
                    _dot(p_ref[p % 2, h, g, :, :n_keys(p)], jnp.where(head_lanes[h], vw, one)))
        return body

    def run(*bodies):
        def step(g, carry):
            for body in bodies:
                body(g)
            return carry
        lax.fori_loop(0, n_all, step, 0, unroll=unroll)

    for p, dil in enumerate(DILATIONS):
        for h in range(2):
            sd = slopes_ref[hp * 2 + h] * (float(dil) * LOG2E)
            cur = jnp.where(row >= col, -sd * d_cur, NEG)
            masked = jnp.full((BLK, BLK), NEG, F32)
            bias_ref[p, h, 0, :, BLK:] = cur
            bias_ref[p, h, 1, :, BLK:] = cur
            bias_ref[p, h, 2, :, BLK:] = cur if n_keys(p) == BLK else masked
            if n_keys(p) > BLK:
                bias_ref[p, h, 0, :, :BLK] = jnp.where(col >= row, -sd * d_prev, NEG)
                bias_ref[p, h, 1, :, :BLK] = masked
                bias_ref[p, h, 2, :, :BLK] = cur

    assert n_pat == 3
    run(probs(0))
    run(values(0), probs(1))
    run(values(1), probs(2))
    run(values(2))

    rows_per_step = 2 * BLK

    def natural_rows(ref, h, p, i):
        dil = DILATIONS[p]
        if not pitch(dil):
            return ref[h, p, pl.ds(pl.multiple_of(i * rows_per_step, rows_per_step), rows_per_step), :]
        per = rows_per_step // dil
        return jnp.concatenate(
            [ref[h, p, pl.ds(i * per + u, dil, stride=pitch(dil)), :] for u in range(per)], axis=0)

    def merge(i, carry):
        sl = pl.ds(pl.multiple_of(i * rows_per_step, rows_per_step), rows_per_step)
        nums = []
        for h in range(2):
            ms = [natural_rows(m_ref, h, p, i) for p in range(n_pat)]
            mx = functools.reduce(jnp.maximum, ms)
            nums.append(sum(jnp.exp2(ms[p] - mx) * natural_rows(acc_ref, h, p, i) for p in range(n_pat)))
        acc = jnp.where(head_lanes[0], nums[0], nums[1])
        den = pltpu.roll(jnp.where(head_lanes[0], nums[1], nums[0]), HEAD_DIM, axis=1)
        o_ref[sl, :] = acc / den
        return carry

    lax.fori_loop(0, seq // rows_per_step, merge, 0, unroll=4)


def _attn_a(slopes, qkv_layouts, unroll=16):
    B, T, _ = qkv_layouts[0].shape
    n_pairs = N_HEADS_A // 2
    n_pat = len(DILATIONS)

    def spec(off):
        return pl.BlockSpec((None, T, LANES), lambda b, h: (b, 0, off + h))

    return pl.pallas_call(
        functools.partial(_attn_a_kernel, seq=T, unroll=unroll),
        grid=(B, n_pairs),
        in_specs=[pl.BlockSpec(memory_space=pltpu.SMEM)]
        + [spec(part * n_pairs) for _ in range(n_pat) for part in range(3)],
        out_specs=pl.BlockSpec((None, T, LANES), lambda b, h: (b, 0, h)),
        out_shape=jax.ShapeDtypeStruct((B, T, W_A), F32),
        scratch_shapes=[pltpu.VMEM((2, n_pat, T + max(DILATIONS), LANES), F32),
                        pltpu.VMEM((2, n_pat, T + max(DILATIONS), LANES), F32),
                        pltpu.VMEM((n_pat, 2, 3, BLK, 2 * BLK), F32),
                        pltpu.VMEM((2, 2, T // BLK, BLK, 2 * BLK), BF16)],
        compiler_params=pltpu.CompilerParams(
            dimension_semantics=("parallel", "parallel"), vmem_limit_bytes=VMEM_LIMIT),
        name="attn_a",
    )(slopes, *[a for a in qkv_layouts for _ in range(3)])


def _bf16_pair(c):
    hi = float(np.asarray(c, np.float32).astype(BF16).astype(np.float32))
    lo = float(np.asarray(c - hi, np.float32).astype(BF16).astype(np.float32))
    return hi, lo


def _fold(x, op):
    return functools.reduce(op, [x[r:r + 8] for r in range(0, x.shape[0], 8)])


def _dsa_kernel(kk_ref, tq_ref, vt_ref, wt_ref, o_ref, sc_ref, s_ref, acc_ref, kpos_ref,
                *, seq, k_sel, snap_unroll, slopes):
    j = pl.program_id(1)
    n_pairs = j + 1
    t_lane = j * DSA_Q + lax.broadcasted_iota(I32, (1, DSA_Q), 1)
    row = lax.broadcasted_iota(I32, (BLK, DSA_Q), 0)
    ws = wt_ref[0:N_IDX_HEADS, :] * (N_IDX_HEADS ** -0.5)
    inf = float("inf")

    def rows(c):
        return pl.ds(pl.multiple_of(c * BLK, BLK), BLK)

    def each_chunk(fn, init, widest=4):
        def group(first, n, carry):
            for u in range(n):
                carry = fn(first + u, carry)
            return carry
        carry, done, width = init, 0, widest
        while width >= 2:
            steps = (2 * n_pairs - done) // width
            carry = lax.fori_loop(
                0, steps, lambda i, c, done=done, width=width: group(done + width * i, width, c), carry)
            done, width = done + steps * width, width // 2
        return carry

    def sub_reduce(x, op):
        return op(x, axis=0, keepdims=True)

    def score_chunk(c, carry):
        mn, mx = carry
        ki = kk_ref[rows(c), R_KI - R_KB:R_KI - R_KB + IDX_DIM]
        lgs = [_dot(ki, tq_ref[T_QI + h * IDX_DIM:T_QI + (h + 1) * IDX_DIM, :])
               for h in range(N_IDX_HEADS)]
        sc = functools.reduce(
            lambda a, b: a + b, [jnp.maximum(lg, 0.0) * ws[h:h + 1, :] for h, lg in enumerate(lgs)])
        causal = (c * BLK + row) <= t_lane
        sc_ref[rows(c), :] = jnp.where(causal, sc, -inf)
        mn = jnp.minimum(mn, _fold(jnp.where(causal, sc, inf), jnp.minimum))
        mx = jnp.maximum(mx, _fold(jnp.where(causal, sc, -inf), jnp.maximum))
        return mn, mx

    mn, mx = each_chunk(score_chunk, (jnp.full((8, DSA_Q), inf, F32), jnp.full((8, DSA_Q), -inf, F32)),
                        widest=8)
    few = t_lane < k_sel
    lo = jnp.where(few, -inf, sub_reduce(mn, jnp.min))
    hi = jnp.where(few, -inf, sub_reduce(mx, jnp.max))

    def count(pred):
        def body(c, cnt):
            return cnt + _fold(pred(sc_ref[rows(c), :], c).astype(I32), jnp.add)
        return sub_reduce(each_chunk(body, jnp.zeros((8, DSA_Q), I32)), jnp.sum)

    def halve(_, bounds):
        lo, hi, n_lo, n_hi = bounds
        mid = 0.5 * lo + 0.5 * hi
        cnt = count(lambda x, c: x >= mid)
        active = lo < hi
        up = active & (cnt >= k_sel)
        down = active & (cnt <= k_sel)
        return (jnp.where(up, mid, lo), jnp.where(down, mid, hi),
                jnp.where(up, cnt, n_lo), jnp.where(down, cnt, n_hi))

    bounds = lax.fori_loop(0, DSA_HALVINGS, halve, (lo, hi, t_lane + 1, jnp.zeros((1, DSA_Q), I32)))

    def snap(bounds):
        lo, hi, n_lo, n_hi = bounds
        mid = 0.5 * lo + 0.5 * hi
        mid = jnp.where(mid > lo, mid, hi)

        def body(c, carry):
            cnt, above, below = carry
            x = sc_ref[rows(c), :]
            ge = x >= mid
            return (cnt + _fold(ge.astype(I32), jnp.add),
                    jnp.minimum(above, _fold(jnp.where(ge, x, inf), jnp.minimum)),
                    jnp.maximum(below, _fold(jnp.where(ge, -inf, x), jnp.maximum)))

        cnt, above, below = each_chunk(body, (jnp.zeros((8, DSA_Q), I32),
                                              jnp.full((8, DSA_Q), inf, F32),
                                              jnp.full((8, DSA_Q), -inf, F32)))
        cnt = sub_reduce(cnt, jnp.sum)
        above = sub_reduce(above, jnp.min)
        below = sub_reduce(below, jnp.max)
        active = lo < hi
        enough = cnt >= k_sel
        up = active & enough
        down = active & (cnt <= k_sel)
        return (jnp.where(up, above, lo), jnp.where(down, jnp.where(enough, above, below), hi),
                jnp.where(up, cnt, n_lo), jnp.where(down, cnt, n_hi))

    def snaps(bounds):
        for _ in range(snap_unroll):
            bounds = snap(bounds)
        return bounds

    def unsettled(bounds):
        return jnp.max((bounds[0] < bounds[1]).astype(I32)) > 0

    tau, _, n_ge, n_gt = lax.while_loop(unsettled, snaps, snaps(bounds))

    need = k_sel - n_gt
    n_idx_bits = seq.bit_length() - 1
    assert 1 << n_idx_bits == seq

    def tie_break():
        def idx_step(it, jp):
            cand = jp + lax.shift_left(jnp.int32(1), n_idx_bits - 1 - it)
            below = count(lambda x, c: (x == tau) & ((c * BLK + row) < cand))
            return jnp.where(below < need, cand, jp)
        last = lax.fori_loop(0, n_idx_bits, idx_step, jnp.zeros((1, DSA_Q), I32))
        return jnp.where(tied, last, seq)

    tied = (n_ge > k_sel) & jnp.logical_not(few)
    surplus = jnp.max(tied.astype(I32)) > 0
    tie_last = lax.cond(surplus, tie_break, lambda: jnp.full((1, DSA_Q), seq, I32))

    @pl.when(j == 0)
    def _():
        pos = lax.broadcasted_iota(I32, (seq, LANES), 0)
        ln = lax.broadcasted_iota(I32, (seq, LANES), 1) - HEAD_DIM
        feat = jnp.where(ln == 0, pos // POS_SPLIT,
                         jnp.where(ln == 1, pos % POS_SPLIT,
                                   jnp.where(ln == 2, pos // POS_SPLIT, jnp.where(ln == 3, pos % POS_SPLIT, 0))))
        kpos_ref[...] = kk_ref[:, 0:LANES] + feat.astype(BF16)

    q_row = lax.broadcasted_iota(I32, (HEAD_DIM, DSA_Q), 0)
    q_aug = []
    for h in range(N_HEADS_B):
        c_hi, c_lo = _bf16_pair(slopes[h] * LOG2E)
        feat = jnp.where(q_row == 0, POS_SPLIT * c_hi,
                         jnp.where(q_row == 1, c_hi,
                                   jnp.where(q_row == 2, POS_SPLIT * c_lo,
                                             jnp.where(q_row == 3, c_lo, 0.0))))
        q_aug.append(jnp.concatenate(
            [tq_ref[T_QB + h * HEAD_DIM:T_QB + (h + 1) * HEAD_DIM, :], feat.astype(BF16)], axis=0))

    tau_all = jnp.where(few, float(jnp.finfo(F32).min), tau)

    def qk_phase(with_ties):
        def qk_chunk(c, ms):
            kb = kpos_ref[rows(c), :]
            x = sc_ref[rows(c), :]
            if with_ties:
                s_idx = c * BLK + row
                tie = jnp.where(x == tau, jnp.where(s_idx <= tie_last, 0.0, NEG), NEG)
                sb = jnp.where(s_idx <= t_lane, jnp.where(x > tau, 0.0, tie), NEG)
            else:
                sb = jnp.where(x >= tau_all, 0.0, NEG)
            out = []
            for h in range(N_HEADS_B):
                s = _dot(kb, q_aug[h]) + sb
                s_ref[h, rows(c), :] = s
                out.append(jnp.maximum(ms[h], _fold(s, jnp.maximum)))
            return tuple(out)

        return each_chunk(qk_chunk, tuple(jnp.full((8, DSA_Q), NEG, F32) for _ in range(N_HEADS_B)),
                          widest=8)

    ms = lax.cond(surplus, lambda: qk_phase(True), lambda: qk_phase(False))
    ms = [sub_reduce(m, jnp.max) for m in ms]

    acc_ref[...] = jnp.zeros((W_B, DSA_Q), F32)

    def exp_pv(first_pair, n, ls):
        ls = list(ls)
        pvs = []
        for u in range(n):
            pair = first_pair + u
            ps = []
            for h in range(N_HEADS_B):
                halves = []
                for c in (2 * pair, 2 * pair + 1):
                    p = jnp.exp2(s_ref[h, rows(c), :] - ms[h])
                    ls[h] = ls[h] + _fold(p, jnp.add)
                    halves.append(p.astype(BF16))
                ps.append(jnp.concatenate(halves, axis=0))
            vt = vt_ref[:, pl.ds(pl.multiple_of(pair * 2 * BLK, 2 * BLK), 2 * BLK)]
            pvs.append([_dot(vt, ps[h]) for h in range(N_HEADS_B)])
        for h in range(N_HEADS_B):
            acc_ref[h * HEAD_DIM:(h + 1) * HEAD_DIM, :] += functools.reduce(
                lambda a, b: a + b, [pv[h] for pv in pvs])
        return tuple(ls)

    ls = tuple(jnp.zeros((8, DSA_Q), F32) for _ in range(N_HEADS_B))
    ls = lax.fori_loop(0, n_pairs // 2, lambda i, c: exp_pv(2 * i, 2, c), ls)
    ls = lax.fori_loop(0, n_pairs % 2, lambda i, c: exp_pv(n_pairs - 1, 1, c), ls)
    l_all = jnp.concatenate(
        [jnp.broadcast_to(sub_reduce(ls[h], jnp.sum), (HEAD_DIM, DSA_Q)) for h in range(N_HEADS_B)],
        axis=0)
    o_ref[...] = (acc_ref[...] / l_all).T


def _dsa(proj_r, proj_t, w_t):
    B, T, _ = proj_r.shape
    k_sel = min(TOPK_MAX, T // 4)
    slopes = _static_alibi_slopes(N_HEADS_B)
    assert T // POS_SPLIT <= 256 and POS_SPLIT <= 256
    return pl.pallas_call(
        functools.partial(_dsa_kernel, seq=T, k_sel=k_sel, snap_unroll=4, slopes=slopes),
        grid=(B, T // DSA_Q),
        in_specs=[
            pl.BlockSpec((None, T, 2 * LANES), lambda b, q: (b, 0, R_KB // (2 * LANES))),
            pl.BlockSpec((None, T_VB, DSA_Q), lambda b, q: (b, 0, q)),
            pl.BlockSpec((None, HEAD_DIM, T), lambda b, q: (b, T_VB // HEAD_DIM, 0)),
            pl.BlockSpec((None, WI_ROWS, DSA_Q), lambda b, q: (b, 0, q)),
        ],
        out_specs=pl.BlockSpec((None, DSA_Q, W_B), lambda b, q: (b, q, 0)),
        out_shape=jax.ShapeDtypeStruct((B, T, W_B), F32),
        scratch_shapes=[pltpu.VMEM((T, DSA_Q), F32),
                        pltpu.VMEM((N_HEADS_B, T, DSA_Q), F32),
                        pltpu.VMEM((W_B, DSA_Q), F32),
                        pltpu.VMEM((T, LANES), BF16)],
        compiler_params=pltpu.CompilerParams(
            dimension_semantics=("parallel", "arbitrary"), vmem_limit_bytes=VMEM_LIMIT),
        name="dsa",
    )(proj_r, proj_t, proj_t, w_t)


def _mem_kv_kernel(mem_ref, g_ref, w_ref, kv_ref):
    kv_ref[0] = _dot(_rms(mem_ref[0], g_ref[...]).astype(BF16), w_ref[...]).astype(BF16)


def _mem_kv(mem, g, w):
    B, M, D = mem.shape
    return pl.pallas_call(
        _mem_kv_kernel,
        grid=(B,),
        in_specs=[
            pl.BlockSpec((1, M, D), lambda b: (b, 0, 0)),
            pl.BlockSpec((1, D), lambda b: (0, 0)),
            pl.BlockSpec(w.shape, lambda b: (0, 0)),
        ],
        out_specs=pl.BlockSpec((1, M, 2 * W_M), lambda b: (b, 0, 0)),
        out_shape=jax.ShapeDtypeStruct((B, M, 2 * W_M), BF16),
        compiler_params=pltpu.CompilerParams(
            dimension_semantics=("parallel",), vmem_limit_bytes=VMEM_LIMIT),
        name="mem_kv",
    )(mem, g, w)


def _out_kernel(x_ref, oa_ref, ob_ref, gate_ref, qm_ref, kv_ref, wo_ref, g_ref, y_ref, mix_ref, *, sub):
    tm = x_ref.shape[0]
    subs = [slice(r0, r0 + sub) for r0 in range(0, tm, sub)]
    lane = lax.broadcasted_iota(I32, (1, W_M), 1)
    head_lanes = [(lane >= h * HEAD_DIM) & (lane < (h + 1) * HEAD_DIM) for h in range(N_HEADS_MEM)]
    km = kv_ref[:, 0:W_M]
    vm = kv_ref[:, W_M:2 * W_M]
    zero = jnp.zeros((), BF16)
    vm_heads = [jnp.where(head_lanes[h], vm, zero) for h in range(N_HEADS_MEM)]

    def silu(g):
        return g * (1.0 / (1.0 + jnp.exp(-g)))

    def gated(o, rows, lo, width):
        return (o * silu(gate_ref[rows, lo:lo + width].astype(F32))).astype(BF16)

    for rows in subs:
        qm = qm_ref[rows, :]
        om = None
        for h in range(N_HEADS_MEM):
            s = _dot_nt(jnp.where(head_lanes[h], qm, zero), km)
            p = jnp.exp2(s - jnp.max(s, axis=1, keepdims=True))
            p = p / jnp.sum(p, axis=1, keepdims=True)
            o = _dot(p.astype(BF16), vm_heads[h])
            om = o if om is None else om + o
        mix_ref[rows, W_A + W_B:] = gated(om, rows, W_A + W_B, W_M)

    for rows in subs:
        half = W_A // 2
        mix_ref[rows, 0:half] = gated(oa_ref[rows, 0:half], rows, 0, half)
        mix_ref[rows, half:W_A] = gated(oa_ref[rows, half:W_A], rows, half, half)
        mix_ref[rows, W_A:W_A + W_B] = gated(ob_ref[rows, :], rows, W_A, W_B)
        y_ref[rows, :] = _dot(mix_ref[rows, :], wo_ref[...])

    for rows in subs:
        y_ref[rows, :] = _rms(x_ref[rows, :] + y_ref[rows, :], g_ref[...])


def _out(x, o_a, o_b, proj_r, kv_m, w_out, g, tm=512, sub=256):
    B, T, D = x.shape
    M = kv_m.shape[1]
    row = lambda b, i: (b, i, 0)
    return pl.pallas_call(
        functools.partial(_out_kernel, sub=sub),
        grid=(B, T // tm),
        in_specs=[
            pl.BlockSpec((None, tm, D), row),
            pl.BlockSpec((None, tm, W_A), row),
            pl.BlockSpec((None, tm, W_B), row),
            pl.BlockSpec((None, tm, MIX_WIDTH), lambda b, i: (b, i, R_GATE // MIX_WIDTH)),
            pl.BlockSpec((None, tm, W_M), lambda b, i: (b, i, R_QM // W_M)),
            pl.BlockSpec((None, M, 2 * W_M), lambda b, i: (b, 0, 0)),
            pl.BlockSpec(w_out.shape, lambda b, i: (0, 0)),
            pl.BlockSpec((1, D), lambda b, i: (0, 0)),
        ],
        out_specs=pl.BlockSpec((None, tm, D), row),
        out_shape=jax.ShapeDtypeStruct((B, T, D), F32),
        scratch_shapes=[pltpu.VMEM((tm, MIX_WIDTH), BF16)],
        compiler_params=pltpu.CompilerParams(
            dimension_semantics=("parallel", "parallel"), vmem_limit_bytes=VMEM_LIMIT),
        name="out",
    )(x, o_a, o_b, proj_r, proj_r, kv_m, w_out, g)


def _alibi_slopes(n):
    return 2.0 ** (-8.0 * jnp.arange(1, n + 1, dtype=F32) / n)


def _static_alibi_slopes(n):
    return tuple(2.0 ** (-8.0 * i / n) for i in range(1, n + 1))


def _split_weights(w):
    bounds = [0]
    for s in SPLIT_SIZES:
        bounds.append(bounds[-1] + s)
    q_a, k_a, v_a, q_b, k_b, v_b, q_m, gate, q_i, k_i, w_i = (
        w[:, bounds[i]:bounds[i + 1]] for i in range(len(SPLIT_SIZES)))
    scale = HEAD_DIM ** -0.5 * LOG2E
    idx_scale = IDX_DIM ** -0.5
    zeros = jnp.zeros((w.shape[0], LANES - HEAD_DIM), w.dtype)
    wa = jnp.concatenate([q_a * scale, k_a, v_a], axis=1)
    wr = jnp.concatenate([gate, q_m * scale, k_b, zeros, k_i, zeros], axis=1)
    wt = jnp.concatenate([q_i * idx_scale, q_b * scale, v_b], axis=1).T
    ww = jnp.concatenate([w_i.T, jnp.zeros((WI_ROWS - N_IDX_HEADS, w.shape[0]), w.dtype)], axis=0)
    return wa.astype(BF16), wr.astype(BF16), wt.astype(BF16), ww.astype(BF16)


def kernel(x, mem, g_in, g_mem, w_in, w_mem_kv, w_out, g_final):
    assert g_in.shape[0] == 1, "single-layer block: the final RMSNorm is fused into the output kernel"
    wa, wr, wt, ww = _split_weights(w_in[0])
    qkv_a, proj_r, proj_t, w_t = _proj(x, g_in, wa, wr, wt, ww)
    o_a = _attn_a(_alibi_slopes(N_HEADS_A), qkv_a)
    o_b = _dsa(proj_r, proj_t, w_t)
    kv_m = _mem_kv(mem, g_mem, w_mem_kv[0].astype(BF16))
    return _out(x, o_a, o_b, proj_r, kv_m, w_out[0].astype(BF16), g_final[None, :])
```

```python
import functools
import math
import types

import numpy as np

import jax
import jax.numpy as jnp
from jax import lax
from jax.experimental import pallas as pl
from jax.experimental.pallas import tpu as pltpu

F32 = jnp.float32
BF16 = jnp.bfloat16
I32 = jnp.int32

D_MODEL = 1024
HEAD_DIM = 64
N_HEADS_A = 8
N_HEADS_B = 4
N_HEADS_MEM = 4
W_A = N_HEADS_A * HEAD_DIM
W_B = N_HEADS_B * HEAD_DIM
W_M = N_HEADS_MEM * HEAD_DIM
MIX_WIDTH = W_A + W_B + W_M
DILATIONS = (1, 4, 16)
BAND = 128
N_IDX_HEADS = 8
IDX_DIM = 64
TOPK_MAX = 256
RMS_EPS = 1e-6
SPLIT_SIZES = (W_A, W_A, W_A, W_B, HEAD_DIM, HEAD_DIM, W_M, MIX_WIDTH,
               N_IDX_HEADS * IDX_DIM, IDX_DIM, N_IDX_HEADS)

LANES = 128
BLK = 128
NEG = -1e30
VMEM_LIMIT = 48 * 1024 * 1024

R_GATE = 0
R_QM = MIX_WIDTH
R_KB = R_QM + W_M
R_KI = R_KB + LANES
R_COLS = R_KI + LANES
T_QI = 0
T_QB = N_IDX_HEADS * IDX_DIM
T_VB = T_QB + W_B
T_ROWS = T_VB + HEAD_DIM
WI_ROWS = 16
DSA_Q = 256
DSA_HALVINGS = 12
POS_SPLIT = 64
LOG2E = math.log2(math.e)


def _dot(a, b):
    return jnp.dot(a, b, preferred_element_type=F32)


def _dot_nt(a, b):
    return lax.dot_general(a, b, (((1,), (1,)), ((), ())), preferred_element_type=F32)


def _rms(x, g):
    return x * lax.rsqrt(jnp.mean(x * x, axis=-1, keepdims=True) + RMS_EPS) * g


def _proj_kernel(x_ref, g_ref, wa_ref, wr_ref, wt_ref, ww_ref,
                 oa_ref, oa4_ref, oa16_ref, or_ref, ot_ref, ow_ref, res_ref, res4_ref):
    hb = _rms(x_ref[0], g_ref[...]).astype(BF16)
    tm = hb.shape[0]
    d4, d16 = DILATIONS[1], DILATIONS[2]
    step = d16 // d4
    res = _dot(hb, wa_ref[...])
    oa_ref[0] = res.astype(BF16)
    for grp in range(res.shape[1] // LANES):
        lanes = slice(grp * LANES, (grp + 1) * LANES)
        res_ref[grp] = res[:, lanes]
        for r in range(d4):
            rows4 = res_ref[grp, pl.ds(r, tm // d4, stride=d4), :]
            oa4_ref[0, r, :, lanes] = rows4.astype(BF16)
            res4_ref[grp, r * (tm // d4):(r + 1) * (tm // d4), :] = rows4
        for r in range(d16):
            start = (r % d4) * (tm // d4) + r // d4
            oa16_ref[0, r, :, lanes] = res4_ref[grp, pl.ds(start, tm // d16, stride=step), :].astype(BF16)
    or_ref[0] = _dot(hb, wr_ref[...]).astype(BF16)
    ot_ref[0] = _dot_nt(wt_ref[...], hb).astype(BF16)
    ow_ref[0] = _dot_nt(ww_ref[...], hb)


def _proj(x, g, wa, wr, wt, ww, tm=512):
    B, T, D = x.shape
    const = lambda b, i: (0, 0)
    d4, d16 = DILATIONS[1], DILATIONS[2]
    nat, cm4, cm16, proj_r, proj_t, w_t = pl.pallas_call(
        _proj_kernel,
        grid=(B, T // tm),
        in_specs=[
            pl.BlockSpec((1, tm, D), lambda b, i: (b, i, 0)),
            pl.BlockSpec((1, D), const),
            pl.BlockSpec(wa.shape, const),
            pl.BlockSpec(wr.shape, const),
            pl.BlockSpec(wt.shape, const),
            pl.BlockSpec(ww.shape, const),
        ],
        out_specs=[
            pl.BlockSpec((1, tm, 3 * W_A), lambda b, i: (b, i, 0)),
            pl.BlockSpec((1, d4, tm // d4, 3 * W_A), lambda b, i: (b, 0, i, 0)),
            pl.BlockSpec((1, d16, tm // d16, 3 * W_A), lambda b, i: (b, 0, i, 0)),
            pl.BlockSpec((1, tm, R_COLS), lambda b, i: (b, i, 0)),
            pl.BlockSpec((1, T_ROWS, tm), lambda b, i: (b, 0, i)),
            pl.BlockSpec((1, WI_ROWS, tm), lambda b, i: (b, 0, i)),
        ],
        out_shape=[
            jax.ShapeDtypeStruct((B, T, 3 * W_A), BF16),
            jax.ShapeDtypeStruct((B, d4, T // d4, 3 * W_A), BF16),
            jax.ShapeDtypeStruct((B, d16, T // d16, 3 * W_A), BF16),
            jax.ShapeDtypeStruct((B, T, R_COLS), BF16),
            jax.ShapeDtypeStruct((B, T_ROWS, T), BF16),
            jax.ShapeDtypeStruct((B, WI_ROWS, T), F32),
        ],
        scratch_shapes=[pltpu.VMEM((3 * W_A // LANES, tm, LANES), F32)] * 2,
        compiler_params=pltpu.CompilerParams(
            dimension_semantics=("parallel", "parallel"), vmem_limit_bytes=VMEM_LIMIT),
        name="proj",
    )(x, g, wa, wr, wt, ww)
    return (nat, cm4.reshape(nat.shape), cm16.reshape(nat.shape)), proj_r, proj_t, w_t


def _attn_a_kernel(slopes_ref, *refs, seq, unroll):
    n_pat = len(DILATIONS)
    qkv = [refs[3 * p:3 * p + 3] for p in range(n_pat)]
    o_ref, acc_ref, m_ref, bias_ref, p_ref = refs[3 * n_pat:]
    hp = pl.program_id(1)
    lane = lax.broadcasted_iota(I32, (1, LANES), 1)
    head_lanes = (lane < HEAD_DIM, lane >= HEAD_DIM)
    row = lax.broadcasted_iota(I32, (BLK, BLK), 0)
    col = lax.broadcasted_iota(I32, (BLK, BLK), 1)
    d_cur = (row - col).astype(F32)
    d_prev = d_cur + float(BAND)
    n_all = seq // BLK
    zero = jnp.zeros((), BF16)
    one = jnp.ones((), BF16)

    def n_blocks(p):
        return n_all // DILATIONS[p]

    def n_keys(p):
        return BLK if n_blocks(p) == 1 else 2 * BLK

    def block(g):
        return pl.ds(pl.multiple_of(g * BLK, BLK), BLK)

    def keys(p, g):
        return pl.ds(pl.multiple_of(jnp.maximum((g + 1) * BLK - n_keys(p), 0), BLK), n_keys(p))

    def variant(p, g):
        g = jnp.asarray(g, I32)
        return jnp.where(g == 0, 2, jnp.where(g % n_blocks(p) == 0, 1, 0))

    def pitch(dil):
        return seq // dil + 1 if seq // dil == BLK and dil > 1 else None

    def put(ref, h, p, g, val):
        dil = DILATIONS[p]
        start = g // n_blocks(p) + (g % n_blocks(p)) * (dil * BLK)
        if dil == 1:
            ref[h, p, pl.ds(pl.multiple_of(start, BLK), BLK), :] = val
        elif pitch(dil):
            ref[h, p, pl.ds(start * pitch(dil), BLK), :] = val
        else:
            ref[h, p, pl.ds(start, BLK, stride=dil), :] = val

    def probs(p):
        q_ref, k_ref, _ = qkv[p]

        def body(g):
            kw = k_ref[keys(p, g), :]
            q = q_ref[block(g), :]
            for h in range(2):
                s = _dot_nt(jnp.where(head_lanes[h], q, zero), kw)
                s = s + bias_ref[p, h, variant(p, g), :, 2 * BLK - n_keys(p):]
                m = jnp.max(s, axis=1, keepdims=True)
                p_ref[p % 2, h, g, :, :n_keys(p)] = jnp.exp2(s - m).astype(BF16)
                put(m_ref, h, p, g, jnp.broadcast_to(m, (BLK, LANES)))
        return body

    def values(p):
        v_ref = qkv[p][2]

        def body(g):
            vw = v_ref[keys(p, g), :]
            for h in range(2):
                put(acc_ref, h, p, g,
                    _dot(p_ref[p % 2, h, g, :, :n_keys(p)], jnp.where(head_lanes[h], vw, one)))
        return body

    def run(*bodies):
        def step(g, carry):
            for body in bodies:
                body(g)
            return carry
        lax.fori_loop(0, n_all, step, 0, unroll=unroll)

    for p, dil in enumerate(DILATIONS):
        for h in range(2):
            sd = slopes_ref[hp * 2 + h] * (float(dil) * LOG2E)
            cur = jnp.where(row >= col, -sd * d_cur, NEG)
            masked = jnp.full((BLK, BLK), NEG, F32)
            bias_ref[p, h, 0, :, BLK:] = cur
            bias_ref[p, h, 1, :, BLK:] = cur
            bias_ref[p, h, 2, :, BLK:] = cur if n_keys(p) == BLK else masked
            if n_keys(p) > BLK:
                bias_ref[p, h, 0, :, :BLK] = jnp.where(col >= row, -sd * d_prev, NEG)
                bias_ref[p, h, 1, :, :BLK] = masked
                bias_ref[p, h, 2, :, :BLK] = cur

    assert n_pat == 3
    run(probs(0))
    run(values(0), probs(1))
    run(values(1), probs(2))
    run(values(2))

    rows_per_step = 2 * BLK

    def natural_rows(ref, h, p, i):
        dil = DILATIONS[p]
        if not pitch(dil):
            return ref[h, p, pl.ds(pl.multiple_of(i * rows_per_step, rows_per_step), rows_per_step), :]
        per = rows_per_step // dil
        return jnp.concatenate(
            [ref[h, p, pl.ds(i * per + u, dil, stride=pitch(dil)), :] for u in range(per)], axis=0)

    def merge(i, carry):
        sl = pl.ds(pl.multiple_of(i * rows_per_step, rows_per_step), rows_per_step)
        nums = []
        for h in range(2):
            ms = [natural_rows(m_ref, h, p, i) for p in range(n_pat)]
            mx = functools.reduce(jnp.maximum, ms)
            nums.append(sum(jnp.exp2(ms[p] - mx) * natural_rows(acc_ref, h, p, i) for p in range(n_pat)))
        acc = jnp.where(head_lanes[0], nums[0], nums[1])
        den = pltpu.roll(jnp.where(head_lanes[0], nums[1], nums[0]), HEAD_DIM, axis=1)
        o_ref[sl, :] = acc / den
        return carry

    lax.fori_loop(0, seq // rows_per_step, merge, 0, unroll=4)


def _attn_a(slopes, qkv_layouts, unroll=16):
    B, T, _ = qkv_layouts[0].shape
    n_pairs = N_HEADS_A // 2
    n_pat = len(DILATIONS)

    def spec(off):
        return pl.BlockSpec((None, T, LANES), lambda b, h: (b, 0, off + h))

    return pl.pallas_call(
        functools.partial(_attn_a_kernel, seq=T, unroll=unroll),
        grid=(B, n_pairs),
        in_specs=[pl.BlockSpec(memory_space=pltpu.SMEM)]
        + [spec(part * n_pairs) for _ in range(n_pat) for part in range(3)],
        out_specs=pl.BlockSpec((None, T, LANES), lambda b, h: (b, 0, h)),
        out_shape=jax.ShapeDtypeStruct((B, T, W_A), F32),
        scratch_shapes=[pltpu.VMEM((2, n_pat, T + max(DILATIONS), LANES), F32),
                        pltpu.VMEM((2, n_pat, T + max(DILATIONS), LANES), F32),
                        pltpu.VMEM((n_pat, 2, 3, BLK, 2 * BLK), F32),
                        pltpu.VMEM((2, 2, T // BLK, BLK, 2 * BLK), BF16)],
        compiler_params=pltpu.CompilerParams(
            dimension_semantics=("parallel", "parallel"), vmem_limit_bytes=VMEM_LIMIT),
        name="attn_a",
    )(slopes, *[a for a in qkv_layouts for _ in range(3)])


def _bf16_pair(c):
    hi = float(np.asarray(c, np.float32).astype(BF16).astype(np.float32))
    lo = float(np.asarray(c - hi, np.float32).astype(BF16).astype(np.float32))
    return hi, lo


def _fold(x, op):
    return functools.reduce(op, [x[r:r + 8] for r in range(0, x.shape[0], 8)])


def _dsa_row(j, kk_ref, tq_ref, vt_ref, wt_ref, o_ref, sc_ref, s_ref, acc_ref, kpos_ref,
             *, seq, k_sel, snap_unroll, slopes):
    n_pairs = j + 1
    t_lane = j * DSA_Q + lax.broadcasted_iota(I32, (1, DSA_Q), 1)
    row = lax.broadcasted_iota(I32, (BLK, DSA_Q), 0)
    ws = wt_ref[0:N_IDX_HEADS, :] * (N_IDX_HEADS ** -0.5)
    inf = float("inf")
    few = t_lane < k_sel

    def rows(c):
        return pl.ds(pl.multiple_of(c * BLK, BLK), BLK)

    def group(fn, first, n, carry):
        for u in range(n):
            carry = fn(first + u, carry)
        return carry

    def each_chunk(fn, init, widest=4):
        carry, done, width = init, 0, widest
        while width >= 2:
            steps = (2 * n_pairs - done) // width
            carry = lax.fori_loop(
                0, steps, lambda i, c, done=done, width=width: group(fn, done + width * i, width, c), carry)
            done, width = done + steps * width, width // 2
        return carry

    def sub_reduce(x, op):
        return op(x, axis=0, keepdims=True)

    def score_chunk(c, carry):
        mn, mx = carry
        ki = kk_ref[rows(c), R_KI - R_KB:R_KI - R_KB + IDX_DIM]
        lgs = [_dot(ki, tq_ref[T_QI + h * IDX_DIM:T_QI + (h + 1) * IDX_DIM, :])
               for h in range(N_IDX_HEADS)]
        sc = functools.reduce(
            lambda a, b: a + b, [jnp.maximum(lg, 0.0) * ws[h:h + 1, :] for h, lg in enumerate(lgs)])
        causal = (c * BLK + row) <= t_lane
        sc_ref[rows(c), :] = jnp.where(causal, sc, -inf)
        mn = jnp.minimum(mn, _fold(jnp.where(causal, sc, inf), jnp.minimum))
        mx = jnp.maximum(mx, _fold(jnp.where(causal, sc, -inf), jnp.maximum))
        return mn, mx

    score_init = (jnp.full((8, DSA_Q), inf, F32), jnp.full((8, DSA_Q), -inf, F32))

    def scores():
        return each_chunk(score_chunk, score_init, widest=8)

    def count(pred):
        def body(c, cnt):
            return cnt + _fold(pred(sc_ref[rows(c), :], c).astype(I32), jnp.add)
        return sub_reduce(each_chunk(body, jnp.zeros((8, DSA_Q), I32)), jnp.sum)

    def halve(_, bounds):
        lo, hi, n_lo, n_hi = bounds
        mid = 0.5 * lo + 0.5 * hi
        cnt = count(lambda x, c: x >= mid)
        active = lo < hi
        up = active & (cnt >= k_sel)
        down = active & (cnt <= k_sel)
        return (jnp.where(up, mid, lo), jnp.where(down, mid, hi),
                jnp.where(up, cnt, n_lo), jnp.where(down, cnt, n_hi))

    def snap(bounds):
        lo, hi, n_lo, n_hi = bounds
        mid = 0.5 * lo + 0.5 * hi
        mid = jnp.where(mid > lo, mid, hi)

        def body(c, carry):
            cnt, above, below = carry
            x = sc_ref[rows(c), :]
            ge = x >= mid
            return (cnt + _fold(ge.astype(I32), jnp.add),
                    jnp.minimum(above, _fold(jnp.where(ge, x, inf), jnp.minimum)),
                    jnp.maximum(below, _fold(jnp.where(ge, -inf, x), jnp.maximum)))

        cnt, above, below = each_chunk(body, (jnp.zeros((8, DSA_Q), I32),
                                              jnp.full((8, DSA_Q), inf, F32),
                                              jnp.full((8, DSA_Q), -inf, F32)))
        cnt = sub_reduce(cnt, jnp.sum)
        above = sub_reduce(above, jnp.min)
        below = sub_reduce(below, jnp.max)
        active = lo < hi
        enough = cnt >= k_sel
        up = active & enough
        down = active & (cnt <= k_sel)
        return (jnp.where(up, above, lo), jnp.where(down, jnp.where(enough, above, below), hi),
                jnp.where(up, cnt, n_lo), jnp.where(down, cnt, n_hi))

    def snaps(bounds):
        for _ in range(snap_unroll):
            bounds = snap(bounds)
        return bounds

    def unsettled(bounds):
        return jnp.max((bounds[0] < bounds[1]).astype(I32)) > 0

    n_idx_bits = seq.bit_length() - 1
    assert 1 << n_idx_bits == seq

    def select(extremes):
        mn, mx = extremes
        lo = jnp.where(few, -inf, sub_reduce(mn, jnp.min))
        hi = jnp.where(few, -inf, sub_reduce(mx, jnp.max))
        bounds = lax.fori_loop(0, DSA_HALVINGS, halve, (lo, hi, t_lane + 1, jnp.zeros((1, DSA_Q), I32)))
        tau, _, n_ge, n_gt = lax.while_loop(unsettled, snaps, snaps(bounds))
        need = k_sel - n_gt
        tied = (n_ge > k_sel) & jnp.logical_not(few)

        def tie_break():
            def idx_step(it, jp):
                cand = jp + lax.shift_left(jnp.int32(1), n_idx_bits - 1 - it)
                below = count(lambda x, c: (x == tau) & ((c * BLK + row) < cand))
                return jnp.where(below < need, cand, jp)
            last = lax.fori_loop(0, n_idx_bits, idx_step, jnp.zeros((1, DSA_Q), I32))
            return jnp.where(tied, last, seq)

        surplus = jnp.max(tied.astype(I32)) > 0
        tie_last = lax.cond(surplus, tie_break, lambda: jnp.full((1, DSA_Q), seq, I32))
        return tau, tie_last, surplus

    def qk(selection):
        tau, tie_last, surplus = selection

        @pl.when(j == 0)
        def _():
            pos = lax.broadcasted_iota(I32, (seq, LANES), 0)
            ln = lax.broadcasted_iota(I32, (seq, LANES), 1) - HEAD_DIM
            feat = jnp.where(ln == 0, pos // POS_SPLIT,
                             jnp.where(ln == 1, pos % POS_SPLIT,
                                       jnp.where(ln == 2, pos // POS_SPLIT,
                                                 jnp.where(ln == 3, pos % POS_SPLIT, 0))))
            kpos_ref[...] = kk_ref[:, 0:LANES] + feat.astype(BF16)

        q_row = lax.broadcasted_iota(I32, (HEAD_DIM, DSA_Q), 0)
        q_aug = []
        for h in range(N_HEADS_B):
            c_hi, c_lo = _bf16_pair(slopes[h] * LOG2E)
            feat = jnp.where(q_row == 0, POS_SPLIT * c_hi,
                             jnp.where(q_row == 1, c_hi,
                                       jnp.where(q_row == 2, POS_SPLIT * c_lo,
                                                 jnp.where(q_row == 3, c_lo, 0.0))))
            q_aug.append(jnp.concatenate(
                [tq_ref[T_QB + h * HEAD_DIM:T_QB + (h + 1) * HEAD_DIM, :], feat.astype(BF16)], axis=0))

        tau_all = jnp.where(few, float(jnp.finfo(F32).min), tau)

        def qk_phase(with_ties):
            def qk_chunk(c, ms):
                kb = kpos_ref[rows(c), :]
                x = sc_ref[rows(c), :]
                if with_ties:
                    s_idx = c * BLK + row
                    tie = jnp.where(x == tau, jnp.where(s_idx <= tie_last, 0.0, NEG), NEG)
                    sb = jnp.where(s_idx <= t_lane, jnp.where(x > tau, 0.0, tie), NEG)
                else:
                    sb = jnp.where(x >= tau_all, 0.0, NEG)
                out = []
                for h in range(N_HEADS_B):
                    s = _dot(kb, q_aug[h]) + sb
                    s_ref[h, rows(c), :] = s
                    out.append(jnp.maximum(ms[h], _fold(s, jnp.maximum)))
                return tuple(out)

            return each_chunk(qk_chunk, tuple(jnp.full((8, DSA_Q), NEG, F32) for _ in range(N_HEADS_B)),
                              widest=8)

        ms = lax.cond(surplus, lambda: qk_phase(True), lambda: qk_phase(False))
        acc_ref[...] = jnp.zeros((W_B, DSA_Q), F32)
        return tuple(sub_reduce(m, jnp.max) for m in ms)

    def exp_pv(ms, first_pair, n, ls):
        ls = list(ls)
        pvs = []
        for u in range(n):
            pair = first_pair + u
            ps = []
            for h in range(N_HEADS_B):
                halves = []
                for c in (2 * pair, 2 * pair + 1):
                    p = jnp.exp2(s_ref[h, rows(c), :] - ms[h])
                    ls[h] = ls[h] + _fold(p, jnp.add)
                    halves.append(p.astype(BF16))
                ps.append(jnp.concatenate(halves, axis=0))
            vt = vt_ref[:, pl.ds(pl.multiple_of(pair * 2 * BLK, 2 * BLK), 2 * BLK)]
            pvs.append([_dot(vt, ps[h]) for h in range(N_HEADS_B)])
        for h in range(N_HEADS_B):
            acc_ref[h * HEAD_DIM:(h + 1) * HEAD_DIM, :] += functools.reduce(
                lambda a, b: a + b, [pv[h] for pv in pvs])
        return tuple(ls)

    exp_init = tuple(jnp.zeros((8, DSA_Q), F32) for _ in range(N_HEADS_B))

    def finish(ls):
        l_all = jnp.concatenate(
            [jnp.broadcast_to(sub_reduce(ls[h], jnp.sum), (HEAD_DIM, DSA_Q)) for h in range(N_HEADS_B)],
            axis=0)
        o_ref[...] = (acc_ref[...] / l_all).T

    return types.SimpleNamespace(scores=scores, select=select, qk=qk, exp_pv=exp_pv, finish=finish,
                                 score_chunk=score_chunk, score_init=score_init, exp_init=exp_init,
                                 group=group, n_pairs=n_pairs)


def _dsa_kernel(kk_ref, tq_ref, vt_ref, wt_ref, o_ref, sc_ref, s_ref, acc_ref, kpos_ref, **static):
    j = pl.program_id(1)
    a, b = (_dsa_row(j, *(ref.at[i] for ref in (kk_ref, tq_ref, vt_ref, wt_ref, o_ref, sc_ref, s_ref,
                                                 acc_ref, kpos_ref)), **static) for i in range(2))
    ms = a.qk(a.select(a.scores()))

    def both(first_pair, n_pair, carry):
        ls, extremes = carry
        ls = a.exp_pv(ms, first_pair, n_pair, ls)
        return ls, b.group(b.score_chunk, 2 * first_pair, 2 * n_pair, extremes)

    carry = (a.exp_init, b.score_init)
    carry = lax.fori_loop(0, a.n_pairs // 2, lambda i, c: both(2 * i, 2, c), carry)
    ls, extremes = lax.fori_loop(0, a.n_pairs % 2, lambda i, c: both(a.n_pairs - 1, 1, c), carry)
    a.finish(ls)

    ms = b.qk(b.select(extremes))
    ls = b.exp_init
    ls = lax.fori_loop(0, b.n_pairs // 2, lambda i, c: b.exp_pv(ms, 2 * i, 2, c), ls)
    ls = lax.fori_loop(0, b.n_pairs % 2, lambda i, c: b.exp_pv(ms, b.n_pairs - 1, 1, c), ls)
    b.finish(ls)


def _dsa(proj_r, proj_t, w_t):
    B, T, _ = proj_r.shape
    k_sel = min(TOPK_MAX, T // 4)
    slopes = _static_alibi_slopes(N_HEADS_B)
    assert T // POS_SPLIT <= 256 and POS_SPLIT <= 256 and B % 2 == 0
    return pl.pallas_call(
        functools.partial(_dsa_kernel, seq=T, k_sel=k_sel, snap_unroll=4, slopes=slopes),
        grid=(B // 2, T // DSA_Q),
        in_specs=[
            pl.BlockSpec((2, T, 2 * LANES), lambda b, q: (b, 0, R_KB // (2 * LANES))),
            pl.BlockSpec((2, T_VB, DSA_Q), lambda b, q: (b, 0, q)),
            pl.BlockSpec((2, HEAD_DIM, T), lambda b, q: (b, T_VB // HEAD_DIM, 0)),
            pl.BlockSpec((2, WI_ROWS, DSA_Q), lambda b, q: (b, 0, q)),
        ],
        out_specs=pl.BlockSpec((2, DSA_Q, W_B), lambda b, q: (b, q, 0)),
        out_shape=jax.ShapeDtypeStruct((B, T, W_B), F32),
        scratch_shapes=[pltpu.VMEM((2, T, DSA_Q), F32),
                        pltpu.VMEM((2, N_HEADS_B, T, DSA_Q), F32),
                        pltpu.VMEM((2, W_B, DSA_Q), F32),
                        pltpu.VMEM((2, T, LANES), BF16)],
        compiler_params=pltpu.CompilerParams(
            dimension_semantics=("parallel", "arbitrary"), vmem_limit_bytes=VMEM_LIMIT),
        name="dsa",
    )(proj_r, proj_t, proj_t, w_t)


def _mem_kv_kernel(mem_ref, g_ref, w_ref, kv_ref):
    kv_ref[0] = _dot(_rms(mem_ref[0], g_ref[...]).astype(BF16), w_ref[...]).astype(BF16)


def _mem_kv(mem, g, w):
    B, M, D = mem.shape
    return pl.pallas_call(
        _mem_kv_kernel,
        grid=(B,),
        in_specs=[
            pl.BlockSpec((1, M, D), lambda b: (b, 0, 0)),
            pl.BlockSpec((1, D), lambda b: (0, 0)),
            pl.BlockSpec(w.shape, lambda b: (0, 0)),
        ],
        out_specs=pl.BlockSpec((1, M, 2 * W_M), lambda b: (b, 0, 0)),
        out_shape=jax.ShapeDtypeStruct((B, M, 2 * W_M), BF16),
        compiler_params=pltpu.CompilerParams(
            dimension_semantics=("parallel",), vmem_limit_bytes=VMEM_LIMIT),
        name="mem_kv",
    )(mem, g, w)


def _out_kernel(x_ref, oa_ref, ob_ref, gate_ref, qm_ref, kv_ref, wo_ref, g_ref, y_ref, mix_ref, *, sub):
    tm = x_ref.shape[0]
    subs = [slice(r0, r0 + sub) for r0 in range(0, tm, sub)]
    lane = lax.broadcasted_iota(I32, (1, W_M), 1)
    head_lanes = [(lane >= h * HEAD_DIM) & (lane < (h + 1) * HEAD_DIM) for h in range(N_HEADS_MEM)]
    km = kv_ref[:, 0:W_M]
    vm = kv_ref[:, W_M:2 * W_M]
    zero = jnp.zeros((), BF16)
    vm_heads = [jnp.where(head_lanes[h], vm, zero) for h in range(N_HEADS_MEM)]

    def silu(g):
        return g * (1.0 / (1.0 + jnp.exp(-g)))

    def gated(o, rows, lo, width):
        return (o * silu(gate_ref[rows, lo:lo + width].astype(F32))).astype(BF16)

    for rows in subs:
        qm = qm_ref[rows, :]
        om = None
        for h in range(N_HEADS_MEM):
            s = _dot_nt(jnp.where(head_lanes[h], qm, zero), km)
            p = jnp.exp2(s - jnp.max(s, axis=1, keepdims=True))
            p = p / jnp.sum(p, axis=1, keepdims=True)
            o = _dot(p.astype(BF16), vm_heads[h])
            om = o if om is None else om + o
        mix_ref[rows, W_A + W_B:] = gated(om, rows, W_A + W_B, W_M)

    for rows in subs:
        half = W_A // 2
        mix_ref[rows, 0:half] = gated(oa_ref[rows, 0:half], rows, 0, half)
        mix_ref[rows, half:W_A] = gated(oa_ref[rows, half:W_A], rows, half, half)
        mix_ref[rows, W_A:W_A + W_B] = gated(ob_ref[rows, :], rows, W_A, W_B)
        y_ref[rows, :] = _dot(mix_ref[rows, :], wo_ref[...])

    for rows in subs:
        y_ref[rows, :] = _rms(x_ref[rows, :] + y_ref[rows, :], g_ref[...])


def _out(x, o_a, o_b, proj_r, kv_m, w_out, g, tm=512, sub=256):
    B, T, D = x.shape
    M = kv_m.shape[1]
    row = lambda b, i: (b, i, 0)
    return pl.pallas_call(
        functools.partial(_out_kernel, sub=sub),
        grid=(B, T // tm),
        in_specs=[
            pl.BlockSpec((None, tm, D), row),
            pl.BlockSpec((None, tm, W_A), row),
            pl.BlockSpec((None, tm, W_B), row),
            pl.BlockSpec((None, tm, MIX_WIDTH), lambda b, i: (b, i, R_GATE // MIX_WIDTH)),
            pl.BlockSpec((None, tm, W_M), lambda b, i: (b, i, R_QM // W_M)),
            pl.BlockSpec((None, M, 2 * W_M), lambda b, i: (b, 0, 0)),
            pl.BlockSpec(w_out.shape, lambda b, i: (0, 0)),
            pl.BlockSpec((1, D), lambda b, i: (0, 0)),
        ],
        out_specs=pl.BlockSpec((None, tm, D), row),
        out_shape=jax.ShapeDtypeStruct((B, T, D), F32),
        scratch_shapes=[pltpu.VMEM((tm, MIX_WIDTH), BF16)],
        compiler_params=pltpu.CompilerParams(
            dimension_semantics=("parallel", "parallel"), vmem_limit_bytes=VMEM_LIMIT),
        name="out",
    )(x, o_a, o_b, proj_r, proj_r, kv_m, w_out, g)


def _alibi_slopes(n):
    return 2.0 ** (-8.0 * jnp.arange(1, n + 1, dtype=F32) / n)


def _static_alibi_slopes(n):
    return tuple(2.0 ** (-8.0 * i / n) for i in range(1, n + 1))


def _split_weights(w):
    bounds = [0]
    for s in SPLIT_SIZES:
        bounds.append(bounds[-1] + s)
    q_a, k_a, v_a, q_b, k_b, v_b, q_m, gate, q_i, k_i, w_i = (
        w[:, bounds[i]:bounds[i + 1]] for i in range(len(SPLIT_SIZES)))
    scale = HEAD_DIM ** -0.5 * LOG2E
    idx_scale = IDX_DIM ** -0.5
    zeros = jnp.zeros((w.shape[0], LANES - HEAD_DIM), w.dtype)
    wa = jnp.concatenate([q_a * scale, k_a, v_a], axis=1)
    wr = jnp.concatenate([gate, q_m * scale, k_b, zeros, k_i, zeros], axis=1)
    wt = jnp.concatenate([q_i * idx_scale, q_b * scale, v_b], axis=1).T
    ww = jnp.concatenate([w_i.T, jnp.zeros((WI_ROWS - N_IDX_HEADS, w.shape[0]), w.dtype)], axis=0)
    return wa.astype(BF16), wr.astype(BF16), wt.astype(BF16), ww.astype(BF16)


def kernel(x, mem, g_in, g_mem, w_in, w_mem_kv, w_out, g_final):
    assert g_in.shape[0] == 1, "single-layer block: the final RMSNorm is fused into the output kernel"
    wa, wr, wt, ww = _split_weights(w_in[0])
    qkv_a, proj_r, proj_t, w_t = _proj(x, g_in, wa, wr, wt, ww)
    o_a = _attn_a(_alibi_slopes(N_HEADS_A), qkv_a)
    o_b = _dsa(proj_r, proj_t, w_t)
    kv_m = _mem_kv(mem, g_mem, w_mem_kv[0].astype(BF16))
    return _out(x, o_a, o_b, proj_r, kv_m, w_out[0].astype(BF16), g_final[None, :])
```

```python
import functools
import math
import types

import numpy as np

import jax
import jax.numpy as jnp
from jax import lax
from jax.experimental import pallas as pl
from jax.experimental.pallas import tpu as pltpu

F32 = jnp.float32
BF16 = jnp.bfloat16
I32 = jnp.int32

D_MODEL = 1024
HEAD_DIM = 64
N_HEADS_A = 8
N_HEADS_B = 4
N_HEADS_MEM = 4
W_A = N_HEADS_A * HEAD_DIM
W_B = N_HEADS_B * HEAD_DIM
W_M = N_HEADS_MEM * HEAD_DIM
MIX_WIDTH = W_A + W_B + W_M
DILATIONS = (1, 4, 16)
BAND = 128
N_IDX_HEADS = 8
IDX_DIM = 64
TOPK_MAX = 256
RMS_EPS = 1e-6
SPLIT_SIZES = (W_A, W_A, W_A, W_B, HEAD_DIM, HEAD_DIM, W_M, MIX_WIDTH,
               N_IDX_HEADS * IDX_DIM, IDX_DIM, N_IDX_HEADS)

LANES = 128
BLK = 128
NEG = -1e30
VMEM_LIMIT = 48 * 1024 * 1024

R_GATE = 0
R_QM = MIX_WIDTH
R_KB = R_QM + W_M
R_KI = R_KB + LANES
R_COLS = R_KI + LANES
T_QI = 0
T_QB = N_IDX_HEADS * IDX_DIM
T_VB = T_QB + W_B
T_ROWS = T_VB + HEAD_DIM
WI_ROWS = 16
DSA_Q = 256
DSA_HALVINGS = 12
POS_SPLIT = 64
LOG2E = math.log2(math.e)


def _dot(a, b):
    return jnp.dot(a, b, preferred_element_type=F32)


def _dot_nt(a, b):
    return lax.dot_general(a, b, (((1,), (1,)), ((), ())), preferred_element_type=F32)


def _rms(x, g):
    return x * lax.rsqrt(jnp.mean(x * x, axis=-1, keepdims=True) + RMS_EPS) * g


def _proj_kernel(x_ref, g_ref, wa_ref, wr_ref, wt_ref, ww_ref,
                 oa_ref, oa4_ref, oa16_ref, or_ref, ot_ref, ow_ref, res_ref, res4_ref):
    hb = _rms(x_ref[0], g_ref[...]).astype(BF16)
    tm = hb.shape[0]
    d4, d16 = DILATIONS[1], DILATIONS[2]
    step = d16 // d4
    res = _dot(hb, wa_ref[...])
    oa_ref[0] = res.astype(BF16)
    for grp in range(res.shape[1] // LANES):
        lanes = slice(grp * LANES, (grp + 1) * LANES)
        res_ref[grp] = res[:, lanes]
        for r in range(d4):
            rows4 = res_ref[grp, pl.ds(r, tm // d4, stride=d4), :]
            oa4_ref[0, r, :, lanes] = rows4.astype(BF16)
            res4_ref[grp, r * (tm // d4):(r + 1) * (tm // d4), :] = rows4
        for r in range(d16):
            start = (r % d4) * (tm // d4) + r // d4
            oa16_ref[0, r, :, lanes] = res4_ref[grp, pl.ds(start, tm // d16, stride=step), :].astype(BF16)
    or_ref[0] = _dot(hb, wr_ref[...]).astype(BF16)
    ot_ref[0] = _dot_nt(wt_ref[...], hb).astype(BF16)
    ow_ref[0] = _dot_nt(ww_ref[...], hb)


def _proj(x, g, wa, wr, wt, ww, tm=512):
    B, T, D = x.shape
    const = lambda b, i: (0, 0)
    d4, d16 = DILATIONS[1], DILATIONS[2]
    nat, cm4, cm16, proj_r, proj_t, w_t = pl.pallas_call(
        _proj_kernel,
        grid=(B, T // tm),
        in_specs=[
            pl.BlockSpec((1, tm, D), lambda b, i: (b, i, 0)),
            pl.BlockSpec((1, D), const),
            pl.BlockSpec(wa.shape, const),
            pl.BlockSpec(wr.shape, const),
            pl.BlockSpec(wt.shape, const),
            pl.BlockSpec(ww.shape, const),
        ],
        out_specs=[
            pl.BlockSpec((1, tm, 3 * W_A), lambda b, i: (b, i, 0)),
            pl.BlockSpec((1, d4, tm // d4, 3 * W_A), lambda b, i: (b, 0, i, 0)),
            pl.BlockSpec((1, d16, tm // d16, 3 * W_A), lambda b, i: (b, 0, i, 0)),
            pl.BlockSpec((1, tm, R_COLS), lambda b, i: (b, i, 0)),
            pl.BlockSpec((1, T_ROWS, tm), lambda b, i: (b, 0, i)),
            pl.BlockSpec((1, WI_ROWS, tm), lambda b, i: (b, 0, i)),
        ],
        out_shape=[
            jax.ShapeDtypeStruct((B, T, 3 * W_A), BF16),
            jax.ShapeDtypeStruct((B, d4, T // d4, 3 * W_A), BF16),
            jax.ShapeDtypeStruct((B, d16, T // d16, 3 * W_A), BF16),
            jax.ShapeDtypeStruct((B, T, R_COLS), BF16),
            jax.ShapeDtypeStruct((B, T_ROWS, T), BF16),
            jax.ShapeDtypeStruct((B, WI_ROWS, T), F32),
        ],
        scratch_shapes=[pltpu.VMEM((3 * W_A // LANES, tm, LANES), F32)] * 2,
        compiler_params=pltpu.CompilerParams(
            dimension_semantics=("parallel", "parallel"), vmem_limit_bytes=VMEM_LIMIT),
        name="proj",
    )(x, g, wa, wr, wt, ww)
    return (nat, cm4.reshape(nat.shape), cm16.reshape(nat.shape)), proj_r, proj_t, w_t


def _attn_a_kernel(slopes_ref, *refs, seq, unroll):
    n_pat = len(DILATIONS)
    qkv = [refs[3 * p:3 * p + 3] for p in range(n_pat)]
    o_ref, acc_ref, m_ref, bias_ref, p_ref = refs[3 * n_pat:]
    hp = pl.program_id(1)
    lane = lax.broadcasted_iota(I32, (1, LANES), 1)
    head_lanes = (lane < HEAD_DIM, lane >= HEAD_DIM)
    row = lax.broadcasted_iota(I32, (BLK, BLK), 0)
    col = lax.broadcasted_iota(I32, (BLK, BLK), 1)
    d_cur = (row - col).astype(F32)
    d_prev = d_cur + float(BAND)
    n_all = seq // BLK
    zero = jnp.zeros((), BF16)
    one = jnp.ones((), BF16)

    def n_blocks(p):
        return n_all // DILATIONS[p]

    def n_keys(p):
        return BLK if n_blocks(p) == 1 else 2 * BLK

    def block(g):
        return pl.ds(pl.multiple_of(g * BLK, BLK), BLK)

    def keys(p, g):
        return pl.ds(pl.multiple_of(jnp.maximum((g + 1) * BLK - n_keys(p), 0), BLK), n_keys(p))

    def variant(p, g):
        g = jnp.asarray(g, I32)
        return jnp.where(g == 0, 2, jnp.where(g % n_blocks(p) == 0, 1, 0))

    def pitch(dil):
        return seq // dil + 1 if seq // dil == BLK and dil > 1 else None

    def put(ref, h, p, g, val):
        dil = DILATIONS[p]
        start = g // n_blocks(p) + (g % n_blocks(p)) * (dil * BLK)
        if dil == 1:
            ref[h, p, pl.ds(pl.multiple_of(start, BLK), BLK), :] = val
        elif pitch(dil):
            ref[h, p, pl.ds(start * pitch(dil), BLK), :] = val
        else:
            ref[h, p, pl.ds(start, BLK, stride=dil), :] = val

    def probs(p):
        q_ref, k_ref, _ = qkv[p]

        def body(g):
            kw = k_ref[keys(p, g), :]
            q = q_ref[block(g), :]
            for h in range(2):
                s = _dot_nt(jnp.where(head_lanes[h], q, zero), kw)
                s = s + bias_ref[p, h, variant(p, g), :, 2 * BLK - n_keys(p):]
                m = jnp.max(s, axis=1, keepdims=True)
                p_ref[p % 2, h, g, :, :n_keys(p)] = jnp.exp2(s - m).astype(BF16)
                put(m_ref, h, p, g, jnp.broadcast_to(m, (BLK, LANES)))
        return body

    def values(p):
        v_ref = qkv[p][2]

        def body(g):
            vw = v_ref[keys(p, g), :]
            for h in range(2):
                put(acc_ref, h, p, g,
                    _dot(p_ref[p % 2, h, g, :, :n_keys(p)], jnp.where(head_lanes[h], vw, one)))
        return body

    def run(*bodies):
        def step(g, carry):
            for body in bodies:
                body(g)
            return carry
        lax.fori_loop(0, n_all, step, 0, unroll=unroll)

    for p, dil in enumerate(DILATIONS):
        for h in range(2):
            sd = slopes_ref[hp * 2 + h] * (float(dil) * LOG2E)
            cur = jnp.where(row >= col, -sd * d_cur, NEG)
            masked = jnp.full((BLK, BLK), NEG, F32)
            bias_ref[p, h, 0, :, BLK:] = cur
            bias_ref[p, h, 1, :, BLK:] = cur
            bias_ref[p, h, 2, :, BLK:] = cur if n_keys(p) == BLK else masked
            if n_keys(p) > BLK:
                bias_ref[p, h, 0, :, :BLK] = jnp.where(col >= row, -sd * d_prev, NEG)
                bias_ref[p, h, 1, :, :BLK] = masked
                bias_ref[p, h, 2, :, :BLK] = cur

    assert n_pat == 3
    run(probs(0))
    run(values(0), probs(1))
    run(values(1), probs(2))
    run(values(2))

    rows_per_step = 2 * BLK

    def natural_rows(ref, h, p, i):
        dil = DILATIONS[p]
        if not pitch(dil):
            return ref[h, p, pl.ds(pl.multiple_of(i * rows_per_step, rows_per_step), rows_per_step), :]
        per = rows_per_step // dil
        return jnp.concatenate(
            [ref[h, p, pl.ds(i * per + u, dil, stride=pitch(dil)), :] for u in range(per)], axis=0)

    def merge(i, carry):
        sl = pl.ds(pl.multiple_of(i * rows_per_step, rows_per_step), rows_per_step)
        nums = []
        for h in range(2):
            ms = [natural_rows(m_ref, h, p, i) for p in range(n_pat)]
            mx = functools.reduce(jnp.maximum, ms)
            nums.append(sum(jnp.exp2(ms[p] - mx) * natural_rows(acc_ref, h, p, i) for p in range(n_pat)))
        acc = jnp.where(head_lanes[0], nums[0], nums[1])
        den = pltpu.roll(jnp.where(head_lanes[0], nums[1], nums[0]), HEAD_DIM, axis=1)
        o_ref[sl, :] = acc / den
        return carry

    lax.fori_loop(0, seq // rows_per_step, merge, 0, unroll=4)


def _attn_a(slopes, qkv_layouts, unroll=16):
    B, T, _ = qkv_layouts[0].shape
    n_pairs = N_HEADS_A // 2
    n_pat = len(DILATIONS)

    def spec(off):
        return pl.BlockSpec((None, T, LANES), lambda b, h: (b, 0, off + h))

    return pl.pallas_call(
        functools.partial(_attn_a_kernel, seq=T, unroll=unroll),
        grid=(B, n_pairs),
        in_specs=[pl.BlockSpec(memory_space=pltpu.SMEM)]
        + [spec(part * n_pairs) for _ in range(n_pat) for part in range(3)],
        out_specs=pl.BlockSpec((None, T, LANES), lambda b, h: (b, 0, h)),
        out_shape=jax.ShapeDtypeStruct((B, T, W_A), F32),
        scratch_shapes=[pltpu.VMEM((2, n_pat, T + max(DILATIONS), LANES), F32),
                        pltpu.VMEM((2, n_pat, T + max(DILATIONS), LANES), F32),
                        pltpu.VMEM((n_pat, 2, 3, BLK, 2 * BLK), F32),
                        pltpu.VMEM((2, 2, T // BLK, BLK, 2 * BLK), BF16)],
        compiler_params=pltpu.CompilerParams(
            dimension_semantics=("parallel", "parallel"), vmem_limit_bytes=VMEM_LIMIT),
        name="attn_a",
    )(slopes, *[a for a in qkv_layouts for _ in range(3)])


def _bf16_pair(c):
    hi = float(np.asarray(c, np.float32).astype(BF16).astype(np.float32))
    lo = float(np.asarray(c - hi, np.float32).astype(BF16).astype(np.float32))
    return hi, lo


def _fold(x, op):
    return functools.reduce(op, [x[r:r + 8] for r in range(0, x.shape[0], 8)])


def _dsa_row(j, kk_ref, tq_ref, vt_ref, wt_ref, o_ref, sc_ref, s_ref, acc_ref, kpos_ref,
             *, seq, k_sel, snap_unroll, slopes):
    n_pairs = j + 1
    t_lane = j * DSA_Q + lax.broadcasted_iota(I32, (1, DSA_Q), 1)
    row = lax.broadcasted_iota(I32, (BLK, DSA_Q), 0)
    ws = wt_ref[0:N_IDX_HEADS, :] * (N_IDX_HEADS ** -0.5)
    inf = float("inf")
    few = t_lane < k_sel

    def rows(c):
        return pl.ds(pl.multiple_of(c * BLK, BLK), BLK)

    def group(fn, first, n, carry):
        for u in range(n):
            carry = fn(first + u, carry)
        return carry

    def each_chunk(fn, init, widest=4):
        carry, done, width = init, 0, widest
        while width >= 2:
            steps = (2 * n_pairs - done) // width
            carry = lax.fori_loop(
                0, steps, lambda i, c, done=done, width=width: group(fn, done + width * i, width, c), carry)
            done, width = done + steps * width, width // 2
        return carry

    def sub_reduce(x, op):
        return op(x, axis=0, keepdims=True)

    def score_chunk(c, carry):
        mn, mx = carry
        ki = kk_ref[rows(c), R_KI - R_KB:R_KI - R_KB + IDX_DIM]
        lgs = [_dot(ki, tq_ref[T_QI + h * IDX_DIM:T_QI + (h + 1) * IDX_DIM, :])
               for h in range(N_IDX_HEADS)]
        sc = functools.reduce(
            lambda a, b: a + b, [jnp.maximum(lg, 0.0) * ws[h:h + 1, :] for h, lg in enumerate(lgs)])
        causal = (c * BLK + row) <= t_lane
        sc_ref[rows(c), :] = jnp.where(causal, sc, -inf)
        mn = jnp.minimum(mn, _fold(jnp.where(causal, sc, inf), jnp.minimum))
        mx = jnp.maximum(mx, _fold(jnp.where(causal, sc, -inf), jnp.maximum))
        return mn, mx

    score_init = (jnp.full((8, DSA_Q), inf, F32), jnp.full((8, DSA_Q), -inf, F32))

    def scores():
        return each_chunk(score_chunk, score_init, widest=8)

    def count(pred):
        def body(c, cnt):
            return cnt + _fold(pred(sc_ref[rows(c), :], c).astype(I32), jnp.add)
        return sub_reduce(each_chunk(body, jnp.zeros((8, DSA_Q), I32)), jnp.sum)

    def halve(_, bounds):
        lo, hi, n_lo, n_hi = bounds
        mid = 0.5 * lo + 0.5 * hi
        cnt = count(lambda x, c: x >= mid)
        active = lo < hi
        up = active & (cnt >= k_sel)
        down = active & (cnt <= k_sel)
        return (jnp.where(up, mid, lo), jnp.where(down, mid, hi),
                jnp.where(up, cnt, n_lo), jnp.where(down, cnt, n_hi))

    def snap(bounds):
        lo, hi, n_lo, n_hi = bounds
        mid = 0.5 * lo + 0.5 * hi
        mid = jnp.where(mid > lo, mid, hi)

        def body(c, carry):
            cnt, above, below = carry
            x = sc_ref[rows(c), :]
            ge = x >= mid
            return (cnt + _fold(ge.astype(I32), jnp.add),
                    jnp.minimum(above, _fold(jnp.where(ge, x, inf), jnp.minimum)),
                    jnp.maximum(below, _fold(jnp.where(ge, -inf, x), jnp.maximum)))

        cnt, above, below = each_chunk(body, (jnp.zeros((8, DSA_Q), I32),
                                              jnp.full((8, DSA_Q), inf, F32),
                                              jnp.full((8, DSA_Q), -inf, F32)))
        cnt = sub_reduce(cnt, jnp.sum)
        above = sub_reduce(above, jnp.min)
        below = sub_reduce(below, jnp.max)
        active = lo < hi
        enough = cnt >= k_sel
        up = active & enough
        down = active & (cnt <= k_sel)
        return (jnp.where(up, above, lo), jnp.where(down, jnp.where(enough, above, below), hi),
                jnp.where(up, cnt, n_lo), jnp.where(down, cnt, n_hi))

    def snaps(bounds):
        for _ in range(snap_unroll):
            bounds = snap(bounds)
        return bounds

    def unsettled(bounds):
        return jnp.max((bounds[0] < bounds[1]).astype(I32)) > 0

    n_idx_bits = seq.bit_length() - 1
    assert 1 << n_idx_bits == seq

    def select(extremes):
        mn, mx = extremes
        lo = jnp.where(few, -inf, sub_reduce(mn, jnp.min))
        hi = jnp.where(few, -inf, sub_reduce(mx, jnp.max))
        bounds = lax.fori_loop(0, DSA_HALVINGS, halve, (lo, hi, t_lane + 1, jnp.zeros((1, DSA_Q), I32)))

        def state(bounds):
            lo, hi, n_lo, _ = bounds
            tied = (n_lo > k_sel) & jnp.logical_not(few)
            return jnp.max(jnp.where(lo < hi, 2, tied.astype(I32)))

        def snap_more(bounds):
            bounds = lax.while_loop(unsettled, snaps, bounds)
            return bounds + (state(bounds),)

        bounds = snaps(bounds)
        code = state(bounds)
        tau, _, n_ge, n_gt, code = lax.cond(code >= 2, lambda: snap_more(bounds), lambda: bounds + (code,))
        need = k_sel - n_gt
        tied = (n_ge > k_sel) & jnp.logical_not(few)

        def tie_break():
            def idx_step(it, jp):
                cand = jp + lax.shift_left(jnp.int32(1), n_idx_bits - 1 - it)
                below = count(lambda x, c: (x == tau) & ((c * BLK + row) < cand))
                return jnp.where(below < need, cand, jp)
            last = lax.fori_loop(0, n_idx_bits, idx_step, jnp.zeros((1, DSA_Q), I32))
            return jnp.where(tied, last, seq)

        surplus = code == 1
        tie_last = lax.cond(surplus, tie_break, lambda: jnp.full((1, DSA_Q), seq, I32))
        return tau, tie_last, surplus

    def qk(selection):
        tau, tie_last, surplus = selection

        @pl.when(j == 0)
        def _():
            pos = lax.broadcasted_iota(I32, (seq, LANES), 0)
            ln = lax.broadcasted_iota(I32, (seq, LANES), 1) - HEAD_DIM
            feat = jnp.where(ln == 0, pos // POS_SPLIT,
                             jnp.where(ln == 1, pos % POS_SPLIT,
                                       jnp.where(ln == 2, pos // POS_SPLIT,
                                                 jnp.where(ln == 3, pos % POS_SPLIT, 0))))
            kpos_ref[...] = kk_ref[:, 0:LANES] + feat.astype(BF16)

        q_row = lax.broadcasted_iota(I32, (HEAD_DIM, DSA_Q), 0)
        q_aug = []
        for h in range(N_HEADS_B):
            c_hi, c_lo = _bf16_pair(slopes[h] * LOG2E)
            feat = jnp.where(q_row == 0, POS_SPLIT * c_hi,
                             jnp.where(q_row == 1, c_hi,
                                       jnp.where(q_row == 2, POS_SPLIT * c_lo,
                                                 jnp.where(q_row == 3, c_lo, 0.0))))
            q_aug.append(jnp.concatenate(
                [tq_ref[T_QB + h * HEAD_DIM:T_QB + (h + 1) * HEAD_DIM, :], feat.astype(BF16)], axis=0))

        tau_all = jnp.where(few, float(jnp.finfo(F32).min), tau)

        def qk_phase(with_ties):
            def qk_chunk(c, ms):
                kb = kpos_ref[rows(c), :]
                x = sc_ref[rows(c), :]
                if with_ties:
                    s_idx = c * BLK + row
                    tie = jnp.where(x == tau, jnp.where(s_idx <= tie_last, 0.0, NEG), NEG)
                    sb = jnp.where(s_idx <= t_lane, jnp.where(x > tau, 0.0, tie), NEG)
                else:
                    sb = jnp.where(x >= tau_all, 0.0, NEG)
                out = []
                for h in range(N_HEADS_B):
                    s = _dot(kb, q_aug[h]) + sb
                    s_ref[h, rows(c), :] = s
                    out.append(jnp.maximum(ms[h], _fold(s, jnp.maximum)))
                return tuple(out)

            return each_chunk(qk_chunk, tuple(jnp.full((8, DSA_Q), NEG, F32) for _ in range(N_HEADS_B)),
                              widest=8)

        ms = lax.cond(surplus, lambda: qk_phase(True), lambda: qk_phase(False))
        acc_ref[...] = jnp.zeros((W_B, DSA_Q), F32)
        return tuple(sub_reduce(m, jnp.max) for m in ms)

    def exp_pv(ms, first_pair, n, ls):
        ls = list(ls)
        pvs = []
        for u in range(n):
            pair = first_pair + u
            ps = []
            for h in range(N_HEADS_B):
                halves = []
                for c in (2 * pair, 2 * pair + 1):
                    p = jnp.exp2(s_ref[h, rows(c), :] - ms[h])
                    ls[h] = ls[h] + _fold(p, jnp.add)
                    halves.append(p.astype(BF16))
                ps.append(jnp.concatenate(halves, axis=0))
            vt = vt_ref[:, pl.ds(pl.multiple_of(pair * 2 * BLK, 2 * BLK), 2 * BLK)]
            pvs.append([_dot(vt, ps[h]) for h in range(N_HEADS_B)])
        for h in range(N_HEADS_B):
            acc_ref[h * HEAD_DIM:(h + 1) * HEAD_DIM, :] += functools.reduce(
                lambda a, b: a + b, [pv[h] for pv in pvs])
        return tuple(ls)

    exp_init = tuple(jnp.zeros((8, DSA_Q), F32) for _ in range(N_HEADS_B))

    def finish(ls):
        l_all = jnp.concatenate(
            [jnp.broadcast_to(sub_reduce(ls[h], jnp.sum), (HEAD_DIM, DSA_Q)) for h in range(N_HEADS_B)],
            axis=0)
        o_ref[...] = (acc_ref[...] / l_all).T

    return types.SimpleNamespace(scores=scores, select=select, qk=qk, exp_pv=exp_pv, finish=finish,
                                 score_chunk=score_chunk, score_init=score_init, exp_init=exp_init,
                                 group=group, n_pairs=n_pairs)


def _dsa_kernel(kk_ref, tq_ref, vt_ref, wt_ref, o_ref, sc_ref, s_ref, acc_ref, kpos_ref, **static):
    j = pl.program_id(1)
    a, b = (_dsa_row(j, *(ref.at[i] for ref in (kk_ref, tq_ref, vt_ref, wt_ref, o_ref, sc_ref, s_ref,
                                                 acc_ref, kpos_ref)), **static) for i in range(2))
    ms = a.qk(a.select(a.scores()))

    def both(first_pair, n_pair, carry):
        ls, extremes = carry
        ls = a.exp_pv(ms, first_pair, n_pair, ls)
        return ls, b.group(b.score_chunk, 2 * first_pair, 2 * n_pair, extremes)

    carry = (a.exp_init, b.score_init)
    carry = lax.fori_loop(0, a.n_pairs // 2, lambda i, c: both(2 * i, 2, c), carry)
    ls, extremes = lax.fori_loop(0, a.n_pairs % 2, lambda i, c: both(a.n_pairs - 1, 1, c), carry)
    a.finish(ls)

    ms = b.qk(b.select(extremes))
    ls = b.exp_init
    ls = lax.fori_loop(0, b.n_pairs // 2, lambda i, c: b.exp_pv(ms, 2 * i, 2, c), ls)
    ls = lax.fori_loop(0, b.n_pairs % 2, lambda i, c: b.exp_pv(ms, b.n_pairs - 1, 1, c), ls)
    b.finish(ls)


def _dsa(proj_r, proj_t, w_t):
    B, T, _ = proj_r.shape
    k_sel = min(TOPK_MAX, T // 4)
    slopes = _static_alibi_slopes(N_HEADS_B)
    assert T // POS_SPLIT <= 256 and POS_SPLIT <= 256 and B % 2 == 0
    return pl.pallas_call(
        functools.partial(_dsa_kernel, seq=T, k_sel=k_sel, snap_unroll=4, slopes=slopes),
        grid=(B // 2, T // DSA_Q),
        in_specs=[
            pl.BlockSpec((2, T, 2 * LANES), lambda b, q: (b, 0, R_KB // (2 * LANES))),
            pl.BlockSpec((2, T_VB, DSA_Q), lambda b, q: (b, 0, q)),
            pl.BlockSpec((2, HEAD_DIM, T), lambda b, q: (b, T_VB // HEAD_DIM, 0)),
            pl.BlockSpec((2, WI_ROWS, DSA_Q), lambda b, q: (b, 0, q)),
        ],
        out_specs=pl.BlockSpec((2, DSA_Q, W_B), lambda b, q: (b, q, 0)),
        out_shape=jax.ShapeDtypeStruct((B, T, W_B), F32),
        scratch_shapes=[pltpu.VMEM((2, T, DSA_Q), F32),
                        pltpu.VMEM((2, N_HEADS_B, T, DSA_Q), F32),
                        pltpu.VMEM((2, W_B, DSA_Q), F32),
                        pltpu.VMEM((2, T, LANES), BF16)],
        compiler_params=pltpu.CompilerParams(
            dimension_semantics=("parallel", "arbitrary"), vmem_limit_bytes=VMEM_LIMIT),
        name="dsa",
    )(proj_r, proj_t, proj_t, w_t)


def _mem_kv_kernel(mem_ref, g_ref, w_ref, kv_ref):
    kv_ref[0] = _dot(_rms(mem_ref[0], g_ref[...]).astype(BF16), w_ref[...]).astype(BF16)


def _mem_kv(mem, g, w):
    B, M, D = mem.shape
    return pl.pallas_call(
        _mem_kv_kernel,
        grid=(B,),
        in_specs=[
            pl.BlockSpec((1, M, D), lambda b: (b, 0, 0)),
            pl.BlockSpec((1, D), lambda b: (0, 0)),
            pl.BlockSpec(w.shape, lambda b: (0, 0)),
        ],
        out_specs=pl.BlockSpec((1, M, 2 * W_M), lambda b: (b, 0, 0)),
        out_shape=jax.ShapeDtypeStruct((B, M, 2 * W_M), BF16),
        compiler_params=pltpu.CompilerParams(
            dimension_semantics=("parallel",), vmem_limit_bytes=VMEM_LIMIT),
        name="mem_kv",
    )(mem, g, w)


def _out_kernel(x_ref, oa_ref, ob_ref, gate_ref, qm_ref, kv_ref, wo_ref, g_ref, y_ref, mix_ref, *, sub):
    tm = x_ref.shape[0]
    subs = [slice(r0, r0 + sub) for r0 in range(0, tm, sub)]
    lane = lax.broadcasted_iota(I32, (1, W_M), 1)
    head_lanes = [(lane >= h * HEAD_DIM) & (lane < (h + 1) * HEAD_DIM) for h in range(N_HEADS_MEM)]
    km = kv_ref[:, 0:W_M]
    vm = kv_ref[:, W_M:2 * W_M]
    zero = jnp.zeros((), BF16)
    vm_heads = [jnp.where(head_lanes[h], vm, zero) for h in range(N_HEADS_MEM)]

    def silu(g):
        return g * jax.nn.sigmoid(g)

    def gated(o, rows, lo, width):
        return (o * silu(gate_ref[rows, lo:lo + width].astype(F32))).astype(BF16)

    for rows in subs:
        qm = qm_ref[rows, :]
        om = None
        for h in range(N_HEADS_MEM):
            s = _dot_nt(jnp.where(head_lanes[h], qm, zero), km)
            p = jnp.exp2(s - jnp.max(s, axis=1, keepdims=True))
            p = p / jnp.sum(p, axis=1, keepdims=True)
            o = _dot(p.astype(BF16), vm_heads[h])
            om = o if om is None else om + o
        mix_ref[rows, W_A + W_B:] = gated(om, rows, W_A + W_B, W_M)

    for rows in subs:
        half = W_A // 2
        mix_ref[rows, 0:half] = gated(oa_ref[rows, 0:half], rows, 0, half)
        mix_ref[rows, half:W_A] = gated(oa_ref[rows, half:W_A], rows, half, half)
        mix_ref[rows, W_A:W_A + W_B] = gated(ob_ref[rows, :], rows, W_A, W_B)
        y_ref[rows, :] = _dot(mix_ref[rows, :], wo_ref[...])

    for rows in subs:
        y_ref[rows, :] = _rms(x_ref[rows, :] + y_ref[rows, :], g_ref[...])


def _out(x, o_a, o_b, proj_r, kv_m, w_out, g, tm=512, sub=256):
    B, T, D = x.shape
    M = kv_m.shape[1]
    row = lambda b, i: (b, i, 0)
    return pl.pallas_call(
        functools.partial(_out_kernel, sub=sub),
        grid=(B, T // tm),
        in_specs=[
            pl.BlockSpec((None, tm, D), row),
            pl.BlockSpec((None, tm, W_A), row),
            pl.BlockSpec((None, tm, W_B), row),
            pl.BlockSpec((None, tm, MIX_WIDTH), lambda b, i: (b, i, R_GATE // MIX_WIDTH)),
            pl.BlockSpec((None, tm, W_M), lambda b, i: (b, i, R_QM // W_M)),
            pl.BlockSpec((None, M, 2 * W_M), lambda b, i: (b, 0, 0)),
            pl.BlockSpec(w_out.shape, lambda b, i: (0, 0)),
            pl.BlockSpec((1, D), lambda b, i: (0, 0)),
        ],
        out_specs=pl.BlockSpec((None, tm, D), row),
        out_shape=jax.ShapeDtypeStruct((B, T, D), F32),
        scratch_shapes=[pltpu.VMEM((tm, MIX_WIDTH), BF16)],
        compiler_params=pltpu.CompilerParams(
            dimension_semantics=("parallel", "parallel"), vmem_limit_bytes=VMEM_LIMIT),
        name="out",
    )(x, o_a, o_b, proj_r, proj_r, kv_m, w_out, g)


def _alibi_slopes(n):
    return 2.0 ** (-8.0 * jnp.arange(1, n + 1, dtype=F32) / n)


def _static_alibi_slopes(n):
    return tuple(2.0 ** (-8.0 * i / n) for i in range(1, n + 1))


def _split_weights(w):
    bounds = [0]
    for s in SPLIT_SIZES:
        bounds.append(bounds[-1] + s)
    q_a, k_a, v_a, q_b, k_b, v_b, q_m, gate, q_i, k_i, w_i = (
        w[:, bounds[i]:bounds[i + 1]] for i in range(len(SPLIT_SIZES)))
    scale = HEAD_DIM ** -0.5 * LOG2E
    idx_scale = IDX_DIM ** -0.5
    zeros = jnp.zeros((w.shape[0], LANES - HEAD_DIM), w.dtype)
    wa = jnp.concatenate([q_a * scale, k_a, v_a], axis=1)
    wr = jnp.concatenate([gate, q_m * scale, k_b, zeros, k_i, zeros], axis=1)
    wt = jnp.concatenate([q_i * idx_scale, q_b * scale, v_b], axis=1).T
    ww = jnp.concatenate([w_i.T, jnp.zeros((WI_ROWS - N_IDX_HEADS, w.shape[0]), w.dtype)], axis=0)
    return wa.astype(BF16), wr.astype(BF16), wt.astype(BF16), ww.astype(BF16)


def kernel(x, mem, g_in, g_mem, w_in, w_mem_kv, w_out, g_final):
    assert g_in.shape[0] == 1, "single-layer block: the final RMSNorm is fused into the output kernel"
    wa, wr, wt, ww = _split_weights(w_in[0])
    qkv_a, proj_r, proj_t, w_t = _proj(x, g_in, wa, wr, wt, ww)
    o_a = _attn_a(_alibi_slopes(N_HEADS_A), qkv_a)
    o_b = _dsa(proj_r, proj_t, w_t)
    kv_m = _mem_kv(mem, g_mem, w_mem_kv[0].astype(BF16))
    return _out(x, o_a, o_b, proj_r, kv_m, w_out[0].astype(BF16), g_final[None, :])
```

```python
import functools
import math
import types

import numpy as np

import jax
import jax.numpy as jnp
from jax import lax
from jax.experimental import pallas as pl
from jax.experimental.pallas import tpu as pltpu

F32 = jnp.float32
BF16 = jnp.bfloat16
I32 = jnp.int32

D_MODEL = 1024
HEAD_DIM = 64
N_HEADS_A = 8
N_HEADS_B = 4
N_HEADS_MEM = 4
W_A = N_HEADS_A * HEAD_DIM
W_B = N_HEADS_B * HEAD_DIM
W_M = N_HEADS_MEM * HEAD_DIM
MIX_WIDTH = W_A + W_B + W_M
DILATIONS = (1, 4, 16)
BAND = 128
N_IDX_HEADS = 8
IDX_DIM = 64
TOPK_MAX = 256
RMS_EPS = 1e-6
SPLIT_SIZES = (W_A, W_A, W_A, W_B, HEAD_DIM, HEAD_DIM, W_M, MIX_WIDTH,
               N_IDX_HEADS * IDX_DIM, IDX_DIM, N_IDX_HEADS)

LANES = 128
BLK = 128
NEG = -1e30
VMEM_LIMIT = 48 * 1024 * 1024

R_GATE = 0
R_QM = MIX_WIDTH
R_KB = R_QM + W_M
R_KI = R_KB + LANES
R_COLS = R_KI + LANES
T_QI = 0
T_QB = N_IDX_HEADS * IDX_DIM
T_VB = T_QB + W_B
T_ROWS = T_VB + HEAD_DIM
WI_ROWS = 16
DSA_Q = 256
DSA_HALVINGS = 12
POS_SPLIT = 64
LOG2E = math.log2(math.e)


def _dot(a, b):
    return jnp.dot(a, b, preferred_element_type=F32)


def _dot_nt(a, b):
    return lax.dot_general(a, b, (((1,), (1,)), ((), ())), preferred_element_type=F32)


def _rms(x, g):
    return x * lax.rsqrt(jnp.mean(x * x, axis=-1, keepdims=True) + RMS_EPS) * g


def _proj_kernel(x_ref, g_ref, wa_ref, wr_ref, wt_ref, ww_ref,
                 oa_ref, oa4_ref, oa16_ref, or_ref, ot_ref, ow_ref, res_ref, res4_ref):
    hb = _rms(x_ref[0], g_ref[...]).astype(BF16)
    tm = hb.shape[0]
    d4, d16 = DILATIONS[1], DILATIONS[2]
    step = d16 // d4
    res = _dot(hb, wa_ref[...])
    oa_ref[0] = res.astype(BF16)
    for grp in range(res.shape[1] // LANES):
        lanes = slice(grp * LANES, (grp + 1) * LANES)
        res_ref[grp] = res[:, lanes]
        for r in range(d4):
            rows4 = res_ref[grp, pl.ds(r, tm // d4, stride=d4), :]
            oa4_ref[0, r, :, lanes] = rows4.astype(BF16)
            res4_ref[grp, r * (tm // d4):(r + 1) * (tm // d4), :] = rows4
        for r in range(d16):
            start = (r % d4) * (tm // d4) + r // d4
            oa16_ref[0, r, :, lanes] = res4_ref[grp, pl.ds(start, tm // d16, stride=step), :].astype(BF16)
    or_ref[0] = _dot(hb, wr_ref[...]).astype(BF16)
    ot_ref[0] = _dot_nt(wt_ref[...], hb).astype(BF16)
    ow_ref[0] = _dot_nt(ww_ref[...], hb)


def _proj(x, g, wa, wr, wt, ww, tm=512):
    B, T, D = x.shape
    const = lambda b, i: (0, 0)
    d4, d16 = DILATIONS[1], DILATIONS[2]
    nat, cm4, cm16, proj_r, proj_t, w_t = pl.pallas_call(
        _proj_kernel,
        grid=(B, T // tm),
        in_specs=[
            pl.BlockSpec((1, tm, D), lambda b, i: (b, i, 0)),
            pl.BlockSpec((1, D), const),
            pl.BlockSpec(wa.shape, const),
            pl.BlockSpec(wr.shape, const),
            pl.BlockSpec(wt.shape, const),
            pl.BlockSpec(ww.shape, const),
        ],
        out_specs=[
            pl.BlockSpec((1, tm, 3 * W_A), lambda b, i: (b, i, 0)),
            pl.BlockSpec((1, d4, tm // d4, 3 * W_A), lambda b, i: (b, 0, i, 0)),
            pl.BlockSpec((1, d16, tm // d16, 3 * W_A), lambda b, i: (b, 0, i, 0)),
            pl.BlockSpec((1, tm, R_COLS), lambda b, i: (b, i, 0)),
            pl.BlockSpec((1, T_ROWS, tm), lambda b, i: (b, 0, i)),
            pl.BlockSpec((1, WI_ROWS, tm), lambda b, i: (b, 0, i)),
        ],
        out_shape=[
            jax.ShapeDtypeStruct((B, T, 3 * W_A), BF16),
            jax.ShapeDtypeStruct((B, d4, T // d4, 3 * W_A), BF16),
            jax.ShapeDtypeStruct((B, d16, T // d16, 3 * W_A), BF16),
            jax.ShapeDtypeStruct((B, T, R_COLS), BF16),
            jax.ShapeDtypeStruct((B, T_ROWS, T), BF16),
            jax.ShapeDtypeStruct((B, WI_ROWS, T), F32),
        ],
        scratch_shapes=[pltpu.VMEM((3 * W_A // LANES, tm, LANES), F32)] * 2,
        compiler_params=pltpu.CompilerParams(
            dimension_semantics=("parallel", "parallel"), vmem_limit_bytes=VMEM_LIMIT),
        name="proj",
    )(x, g, wa, wr, wt, ww)
    return (nat, cm4.reshape(nat.shape), cm16.reshape(nat.shape)), proj_r, proj_t, w_t


def _attn_a_kernel(slopes_ref, *refs, seq, unroll):
    n_pat = len(DILATIONS)
    qkv = [refs[3 * p:3 * p + 3] for p in range(n_pat)]
    o_ref, acc_ref, m_ref, bias_ref, p_ref = refs[3 * n_pat:]
    hp = pl.program_id(1)
    lane = lax.broadcasted_iota(I32, (1, LANES), 1)
    head_lanes = (lane < HEAD_DIM, lane >= HEAD_DIM)
    row = lax.broadcasted_iota(I32, (BLK, BLK), 0)
    col = lax.broadcasted_iota(I32, (BLK, BLK), 1)
    d_cur = (row - col).astype(F32)
    d_prev = d_cur + float(BAND)
    n_all = seq // BLK
    zero = jnp.zeros((), BF16)
    one = jnp.ones((), BF16)

    def n_blocks(p):
        return n_all // DILATIONS[p]

    def n_keys(p):
        return BLK if n_blocks(p) == 1 else 2 * BLK

    def block(g):
        return pl.ds(pl.multiple_of(g * BLK, BLK), BLK)

    def keys(p, g):
        return pl.ds(pl.multiple_of(jnp.maximum((g + 1) * BLK - n_keys(p), 0), BLK), n_keys(p))

    def variant(p, g):
        g = jnp.asarray(g, I32)
        return jnp.where(g == 0, 2, jnp.where(g % n_blocks(p) == 0, 1, 0))

    def pitch(dil):
        return seq // dil + 1 if seq // dil == BLK and dil > 1 else None

    def put(ref, h, p, g, val):
        dil = DILATIONS[p]
        start = g // n_blocks(p) + (g % n_blocks(p)) * (dil * BLK)
        if dil == 1:
            ref[h, p, pl.ds(pl.multiple_of(start, BLK), BLK), :] = val
        elif pitch(dil):
            ref[h, p, pl.ds(start * pitch(dil), BLK), :] = val
        else:
            ref[h, p, pl.ds(start, BLK, stride=dil), :] = val

    def probs(p):
        q_ref, k_ref, _ = qkv[p]

        def body(g):
            kw = k_ref[keys(p, g), :]
            q = q_ref[block(g), :]
            for h in range(2):
                s = _dot_nt(jnp.where(head_lanes[h], q, zero), kw)
                s = s + bias_ref[p, h, variant(p, g), :, 2 * BLK - n_keys(p):]
                m = jnp.max(s, axis=1, keepdims=True)
                p_ref[p % 2, h, g, :, :n_keys(p)] = jnp.exp2(s - m).astype(BF16)
                put(m_ref, h, p, g, jnp.broadcast_to(m, (BLK, LANES)))
        return body

    def values(p):
        v_ref = qkv[p][2]

        def body(g):
            vw = v_ref[keys(p, g), :]
            for h in range(2):
                put(acc_ref, h, p, g,
                    _dot(p_ref[p % 2, h, g, :, :n_keys(p)], jnp.where(head_lanes[h], vw, one)))
        return body

    def run(*bodies):
        def step(g, carry):
            for body in bodies:
                body(g)
            return carry
        lax.fori_loop(0, n_all, step, 0, unroll=unroll)

    for p, dil in enumerate(DILATIONS):
        for h in range(2):
            sd = slopes_ref[hp * 2 + h] * (float(dil) * LOG2E)
            cur = jnp.where(row >= col, -sd * d_cur, NEG)
            masked = jnp.full((BLK, BLK), NEG, F32)
            bias_ref[p, h, 0, :, BLK:] = cur
            bias_ref[p, h, 1, :, BLK:] = cur
            bias_ref[p, h, 2, :, BLK:] = cur if n_keys(p) == BLK else masked
            if n_keys(p) > BLK:
                bias_ref[p, h, 0, :, :BLK] = jnp.where(col >= row, -sd * d_prev, NEG)
                bias_ref[p, h, 1, :, :BLK] = masked
                bias_ref[p, h, 2, :, :BLK] = cur

    assert n_pat == 3
    run(probs(0))
    run(values(0), probs(1))
    run(values(1), probs(2))
    run(values(2))

    rows_per_step = 2 * BLK

    def natural_rows(ref, h, p, i):
        dil = DILATIONS[p]
        if not pitch(dil):
            return ref[h, p, pl.ds(pl.multiple_of(i * rows_per_step, rows_per_step), rows_per_step), :]
        per = rows_per_step // dil
        return jnp.concatenate(
            [ref[h, p, pl.ds(i * per + u, dil, stride=pitch(dil)), :] for u in range(per)], axis=0)

    def merge(i, carry):
        sl = pl.ds(pl.multiple_of(i * rows_per_step, rows_per_step), rows_per_step)
        nums = []
        for h in range(2):
            ms = [natural_rows(m_ref, h, p, i) for p in range(n_pat)]
            mx = functools.reduce(jnp.maximum, ms)
            nums.append(sum(jnp.exp2(ms[p] - mx) * natural_rows(acc_ref, h, p, i) for p in range(n_pat)))
        acc = jnp.where(head_lanes[0], nums[0], nums[1])
        den = pltpu.roll(jnp.where(head_lanes[0], nums[1], nums[0]), HEAD_DIM, axis=1)
        o_ref[sl, :] = acc / den
        return carry

    lax.fori_loop(0, seq // rows_per_step, merge, 0, unroll=4)


def _attn_a(slopes, qkv_layouts, unroll=16):
    B, T, _ = qkv_layouts[0].shape
    n_pairs = N_HEADS_A // 2
    n_pat = len(DILATIONS)

    def spec(off):
        return pl.BlockSpec((None, T, LANES), lambda b, h: (b, 0, off + h))

    return pl.pallas_call(
        functools.partial(_attn_a_kernel, seq=T, unroll=unroll),
        grid=(B, n_pairs),
        in_specs=[pl.BlockSpec(memory_space=pltpu.SMEM)]
        + [spec(part * n_pairs) for _ in range(n_pat) for part in range(3)],
        out_specs=pl.BlockSpec((None, T, LANES), lambda b, h: (b, 0, h)),
        out_shape=jax.ShapeDtypeStruct((B, T, W_A), F32),
        scratch_shapes=[pltpu.VMEM((2, n_pat, T + max(DILATIONS), LANES), F32),
                        pltpu.VMEM((2, n_pat, T + max(DILATIONS), LANES), F32),
                        pltpu.VMEM((n_pat, 2, 3, BLK, 2 * BLK), F32),
                        pltpu.VMEM((2, 2, T // BLK, BLK, 2 * BLK), BF16)],
        compiler_params=pltpu.CompilerParams(
            dimension_semantics=("parallel", "parallel"), vmem_limit_bytes=VMEM_LIMIT),
        name="attn_a",
    )(slopes, *[a for a in qkv_layouts for _ in range(3)])


def _bf16_pair(c):
    hi = float(np.asarray(c, np.float32).astype(BF16).astype(np.float32))
    lo = float(np.asarray(c - hi, np.float32).astype(BF16).astype(np.float32))
    return hi, lo


def _fold(x, op):
    return functools.reduce(op, [x[r:r + 8] for r in range(0, x.shape[0], 8)])


def _dsa_row(j, kk_ref, tq_ref, vt_ref, wt_ref, o_ref, sc_ref, s_ref, acc_ref, kpos_ref,
             *, seq, k_sel, snap_unroll, slopes):
    n_pairs = j + 1
    t_lane = j * DSA_Q + lax.broadcasted_iota(I32, (1, DSA_Q), 1)
    row = lax.broadcasted_iota(I32, (BLK, DSA_Q), 0)
    ws = wt_ref[0:N_IDX_HEADS, :] * (N_IDX_HEADS ** -0.5)
    inf = float("inf")
    few = t_lane < k_sel

    def rows(c):
        return pl.ds(pl.multiple_of(c * BLK, BLK), BLK)

    def group(fn, first, n, carry):
        for u in range(n):
            carry = fn(first + u, carry)
        return carry

    def each_chunk(fn, init, widest=4):
        carry, done, width = init, 0, widest
        while width >= 2:
            steps = (2 * n_pairs - done) // width
            carry = lax.fori_loop(
                0, steps, lambda i, c, done=done, width=width: group(fn, done + width * i, width, c), carry)
            done, width = done + steps * width, width // 2
        return carry

    def sub_reduce(x, op):
        return op(x, axis=0, keepdims=True)

    def score_chunk(c, carry):
        mn, mx = carry
        ki = kk_ref[rows(c), R_KI - R_KB:R_KI - R_KB + IDX_DIM]
        lgs = [_dot(ki, tq_ref[T_QI + h * IDX_DIM:T_QI + (h + 1) * IDX_DIM, :])
               for h in range(N_IDX_HEADS)]
        sc = functools.reduce(
            lambda a, b: a + b, [jnp.maximum(lg, 0.0) * ws[h:h + 1, :] for h, lg in enumerate(lgs)])
        causal = (c * BLK + row) <= t_lane
        sc_ref[rows(c), :] = jnp.where(causal, sc, -inf)
        mn = jnp.minimum(mn, _fold(jnp.where(causal, sc, inf), jnp.minimum))
        mx = jnp.maximum(mx, _fold(jnp.where(causal, sc, -inf), jnp.maximum))
        return mn, mx

    score_init = (jnp.full((8, DSA_Q), inf, F32), jnp.full((8, DSA_Q), -inf, F32))

    def scores():
        return each_chunk(score_chunk, score_init, widest=8)

    def count(pred):
        def body(c, cnt):
            return cnt + _fold(pred(sc_ref[rows(c), :], c).astype(I32), jnp.add)
        return sub_reduce(each_chunk(body, jnp.zeros((8, DSA_Q), I32)), jnp.sum)

    def halve(_, bounds):
        lo, hi, n_lo, n_hi = bounds
        mid = 0.5 * lo + 0.5 * hi
        cnt = count(lambda x, c: x >= mid)
        active = lo < hi
        up = active & (cnt >= k_sel)
        down = active & (cnt <= k_sel)
        return (jnp.where(up, mid, lo), jnp.where(down, mid, hi),
                jnp.where(up, cnt, n_lo), jnp.where(down, cnt, n_hi))

    def snap(bounds):
        lo, hi, n_lo, n_hi = bounds
        mid = 0.5 * lo + 0.5 * hi
        mid = jnp.where(mid > lo, mid, hi)

        def body(c, carry):
            cnt, above, below = carry
            x = sc_ref[rows(c), :]
            ge = x >= mid
            return (cnt + _fold(ge.astype(I32), jnp.add),
                    jnp.minimum(above, _fold(jnp.where(ge, x, inf), jnp.minimum)),
                    jnp.maximum(below, _fold(jnp.where(ge, -inf, x), jnp.maximum)))

        cnt, above, below = each_chunk(body, (jnp.zeros((8, DSA_Q), I32),
                                              jnp.full((8, DSA_Q), inf, F32),
                                              jnp.full((8, DSA_Q), -inf, F32)))
        cnt = sub_reduce(cnt, jnp.sum)
        above = sub_reduce(above, jnp.min)
        below = sub_reduce(below, jnp.max)
        active = lo < hi
        enough = cnt >= k_sel
        up = active & enough
        down = active & (cnt <= k_sel)
        return (jnp.where(up, above, lo), jnp.where(down, jnp.where(enough, above, below), hi),
                jnp.where(up, cnt, n_lo), jnp.where(down, cnt, n_hi))

    def snaps(bounds):
        for _ in range(snap_unroll):
            bounds = snap(bounds)
        return bounds

    def unsettled(bounds):
        return jnp.max((bounds[0] < bounds[1]).astype(I32)) > 0

    n_idx_bits = seq.bit_length() - 1
    assert 1 << n_idx_bits == seq

    def select(extremes):
        mn, mx = extremes
        lo = jnp.where(few, -inf, sub_reduce(mn, jnp.min))
        hi = jnp.where(few, -inf, sub_reduce(mx, jnp.max))
        bounds = lax.fori_loop(0, DSA_HALVINGS, halve, (lo, hi, t_lane + 1, jnp.zeros((1, DSA_Q), I32)))

        def state(bounds):
            lo, hi, n_lo, _ = bounds
            tied = (n_lo > k_sel) & jnp.logical_not(few)
            return jnp.max(jnp.where(lo < hi, 2, tied.astype(I32)))

        def snap_more(bounds):
            bounds = lax.while_loop(unsettled, snaps, bounds)
            return bounds + (state(bounds),)

        bounds = snaps(bounds)
        code = state(bounds)
        tau, _, n_ge, n_gt, code = lax.cond(code >= 2, lambda: snap_more(bounds), lambda: bounds + (code,))
        need = k_sel - n_gt
        tied = (n_ge > k_sel) & jnp.logical_not(few)

        def tie_break():
            def idx_step(it, jp):
                cand = jp + lax.shift_left(jnp.int32(1), n_idx_bits - 1 - it)
                below = count(lambda x, c: (x == tau) & ((c * BLK + row) < cand))
                return jnp.where(below < need, cand, jp)
            last = lax.fori_loop(0, n_idx_bits, idx_step, jnp.zeros((1, DSA_Q), I32))
            return jnp.where(tied, last, seq)

        surplus = code == 1
        tie_last = lax.cond(surplus, tie_break, lambda: jnp.full((1, DSA_Q), seq, I32))
        return tau, tie_last, surplus

    def qk(selection):
        tau, tie_last, surplus = selection

        @pl.when(j == 0)
        def _():
            pos = lax.broadcasted_iota(I32, (seq, LANES), 0)
            ln = lax.broadcasted_iota(I32, (seq, LANES), 1) - HEAD_DIM
            feat = jnp.where(ln == 0, pos // POS_SPLIT,
                             jnp.where(ln == 1, pos % POS_SPLIT,
                                       jnp.where(ln == 2, pos // POS_SPLIT,
                                                 jnp.where(ln == 3, pos % POS_SPLIT, 0))))
            kpos_ref[...] = kk_ref[:, 0:LANES] + feat.astype(BF16)

        q_row = lax.broadcasted_iota(I32, (HEAD_DIM, DSA_Q), 0)
        q_aug = []
        for h in range(N_HEADS_B):
            c_hi, c_lo = _bf16_pair(slopes[h] * LOG2E)
            feat = jnp.where(q_row == 0, POS_SPLIT * c_hi,
                             jnp.where(q_row == 1, c_hi,
                                       jnp.where(q_row == 2, POS_SPLIT * c_lo,
                                                 jnp.where(q_row == 3, c_lo, 0.0))))
            q_aug.append(jnp.concatenate(
                [tq_ref[T_QB + h * HEAD_DIM:T_QB + (h + 1) * HEAD_DIM, :], feat.astype(BF16)], axis=0))

        tau_all = jnp.where(few, float(jnp.finfo(F32).min), tau)

        def qk_phase(with_ties):
            def qk_chunk(c, ms):
                kb = kpos_ref[rows(c), :]
                x = sc_ref[rows(c), :]
                if with_ties:
                    s_idx = c * BLK + row
                    tie = jnp.where(x == tau, jnp.where(s_idx <= tie_last, 0.0, NEG), NEG)
                    sb = jnp.where(s_idx <= t_lane, jnp.where(x > tau, 0.0, tie), NEG)
                else:
                    sb = jnp.where(x >= tau_all, 0.0, NEG)
                out = []
                for h in range(N_HEADS_B):
                    s = _dot(kb, q_aug[h]) + sb
                    s_ref[h, rows(c), :] = s
                    out.append(jnp.maximum(ms[h], _fold(s, jnp.maximum)))
                return tuple(out)

            return each_chunk(qk_chunk, tuple(jnp.full((8, DSA_Q), NEG, F32) for _ in range(N_HEADS_B)),
                              widest=8)

        ms = lax.cond(surplus, lambda: qk_phase(True), lambda: qk_phase(False))
        acc_ref[...] = jnp.zeros((W_B, DSA_Q), F32)
        return tuple(sub_reduce(m, jnp.max) for m in ms)

    def exp_pv(ms, first_pair, n, ls):
        ls = list(ls)
        pvs = []
        for u in range(n):
            pair = first_pair + u
            ps = []
            for h in range(N_HEADS_B):
                halves = []
                for c in (2 * pair, 2 * pair + 1):
                    p = jnp.exp2(s_ref[h, rows(c), :] - ms[h])
                    ls[h] = ls[h] + _fold(p, jnp.add)
                    halves.append(p.astype(BF16))
                ps.append(jnp.concatenate(halves, axis=0))
            vt = vt_ref[:, pl.ds(pl.multiple_of(pair * 2 * BLK, 2 * BLK), 2 * BLK)]
            pvs.append([_dot(vt, ps[h]) for h in range(N_HEADS_B)])
        for h in range(N_HEADS_B):
            acc_ref[h * HEAD_DIM:(h + 1) * HEAD_DIM, :] += functools.reduce(
                lambda a, b: a + b, [pv[h] for pv in pvs])
        return tuple(ls)

    exp_init = tuple(jnp.zeros((8, DSA_Q), F32) for _ in range(N_HEADS_B))

    def finish(ls):
        l_all = jnp.concatenate(
            [jnp.broadcast_to(sub_reduce(ls[h], jnp.sum), (HEAD_DIM, DSA_Q)) for h in range(N_HEADS_B)],
            axis=0)
        o_ref[...] = (acc_ref[...] / l_all).T

    return types.SimpleNamespace(scores=scores, select=select, qk=qk, exp_pv=exp_pv, finish=finish,
                                 score_chunk=score_chunk, score_init=score_init, exp_init=exp_init,
                                 group=group, n_pairs=n_pairs)


def _dsa_kernel(kk_ref, tq_ref, vt_ref, wt_ref, o_ref, sc_ref, s_ref, acc_ref, kpos_ref, **static):
    j = pl.program_id(1)
    a, b = (_dsa_row(j, *(ref.at[i] for ref in (kk_ref, tq_ref, vt_ref, wt_ref, o_ref, sc_ref, s_ref,
                                                 acc_ref, kpos_ref)), **static) for i in range(2))
    ms = a.qk(a.select(a.scores()))

    def both(first_pair, n_pair, carry):
        ls, extremes = carry
        ls = a.exp_pv(ms, first_pair, n_pair, ls)
        return ls, b.group(b.score_chunk, 2 * first_pair, 2 * n_pair, extremes)

    carry = (a.exp_init, b.score_init)
    carry = lax.fori_loop(0, a.n_pairs // 2, lambda i, c: both(2 * i, 2, c), carry)
    ls, extremes = lax.fori_loop(0, a.n_pairs % 2, lambda i, c: both(a.n_pairs - 1, 1, c), carry)
    a.finish(ls)

    ms = b.qk(b.select(extremes))
    ls = b.exp_init
    ls = lax.fori_loop(0, b.n_pairs // 2, lambda i, c: b.exp_pv(ms, 2 * i, 2, c), ls)
    ls = lax.fori_loop(0, b.n_pairs % 2, lambda i, c: b.exp_pv(ms, b.n_pairs - 1, 1, c), ls)
    b.finish(ls)


def _dsa(proj_r, proj_t, w_t):
    B, T, _ = proj_r.shape
    k_sel = min(TOPK_MAX, T // 4)
    slopes = _static_alibi_slopes(N_HEADS_B)
    assert T // POS_SPLIT <= 256 and POS_SPLIT <= 256 and B % 2 == 0
    return pl.pallas_call(
        functools.partial(_dsa_kernel, seq=T, k_sel=k_sel, snap_unroll=4, slopes=slopes),
        grid=(B // 2, T // DSA_Q),
        in_specs=[
            pl.BlockSpec((2, T, 2 * LANES), lambda b, q: (b, 0, R_KB // (2 * LANES))),
            pl.BlockSpec((2, T_VB, DSA_Q), lambda b, q: (b, 0, q)),
            pl.BlockSpec((2, HEAD_DIM, T), lambda b, q: (b, T_VB // HEAD_DIM, 0)),
            pl.BlockSpec((2, WI_ROWS, DSA_Q), lambda b, q: (b, 0, q)),
        ],
        out_specs=pl.BlockSpec((2, DSA_Q, W_B), lambda b, q: (b, q, 0)),
        out_shape=jax.ShapeDtypeStruct((B, T, W_B), F32),
        scratch_shapes=[pltpu.VMEM((2, T, DSA_Q), F32),
                        pltpu.VMEM((2, N_HEADS_B, T, DSA_Q), F32),
                        pltpu.VMEM((2, W_B, DSA_Q), F32),
                        pltpu.VMEM((2, T, LANES), BF16)],
        compiler_params=pltpu.CompilerParams(
            dimension_semantics=("parallel", "arbitrary"), vmem_limit_bytes=VMEM_LIMIT),
        name="dsa",
    )(proj_r, proj_t, proj_t, w_t)


def _mem_kv_kernel(mem_ref, g_ref, w_ref, kv_ref):
    kv_ref[0] = _dot(_rms(mem_ref[0], g_ref[...]).astype(BF16), w_ref[...]).astype(BF16)


def _mem_kv(mem, g, w):
    B, M, D = mem.shape
    return pl.pallas_call(
        _mem_kv_kernel,
        grid=(B,),
        in_specs=[
            pl.BlockSpec((1, M, D), lambda b: (b, 0, 0)),
            pl.BlockSpec((1, D), lambda b: (0, 0)),
            pl.BlockSpec(w.shape, lambda b: (0, 0)),
        ],
        out_specs=pl.BlockSpec((1, M, 2 * W_M), lambda b: (b, 0, 0)),
        out_shape=jax.ShapeDtypeStruct((B, M, 2 * W_M), BF16),
        compiler_params=pltpu.CompilerParams(
            dimension_semantics=("parallel",), vmem_limit_bytes=VMEM_LIMIT),
        name="mem_kv",
    )(mem, g, w)


def _out_kernel(x_ref, oa_ref, ob_ref, gate_ref, qm_ref, kv_ref, wo_ref, g_ref, y_ref, mix_ref, *, sub):
    tm = x_ref.shape[0]
    subs = [slice(r0, r0 + sub) for r0 in range(0, tm, sub)]
    lane = lax.broadcasted_iota(I32, (1, W_M), 1)
    head_lanes = [(lane >= h * HEAD_DIM) & (lane < (h + 1) * HEAD_DIM) for h in range(N_HEADS_MEM)]
    km = kv_ref[:, 0:W_M]
    vm = kv_ref[:, W_M:2 * W_M]
    zero = jnp.zeros((), BF16)
    vm_heads = [jnp.where(head_lanes[h], vm, zero) for h in range(N_HEADS_MEM)]

    def silu(g):
        return g * jax.nn.sigmoid(g)

    def gated(o, rows, lo, width):
        return (o * silu(gate_ref[rows, lo:lo + width].astype(F32))).astype(BF16)

    for rows in subs:
        qm = qm_ref[rows, :]
        om = None
        for h in range(N_HEADS_MEM):
            s = _dot_nt(jnp.where(head_lanes[h], qm, zero), km)
            p = jnp.exp2(s - jnp.max(s, axis=1, keepdims=True))
            p = p / jnp.sum(p, axis=1, keepdims=True)
            o = _dot(p.astype(BF16), vm_heads[h])
            om = o if om is None else om + o
        mix_ref[rows, W_A + W_B:] = gated(om, rows, W_A + W_B, W_M)

    for rows in subs:
        half = W_A // 2
        mix_ref[rows, 0:half] = gated(oa_ref[rows, 0:half], rows, 0, half)
        mix_ref[rows, half:W_A] = gated(oa_ref[rows, half:W_A], rows, half, half)
        mix_ref[rows, W_A:W_A + W_B] = gated(ob_ref[rows, :], rows, W_A, W_B)
        y_ref[rows, :] = _dot(mix_ref[rows, :], wo_ref[...])

    for rows in subs:
        y_ref[rows, :] = _rms(x_ref[rows, :] + y_ref[rows, :], g_ref[...])


def _out(x, o_a, o_b, proj_r, kv_m, w_out, g, tm=1024, sub=256):
    B, T, D = x.shape
    M = kv_m.shape[1]
    row = lambda b, i: (b, i, 0)
    return pl.pallas_call(
        functools.partial(_out_kernel, sub=sub),
        grid=(B, T // tm),
        in_specs=[
            pl.BlockSpec((None, tm, D), row),
            pl.BlockSpec((None, tm, W_A), row),
            pl.BlockSpec((None, tm, W_B), row),
            pl.BlockSpec((None, tm, MIX_WIDTH), lambda b, i: (b, i, R_GATE // MIX_WIDTH)),
            pl.BlockSpec((None, tm, W_M), lambda b, i: (b, i, R_QM // W_M)),
            pl.BlockSpec((None, M, 2 * W_M), lambda b, i: (b, 0, 0)),
            pl.BlockSpec(w_out.shape, lambda b, i: (0, 0)),
            pl.BlockSpec((1, D), lambda b, i: (0, 0)),
        ],
        out_specs=pl.BlockSpec((None, tm, D), row),
        out_shape=jax.ShapeDtypeStruct((B, T, D), F32),
        scratch_shapes=[pltpu.VMEM((tm, MIX_WIDTH), BF16)],
        compiler_params=pltpu.CompilerParams(
            dimension_semantics=("parallel", "parallel"), vmem_limit_bytes=VMEM_LIMIT),
        name="out",
    )(x, o_a, o_b, proj_r, proj_r, kv_m, w_out, g)


def _alibi_slopes(n):
    return 2.0 ** (-8.0 * jnp.arange(1, n + 1, dtype=F32) / n)


def _static_alibi_slopes(n):
    return tuple(2.0 ** (-8.0 * i / n) for i in range(1, n + 1))


def _split_weights(w):
    bounds = [0]
    for s in SPLIT_SIZES:
        bounds.append(bounds[-1] + s)
    q_a, k_a, v_a, q_b, k_b, v_b, q_m, gate, q_i, k_i, w_i = (
        w[:, bounds[i]:bounds[i + 1]] for i in range(len(SPLIT_SIZES)))
    scale = HEAD_DIM ** -0.5 * LOG2E
    idx_scale = IDX_DIM ** -0.5
    zeros = jnp.zeros((w.shape[0], LANES - HEAD_DIM), w.dtype)
    wa = jnp.concatenate([q_a * scale, k_a, v_a], axis=1)
    wr = jnp.concatenate([gate, q_m * scale, k_b, zeros, k_i, zeros], axis=1)
    wt = jnp.concatenate([q_i * idx_scale, q_b * scale, v_b], axis=1).T
    ww = jnp.concatenate([w_i.T, jnp.zeros((WI_ROWS - N_IDX_HEADS, w.shape[0]), w.dtype)], axis=0)
    return wa.astype(BF16), wr.astype(BF16), wt.astype(BF16), ww.astype(BF16)


def kernel(x, mem, g_in, g_mem, w_in, w_mem_kv, w_out, g_final):
    assert g_in.shape[0] == 1, "single-layer block: the final RMSNorm is fused into the output kernel"
    wa, wr, wt, ww = _split_weights(w_in[0])
    qkv_a, proj_r, proj_t, w_t = _proj(x, g_in, wa, wr, wt, ww)
    o_a = _attn_a(_alibi_slopes(N_HEADS_A), qkv_a)
    o_b = _dsa(proj_r, proj_t, w_t)
    kv_m = _mem_kv(mem, g_mem, w_mem_kv[0].astype(BF16))
    return _out(x, o_a, o_b, proj_r, kv_m, w_out[0].astype(BF16), g_final[None, :])
```

```python
import functools
import math
import types

import numpy as np

import jax
import jax.numpy as jnp
from jax import lax
from jax.experimental import pallas as pl
from jax.experimental.pallas import tpu as pltpu

F32 = jnp.float32
BF16 = jnp.bfloat16
I32 = jnp.int32

D_MODEL = 1024
HEAD_DIM = 64
N_HEADS_A = 8
N_HEADS_B = 4
N_HEADS_MEM = 4
W_A = N_HEADS_A * HEAD_DIM
W_B = N_HEADS_B * HEAD_DIM
W_M = N_HEADS_MEM * HEAD_DIM
MIX_WIDTH = W_A + W_B + W_M
DILATIONS = (1, 4, 16)
BAND = 128
N_IDX_HEADS = 8
IDX_DIM = 64
TOPK_MAX = 256
RMS_EPS = 1e-6
SPLIT_SIZES = (W_A, W_A, W_A, W_B, HEAD_DIM, HEAD_DIM, W_M, MIX_WIDTH,
               N_IDX_HEADS * IDX_DIM, IDX_DIM, N_IDX_HEADS)

LANES = 128
BLK = 128
NEG = -1e30
VMEM_LIMIT = 48 * 1024 * 1024

R_GATE = 0
R_QM = MIX_WIDTH
R_KB = R_QM + W_M
R_KI = R_KB + LANES
R_COLS = R_KI + LANES
T_QI = 0
T_QB = N_IDX_HEADS * IDX_DIM
T_VB = T_QB + W_B
T_ROWS = T_VB + HEAD_DIM
WI_ROWS = 16
DSA_Q = 256
DSA_HALVINGS = 12
POS_SPLIT = 64
LOG2E = math.log2(math.e)


def _dot(a, b):
    return jnp.dot(a, b, preferred_element_type=F32)


def _dot_nt(a, b):
    return lax.dot_general(a, b, (((1,), (1,)), ((), ())), preferred_element_type=F32)


def _rms(x, g):
    return x * lax.rsqrt(jnp.mean(x * x, axis=-1, keepdims=True) + RMS_EPS) * g


def _proj_kernel(x_ref, g_ref, wa_ref, wr_ref, wt_ref, ww_ref,
                 oa_ref, oa4_ref, oa16_ref, or_ref, ot_ref, ow_ref, res_ref, res4_ref):
    hb = _rms(x_ref[0], g_ref[...]).astype(BF16)
    tm = hb.shape[0]
    d4, d16 = DILATIONS[1], DILATIONS[2]
    step = d16 // d4
    res = _dot(hb, wa_ref[...])
    oa_ref[0] = res.astype(BF16)
    for grp in range(res.shape[1] // LANES):
        lanes = slice(grp * LANES, (grp + 1) * LANES)
        res_ref[grp] = res[:, lanes]
        for r in range(d4):
            rows4 = res_ref[grp, pl.ds(r, tm // d4, stride=d4), :]
            oa4_ref[0, r, :, lanes] = rows4.astype(BF16)
            res4_ref[grp, r * (tm // d4):(r + 1) * (tm // d4), :] = rows4
        for r in range(d16):
            start = (r % d4) * (tm // d4) + r // d4
            oa16_ref[0, r, :, lanes] = res4_ref[grp, pl.ds(start, tm // d16, stride=step), :].astype(BF16)
    or_ref[0] = _dot(hb, wr_ref[...]).astype(BF16)
    ot_ref[0] = _dot_nt(wt_ref[...], hb).astype(BF16)
    ow_ref[0] = _dot_nt(ww_ref[...], hb)


def _proj(x, g, wa, wr, wt, ww, tm=512):
    B, T, D = x.shape
    const = lambda b, i: (0, 0)
    d4, d16 = DILATIONS[1], DILATIONS[2]
    nat, cm4, cm16, proj_r, proj_t, w_t = pl.pallas_call(
        _proj_kernel,
        grid=(B, T // tm),
        in_specs=[
            pl.BlockSpec((1, tm, D), lambda b, i: (b, i, 0)),
            pl.BlockSpec((1, D), const),
            pl.BlockSpec(wa.shape, const),
            pl.BlockSpec(wr.shape, const),
            pl.BlockSpec(wt.shape, const),
            pl.BlockSpec(ww.shape, const),
        ],
        out_specs=[
            pl.BlockSpec((1, tm, 3 * W_A), lambda b, i: (b, i, 0)),
            pl.BlockSpec((1, d4, tm // d4, 3 * W_A), lambda b, i: (b, 0, i, 0)),
            pl.BlockSpec((1, d16, tm // d16, 3 * W_A), lambda b, i: (b, 0, i, 0)),
            pl.BlockSpec((1, tm, R_COLS), lambda b, i: (b, i, 0)),
            pl.BlockSpec((1, T_ROWS, tm), lambda b, i: (b, 0, i)),
            pl.BlockSpec((1, WI_ROWS, tm), lambda b, i: (b, 0, i)),
        ],
        out_shape=[
            jax.ShapeDtypeStruct((B, T, 3 * W_A), BF16),
            jax.ShapeDtypeStruct((B, d4, T // d4, 3 * W_A), BF16),
            jax.ShapeDtypeStruct((B, d16, T // d16, 3 * W_A), BF16),
            jax.ShapeDtypeStruct((B, T, R_COLS), BF16),
            jax.ShapeDtypeStruct((B, T_ROWS, T), BF16),
            jax.ShapeDtypeStruct((B, WI_ROWS, T), F32),
        ],
        scratch_shapes=[pltpu.VMEM((3 * W_A // LANES, tm, LANES), F32)] * 2,
        compiler_params=pltpu.CompilerParams(
            dimension_semantics=("parallel", "parallel"), vmem_limit_bytes=VMEM_LIMIT),
        name="proj",
    )(x, g, wa, wr, wt, ww)
    return (nat, cm4.reshape(nat.shape), cm16.reshape(nat.shape)), proj_r, proj_t, w_t


def _attn_a_kernel(slopes_ref, *refs, seq, unroll):
    n_pat = len(DILATIONS)
    qkv = [refs[3 * p:3 * p + 3] for p in range(n_pat)]
    o_ref, acc_ref, m_ref, bias_ref, p_ref = refs[3 * n_pat:]
    hp = pl.program_id(1)
    lane = lax.broadcasted_iota(I32, (1, LANES), 1)
    head_lanes = (lane < HEAD_DIM, lane >= HEAD_DIM)
    row = lax.broadcasted_iota(I32, (BLK, BLK), 0)
    col = lax.broadcasted_iota(I32, (BLK, BLK), 1)
    d_cur = (row - col).astype(F32)
    d_prev = d_cur + float(BAND)
    n_all = seq // BLK
    zero = jnp.zeros((), BF16)
    one = jnp.ones((), BF16)

    def n_blocks(p):
        return n_all // DILATIONS[p]

    def n_keys(p):
        return BLK if n_blocks(p) == 1 else 2 * BLK

    def block(g):
        return pl.ds(pl.multiple_of(g * BLK, BLK), BLK)

    def keys(p, g):
        return pl.ds(pl.multiple_of(jnp.maximum((g + 1) * BLK - n_keys(p), 0), BLK), n_keys(p))

    def variant(p, g):
        g = jnp.asarray(g, I32)
        return jnp.where(g == 0, 2, jnp.where(g % n_blocks(p) == 0, 1, 0))

    def pitch(dil):
        return seq // dil + 1 if seq // dil == BLK and dil > 1 else None

    def put(ref, h, p, g, val):
        dil = DILATIONS[p]
        start = g // n_blocks(p) + (g % n_blocks(p)) * (dil * BLK)
        if dil == 1:
            ref[h, p, pl.ds(pl.multiple_of(start, BLK), BLK), :] = val
        elif pitch(dil):
            ref[h, p, pl.ds(start * pitch(dil), BLK), :] = val
        else:
            ref[h, p, pl.ds(start, BLK, stride=dil), :] = val

    def probs(p):
        q_ref, k_ref, _ = qkv[p]

        def body(g):
            kw = k_ref[keys(p, g), :]
            q = q_ref[block(g), :]
            for h in range(2):
                s = _dot_nt(jnp.where(head_lanes[h], q, zero), kw)
                s = s + bias_ref[p, h, variant(p, g), :, 2 * BLK - n_keys(p):]
                m = jnp.max(s, axis=1, keepdims=True)
                p_ref[p % 2, h, g, :, :n_keys(p)] = jnp.exp2(s - m).astype(BF16)
                put(m_ref, h, p, g, jnp.broadcast_to(m, (BLK, LANES)))
        return body

    def values(p):
        v_ref = qkv[p][2]

        def body(g):
            vw = v_ref[keys(p, g), :]
            for h in range(2):
                put(acc_ref, h, p, g,
                    _dot(p_ref[p % 2, h, g, :, :n_keys(p)], jnp.where(head_lanes[h], vw, one)))
        return body

    def run(*bodies):
        def step(g, carry):
            for body in bodies:
                body(g)
            return carry
        lax.fori_loop(0, n_all, step, 0, unroll=unroll)

    for p, dil in enumerate(DILATIONS):
        for h in range(2):
            sd = slopes_ref[hp * 2 + h] * (float(dil) * LOG2E)
            cur = jnp.where(row >= col, -sd * d_cur, NEG)
            masked = jnp.full((BLK, BLK), NEG, F32)
            bias_ref[p, h, 0, :, BLK:] = cur
            bias_ref[p, h, 1, :, BLK:] = cur
            bias_ref[p, h, 2, :, BLK:] = cur if n_keys(p) == BLK else masked
            if n_keys(p) > BLK:
                bias_ref[p, h, 0, :, :BLK] = jnp.where(col >= row, -sd * d_prev, NEG)
                bias_ref[p, h, 1, :, :BLK] = masked
                bias_ref[p, h, 2, :, :BLK] = cur

    assert n_pat == 3
    run(probs(0))
    run(values(0), probs(1))
    run(values(1), probs(2))
    run(values(2))

    rows_per_step = 2 * BLK

    def natural_rows(ref, h, p, i):
        dil = DILATIONS[p]
        if not pitch(dil):
            return ref[h, p, pl.ds(pl.multiple_of(i * rows_per_step, rows_per_step), rows_per_step), :]
        per = rows_per_step // dil
        return jnp.concatenate(
            [ref[h, p, pl.ds(i * per + u, dil, stride=pitch(dil)), :] for u in range(per)], axis=0)

    def merge(i, carry):
        sl = pl.ds(pl.multiple_of(i * rows_per_step, rows_per_step), rows_per_step)
        nums = []
        for h in range(2):
            ms = [natural_rows(m_ref, h, p, i) for p in range(n_pat)]
            mx = functools.reduce(jnp.maximum, ms)
            nums.append(sum(jnp.exp2(ms[p] - mx) * natural_rows(acc_ref, h, p, i) for p in range(n_pat)))
        acc = jnp.where(head_lanes[0], nums[0], nums[1])
        den = pltpu.roll(jnp.where(head_lanes[0], nums[1], nums[0]), HEAD_DIM, axis=1)
        o_ref[sl, :] = acc / den
        return carry

    lax.fori_loop(0, seq // rows_per_step, merge, 0, unroll=4)


def _attn_a(slopes, qkv_layouts, unroll=16):
    B, T, _ = qkv_layouts[0].shape
    n_pairs = N_HEADS_A // 2
    n_pat = len(DILATIONS)

    def spec(off):
        return pl.BlockSpec((None, T, LANES), lambda b, h: (b, 0, off + h))

    return pl.pallas_call(
        functools.partial(_attn_a_kernel, seq=T, unroll=unroll),
        grid=(B, n_pairs),
        in_specs=[pl.BlockSpec(memory_space=pltpu.SMEM)]
        + [spec(part * n_pairs) for _ in range(n_pat) for part in range(3)],
        out_specs=pl.BlockSpec((None, T, LANES), lambda b, h: (b, 0, h)),
        out_shape=jax.ShapeDtypeStruct((B, T, W_A), F32),
        scratch_shapes=[pltpu.VMEM((2, n_pat, T + max(DILATIONS), LANES), F32),
                        pltpu.VMEM((2, n_pat, T + max(DILATIONS), LANES), F32),
                        pltpu.VMEM((n_pat, 2, 3, BLK, 2 * BLK), F32),
                        pltpu.VMEM((2, 2, T // BLK, BLK, 2 * BLK), BF16)],
        compiler_params=pltpu.CompilerParams(
            dimension_semantics=("parallel", "parallel"), vmem_limit_bytes=VMEM_LIMIT),
        name="attn_a",
    )(slopes, *[a for a in qkv_layouts for _ in range(3)])


def _bf16_pair(c):
    hi = float(np.asarray(c, np.float32).astype(BF16).astype(np.float32))
    lo = float(np.asarray(c - hi, np.float32).astype(BF16).astype(np.float32))
    return hi, lo


def _fold(x, op):
    return functools.reduce(op, [x[r:r + 8] for r in range(0, x.shape[0], 8)])


def _dsa_row(j, kk_ref, tq_ref, vt_ref, wt_ref, o_ref, sc_ref, s_ref, acc_ref, kpos_ref,
             *, seq, k_sel, snap_unroll, slopes):
    n_pairs = j + 1
    t_lane = j * DSA_Q + lax.broadcasted_iota(I32, (1, DSA_Q), 1)
    row = lax.broadcasted_iota(I32, (BLK, DSA_Q), 0)
    ws = wt_ref[0:N_IDX_HEADS, :] * (N_IDX_HEADS ** -0.5)
    inf = float("inf")
    few = t_lane < k_sel

    def rows(c):
        return pl.ds(pl.multiple_of(c * BLK, BLK), BLK)

    def group(fn, first, n, carry):
        for u in range(n):
            carry = fn(first + u, carry)
        return carry

    def each_chunk(fn, init, widest=4):
        carry, done, width = init, 0, widest
        while width >= 2:
            steps = (2 * n_pairs - done) // width
            carry = lax.fori_loop(
                0, steps, lambda i, c, done=done, width=width: group(fn, done + width * i, width, c), carry)
            done, width = done + steps * width, width // 2
        return carry

    def sub_reduce(x, op):
        return op(x, axis=0, keepdims=True)

    def score_chunk(c, carry):
        mn, mx = carry
        ki = kk_ref[rows(c), R_KI - R_KB:R_KI - R_KB + IDX_DIM]
        lgs = [_dot(ki, tq_ref[T_QI + h * IDX_DIM:T_QI + (h + 1) * IDX_DIM, :])
               for h in range(N_IDX_HEADS)]
        sc = functools.reduce(
            lambda a, b: a + b, [jnp.maximum(lg, 0.0) * ws[h:h + 1, :] for h, lg in enumerate(lgs)])
        causal = (c * BLK + row) <= t_lane
        sc_ref[rows(c), :] = jnp.where(causal, sc, -inf)
        mn = jnp.minimum(mn, _fold(jnp.where(causal, sc, inf), jnp.minimum))
        mx = jnp.maximum(mx, _fold(jnp.where(causal, sc, -inf), jnp.maximum))
        return mn, mx

    score_init = (jnp.full((8, DSA_Q), inf, F32), jnp.full((8, DSA_Q), -inf, F32))

    def scores():
        return each_chunk(score_chunk, score_init, widest=8)

    def count(pred):
        def body(c, cnt):
            return cnt + _fold(pred(sc_ref[rows(c), :], c).astype(I32), jnp.add)
        return sub_reduce(each_chunk(body, jnp.zeros((8, DSA_Q), I32)), jnp.sum)

    def halve(_, bounds):
        lo, hi, n_lo, n_hi = bounds
        mid = 0.5 * lo + 0.5 * hi
        cnt = count(lambda x, c: x >= mid)
        active = lo < hi
        up = active & (cnt >= k_sel)
        down = active & (cnt <= k_sel)
        return (jnp.where(up, mid, lo), jnp.where(down, mid, hi),
                jnp.where(up, cnt, n_lo), jnp.where(down, cnt, n_hi))

    def snap(bounds):
        lo, hi, n_lo, n_hi = bounds
        mid = 0.5 * lo + 0.5 * hi
        mid = jnp.where(mid > lo, mid, hi)

        def body(c, carry):
            cnt, above, below = carry
            x = sc_ref[rows(c), :]
            ge = x >= mid
            return (cnt + _fold(ge.astype(I32), jnp.add),
                    jnp.minimum(above, _fold(jnp.where(ge, x, inf), jnp.minimum)),
                    jnp.maximum(below, _fold(jnp.where(ge, -inf, x), jnp.maximum)))

        cnt, above, below = each_chunk(body, (jnp.zeros((8, DSA_Q), I32),
                                              jnp.full((8, DSA_Q), inf, F32),
                                              jnp.full((8, DSA_Q), -inf, F32)))
        cnt = sub_reduce(cnt, jnp.sum)
        above = sub_reduce(above, jnp.min)
        below = sub_reduce(below, jnp.max)
        active = lo < hi
        enough = cnt >= k_sel
        up = active & enough
        down = active & (cnt <= k_sel)
        return (jnp.where(up, above, lo), jnp.where(down, jnp.where(enough, above, below), hi),
                jnp.where(up, cnt, n_lo), jnp.where(down, cnt, n_hi))

    def snaps(bounds):
        for _ in range(snap_unroll):
            bounds = snap(bounds)
        return bounds

    def unsettled(bounds):
        return jnp.max((bounds[0] < bounds[1]).astype(F32)) > 0

    n_idx_bits = seq.bit_length() - 1
    assert 1 << n_idx_bits == seq

    def select(extremes):
        mn, mx = extremes
        lo = jnp.where(few, -inf, sub_reduce(mn, jnp.min))
        hi = jnp.where(few, -inf, sub_reduce(mx, jnp.max))
        bounds = lax.fori_loop(0, DSA_HALVINGS, halve, (lo, hi, t_lane + 1, jnp.zeros((1, DSA_Q), I32)))

        def state(bounds):
            lo, hi, n_lo, _ = bounds
            tied = (n_lo > k_sel) & jnp.logical_not(few)
            return jnp.max(jnp.where(lo < hi, 2.0, tied.astype(F32)))

        def snap_more(bounds):
            bounds = lax.while_loop(unsettled, snaps, bounds)
            return bounds + (state(bounds),)

        bounds = snaps(bounds)
        code = state(bounds)
        tau, _, n_ge, n_gt, code = lax.cond(code >= 2, lambda: snap_more(bounds), lambda: bounds + (code,))
        need = k_sel - n_gt
        tied = (n_ge > k_sel) & jnp.logical_not(few)

        def tie_break():
            def idx_step(it, jp):
                cand = jp + lax.shift_left(jnp.int32(1), n_idx_bits - 1 - it)
                below = count(lambda x, c: (x == tau) & ((c * BLK + row) < cand))
                return jnp.where(below < need, cand, jp)
            last = lax.fori_loop(0, n_idx_bits, idx_step, jnp.zeros((1, DSA_Q), I32))
            return jnp.where(tied, last, seq)

        surplus = code == 1
        tie_last = lax.cond(surplus, tie_break, lambda: jnp.full((1, DSA_Q), seq, I32))
        return tau, tie_last, surplus

    def qk(selection):
        tau, tie_last, surplus = selection

        @pl.when(j == 0)
        def _():
            pos = lax.broadcasted_iota(I32, (seq, LANES), 0)
            ln = lax.broadcasted_iota(I32, (seq, LANES), 1) - HEAD_DIM
            feat = jnp.where(ln == 0, pos // POS_SPLIT,
                             jnp.where(ln == 1, pos % POS_SPLIT,
                                       jnp.where(ln == 2, pos // POS_SPLIT,
                                                 jnp.where(ln == 3, pos % POS_SPLIT, 0))))
            kpos_ref[...] = kk_ref[:, 0:LANES] + feat.astype(BF16)

        q_row = lax.broadcasted_iota(I32, (HEAD_DIM, DSA_Q), 0)
        q_aug = []
        for h in range(N_HEADS_B):
            c_hi, c_lo = _bf16_pair(slopes[h] * LOG2E)
            feat = jnp.where(q_row == 0, POS_SPLIT * c_hi,
                             jnp.where(q_row == 1, c_hi,
                                       jnp.where(q_row == 2, POS_SPLIT * c_lo,
                                                 jnp.where(q_row == 3, c_lo, 0.0))))
            q_aug.append(jnp.concatenate(
                [tq_ref[T_QB + h * HEAD_DIM:T_QB + (h + 1) * HEAD_DIM, :], feat.astype(BF16)], axis=0))

        tau_all = jnp.where(few, float(jnp.finfo(F32).min), tau)

        def qk_phase(with_ties):
            def qk_chunk(c, ms):
                kb = kpos_ref[rows(c), :]
                x = sc_ref[rows(c), :]
                if with_ties:
                    s_idx = c * BLK + row
                    tie = jnp.where(x == tau, jnp.where(s_idx <= tie_last, 0.0, NEG), NEG)
                    sb = jnp.where(s_idx <= t_lane, jnp.where(x > tau, 0.0, tie), NEG)
                else:
                    sb = jnp.where(x >= tau_all, 0.0, NEG)
                out = []
                for h in range(N_HEADS_B):
                    s = _dot(kb, q_aug[h]) + sb
                    s_ref[h, rows(c), :] = s
                    out.append(jnp.maximum(ms[h], _fold(s, jnp.maximum)))
                return tuple(out)

            return each_chunk(qk_chunk, tuple(jnp.full((8, DSA_Q), NEG, F32) for _ in range(N_HEADS_B)),
                              widest=8)

        ms = lax.cond(surplus, lambda: qk_phase(True), lambda: qk_phase(False))
        acc_ref[...] = jnp.zeros((W_B, DSA_Q), F32)
        return tuple(sub_reduce(m, jnp.max) for m in ms)

    def exp_pv(ms, first_pair, n, ls):
        ls = list(ls)
        pvs = []
        for u in range(n):
            pair = first_pair + u
            ps = []
            for h in range(N_HEADS_B):
                halves = []
                for c in (2 * pair, 2 * pair + 1):
                    p = jnp.exp2(s_ref[h, rows(c), :] - ms[h])
                    ls[h] = ls[h] + _fold(p, jnp.add)
                    halves.append(p.astype(BF16))
                ps.append(jnp.concatenate(halves, axis=0))
            vt = vt_ref[:, pl.ds(pl.multiple_of(pair * 2 * BLK, 2 * BLK), 2 * BLK)]
            pvs.append([_dot(vt, ps[h]) for h in range(N_HEADS_B)])
        for h in range(N_HEADS_B):
            acc_ref[h * HEAD_DIM:(h + 1) * HEAD_DIM, :] += functools.reduce(
                lambda a, b: a + b, [pv[h] for pv in pvs])
        return tuple(ls)

    exp_init = tuple(jnp.zeros((8, DSA_Q), F32) for _ in range(N_HEADS_B))

    def finish(ls):
        l_all = jnp.concatenate(
            [jnp.broadcast_to(sub_reduce(ls[h], jnp.sum), (HEAD_DIM, DSA_Q)) for h in range(N_HEADS_B)],
            axis=0)
        o_ref[...] = (acc_ref[...] / l_all).T

    return types.SimpleNamespace(scores=scores, select=select, qk=qk, exp_pv=exp_pv, finish=finish,
                                 score_chunk=score_chunk, score_init=score_init, exp_init=exp_init,
                                 group=group, n_pairs=n_pairs)


def _dsa_kernel(kk_ref, tq_ref, vt_ref, wt_ref, o_ref, sc_ref, s_ref, acc_ref, kpos_ref, **static):
    j = pl.program_id(1)
    a, b = (_dsa_row(j, *(ref.at[i] for ref in (kk_ref, tq_ref, vt_ref, wt_ref, o_ref, sc_ref, s_ref,
                                                 acc_ref, kpos_ref)), **static) for i in range(2))
    ms = a.qk(a.select(a.scores()))

    def both(first_pair, n_pair, carry):
        ls, extremes = carry
        ls = a.exp_pv(ms, first_pair, n_pair, ls)
        return ls, b.group(b.score_chunk, 2 * first_pair, 2 * n_pair, extremes)

    carry = (a.exp_init, b.score_init)
    carry = lax.fori_loop(0, a.n_pairs // 2, lambda i, c: both(2 * i, 2, c), carry)
    ls, extremes = lax.fori_loop(0, a.n_pairs % 2, lambda i, c: both(a.n_pairs - 1, 1, c), carry)
    a.finish(ls)

    ms = b.qk(b.select(extremes))
    ls = b.exp_init
    ls = lax.fori_loop(0, b.n_pairs // 2, lambda i, c: b.exp_pv(ms, 2 * i, 2, c), ls)
    ls = lax.fori_loop(0, b.n_pairs % 2, lambda i, c: b.exp_pv(ms, b.n_pairs - 1, 1, c), ls)
    b.finish(ls)


def _dsa(proj_r, proj_t, w_t):
    B, T, _ = proj_r.shape
    k_sel = min(TOPK_MAX, T // 4)
    slopes = _static_alibi_slopes(N_HEADS_B)
    assert T // POS_SPLIT <= 256 and POS_SPLIT <= 256 and B % 2 == 0
    return pl.pallas_call(
        functools.partial(_dsa_kernel, seq=T, k_sel=k_sel, snap_unroll=4, slopes=slopes),
        grid=(B // 2, T // DSA_Q),
        in_specs=[
            pl.BlockSpec((2, T, 2 * LANES), lambda b, q: (b, 0, R_KB // (2 * LANES))),
            pl.BlockSpec((2, T_VB, DSA_Q), lambda b, q: (b, 0, q)),
            pl.BlockSpec((2, HEAD_DIM, T), lambda b, q: (b, T_VB // HEAD_DIM, 0)),
            pl.BlockSpec((2, WI_ROWS, DSA_Q), lambda b, q: (b, 0, q)),
        ],
        out_specs=pl.BlockSpec((2, DSA_Q, W_B), lambda b, q: (b, q, 0)),
        out_shape=jax.ShapeDtypeStruct((B, T, W_B), F32),
        scratch_shapes=[pltpu.VMEM((2, T, DSA_Q), F32),
                        pltpu.VMEM((2, N_HEADS_B, T, DSA_Q), F32),
                        pltpu.VMEM((2, W_B, DSA_Q), F32),
                        pltpu.VMEM((2, T, LANES), BF16)],
        compiler_params=pltpu.CompilerParams(
            dimension_semantics=("parallel", "arbitrary"), vmem_limit_bytes=VMEM_LIMIT),
        name="dsa",
    )(proj_r, proj_t, proj_t, w_t)


def _mem_kv_kernel(mem_ref, g_ref, w_ref, kv_ref):
    kv_ref[0] = _dot(_rms(mem_ref[0], g_ref[...]).astype(BF16), w_ref[...]).astype(BF16)


def _mem_kv(mem, g, w):
    B, M, D = mem.shape
    return pl.pallas_call(
        _mem_kv_kernel,
        grid=(B,),
        in_specs=[
            pl.BlockSpec((1, M, D), lambda b: (b, 0, 0)),
            pl.BlockSpec((1, D), lambda b: (0, 0)),
            pl.BlockSpec(w.shape, lambda b: (0, 0)),
        ],
        out_specs=pl.BlockSpec((1, M, 2 * W_M), lambda b: (b, 0, 0)),
        out_shape=jax.ShapeDtypeStruct((B, M, 2 * W_M), BF16),
        compiler_params=pltpu.CompilerParams(
            dimension_semantics=("parallel",), vmem_limit_bytes=VMEM_LIMIT),
        name="mem_kv",
    )(mem, g, w)


def _out_kernel(x_ref, oa_ref, ob_ref, gate_ref, qm_ref, kv_ref, wo_ref, g_ref, y_ref, mix_ref, *, sub):
    tm = x_ref.shape[0]
    subs = [slice(r0, r0 + sub) for r0 in range(0, tm, sub)]
    lane = lax.broadcasted_iota(I32, (1, W_M), 1)
    head_lanes = [(lane >= h * HEAD_DIM) & (lane < (h + 1) * HEAD_DIM) for h in range(N_HEADS_MEM)]
    km = kv_ref[:, 0:W_M]
    vm = kv_ref[:, W_M:2 * W_M]
    zero = jnp.zeros((), BF16)
    vm_heads = [jnp.where(head_lanes[h], vm, zero) for h in range(N_HEADS_MEM)]

    def silu(g):
        return g * jax.nn.sigmoid(g)

    def gated(o, rows, lo, width):
        return (o * silu(gate_ref[rows, lo:lo + width].astype(F32))).astype(BF16)

    for rows in subs:
        qm = qm_ref[rows, :]
        om = None
        for h in range(N_HEADS_MEM):
            s = _dot_nt(jnp.where(head_lanes[h], qm, zero), km)
            p = jnp.exp2(s - jnp.max(s, axis=1, keepdims=True))
            p = p / jnp.sum(p, axis=1, keepdims=True)
            o = _dot(p.astype(BF16), vm_heads[h])
            om = o if om is None else om + o
        mix_ref[rows, W_A + W_B:] = gated(om, rows, W_A + W_B, W_M)

    for rows in subs:
        half = W_A // 2
        mix_ref[rows, 0:half] = gated(oa_ref[rows, 0:half], rows, 0, half)
        mix_ref[rows, half:W_A] = gated(oa_ref[rows, half:W_A], rows, half, half)
        mix_ref[rows, W_A:W_A + W_B] = gated(ob_ref[rows, :], rows, W_A, W_B)
        y_ref[rows, :] = _dot(mix_ref[rows, :], wo_ref[...])

    for rows in subs:
        y_ref[rows, :] = _rms(x_ref[rows, :] + y_ref[rows, :], g_ref[...])


def _out(x, o_a, o_b, proj_r, kv_m, w_out, g, tm=1024, sub=256):
    B, T, D = x.shape
    M = kv_m.shape[1]
    row = lambda b, i: (b, i, 0)
    return pl.pallas_call(
        functools.partial(_out_kernel, sub=sub),
        grid=(B, T // tm),
        in_specs=[
            pl.BlockSpec((None, tm, D), row),
            pl.BlockSpec((None, tm, W_A), row),
            pl.BlockSpec((None, tm, W_B), row),
            pl.BlockSpec((None, tm, MIX_WIDTH), lambda b, i: (b, i, R_GATE // MIX_WIDTH)),
            pl.BlockSpec((None, tm, W_M), lambda b, i: (b, i, R_QM // W_M)),
            pl.BlockSpec((None, M, 2 * W_M), lambda b, i: (b, 0, 0)),
            pl.BlockSpec(w_out.shape, lambda b, i: (0, 0)),
            pl.BlockSpec((1, D), lambda b, i: (0, 0)),
        ],
        out_specs=pl.BlockSpec((None, tm, D), row),
        out_shape=jax.ShapeDtypeStruct((B, T, D), F32),
        scratch_shapes=[pltpu.VMEM((tm, MIX_WIDTH), BF16)],
        compiler_params=pltpu.CompilerParams(
            dimension_semantics=("parallel", "parallel"), vmem_limit_bytes=VMEM_LIMIT),
        name="out",
    )(x, o_a, o_b, proj_r, proj_r, kv_m, w_out, g)


def _alibi_slopes(n):
    return 2.0 ** (-8.0 * jnp.arange(1, n + 1, dtype=F32) / n)


def _static_alibi_slopes(n):
    return tuple(2.0 ** (-8.0 * i / n) for i in range(1, n + 1))


def _split_weights(w):
    bounds = [0]
    for s in SPLIT_SIZES:
        bounds.append(bounds[-1] + s)
    q_a, k_a, v_a, q_b, k_b, v_b, q_m, gate, q_i, k_i, w_i = (
        w[:, bounds[i]:bounds[i + 1]] for i in range(len(SPLIT_SIZES)))
    scale = HEAD_DIM ** -0.5 * LOG2E
    idx_scale = IDX_DIM ** -0.5
    zeros = jnp.zeros((w.shape[0], LANES - HEAD_DIM), w.dtype)
    wa = jnp.concatenate([q_a * scale, k_a, v_a], axis=1)
    wr = jnp.concatenate([gate, q_m * scale, k_b, zeros, k_i, zeros], axis=1)
    wt = jnp.concatenate([q_i * idx_scale, q_b * scale, v_b], axis=1).T
    ww = jnp.concatenate([w_i.T, jnp.zeros((WI_ROWS - N_IDX_HEADS, w.shape[0]), w.dtype)], axis=0)
    return wa.astype(BF16), wr.astype(BF16), wt.astype(BF16), ww.astype(BF16)


def kernel(x, mem, g_in, g_mem, w_in, w_mem_kv, w_out, g_final):
    assert g_in.shape[0] == 1, "single-layer block: the final RMSNorm is fused into the output kernel"
    wa, wr, wt, ww = _split_weights(w_in[0])
    qkv_a, proj_r, proj_t, w_t = _proj(x, g_in, wa, wr, wt, ww)
    o_a = _attn_a(_alibi_slopes(N_HEADS_A), qkv_a)
    o_b = _dsa(proj_r, proj_t, w_t)
    kv_m = _mem_kv(mem, g_mem, w_mem_kv[0].astype(BF16))
    return _out(x, o_a, o_b, proj_r, kv_m, w_out[0].astype(BF16), g_final[None, :])
```

```python
import functools
import math
import types

import numpy as np

import jax
import jax.numpy as jnp
from jax import lax
from jax.experimental import pallas as pl
from jax.experimental.pallas import tpu as pltpu

F32 = jnp.float32
BF16 = jnp.bfloat16
I32 = jnp.int32

D_MODEL = 1024
HEAD_DIM = 64
N_HEADS_A = 8
N_HEADS_B = 4
N_HEADS_MEM = 4
W_A = N_HEADS_A * HEAD_DIM
W_B = N_HEADS_B * HEAD_DIM
W_M = N_HEADS_MEM * HEAD_DIM
MIX_WIDTH = W_A + W_B + W_M
DILATIONS = (1, 4, 16)
BAND = 128
N_IDX_HEADS = 8
IDX_DIM = 64
TOPK_MAX = 256
RMS_EPS = 1e-6
SPLIT_SIZES = (W_A, W_A, W_A, W_B, HEAD_DIM, HEAD_DIM, W_M, MIX_WIDTH,
               N_IDX_HEADS * IDX_DIM, IDX_DIM, N_IDX_HEADS)

LANES = 128
BLK = 128
NEG = -1e30
VMEM_LIMIT = 48 * 1024 * 1024

R_GATE = 0
R_QM = MIX_WIDTH
R_KB = R_QM + W_M
R_KI = R_KB + LANES
R_COLS = R_KI + LANES
T_QI = 0
T_QB = N_IDX_HEADS * IDX_DIM
T_VB = T_QB + W_B
T_ROWS = T_VB + HEAD_DIM
WI_ROWS = 16
DSA_Q = 256
DSA_COARSE = 10
DSA_HALVINGS = 3
POS_SPLIT = 64
LOG2E = math.log2(math.e)


def _dot(a, b):
    return jnp.dot(a, b, preferred_element_type=F32)


def _dot_nt(a, b):
    return lax.dot_general(a, b, (((1,), (1,)), ((), ())), preferred_element_type=F32)


def _rms(x, g):
    return x * lax.rsqrt(jnp.mean(x * x, axis=-1, keepdims=True) + RMS_EPS) * g


def _proj_kernel(x_ref, g_ref, wa_ref, wr_ref, wt_ref, ww_ref,
                 oa_ref, oa4_ref, oa16_ref, or_ref, ot_ref, ow_ref, res_ref, res4_ref):
    hb = _rms(x_ref[0], g_ref[...]).astype(BF16)
    tm = hb.shape[0]
    d4, d16 = DILATIONS[1], DILATIONS[2]
    step = d16 // d4
    res = _dot(hb, wa_ref[...])
    oa_ref[0] = res.astype(BF16)
    for grp in range(res.shape[1] // LANES):
        lanes = slice(grp * LANES, (grp + 1) * LANES)
        res_ref[grp] = res[:, lanes]
        for r in range(d4):
            rows4 = res_ref[grp, pl.ds(r, tm // d4, stride=d4), :]
            oa4_ref[0, r, :, lanes] = rows4.astype(BF16)
            res4_ref[grp, r * (tm // d4):(r + 1) * (tm // d4), :] = rows4
        for r in range(d16):
            start = (r % d4) * (tm // d4) + r // d4
            oa16_ref[0, r, :, lanes] = res4_ref[grp, pl.ds(start, tm // d16, stride=step), :].astype(BF16)
    or_ref[0] = _dot(hb, wr_ref[...]).astype(BF16)
    ot_ref[0] = _dot_nt(wt_ref[...], hb).astype(BF16)
    ow_ref[0] = _dot_nt(ww_ref[...], hb)


def _proj(x, g, wa, wr, wt, ww, tm=512):
    B, T, D = x.shape
    const = lambda b, i: (0, 0)
    d4, d16 = DILATIONS[1], DILATIONS[2]
    nat, cm4, cm16, proj_r, proj_t, w_t = pl.pallas_call(
        _proj_kernel,
        grid=(B, T // tm),
        in_specs=[
            pl.BlockSpec((1, tm, D), lambda b, i: (b, i, 0)),
            pl.BlockSpec((1, D), const),
            pl.BlockSpec(wa.shape, const),
            pl.BlockSpec(wr.shape, const),
            pl.BlockSpec(wt.shape, const),
            pl.BlockSpec(ww.shape, const),
        ],
        out_specs=[
            pl.BlockSpec((1, tm, 3 * W_A), lambda b, i: (b, i, 0)),
            pl.BlockSpec((1, d4, tm // d4, 3 * W_A), lambda b, i: (b, 0, i, 0)),
            pl.BlockSpec((1, d16, tm // d16, 3 * W_A), lambda b, i: (b, 0, i, 0)),
            pl.BlockSpec((1, tm, R_COLS), lambda b, i: (b, i, 0)),
            pl.BlockSpec((1, T_ROWS, tm), lambda b, i: (b, 0, i)),
            pl.BlockSpec((1, WI_ROWS, tm), lambda b, i: (b, 0, i)),
        ],
        out_shape=[
            jax.ShapeDtypeStruct((B, T, 3 * W_A), BF16),
            jax.ShapeDtypeStruct((B, d4, T // d4, 3 * W_A), BF16),
            jax.ShapeDtypeStruct((B, d16, T // d16, 3 * W_A), BF16),
            jax.ShapeDtypeStruct((B, T, R_COLS), BF16),
            jax.ShapeDtypeStruct((B, T_ROWS, T), BF16),
            jax.ShapeDtypeStruct((B, WI_ROWS, T), F32),
        ],
        scratch_shapes=[pltpu.VMEM((3 * W_A // LANES, tm, LANES), F32)] * 2,
        compiler_params=pltpu.CompilerParams(
            dimension_semantics=("parallel", "parallel"), vmem_limit_bytes=VMEM_LIMIT),
        name="proj",
    )(x, g, wa, wr, wt, ww)
    return (nat, cm4.reshape(nat.shape), cm16.reshape(nat.shape)), proj_r, proj_t, w_t


def _attn_a_kernel(slopes_ref, *refs, seq, unroll):
    n_pat = len(DILATIONS)
    qkv = [refs[3 * p:3 * p + 3] for p in range(n_pat)]
    o_ref, acc_ref, m_ref, bias_ref, p_ref = refs[3 * n_pat:]
    hp = pl.program_id(1)
    lane = lax.broadcasted_iota(I32, (1, LANES), 1)
    head_lanes = (lane < HEAD_DIM, lane >= HEAD_DIM)
    row = lax.broadcasted_iota(I32, (BLK, BLK), 0)
    col = lax.broadcasted_iota(I32, (BLK, BLK), 1)
    d_cur = (row - col).astype(F32)
    d_prev = d_cur + float(BAND)
    n_all = seq // BLK
    zero = jnp.zeros((), BF16)
    one = jnp.ones((), BF16)

    def n_blocks(p):
        return n_all // DILATIONS[p]

    def n_keys(p):
        return BLK if n_blocks(p) == 1 else 2 * BLK

    def block(g):
        return pl.ds(pl.multiple_of(g * BLK, BLK), BLK)

    def keys(p, g):
        return pl.ds(pl.multiple_of(jnp.maximum((g + 1) * BLK - n_keys(p), 0), BLK), n_keys(p))

    def variant(p, g):
        g = jnp.asarray(g, I32)
        return jnp.where(g == 0, 2, jnp.where(g % n_blocks(p) == 0, 1, 0))

    def pitch(dil):
        return seq // dil + 1 if seq // dil == BLK and dil > 1 else None

    def put(ref, h, p, g, val):
        dil = DILATIONS[p]
        start = g // n_blocks(p) + (g % n_blocks(p)) * (dil * BLK)
        if dil == 1:
            ref[h, p, pl.ds(pl.multiple_of(start, BLK), BLK), :] = val
        elif pitch(dil):
            ref[h, p, pl.ds(start * pitch(dil), BLK), :] = val
        else:
            ref[h, p, pl.ds(start, BLK, stride=dil), :] = val

    def probs(p):
        q_ref, k_ref, _ = qkv[p]

        def body(g):
            kw = k_ref[keys(p, g), :]
            q = q_ref[block(g), :]
            for h in range(2):
                s = _dot_nt(jnp.where(head_lanes[h], q, zero), kw)
                s = s + bias_ref[p, h, variant(p, g), :, 2 * BLK - n_keys(p):]
                m = jnp.max(s, axis=1, keepdims=True)
                p_ref[p % 2, h, g, :, :n_keys(p)] = jnp.exp2(s - m).astype(BF16)
                put(m_ref, h, p, g, jnp.broadcast_to(m, (BLK, LANES)))
        return body

    def values(p):
        v_ref = qkv[p][2]

        def body(g):
            vw = v_ref[keys(p, g), :]
            for h in range(2):
                put(acc_ref, h, p, g,
                    _dot(p_ref[p % 2, h, g, :, :n_keys(p)], jnp.where(head_lanes[h], vw, one)))
        return body

    def run(*bodies):
        def step(g, carry):
            for body in bodies:
                body(g)
            return carry
        lax.fori_loop(0, n_all, step, 0, unroll=unroll)

    for p, dil in enumerate(DILATIONS):
        for h in range(2):
            sd = slopes_ref[hp * 2 + h] * (float(dil) * LOG2E)
            cur = jnp.where(row >= col, -sd * d_cur, NEG)
            masked = jnp.full((BLK, BLK), NEG, F32)
            bias_ref[p, h, 0, :, BLK:] = cur
            bias_ref[p, h, 1, :, BLK:] = cur
            bias_ref[p, h, 2, :, BLK:] = cur if n_keys(p) == BLK else masked
            if n_keys(p) > BLK:
                bias_ref[p, h, 0, :, :BLK] = jnp.where(col >= row, -sd * d_prev, NEG)
                bias_ref[p, h, 1, :, :BLK] = masked
                bias_ref[p, h, 2, :, :BLK] = cur

    assert n_pat == 3
    run(probs(0))
    run(values(0), probs(1))
    run(values(1), probs(2))
    run(values(2))

    rows_per_step = 2 * BLK

    def natural_rows(ref, h, p, i):
        dil = DILATIONS[p]
        if not pitch(dil):
            return ref[h, p, pl.ds(pl.multiple_of(i * rows_per_step, rows_per_step), rows_per_step), :]
        per = rows_per_step // dil
        return jnp.concatenate(
            [ref[h, p, pl.ds(i * per + u, dil, stride=pitch(dil)), :] for u in range(per)], axis=0)

    def merge(i, carry):
        sl = pl.ds(pl.multiple_of(i * rows_per_step, rows_per_step), rows_per_step)
        nums = []
        for h in range(2):
            ms = [natural_rows(m_ref, h, p, i) for p in range(n_pat)]
            mx = functools.reduce(jnp.maximum, ms)
            nums.append(sum(jnp.exp2(ms[p] - mx) * natural_rows(acc_ref, h, p, i) for p in range(n_pat)))
        acc = jnp.where(head_lanes[0], nums[0], nums[1])
        den = pltpu.roll(jnp.where(head_lanes[0], nums[1], nums[0]), HEAD_DIM, axis=1)
        o_ref[sl, :] = acc / den
        return carry

    lax.fori_loop(0, seq // rows_per_step, merge, 0, unroll=4)


def _attn_a(slopes, qkv_layouts, unroll=16):
    B, T, _ = qkv_layouts[0].shape
    n_pairs = N_HEADS_A // 2
    n_pat = len(DILATIONS)

    def spec(off):
        return pl.BlockSpec((None, T, LANES), lambda b, h: (b, 0, off + h))

    return pl.pallas_call(
        functools.partial(_attn_a_kernel, seq=T, unroll=unroll),
        grid=(B, n_pairs),
        in_specs=[pl.BlockSpec(memory_space=pltpu.SMEM)]
        + [spec(part * n_pairs) for _ in range(n_pat) for part in range(3)],
        out_specs=pl.BlockSpec((None, T, LANES), lambda b, h: (b, 0, h)),
        out_shape=jax.ShapeDtypeStruct((B, T, W_A), F32),
        scratch_shapes=[pltpu.VMEM((2, n_pat, T + max(DILATIONS), LANES), F32),
                        pltpu.VMEM((2, n_pat, T + max(DILATIONS), LANES), F32),
                        pltpu.VMEM((n_pat, 2, 3, BLK, 2 * BLK), F32),
                        pltpu.VMEM((2, 2, T // BLK, BLK, 2 * BLK), BF16)],
        compiler_params=pltpu.CompilerParams(
            dimension_semantics=("parallel", "parallel"), vmem_limit_bytes=VMEM_LIMIT),
        name="attn_a",
    )(slopes, *[a for a in qkv_layouts for _ in range(3)])


def _bf16_pair(c):
    hi = float(np.asarray(c, np.float32).astype(BF16).astype(np.float32))
    lo = float(np.asarray(c - hi, np.float32).astype(BF16).astype(np.float32))
    return hi, lo


def _fold(x, op):
    return functools.reduce(op, [x[r:r + 8] for r in range(0, x.shape[0], 8)])


def _dsa_row(j, kk_ref, tq_ref, vt_ref, wt_ref, o_ref, sc_ref, s_ref, acc_ref, kpos_ref, xb_ref,
             *, seq, k_sel, snap_unroll, slopes):
    n_pairs = j + 1
    t_lane = j * DSA_Q + lax.broadcasted_iota(I32, (1, DSA_Q), 1)
    row = lax.broadcasted_iota(I32, (BLK, DSA_Q), 0)
    ws = wt_ref[0:N_IDX_HEADS, :] * (N_IDX_HEADS ** -0.5)
    inf = float("inf")
    few = t_lane < k_sel

    def rows(c):
        return pl.ds(pl.multiple_of(c * BLK, BLK), BLK)

    def group(fn, first, n, carry):
        for u in range(n):
            carry = fn(first + u, carry)
        return carry

    def each_chunk(fn, init, widest=4):
        carry, done, width = init, 0, widest
        while width >= 2:
            steps = (2 * n_pairs - done) // width
            carry = lax.fori_loop(
                0, steps, lambda i, c, done=done, width=width: group(fn, done + width * i, width, c), carry)
            done, width = done + steps * width, width // 2
        return carry

    def sub_reduce(x, op):
        return op(x, axis=0, keepdims=True)

    def score_chunk(c, carry):
        mn, mx = carry
        ki = kk_ref[rows(c), R_KI - R_KB:R_KI - R_KB + IDX_DIM]
        lgs = [_dot(ki, tq_ref[T_QI + h * IDX_DIM:T_QI + (h + 1) * IDX_DIM, :])
               for h in range(N_IDX_HEADS)]
        sc = functools.reduce(
            lambda a, b: a + b, [jnp.maximum(lg, 0.0) * ws[h:h + 1, :] for h, lg in enumerate(lgs)])
        causal = (c * BLK + row) <= t_lane
        sc_ref[rows(c), :] = jnp.where(causal, sc, -inf)
        xb_ref[rows(c), :] = jnp.where(causal, sc, -inf).astype(BF16)
        mn = jnp.minimum(mn, _fold(jnp.where(causal, sc, inf), jnp.minimum))
        mx = jnp.maximum(mx, _fold(jnp.where(causal, sc, -inf), jnp.maximum))
        return mn, mx

    score_init = (jnp.full((8, DSA_Q), inf, F32), jnp.full((8, DSA_Q), -inf, F32))

    def scores():
        return each_chunk(score_chunk, score_init, widest=8)

    def count(pred):
        def body(c, cnt):
            return cnt + _fold(pred(sc_ref[rows(c), :], c).astype(I32), jnp.add)
        return sub_reduce(each_chunk(body, jnp.zeros((8, DSA_Q), I32)), jnp.sum)

    def halve(_, bounds):
        lo, hi, n_lo, n_hi = bounds
        mid = 0.5 * lo + 0.5 * hi
        cnt = count(lambda x, c: x >= mid)
        active = lo < hi
        up = active & (cnt >= k_sel)
        down = active & (cnt <= k_sel)
        return (jnp.where(up, mid, lo), jnp.where(down, mid, hi),
                jnp.where(up, cnt, n_lo), jnp.where(down, cnt, n_hi))

    def coarse(_, bounds):
        lo, hi = bounds
        mid = (0.5 * lo + 0.5 * hi).astype(BF16)

        def body(c, cnt):
            hit = jnp.where(xb_ref[rows(c), :] >= mid, jnp.ones((), BF16), jnp.zeros((), BF16))
            return cnt + functools.reduce(lambda a, b: a + b, [hit[r:r + 16] for r in range(0, BLK, 16)])

        cnt = each_chunk(body, jnp.zeros((16, DSA_Q), BF16))
        cnt = jnp.sum(cnt.astype(F32), axis=0, keepdims=True)
        mid = mid.astype(F32)
        active = lo < hi
        enough = cnt >= k_sel
        margin = jnp.abs(mid) * 2.0 ** -7 + 1e-30
        return (jnp.where(active & enough, jnp.maximum(lo, mid - margin), lo),
                jnp.where(active & jnp.logical_not(enough), jnp.minimum(hi, mid), hi))

    def snap(bounds):
        lo, hi, n_lo, n_hi = bounds
        mid = 0.5 * lo + 0.5 * hi
        mid = jnp.where(mid > lo, mid, hi)

        def body(c, carry):
            cnt, above, below = carry
            x = sc_ref[rows(c), :]
            ge = x >= mid
            return (cnt + _fold(ge.astype(I32), jnp.add),
                    jnp.minimum(above, _fold(jnp.where(ge, x, inf), jnp.minimum)),
                    jnp.maximum(below, _fold(jnp.where(ge, -inf, x), jnp.maximum)))

        cnt, above, below = each_chunk(body, (jnp.zeros((8, DSA_Q), I32),
                                              jnp.full((8, DSA_Q), inf, F32),
                                              jnp.full((8, DSA_Q), -inf, F32)))
        cnt = sub_reduce(cnt, jnp.sum)
        above = sub_reduce(above, jnp.min)
        below = sub_reduce(below, jnp.max)
        active = lo < hi
        enough = cnt >= k_sel
        up = active & enough
        down = active & (cnt <= k_sel)
        return (jnp.where(up, above, lo), jnp.where(down, jnp.where(enough, above, below), hi),
                jnp.where(up, cnt, n_lo), jnp.where(down, cnt, n_hi))

    def snaps(bounds):
        for _ in range(snap_unroll):
            bounds = snap(bounds)
        return bounds

    def unsettled(bounds):
        return jnp.max((bounds[0] < bounds[1]).astype(F32)) > 0

    n_idx_bits = seq.bit_length() - 1
    assert 1 << n_idx_bits == seq

    def select(extremes):
        mn, mx = extremes
        lo = jnp.where(few, -inf, sub_reduce(mn, jnp.min))
        hi = jnp.where(few, -inf, sub_reduce(mx, jnp.max))
        lo, hi = lax.fori_loop(0, DSA_COARSE, coarse, (lo, hi))

        def recount(c, carry):
            x = sc_ref[rows(c), :]
            return (carry[0] + _fold((x >= lo).astype(I32), jnp.add),
                    carry[1] + _fold((x > hi).astype(I32), jnp.add))

        n_lo, n_hi = each_chunk(recount, (jnp.zeros((8, DSA_Q), I32),) * 2)
        bounds = (lo, hi, sub_reduce(n_lo, jnp.sum), sub_reduce(n_hi, jnp.sum))
        bounds = lax.fori_loop(0, DSA_HALVINGS, halve, bounds)

        def state(bounds):
            lo, hi, n_lo, _ = bounds
            tied = (n_lo > k_sel) & jnp.logical_not(few)
            return jnp.max(jnp.where(lo < hi, 2.0, tied.astype(F32)))

        def snap_more(bounds):
            bounds = lax.while_loop(unsettled, snaps, bounds)
            return bounds + (state(bounds),)

        bounds = snaps(bounds)
        code = state(bounds)
        tau, _, n_ge, n_gt, code = lax.cond(code >= 2, lambda: snap_more(bounds), lambda: bounds + (code,))
        need = k_sel - n_gt
        tied = (n_ge > k_sel) & jnp.logical_not(few)

        def tie_break():
            def idx_step(it, jp):
                cand = jp + lax.shift_left(jnp.int32(1), n_idx_bits - 1 - it)
                below = count(lambda x, c: (x == tau) & ((c * BLK + row) < cand))
                return jnp.where(below < need, cand, jp)
            last = lax.fori_loop(0, n_idx_bits, idx_step, jnp.zeros((1, DSA_Q), I32))
            return jnp.where(tied, last, seq)

        surplus = code == 1
        tie_last = lax.cond(surplus, tie_break, lambda: jnp.full((1, DSA_Q), seq, I32))
        return tau, tie_last, surplus

    def qk(selection):
        tau, tie_last, surplus = selection

        @pl.when(j == 0)
        def _():
            pos = lax.broadcasted_iota(I32, (seq, LANES), 0)
            ln = lax.broadcasted_iota(I32, (seq, LANES), 1) - HEAD_DIM
            feat = jnp.where(ln == 0, pos // POS_SPLIT,
                             jnp.where(ln == 1, pos % POS_SPLIT,
                                       jnp.where(ln == 2, pos // POS_SPLIT,
                                                 jnp.where(ln == 3, pos % POS_SPLIT, 0))))
            kpos_ref[...] = kk_ref[:, 0:LANES] + feat.astype(BF16)

        q_row = lax.broadcasted_iota(I32, (HEAD_DIM, DSA_Q), 0)
        q_aug = []
        for h in range(N_HEADS_B):
            c_hi, c_lo = _bf16_pair(slopes[h] * LOG2E)
            feat = jnp.where(q_row == 0, POS_SPLIT * c_hi,
                             jnp.where(q_row == 1, c_hi,
                                       jnp.where(q_row == 2, POS_SPLIT * c_lo,
                                                 jnp.where(q_row == 3, c_lo, 0.0))))
            q_aug.append(jnp.concatenate(
                [tq_ref[T_QB + h * HEAD_DIM:T_QB + (h + 1) * HEAD_DIM, :], feat.astype(BF16)], axis=0))

        tau_all = jnp.where(few, float(jnp.finfo(F32).min), tau)

        def qk_phase(with_ties):
            def qk_chunk(c, ms):
                kb = kpos_ref[rows(c), :]
                x = sc_ref[rows(c), :]
                if with_ties:
                    s_idx = c * BLK + row
                    tie = jnp.where(x == tau, jnp.where(s_idx <= tie_last, 0.0, NEG), NEG)
                    sb = jnp.where(s_idx <= t_lane, jnp.where(x > tau, 0.0, tie), NEG)
                else:
                    sb = jnp.where(x >= tau_all, 0.0, NEG)
                out = []
                for h in range(N_HEADS_B):
                    s = _dot(kb, q_aug[h]) + sb
                    s_ref[h, rows(c), :] = s
                    out.append(jnp.maximum(ms[h], _fold(s, jnp.maximum)))
                return tuple(out)

            return each_chunk(qk_chunk, tuple(jnp.full((8, DSA_Q), NEG, F32) for _ in range(N_HEADS_B)),
                              widest=8)

        ms = lax.cond(surplus, lambda: qk_phase(True), lambda: qk_phase(False))
        acc_ref[...] = jnp.zeros((W_B, DSA_Q), F32)
        return tuple(sub_reduce(m, jnp.max) for m in ms)

    def exp_pv(ms, first_pair, n, ls):
        ls = list(ls)
        pvs = []
        for u in range(n):
            pair = first_pair + u
            ps = []
            for h in range(N_HEADS_B):
                halves = []
                for c in (2 * pair, 2 * pair + 1):
                    p = jnp.exp2(s_ref[h, rows(c), :] - ms[h])
                    ls[h] = ls[h] + _fold(p, jnp.add)
                    halves.append(p.astype(BF16))
                ps.append(jnp.concatenate(halves, axis=0))
            vt = vt_ref[:, pl.ds(pl.multiple_of(pair * 2 * BLK, 2 * BLK), 2 * BLK)]
            pvs.append([_dot(vt, ps[h]) for h in range(N_HEADS_B)])
        for h in range(N_HEADS_B):
            acc_ref[h * HEAD_DIM:(h + 1) * HEAD_DIM, :] += functools.reduce(
                lambda a, b: a + b, [pv[h] for pv in pvs])
        return tuple(ls)

    exp_init = tuple(jnp.zeros((8, DSA_Q), F32) for _ in range(N_HEADS_B))

    def finish(ls):
        l_all = jnp.concatenate(
            [jnp.broadcast_to(sub_reduce(ls[h], jnp.sum), (HEAD_DIM, DSA_Q)) for h in range(N_HEADS_B)],
            axis=0)
        o_ref[...] = (acc_ref[...] / l_all).T

    return types.SimpleNamespace(scores=scores, select=select, qk=qk, exp_pv=exp_pv, finish=finish,
                                 score_chunk=score_chunk, score_init=score_init, exp_init=exp_init,
                                 group=group, n_pairs=n_pairs)


def _dsa_kernel(kk_ref, tq_ref, vt_ref, wt_ref, o_ref, sc_ref, s_ref, acc_ref, kpos_ref, xb_ref, **static):
    j = pl.program_id(1)
    a, b = (_dsa_row(j, *(ref.at[i] for ref in (kk_ref, tq_ref, vt_ref, wt_ref, o_ref, sc_ref, s_ref,
                                                 acc_ref, kpos_ref, xb_ref)), **static) for i in range(2))
    ms = a.qk(a.select(a.scores()))

    def both(first_pair, n_pair, carry):
        ls, extremes = carry
        ls = a.exp_pv(ms, first_pair, n_pair, ls)
        return ls, b.group(b.score_chunk, 2 * first_pair, 2 * n_pair, extremes)

    carry = (a.exp_init, b.score_init)
    carry = lax.fori_loop(0, a.n_pairs // 2, lambda i, c: both(2 * i, 2, c), carry)
    ls, extremes = lax.fori_loop(0, a.n_pairs % 2, lambda i, c: both(a.n_pairs - 1, 1, c), carry)
    a.finish(ls)

    ms = b.qk(b.select(extremes))
    ls = b.exp_init
    ls = lax.fori_loop(0, b.n_pairs // 2, lambda i, c: b.exp_pv(ms, 2 * i, 2, c), ls)
    ls = lax.fori_loop(0, b.n_pairs % 2, lambda i, c: b.exp_pv(ms, b.n_pairs - 1, 1, c), ls)
    b.finish(ls)


def _dsa(proj_r, proj_t, w_t):
    B, T, _ = proj_r.shape
    k_sel = min(TOPK_MAX, T // 4)
    slopes = _static_alibi_slopes(N_HEADS_B)
    assert T // POS_SPLIT <= 256 and POS_SPLIT <= 256 and B % 2 == 0
    return pl.pallas_call(
        functools.partial(_dsa_kernel, seq=T, k_sel=k_sel, snap_unroll=4, slopes=slopes),
        grid=(B // 2, T // DSA_Q),
        in_specs=[
            pl.BlockSpec((2, T, 2 * LANES), lambda b, q: (b, 0, R_KB // (2 * LANES))),
            pl.BlockSpec((2, T_VB, DSA_Q), lambda b, q: (b, 0, q)),
            pl.BlockSpec((2, HEAD_DIM, T), lambda b, q: (b, T_VB // HEAD_DIM, 0)),
            pl.BlockSpec((2, WI_ROWS, DSA_Q), lambda b, q: (b, 0, q)),
        ],
        out_specs=pl.BlockSpec((2, DSA_Q, W_B), lambda b, q: (b, q, 0)),
        out_shape=jax.ShapeDtypeStruct((B, T, W_B), F32),
        scratch_shapes=[pltpu.VMEM((2, T, DSA_Q), F32),
                        pltpu.VMEM((2, N_HEADS_B, T, DSA_Q), F32),
                        pltpu.VMEM((2, W_B, DSA_Q), F32),
                        pltpu.VMEM((2, T, LANES), BF16),
                        pltpu.VMEM((2, T, DSA_Q), BF16)],
        compiler_params=pltpu.CompilerParams(
            dimension_semantics=("parallel", "arbitrary"), vmem_limit_bytes=VMEM_LIMIT),
        name="dsa",
    )(proj_r, proj_t, proj_t, w_t)


def _mem_kv_kernel(mem_ref, g_ref, w_ref, kv_ref):
    kv_ref[0] = _dot(_rms(mem_ref[0], g_ref[...]).astype(BF16), w_ref[...]).astype(BF16)


def _mem_kv(mem, g, w):
    B, M, D = mem.shape
    return pl.pallas_call(
        _mem_kv_kernel,
        grid=(B,),
        in_specs=[
            pl.BlockSpec((1, M, D), lambda b: (b, 0, 0)),
            pl.BlockSpec((1, D), lambda b: (0, 0)),
            pl.BlockSpec(w.shape, lambda b: (0, 0)),
        ],
        out_specs=pl.BlockSpec((1, M, 2 * W_M), lambda b: (b, 0, 0)),
        out_shape=jax.ShapeDtypeStruct((B, M, 2 * W_M), BF16),
        compiler_params=pltpu.CompilerParams(
            dimension_semantics=("parallel",), vmem_limit_bytes=VMEM_LIMIT),
        name="mem_kv",
    )(mem, g, w)


def _out_kernel(x_ref, oa_ref, ob_ref, gate_ref, qm_ref, kv_ref, wo_ref, g_ref, y_ref, mix_ref, *, sub):
    tm = x_ref.shape[0]
    subs = [slice(r0, r0 + sub) for r0 in range(0, tm, sub)]
    lane = lax.broadcasted_iota(I32, (1, W_M), 1)
    head_lanes = [(lane >= h * HEAD_DIM) & (lane < (h + 1) * HEAD_DIM) for h in range(N_HEADS_MEM)]
    km = kv_ref[:, 0:W_M]
    vm = kv_ref[:, W_M:2 * W_M]
    zero = jnp.zeros((), BF16)
    vm_heads = [jnp.where(head_lanes[h], vm, zero) for h in range(N_HEADS_MEM)]

    def silu(g):
        return g * jax.nn.sigmoid(g)

    def gated(o, rows, lo, width):
        return (o * silu(gate_ref[rows, lo:lo + width].astype(F32))).astype(BF16)

    for rows in subs:
        qm = qm_ref[rows, :]
        om = None
        for h in range(N_HEADS_MEM):
            s = _dot_nt(jnp.where(head_lanes[h], qm, zero), km)
            p = jnp.exp2(s - jnp.max(s, axis=1, keepdims=True))
            p = p / jnp.sum(p, axis=1, keepdims=True)
            o = _dot(p.astype(BF16), vm_heads[h])
            om = o if om is None else om + o
        mix_ref[rows, W_A + W_B:] = gated(om, rows, W_A + W_B, W_M)

    for rows in subs:
        half = W_A // 2
        mix_ref[rows, 0:half] = gated(oa_ref[rows, 0:half], rows, 0, half)
        mix_ref[rows, half:W_A] = gated(oa_ref[rows, half:W_A], rows, half, half)
        mix_ref[rows, W_A:W_A + W_B] = gated(ob_ref[rows, :], rows, W_A, W_B)
        y_ref[rows, :] = _dot(mix_ref[rows, :], wo_ref[...])

    for rows in subs:
        y_ref[rows, :] = _rms(x_ref[rows, :] + y_ref[rows, :], g_ref[...])


def _out(x, o_a, o_b, proj_r, kv_m, w_out, g, tm=1024, sub=256):
    B, T, D = x.shape
    M = kv_m.shape[1]
    row = lambda b, i: (b, i, 0)
    return pl.pallas_call(
        functools.partial(_out_kernel, sub=sub),
        grid=(B, T // tm),
        in_specs=[
            pl.BlockSpec((None, tm, D), row),
            pl.BlockSpec((None, tm, W_A), row),
            pl.BlockSpec((None, tm, W_B), row),
            pl.BlockSpec((None, tm, MIX_WIDTH), lambda b, i: (b, i, R_GATE // MIX_WIDTH)),
            pl.BlockSpec((None, tm, W_M), lambda b, i: (b, i, R_QM // W_M)),
            pl.BlockSpec((None, M, 2 * W_M), lambda b, i: (b, 0, 0)),
            pl.BlockSpec(w_out.shape, lambda b, i: (0, 0)),
            pl.BlockSpec((1, D), lambda b, i: (0, 0)),
        ],
        out_specs=pl.BlockSpec((None, tm, D), row),
        out_shape=jax.ShapeDtypeStruct((B, T, D), F32),
        scratch_shapes=[pltpu.VMEM((tm, MIX_WIDTH), BF16)],
        compiler_params=pltpu.CompilerParams(
            dimension_semantics=("parallel", "parallel"), vmem_limit_bytes=VMEM_LIMIT),
        name="out",
    )(x, o_a, o_b, proj_r, proj_r, kv_m, w_out, g)


def _alibi_slopes(n):
    return 2.0 ** (-8.0 * jnp.arange(1, n + 1, dtype=F32) / n)


def _static_alibi_slopes(n):
    return tuple(2.0 ** (-8.0 * i / n) for i in range(1, n + 1))


def _split_weights(w):
    bounds = [0]
    for s in SPLIT_SIZES:
        bounds.append(bounds[-1] + s)
    q_a, k_a, v_a, q_b, k_b, v_b, q_m, gate, q_i, k_i, w_i = (
        w[:, bounds[i]:bounds[i + 1]] for i in range(len(SPLIT_SIZES)))
    scale = HEAD_DIM ** -0.5 * LOG2E
    idx_scale = IDX_DIM ** -0.5
    zeros = jnp.zeros((w.shape[0], LANES - HEAD_DIM), w.dtype)
    wa = jnp.concatenate([q_a * scale, k_a, v_a], axis=1)
    wr = jnp.concatenate([gate, q_m * scale, k_b, zeros, k_i, zeros], axis=1)
    wt = jnp.concatenate([q_i * idx_scale, q_b * scale, v_b], axis=1).T
    ww = jnp.concatenate([w_i.T, jnp.zeros((WI_ROWS - N_IDX_HEADS, w.shape[0]), w.dtype)], axis=0)
    return wa.astype(BF16), wr.astype(BF16), wt.astype(BF16), ww.astype(BF16)


def kernel(x, mem, g_in, g_mem, w_in, w_mem_kv, w_out, g_final):
    assert g_in.shape[0] == 1, "single-layer block: the final RMSNorm is fused into the output kernel"
    wa, wr, wt, ww = _split_weights(w_in[0])
    qkv_a, proj_r, proj_t, w_t = _proj(x, g_in, wa, wr, wt, ww)
    o_a = _attn_a(_alibi_slopes(N_HEADS_A), qkv_a)
    o_b = _dsa(proj_r, proj_t, w_t)
    kv_m = _mem_kv(mem, g_mem, w_mem_kv[0].astype(BF16))
    return _out(x, o_a, o_b, proj_r, kv_m, w_out[0].astype(BF16), g_final[None, :])
```

```python
import functools
import math
import types

import numpy as np

import jax
import jax.numpy as jnp
from jax import lax
from jax.experimental import pallas as pl
from jax.experimental.pallas import tpu as pltpu

F32 = jnp.float32
BF16 = jnp.bfloat16
I32 = jnp.int32

D_MODEL = 1024
HEAD_DIM = 64
N_HEADS_A = 8
N_HEADS_B = 4
N_HEADS_MEM = 4
W_A = N_HEADS_A * HEAD_DIM
W_B = N_HEADS_B * HEAD_DIM
W_M = N_HEADS_MEM * HEAD_DIM
MIX_WIDTH = W_A + W_B + W_M
DILATIONS = (1, 4, 16)
BAND = 128
N_IDX_HEADS = 8
IDX_DIM = 64
TOPK_MAX = 256
RMS_EPS = 1e-6
SPLIT_SIZES = (W_A, W_A, W_A, W_B, HEAD_DIM, HEAD_DIM, W_M, MIX_WIDTH,
               N_IDX_HEADS * IDX_DIM, IDX_DIM, N_IDX_HEADS)

LANES = 128
BLK = 128
NEG = -1e30
VMEM_LIMIT = 48 * 1024 * 1024

R_GATE = 0
R_QM = MIX_WIDTH
R_KB = R_QM + W_M
R_KI = R_KB + LANES
R_COLS = R_KI + LANES
T_QI = 0
T_QB = N_IDX_HEADS * IDX_DIM
T_VB = T_QB + W_B
T_ROWS = T_VB + HEAD_DIM
WI_ROWS = 16
DSA_Q = 256
DSA_COARSE = 10
DSA_HALVINGS = 3
POS_SPLIT = 64
LOG2E = math.log2(math.e)


def _dot(a, b):
    return jnp.dot(a, b, preferred_element_type=F32)


def _dot_nt(a, b):
    return lax.dot_general(a, b, (((1,), (1,)), ((), ())), preferred_element_type=F32)


def _rms(x, g):
    return x * lax.rsqrt(jnp.mean(x * x, axis=-1, keepdims=True) + RMS_EPS) * g


def _proj_kernel(x_ref, g_ref, wa_ref, wr_ref, wt_ref, ww_ref,
                 oa_ref, oa4_ref, oa16_ref, or_ref, ot_ref, ow_ref, res_ref, res4_ref):
    hb = _rms(x_ref[0], g_ref[...]).astype(BF16)
    tm = hb.shape[0]
    d4, d16 = DILATIONS[1], DILATIONS[2]
    step = d16 // d4
    res = _dot(hb, wa_ref[...])
    oa_ref[0] = res.astype(BF16)
    for grp in range(res.shape[1] // LANES):
        lanes = slice(grp * LANES, (grp + 1) * LANES)
        res_ref[grp] = res[:, lanes]
        for r in range(d4):
            rows4 = res_ref[grp, pl.ds(r, tm // d4, stride=d4), :]
            oa4_ref[0, r, :, lanes] = rows4.astype(BF16)
            res4_ref[grp, r * (tm // d4):(r + 1) * (tm // d4), :] = rows4
        for r in range(d16):
            start = (r % d4) * (tm // d4) + r // d4
            oa16_ref[0, r, :, lanes] = res4_ref[grp, pl.ds(start, tm // d16, stride=step), :].astype(BF16)
    or_ref[0] = _dot(hb, wr_ref[...]).astype(BF16)
    ot_ref[0] = _dot_nt(wt_ref[...], hb).astype(BF16)
    ow_ref[0] = _dot_nt(ww_ref[...], hb)


def _proj(x, g, wa, wr, wt, ww, tm=512):
    B, T, D = x.shape
    const = lambda b, i: (0, 0)
    d4, d16 = DILATIONS[1], DILATIONS[2]
    nat, cm4, cm16, proj_r, proj_t, w_t = pl.pallas_call(
        _proj_kernel,
        grid=(B, T // tm),
        in_specs=[
            pl.BlockSpec((1, tm, D), lambda b, i: (b, i, 0)),
            pl.BlockSpec((1, D), const),
            pl.BlockSpec(wa.shape, const),
            pl.BlockSpec(wr.shape, const),
            pl.BlockSpec(wt.shape, const),
            pl.BlockSpec(ww.shape, const),
        ],
        out_specs=[
            pl.BlockSpec((1, tm, 3 * W_A), lambda b, i: (b, i, 0)),
            pl.BlockSpec((1, d4, tm // d4, 3 * W_A), lambda b, i: (b, 0, i, 0)),
            pl.BlockSpec((1, d16, tm // d16, 3 * W_A), lambda b, i: (b, 0, i, 0)),
            pl.BlockSpec((1, tm, R_COLS), lambda b, i: (b, i, 0)),
            pl.BlockSpec((1, T_ROWS, tm), lambda b, i: (b, 0, i)),
            pl.BlockSpec((1, WI_ROWS, tm), lambda b, i: (b, 0, i)),
        ],
        out_shape=[
            jax.ShapeDtypeStruct((B, T, 3 * W_A), BF16),
            jax.ShapeDtypeStruct((B, d4, T // d4, 3 * W_A), BF16),
            jax.ShapeDtypeStruct((B, d16, T // d16, 3 * W_A), BF16),
            jax.ShapeDtypeStruct((B, T, R_COLS), BF16),
            jax.ShapeDtypeStruct((B, T_ROWS, T), BF16),
            jax.ShapeDtypeStruct((B, WI_ROWS, T), F32),
        ],
        scratch_shapes=[pltpu.VMEM((3 * W_A // LANES, tm, LANES), F32)] * 2,
        compiler_params=pltpu.CompilerParams(
            dimension_semantics=("parallel", "parallel"), vmem_limit_bytes=VMEM_LIMIT),
        name="proj",
    )(x, g, wa, wr, wt, ww)
    return (nat, cm4.reshape(nat.shape), cm16.reshape(nat.shape)), proj_r, proj_t, w_t


def _attn_a_kernel(slopes_ref, *refs, seq, unroll):
    n_pat = len(DILATIONS)
    qkv = [refs[3 * p:3 * p + 3] for p in range(n_pat)]
    o_ref, acc_ref, m_ref, bias_ref, p_ref = refs[3 * n_pat:]
    hp = pl.program_id(1)
    lane = lax.broadcasted_iota(I32, (1, LANES), 1)
    head_lanes = (lane < HEAD_DIM, lane >= HEAD_DIM)
    row = lax.broadcasted_iota(I32, (BLK, BLK), 0)
    col = lax.broadcasted_iota(I32, (BLK, BLK), 1)
    d_cur = (row - col).astype(F32)
    d_prev = d_cur + float(BAND)
    n_all = seq // BLK
    zero = jnp.zeros((), BF16)
    one = jnp.ones((), BF16)

    def n_blocks(p):
        return n_all // DILATIONS[p]

    def n_keys(p):
        return BLK if n_blocks(p) == 1 else 2 * BLK

    def block(g):
        return pl.ds(pl.multiple_of(g * BLK, BLK), BLK)

    def keys(p, g):
        return pl.ds(pl.multiple_of(jnp.maximum((g + 1) * BLK - n_keys(p), 0), BLK), n_keys(p))

    def variant(p, g):
        g = jnp.asarray(g, I32)
        return jnp.where(g == 0, 2, jnp.where(g % n_blocks(p) == 0, 1, 0))

    def pitch(dil):
        return seq // dil + 1 if seq // dil == BLK and dil > 1 else None

    def put(ref, h, p, g, val):
        dil = DILATIONS[p]
        start = g // n_blocks(p) + (g % n_blocks(p)) * (dil * BLK)
        if dil == 1:
            ref[h, p, pl.ds(pl.multiple_of(start, BLK), BLK), :] = val
        elif pitch(dil):
            ref[h, p, pl.ds(start * pitch(dil), BLK), :] = val
        else:
            ref[h, p, pl.ds(start, BLK, stride=dil), :] = val

    def probs(p):
        q_ref, k_ref, _ = qkv[p]

        def body(g):
            kw = k_ref[keys(p, g), :]
            q = q_ref[block(g), :]
            for h in range(2):
                s = _dot_nt(jnp.where(head_lanes[h], q, zero), kw)
                s = s + bias_ref[p, h, variant(p, g), :, 2 * BLK - n_keys(p):]
                m = jnp.max(s, axis=1, keepdims=True)
                p_ref[p % 2, h, g, :, :n_keys(p)] = jnp.exp2(s - m).astype(BF16)
                put(m_ref, h, p, g, jnp.broadcast_to(m, (BLK, LANES)))
        return body

    def values(p):
        v_ref = qkv[p][2]

        def body(g):
            vw = v_ref[keys(p, g), :]
            for h in range(2):
                put(acc_ref, h, p, g,
                    _dot(p_ref[p % 2, h, g, :, :n_keys(p)], jnp.where(head_lanes[h], vw, one)))
        return body

    def run(*bodies):
        def step(g, carry):
            for body in bodies:
                body(g)
            return carry
        lax.fori_loop(0, n_all, step, 0, unroll=unroll)

    for p, dil in enumerate(DILATIONS):
        for h in range(2):
            sd = slopes_ref[hp * 2 + h] * (float(dil) * LOG2E)
            cur = jnp.where(row >= col, -sd * d_cur, NEG)
            masked = jnp.full((BLK, BLK), NEG, F32)
            bias_ref[p, h, 0, :, BLK:] = cur
            bias_ref[p, h, 1, :, BLK:] = cur
            bias_ref[p, h, 2, :, BLK:] = cur if n_keys(p) == BLK else masked
            if n_keys(p) > BLK:
                bias_ref[p, h, 0, :, :BLK] = jnp.where(col >= row, -sd * d_prev, NEG)
                bias_ref[p, h, 1, :, :BLK] = masked
                bias_ref[p, h, 2, :, :BLK] = cur

    assert n_pat == 3
    run(probs(0))
    run(values(0), probs(1))
    run(values(1), probs(2))
    run(values(2))

    rows_per_step = 2 * BLK

    def natural_rows(ref, h, p, i):
        dil = DILATIONS[p]
        if not pitch(dil):
            return ref[h, p, pl.ds(pl.multiple_of(i * rows_per_step, rows_per_step), rows_per_step), :]
        per = rows_per_step // dil
        return jnp.concatenate(
            [ref[h, p, pl.ds(i * per + u, dil, stride=pitch(dil)), :] for u in range(per)], axis=0)

    def merge(i, carry):
        sl = pl.ds(pl.multiple_of(i * rows_per_step, rows_per_step), rows_per_step)
        nums = []
        for h in range(2):
            ms = [natural_rows(m_ref, h, p, i) for p in range(n_pat)]
            mx = functools.reduce(jnp.maximum, ms)
            nums.append(sum(jnp.exp2(ms[p] - mx) * natural_rows(acc_ref, h, p, i) for p in range(n_pat)))
        acc = jnp.where(head_lanes[0], nums[0], nums[1])
        den = pltpu.roll(jnp.where(head_lanes[0], nums[1], nums[0]), HEAD_DIM, axis=1)
        o_ref[sl, :] = acc / den
        return carry

    lax.fori_loop(0, seq // rows_per_step, merge, 0, unroll=4)


def _attn_a(slopes, qkv_layouts, unroll=16):
    B, T, _ = qkv_layouts[0].shape
    n_pairs = N_HEADS_A // 2
    n_pat = len(DILATIONS)

    def spec(off):
        return pl.BlockSpec((None, T, LANES), lambda b, h: (b, 0, off + h))

    return pl.pallas_call(
        functools.partial(_attn_a_kernel, seq=T, unroll=unroll),
        grid=(B, n_pairs),
        in_specs=[pl.BlockSpec(memory_space=pltpu.SMEM)]
        + [spec(part * n_pairs) for _ in range(n_pat) for part in range(3)],
        out_specs=pl.BlockSpec((None, T, LANES), lambda b, h: (b, 0, h)),
        out_shape=jax.ShapeDtypeStruct((B, T, W_A), F32),
        scratch_shapes=[pltpu.VMEM((2, n_pat, T + max(DILATIONS), LANES), F32),
                        pltpu.VMEM((2, n_pat, T + max(DILATIONS), LANES), F32),
                        pltpu.VMEM((n_pat, 2, 3, BLK, 2 * BLK), F32),
                        pltpu.VMEM((2, 2, T // BLK, BLK, 2 * BLK), BF16)],
        compiler_params=pltpu.CompilerParams(
            dimension_semantics=("parallel", "parallel"), vmem_limit_bytes=VMEM_LIMIT),
        name="attn_a",
    )(slopes, *[a for a in qkv_layouts for _ in range(3)])


def _bf16_pair(c):
    hi = float(np.asarray(c, np.float32).astype(BF16).astype(np.float32))
    lo = float(np.asarray(c - hi, np.float32).astype(BF16).astype(np.float32))
    return hi, lo


def _fold(x, op):
    return functools.reduce(op, [x[r:r + 8] for r in range(0, x.shape[0], 8)])


def _dsa_row(j, kk_ref, tq_ref, vt_ref, wt_ref, o_ref, sc_ref, s_ref, acc_ref, kpos_ref, xb_ref,
             *, seq, k_sel, snap_unroll, slopes):
    n_pairs = j + 1
    t_lane = j * DSA_Q + lax.broadcasted_iota(I32, (1, DSA_Q), 1)
    row = lax.broadcasted_iota(I32, (BLK, DSA_Q), 0)
    ws = wt_ref[0:N_IDX_HEADS, :] * (N_IDX_HEADS ** -0.5)
    inf = float("inf")
    few = t_lane < k_sel

    def rows(c):
        return pl.ds(pl.multiple_of(c * BLK, BLK), BLK)

    def group(fn, first, n, carry):
        for u in range(n):
            carry = fn(first + u, carry)
        return carry

    def each_chunk(fn, init, widest=4):
        carry, done, width = init, 0, widest
        while width >= 2:
            steps = (2 * n_pairs - done) // width
            carry = lax.fori_loop(
                0, steps, lambda i, c, done=done, width=width: group(fn, done + width * i, width, c), carry)
            done, width = done + steps * width, width // 2
        return carry

    def sub_reduce(x, op):
        return op(x, axis=0, keepdims=True)

    def score_chunk(c, carry):
        mn, mx = carry
        ki = kk_ref[rows(c), R_KI - R_KB:R_KI - R_KB + IDX_DIM]
        lgs = [_dot(ki, tq_ref[T_QI + h * IDX_DIM:T_QI + (h + 1) * IDX_DIM, :])
               for h in range(N_IDX_HEADS)]
        sc = functools.reduce(
            lambda a, b: a + b, [jnp.maximum(lg, 0.0) * ws[h:h + 1, :] for h, lg in enumerate(lgs)])
        causal = (c * BLK + row) <= t_lane
        sc_ref[rows(c), :] = jnp.where(causal, sc, -inf)
        xb_ref[rows(c), :] = jnp.where(causal, sc, -inf).astype(BF16)
        mn = jnp.minimum(mn, _fold(jnp.where(causal, sc, inf), jnp.minimum))
        mx = jnp.maximum(mx, _fold(jnp.where(causal, sc, -inf), jnp.maximum))
        return mn, mx

    score_init = (jnp.full((8, DSA_Q), inf, F32), jnp.full((8, DSA_Q), -inf, F32))

    def scores():
        return each_chunk(score_chunk, score_init, widest=8)

    def count(pred):
        def body(c, cnt):
            return cnt + _fold(pred(sc_ref[rows(c), :], c).astype(I32), jnp.add)
        return sub_reduce(each_chunk(body, jnp.zeros((8, DSA_Q), I32)), jnp.sum)

    def halve(_, bounds):
        lo, hi, n_lo, n_hi = bounds
        mid = 0.5 * lo + 0.5 * hi
        cnt = count(lambda x, c: x >= mid)
        active = lo < hi
        up = active & (cnt >= k_sel)
        down = active & (cnt <= k_sel)
        return (jnp.where(up, mid, lo), jnp.where(down, mid, hi),
                jnp.where(up, cnt, n_lo), jnp.where(down, cnt, n_hi))

    def coarse(_, bounds):
        lo, hi = bounds
        mid = (0.5 * lo + 0.5 * hi).astype(BF16)

        def body(c, cnt):
            hit = jnp.where(xb_ref[rows(c), :] >= mid, jnp.ones((), BF16), jnp.zeros((), BF16))
            return cnt + functools.reduce(lambda a, b: a + b, [hit[r:r + 16] for r in range(0, BLK, 16)])

        cnt = each_chunk(body, jnp.zeros((16, DSA_Q), BF16))
        cnt = jnp.sum(cnt.astype(F32), axis=0, keepdims=True)
        mid = mid.astype(F32)
        active = lo < hi
        enough = cnt >= k_sel
        margin = jnp.abs(mid) * 2.0 ** -7 + 1e-30
        return (jnp.where(active & enough, jnp.maximum(lo, mid - margin), lo),
                jnp.where(active & jnp.logical_not(enough), jnp.minimum(hi, mid), hi))

    def snap(bounds):
        lo, hi, n_lo, n_hi = bounds
        mid = 0.5 * lo + 0.5 * hi
        mid = jnp.where(mid > lo, mid, hi)

        def body(c, carry):
            cnt, above, below = carry
            x = sc_ref[rows(c), :]
            ge = x >= mid
            return (cnt + _fold(ge.astype(I32), jnp.add),
                    jnp.minimum(above, _fold(jnp.where(ge, x, inf), jnp.minimum)),
                    jnp.maximum(below, _fold(jnp.where(ge, -inf, x), jnp.maximum)))

        cnt, above, below = each_chunk(body, (jnp.zeros((8, DSA_Q), I32),
                                              jnp.full((8, DSA_Q), inf, F32),
                                              jnp.full((8, DSA_Q), -inf, F32)))
        cnt = sub_reduce(cnt, jnp.sum)
        above = sub_reduce(above, jnp.min)
        below = sub_reduce(below, jnp.max)
        active = lo < hi
        enough = cnt >= k_sel
        up = active & enough
        down = active & (cnt <= k_sel)
        return (jnp.where(up, above, lo), jnp.where(down, jnp.where(enough, above, below), hi),
                jnp.where(up, cnt, n_lo), jnp.where(down, cnt, n_hi))

    def snaps(bounds):
        for _ in range(snap_unroll):
            bounds = snap(bounds)
        return bounds

    def unsettled(bounds):
        return jnp.max((bounds[0] < bounds[1]).astype(F32)) > 0

    n_idx_bits = seq.bit_length() - 1
    assert 1 << n_idx_bits == seq

    def select(extremes):
        mn, mx = extremes
        lo = jnp.where(few, -inf, sub_reduce(mn, jnp.min))
        hi = jnp.where(few, -inf, sub_reduce(mx, jnp.max))
        lo, hi = lax.fori_loop(0, DSA_COARSE, coarse, (lo, hi))

        def recount(c, carry):
            x = sc_ref[rows(c), :]
            return (carry[0] + _fold((x >= lo).astype(I32), jnp.add),
                    carry[1] + _fold((x > hi).astype(I32), jnp.add))

        n_lo, n_hi = each_chunk(recount, (jnp.zeros((8, DSA_Q), I32),) * 2)
        bounds = (lo, hi, sub_reduce(n_lo, jnp.sum), sub_reduce(n_hi, jnp.sum))
        bounds = lax.fori_loop(0, DSA_HALVINGS, halve, bounds)

        def state(bounds):
            lo, hi, n_lo, _ = bounds
            tied = (n_lo > k_sel) & jnp.logical_not(few)
            return jnp.max(jnp.where(lo < hi, 2.0, tied.astype(F32)))

        def snap_more(bounds):
            bounds = lax.while_loop(unsettled, snaps, bounds)
            return bounds + (state(bounds),)

        bounds = snaps(bounds)
        code = state(bounds)
        tau, _, n_ge, n_gt, code = lax.cond(code >= 2, lambda: snap_more(bounds), lambda: bounds + (code,))
        need = k_sel - n_gt
        tied = (n_ge > k_sel) & jnp.logical_not(few)

        def tie_break():
            def idx_step(it, jp):
                cand = jp + lax.shift_left(jnp.int32(1), n_idx_bits - 1 - it)
                below = count(lambda x, c: (x == tau) & ((c * BLK + row) < cand))
                return jnp.where(below < need, cand, jp)
            last = lax.fori_loop(0, n_idx_bits, idx_step, jnp.zeros((1, DSA_Q), I32))
            return jnp.where(tied, last, seq)

        surplus = code == 1
        tie_last = lax.cond(surplus, tie_break, lambda: jnp.full((1, DSA_Q), seq, I32))
        return tau, tie_last, surplus

    def qk(selection):
        tau, tie_last, surplus = selection

        @pl.when(j == 0)
        def _():
            pos = lax.broadcasted_iota(I32, (seq, LANES), 0)
            ln = lax.broadcasted_iota(I32, (seq, LANES), 1) - HEAD_DIM
            feat = jnp.where(ln == 0, pos // POS_SPLIT,
                             jnp.where(ln == 1, pos % POS_SPLIT,
                                       jnp.where(ln == 2, pos // POS_SPLIT,
                                                 jnp.where(ln == 3, pos % POS_SPLIT, 0))))
            kpos_ref[...] = kk_ref[:, 0:LANES] + feat.astype(BF16)

        q_row = lax.broadcasted_iota(I32, (HEAD_DIM, DSA_Q), 0)
        q_aug = []
        for h in range(N_HEADS_B):
            c_hi, c_lo = _bf16_pair(slopes[h] * LOG2E)
            feat = jnp.where(q_row == 0, POS_SPLIT * c_hi,
                             jnp.where(q_row == 1, c_hi,
                                       jnp.where(q_row == 2, POS_SPLIT * c_lo,
                                                 jnp.where(q_row == 3, c_lo, 0.0))))
            q_aug.append(jnp.concatenate(
                [tq_ref[T_QB + h * HEAD_DIM:T_QB + (h + 1) * HEAD_DIM, :], feat.astype(BF16)], axis=0))

        tau_all = jnp.where(few, float(jnp.finfo(F32).min), tau)

        def qk_phase(with_ties):
            def qk_chunk(c, ms):
                kb = kpos_ref[rows(c), :]
                x = sc_ref[rows(c), :]
                if with_ties:
                    s_idx = c * BLK + row
                    tie = jnp.where(x == tau, jnp.where(s_idx <= tie_last, 0.0, NEG), NEG)
                    sb = jnp.where(s_idx <= t_lane, jnp.where(x > tau, 0.0, tie), NEG)
                else:
                    sb = jnp.where(x >= tau_all, 0.0, NEG)
                out = []
                for h in range(N_HEADS_B):
                    s = _dot(kb, q_aug[h]) + sb
                    s_ref[h, rows(c), :] = s
                    out.append(jnp.maximum(ms[h], _fold(s, jnp.maximum)))
                return tuple(out)

            return each_chunk(qk_chunk, tuple(jnp.full((8, DSA_Q), NEG, F32) for _ in range(N_HEADS_B)),
                              widest=8)

        ms = lax.cond(surplus, lambda: qk_phase(True), lambda: qk_phase(False))
        acc_ref[...] = jnp.zeros((W_B, DSA_Q), F32)
        return tuple(sub_reduce(m, jnp.max) for m in ms)

    def exp_pv(ms, first_pair, n, ls):
        ls = list(ls)
        pvs = []
        for u in range(n):
            pair = first_pair + u
            ps = []
            for h in range(N_HEADS_B):
                halves = []
                for c in (2 * pair, 2 * pair + 1):
                    p = jnp.exp2(s_ref[h, rows(c), :] - ms[h])
                    ls[h] = ls[h] + _fold(p, jnp.add)
                    halves.append(p.astype(BF16))
                ps.append(jnp.concatenate(halves, axis=0))
            vt = vt_ref[:, pl.ds(pl.multiple_of(pair * 2 * BLK, 2 * BLK), 2 * BLK)]
            pvs.append([_dot(vt, ps[h]) for h in range(N_HEADS_B)])
        for h in range(N_HEADS_B):
            acc_ref[h * HEAD_DIM:(h + 1) * HEAD_DIM, :] += functools.reduce(
                lambda a, b: a + b, [pv[h] for pv in pvs])
        return tuple(ls)

    exp_init = tuple(jnp.zeros((8, DSA_Q), F32) for _ in range(N_HEADS_B))

    def finish(ls):
        l_all = jnp.concatenate(
            [jnp.broadcast_to(sub_reduce(ls[h], jnp.sum), (HEAD_DIM, DSA_Q)) for h in range(N_HEADS_B)],
            axis=0)
        o_ref[...] = (acc_ref[...] / l_all).T

    return types.SimpleNamespace(scores=scores, select=select, qk=qk, exp_pv=exp_pv, finish=finish,
                                 score_chunk=score_chunk, score_init=score_init, exp_init=exp_init,
                                 group=group, n_pairs=n_pairs)


def _dsa_kernel(kk_ref, tq_ref, vt_ref, wt_ref, o_ref, sc_ref, s_ref, acc_ref, kpos_ref, xb_ref, **static):
    j = pl.program_id(1)
    a, b = (_dsa_row(j, *(ref.at[i] for ref in (kk_ref, tq_ref, vt_ref, wt_ref, o_ref, sc_ref, s_ref,
                                                 acc_ref, kpos_ref, xb_ref)), **static) for i in range(2))
    ms = a.qk(a.select(a.scores()))

    def both(first_pair, n_pair, carry):
        ls, extremes = carry
        ls = a.exp_pv(ms, first_pair, n_pair, ls)
        return ls, b.group(b.score_chunk, 2 * first_pair, 2 * n_pair, extremes)

    carry = (a.exp_init, b.score_init)
    carry = lax.fori_loop(0, a.n_pairs // 2, lambda i, c: both(2 * i, 2, c), carry)
    ls, extremes = lax.fori_loop(0, a.n_pairs % 2, lambda i, c: both(a.n_pairs - 1, 1, c), carry)
    a.finish(ls)

    ms = b.qk(b.select(extremes))
    ls = b.exp_init
    ls = lax.fori_loop(0, b.n_pairs // 2, lambda i, c: b.exp_pv(ms, 2 * i, 2, c), ls)
    ls = lax.fori_loop(0, b.n_pairs % 2, lambda i, c: b.exp_pv(ms, b.n_pairs - 1, 1, c), ls)
    b.finish(ls)


def _dsa(proj_r, proj_t, w_t):
    B, T, _ = proj_r.shape
    k_sel = min(TOPK_MAX, T // 4)
    slopes = _static_alibi_slopes(N_HEADS_B)
    assert T // POS_SPLIT <= 256 and POS_SPLIT <= 256 and B % 2 == 0
    return pl.pallas_call(
        functools.partial(_dsa_kernel, seq=T, k_sel=k_sel, snap_unroll=4, slopes=slopes),
        grid=(B // 2, T // DSA_Q),
        in_specs=[
            pl.BlockSpec((2, T, 2 * LANES), lambda b, q: (b, 0, R_KB // (2 * LANES))),
            pl.BlockSpec((2, T_VB, DSA_Q), lambda b, q: (b, 0, q)),
            pl.BlockSpec((2, HEAD_DIM, T), lambda b, q: (b, T_VB // HEAD_DIM, 0)),
            pl.BlockSpec((2, WI_ROWS, DSA_Q), lambda b, q: (b, 0, q)),
        ],
        out_specs=pl.BlockSpec((2, DSA_Q, W_B), lambda b, q: (b, q, 0)),
        out_shape=jax.ShapeDtypeStruct((B, T, W_B), F32),
        scratch_shapes=[pltpu.VMEM((2, T, DSA_Q), F32),
                        pltpu.VMEM((2, N_HEADS_B, T, DSA_Q), F32),
                        pltpu.VMEM((2, W_B, DSA_Q), F32),
                        pltpu.VMEM((2, T, LANES), BF16),
                        pltpu.VMEM((2, T, DSA_Q), BF16)],
        compiler_params=pltpu.CompilerParams(
            dimension_semantics=("parallel", "arbitrary"), vmem_limit_bytes=VMEM_LIMIT),
        name="dsa",
    )(proj_r, proj_t, proj_t, w_t)


def _out_kernel(x_ref, oa_ref, ob_ref, gate_ref, qm_ref, mem_ref, gm_ref, wm_ref, wo_ref, g_ref, y_ref,
                mix_ref, kv_ref, *, sub):
    tm = x_ref.shape[0]

    @pl.when(pl.program_id(1) == 0)
    def _():
        kv_ref[...] = _dot(_rms(mem_ref[...], gm_ref[...]).astype(BF16), wm_ref[...]).astype(BF16)

    subs = [slice(r0, r0 + sub) for r0 in range(0, tm, sub)]
    lane = lax.broadcasted_iota(I32, (1, W_M), 1)
    head_lanes = [(lane >= h * HEAD_DIM) & (lane < (h + 1) * HEAD_DIM) for h in range(N_HEADS_MEM)]
    km = kv_ref[:, 0:W_M]
    vm = kv_ref[:, W_M:2 * W_M]
    zero = jnp.zeros((), BF16)
    vm_heads = [jnp.where(head_lanes[h], vm, zero) for h in range(N_HEADS_MEM)]

    def silu(g):
        return g * jax.nn.sigmoid(g)

    def gated(o, rows, lo, width):
        return (o * silu(gate_ref[rows, lo:lo + width].astype(F32))).astype(BF16)

    for rows in subs:
        qm = qm_ref[rows, :]
        om = None
        for h in range(N_HEADS_MEM):
            s = _dot_nt(jnp.where(head_lanes[h], qm, zero), km)
            p = jnp.exp2(s - jnp.max(s, axis=1, keepdims=True))
            p = p / jnp.sum(p, axis=1, keepdims=True)
            o = _dot(p.astype(BF16), vm_heads[h])
            om = o if om is None else om + o
        mix_ref[rows, W_A + W_B:] = gated(om, rows, W_A + W_B, W_M)

    for rows in subs:
        half = W_A // 2
        mix_ref[rows, 0:half] = gated(oa_ref[rows, 0:half], rows, 0, half)
        mix_ref[rows, half:W_A] = gated(oa_ref[rows, half:W_A], rows, half, half)
        mix_ref[rows, W_A:W_A + W_B] = gated(ob_ref[rows, :], rows, W_A, W_B)
        y_ref[rows, :] = _dot(mix_ref[rows, :], wo_ref[...])

    for rows in subs:
        y_ref[rows, :] = _rms(x_ref[rows, :] + y_ref[rows, :], g_ref[...])


def _out(x, o_a, o_b, proj_r, mem, g_mem, w_mem, w_out, g, tm=1024, sub=256):
    B, T, D = x.shape
    M = mem.shape[1]
    row = lambda b, i: (b, i, 0)
    return pl.pallas_call(
        functools.partial(_out_kernel, sub=sub),
        grid=(B, T // tm),
        in_specs=[
            pl.BlockSpec((None, tm, D), row),
            pl.BlockSpec((None, tm, W_A), row),
            pl.BlockSpec((None, tm, W_B), row),
            pl.BlockSpec((None, tm, MIX_WIDTH), lambda b, i: (b, i, R_GATE // MIX_WIDTH)),
            pl.BlockSpec((None, tm, W_M), lambda b, i: (b, i, R_QM // W_M)),
            pl.BlockSpec((None, M, D), lambda b, i: (b, 0, 0)),
            pl.BlockSpec((1, D), lambda b, i: (0, 0)),
            pl.BlockSpec(w_mem.shape, lambda b, i: (0, 0)),
            pl.BlockSpec(w_out.shape, lambda b, i: (0, 0)),
            pl.BlockSpec((1, D), lambda b, i: (0, 0)),
        ],
        out_specs=pl.BlockSpec((None, tm, D), row),
        out_shape=jax.ShapeDtypeStruct((B, T, D), F32),
        scratch_shapes=[pltpu.VMEM((tm, MIX_WIDTH), BF16), pltpu.VMEM((M, 2 * W_M), BF16)],
        compiler_params=pltpu.CompilerParams(
            dimension_semantics=("parallel", "arbitrary"), vmem_limit_bytes=VMEM_LIMIT),
        name="out",
    )(x, o_a, o_b, proj_r, proj_r, mem, g_mem, w_mem, w_out, g)


def _alibi_slopes(n):
    return 2.0 ** (-8.0 * jnp.arange(1, n + 1, dtype=F32) / n)


def _static_alibi_slopes(n):
    return tuple(2.0 ** (-8.0 * i / n) for i in range(1, n + 1))


def _split_weights(w):
    bounds = [0]
    for s in SPLIT_SIZES:
        bounds.append(bounds[-1] + s)
    q_a, k_a, v_a, q_b, k_b, v_b, q_m, gate, q_i, k_i, w_i = (
        w[:, bounds[i]:bounds[i + 1]] for i in range(len(SPLIT_SIZES)))
    scale = HEAD_DIM ** -0.5 * LOG2E
    idx_scale = IDX_DIM ** -0.5
    zeros = jnp.zeros((w.shape[0], LANES - HEAD_DIM), w.dtype)
    wa = jnp.concatenate([q_a * scale, k_a, v_a], axis=1)
    wr = jnp.concatenate([gate, q_m * scale, k_b, zeros, k_i, zeros], axis=1)
    wt = jnp.concatenate([q_i * idx_scale, q_b * scale, v_b], axis=1).T
    ww = jnp.concatenate([w_i.T, jnp.zeros((WI_ROWS - N_IDX_HEADS, w.shape[0]), w.dtype)], axis=0)
    return wa.astype(BF16), wr.astype(BF16), wt.astype(BF16), ww.astype(BF16)


def kernel(x, mem, g_in, g_mem, w_in, w_mem_kv, w_out, g_final):
    assert g_in.shape[0] == 1, "single-layer block: the final RMSNorm is fused into the output kernel"
    wa, wr, wt, ww = _split_weights(w_in[0])
    qkv_a, proj_r, proj_t, w_t = _proj(x, g_in, wa, wr, wt, ww)
    o_a = _attn_a(_alibi_slopes(N_HEADS_A), qkv_a)
    o_b = _dsa(proj_r, proj_t, w_t)
    return _out(x, o_a, o_b, proj_r, mem, g_mem, w_mem_kv[0].astype(BF16), w_out[0].astype(BF16),
                g_final[None, :])
```

```python
import functools
import math
import types

import numpy as np

import jax
import jax.numpy as jnp
from jax import lax
from jax.experimental import pallas as pl
from jax.experimental.pallas import tpu as pltpu

F32 = jnp.float32
BF16 = jnp.bfloat16
I32 = jnp.int32

D_MODEL = 1024
HEAD_DIM = 64
N_HEADS_A = 8
N_HEADS_B = 4
N_HEADS_MEM = 4
W_A = N_HEADS_A * HEAD_DIM
W_B = N_HEADS_B * HEAD_DIM
W_M = N_HEADS_MEM * HEAD_DIM
MIX_WIDTH = W_A + W_B + W_M
DILATIONS = (1, 4, 16)
BAND = 128
N_IDX_HEADS = 8
IDX_DIM = 64
TOPK_MAX = 256
RMS_EPS = 1e-6
SPLIT_SIZES = (W_A, W_A, W_A, W_B, HEAD_DIM, HEAD_DIM, W_M, MIX_WIDTH,
               N_IDX_HEADS * IDX_DIM, IDX_DIM, N_IDX_HEADS)

LANES = 128
BLK = 128
NEG = -1e30
VMEM_LIMIT = 48 * 1024 * 1024

R_GATE = 0
R_QM = MIX_WIDTH
R_KB = R_QM + W_M
R_KI = R_KB + LANES
R_COLS = R_KI + LANES
T_QI = 0
T_QB = N_IDX_HEADS * IDX_DIM
T_VB = T_QB + W_B
T_ROWS = T_VB + HEAD_DIM
WI_ROWS = 16
DSA_Q = 256
DSA_COARSE = 10
DSA_HALVINGS = 3
POS_SPLIT = 64
LOG2E = math.log2(math.e)


def _dot(a, b):
    return jnp.dot(a, b, preferred_element_type=F32)


def _dot_nt(a, b):
    return lax.dot_general(a, b, (((1,), (1,)), ((), ())), preferred_element_type=F32)


def _rms(x, g):
    return x * lax.rsqrt(jnp.mean(x * x, axis=-1, keepdims=True) + RMS_EPS) * g


def _proj_kernel(x_ref, g_ref, wa_ref, wr_ref, wt_ref, ww_ref,
                 oa_ref, oa4_ref, oa16_ref, or_ref, ot_ref, ow_ref, res_ref, res4_ref):
    hb = _rms(x_ref[0], g_ref[...]).astype(BF16)
    tm = hb.shape[0]
    d4, d16 = DILATIONS[1], DILATIONS[2]
    step = d16 // d4
    res = _dot(hb, wa_ref[...])
    oa_ref[0] = res.astype(BF16)
    for grp in range(res.shape[1] // LANES):
        lanes = slice(grp * LANES, (grp + 1) * LANES)
        res_ref[grp] = res[:, lanes]
        for r in range(d4):
            rows4 = res_ref[grp, pl.ds(r, tm // d4, stride=d4), :]
            oa4_ref[0, r, :, lanes] = rows4.astype(BF16)
            res4_ref[grp, r * (tm // d4):(r + 1) * (tm // d4), :] = rows4
        for r in range(d16):
            start = (r % d4) * (tm // d4) + r // d4
            oa16_ref[0, r, :, lanes] = res4_ref[grp, pl.ds(start, tm // d16, stride=step), :].astype(BF16)
    or_ref[0] = _dot(hb, wr_ref[...]).astype(BF16)
    ot_ref[0] = _dot_nt(wt_ref[...], hb).astype(BF16)
    ow_ref[0] = _dot_nt(ww_ref[...], hb)


def _proj(x, g, wa, wr, wt, ww, tm=512):
    B, T, D = x.shape
    const = lambda b, i: (0, 0)
    d4, d16 = DILATIONS[1], DILATIONS[2]
    nat, cm4, cm16, proj_r, proj_t, w_t = pl.pallas_call(
        _proj_kernel,
        grid=(B, T // tm),
        in_specs=[
            pl.BlockSpec((1, tm, D), lambda b, i: (b, i, 0)),
            pl.BlockSpec((1, D), const),
            pl.BlockSpec(wa.shape, const),
            pl.BlockSpec(wr.shape, const),
            pl.BlockSpec(wt.shape, const),
            pl.BlockSpec(ww.shape, const),
        ],
        out_specs=[
            pl.BlockSpec((1, tm, 3 * W_A), lambda b, i: (b, i, 0)),
            pl.BlockSpec((1, d4, tm // d4, 3 * W_A), lambda b, i: (b, 0, i, 0)),
            pl.BlockSpec((1, d16, tm // d16, 3 * W_A), lambda b, i: (b, 0, i, 0)),
            pl.BlockSpec((1, tm, R_COLS), lambda b, i: (b, i, 0)),
            pl.BlockSpec((1, T_ROWS, tm), lambda b, i: (b, 0, i)),
            pl.BlockSpec((1, WI_ROWS, tm), lambda b, i: (b, 0, i)),
        ],
        out_shape=[
            jax.ShapeDtypeStruct((B, T, 3 * W_A), BF16),
            jax.ShapeDtypeStruct((B, d4, T // d4, 3 * W_A), BF16),
            jax.ShapeDtypeStruct((B, d16, T // d16, 3 * W_A), BF16),
            jax.ShapeDtypeStruct((B, T, R_COLS), BF16),
            jax.ShapeDtypeStruct((B, T_ROWS, T), BF16),
            jax.ShapeDtypeStruct((B, WI_ROWS, T), F32),
        ],
        scratch_shapes=[pltpu.VMEM((3 * W_A // LANES, tm, LANES), F32)] * 2,
        compiler_params=pltpu.CompilerParams(
            dimension_semantics=("parallel", "parallel"), vmem_limit_bytes=VMEM_LIMIT),
        name="proj",
    )(x, g, wa, wr, wt, ww)
    return (nat, cm4.reshape(nat.shape), cm16.reshape(nat.shape)), proj_r, proj_t, w_t


def _attn_a_kernel(slopes_ref, *refs, seq, unroll):
    n_pat = len(DILATIONS)
    qkv = [refs[3 * p:3 * p + 3] for p in range(n_pat)]
    o_ref, acc_ref, m_ref, bias_ref, p_ref = refs[3 * n_pat:]
    hp = pl.program_id(1)
    lane = lax.broadcasted_iota(I32, (1, LANES), 1)
    head_lanes = (lane < HEAD_DIM, lane >= HEAD_DIM)
    row = lax.broadcasted_iota(I32, (BLK, BLK), 0)
    col = lax.broadcasted_iota(I32, (BLK, BLK), 1)
    d_cur = (row - col).astype(F32)
    d_prev = d_cur + float(BAND)
    n_all = seq // BLK
    zero = jnp.zeros((), BF16)
    one = jnp.ones((), BF16)

    def n_blocks(p):
        return n_all // DILATIONS[p]

    def n_keys(p):
        return BLK if n_blocks(p) == 1 else 2 * BLK

    def block(g):
        return pl.ds(pl.multiple_of(g * BLK, BLK), BLK)

    def keys(p, g):
        return pl.ds(pl.multiple_of(jnp.maximum((g + 1) * BLK - n_keys(p), 0), BLK), n_keys(p))

    def variant(p, g):
        g = jnp.asarray(g, I32)
        return jnp.where(g == 0, 2, jnp.where(g % n_blocks(p) == 0, 1, 0))

    def pitch(dil):
        return seq // dil + 1 if seq // dil == BLK and dil > 1 else None

    def put(ref, h, p, g, val):
        dil = DILATIONS[p]
        start = g // n_blocks(p) + (g % n_blocks(p)) * (dil * BLK)
        if dil == 1:
            ref[h, p, pl.ds(pl.multiple_of(start, BLK), BLK), :] = val
        elif pitch(dil):
            ref[h, p, pl.ds(start * pitch(dil), BLK), :] = val
        else:
            ref[h, p, pl.ds(start, BLK, stride=dil), :] = val

    def probs(p):
        q_ref, k_ref, _ = qkv[p]

        def body(g):
            kw = k_ref[keys(p, g), :]
            q = q_ref[block(g), :]
            for h in range(2):
                s = _dot_nt(jnp.where(head_lanes[h], q, zero), kw)
                s = s + bias_ref[p, h, variant(p, g), :, 2 * BLK - n_keys(p):]
                m = jnp.max(s, axis=1, keepdims=True)
                p_ref[p % 2, h, g, :, :n_keys(p)] = jnp.exp2(s - m).astype(BF16)
                put(m_ref, h, p, g, jnp.broadcast_to(m, (BLK, LANES)))
        return body

    def values(p):
        v_ref = qkv[p][2]

        def body(g):
            vw = v_ref[keys(p, g), :]
            for h in range(2):
                put(acc_ref, h, p, g,
                    _dot(p_ref[p % 2, h, g, :, :n_keys(p)], jnp.where(head_lanes[h], vw, one)))
        return body

    def run(*bodies):
        def step(g, carry):
            for body in bodies:
                body(g)
            return carry
        lax.fori_loop(0, n_all, step, 0, unroll=unroll)

    for p, dil in enumerate(DILATIONS):
        for h in range(2):
            sd = slopes_ref[hp * 2 + h] * (float(dil) * LOG2E)
            cur = jnp.where(row >= col, -sd * d_cur, NEG)
            masked = jnp.full((BLK, BLK), NEG, F32)
            bias_ref[p, h, 0, :, BLK:] = cur
            bias_ref[p, h, 1, :, BLK:] = cur
            bias_ref[p, h, 2, :, BLK:] = cur if n_keys(p) == BLK else masked
            if n_keys(p) > BLK:
                bias_ref[p, h, 0, :, :BLK] = jnp.where(col >= row, -sd * d_prev, NEG)
                bias_ref[p, h, 1, :, :BLK] = masked
                bias_ref[p, h, 2, :, :BLK] = cur

    assert n_pat == 3
    run(probs(0))
    run(values(0), probs(1))
    run(values(1), probs(2))
    run(values(2))

    rows_per_step = 2 * BLK

    def natural_rows(ref, h, p, i):
        dil = DILATIONS[p]
        if not pitch(dil):
            return ref[h, p, pl.ds(pl.multiple_of(i * rows_per_step, rows_per_step), rows_per_step), :]
        per = rows_per_step // dil
        return jnp.concatenate(
            [ref[h, p, pl.ds(i * per + u, dil, stride=pitch(dil)), :] for u in range(per)], axis=0)

    def merge(i, carry):
        sl = pl.ds(pl.multiple_of(i * rows_per_step, rows_per_step), rows_per_step)
        nums = []
        for h in range(2):
            ms = [natural_rows(m_ref, h, p, i) for p in range(n_pat)]
            mx = functools.reduce(jnp.maximum, ms)
            nums.append(sum(jnp.exp2(ms[p] - mx) * natural_rows(acc_ref, h, p, i) for p in range(n_pat)))
        acc = jnp.where(head_lanes[0], nums[0], nums[1])
        den = pltpu.roll(jnp.where(head_lanes[0], nums[1], nums[0]), HEAD_DIM, axis=1)
        o_ref[sl, :] = acc / den
        return carry

    lax.fori_loop(0, seq // rows_per_step, merge, 0, unroll=4)


def _attn_a(slopes, qkv_layouts, unroll=16):
    B, T, _ = qkv_layouts[0].shape
    n_pairs = N_HEADS_A // 2
    n_pat = len(DILATIONS)

    def spec(off):
        return pl.BlockSpec((None, T, LANES), lambda b, h: (b, 0, off + h))

    return pl.pallas_call(
        functools.partial(_attn_a_kernel, seq=T, unroll=unroll),
        grid=(B, n_pairs),
        in_specs=[pl.BlockSpec(memory_space=pltpu.SMEM)]
        + [spec(part * n_pairs) for _ in range(n_pat) for part in range(3)],
        out_specs=pl.BlockSpec((None, T, LANES), lambda b, h: (b, 0, h)),
        out_shape=jax.ShapeDtypeStruct((B, T, W_A), F32),
        scratch_shapes=[pltpu.VMEM((2, n_pat, T + max(DILATIONS), LANES), F32),
                        pltpu.VMEM((2, n_pat, T + max(DILATIONS), LANES), F32),
                        pltpu.VMEM((n_pat, 2, 3, BLK, 2 * BLK), F32),
                        pltpu.VMEM((2, 2, T // BLK, BLK, 2 * BLK), BF16)],
        compiler_params=pltpu.CompilerParams(
            dimension_semantics=("parallel", "parallel"), vmem_limit_bytes=VMEM_LIMIT),
        name="attn_a",
    )(slopes, *[a for a in qkv_layouts for _ in range(3)])


def _bf16_pair(c):
    hi = float(np.asarray(c, np.float32).astype(BF16).astype(np.float32))
    lo = float(np.asarray(c - hi, np.float32).astype(BF16).astype(np.float32))
    return hi, lo


def _fold(x, op):
    return functools.reduce(op, [x[r:r + 8] for r in range(0, x.shape[0], 8)])


def _dsa_row(j, kk_ref, tq_ref, vt_ref, wt_ref, o_ref, sc_ref, s_ref, acc_ref, kpos_ref, xb_ref,
             *, seq, k_sel, snap_unroll, slopes):
    n_pairs = j + 1
    t_lane = j * DSA_Q + lax.broadcasted_iota(I32, (1, DSA_Q), 1)
    row = lax.broadcasted_iota(I32, (BLK, DSA_Q), 0)
    ws = wt_ref[0:N_IDX_HEADS, :] * (N_IDX_HEADS ** -0.5)
    inf = float("inf")
    few = t_lane < k_sel

    def rows(c):
        return pl.ds(pl.multiple_of(c * BLK, BLK), BLK)

    def group(fn, first, n, carry):
        for u in range(n):
            carry = fn(first + u, carry)
        return carry

    def each_chunk(fn, init, widest=4):
        carry, done, width = init, 0, widest
        while width >= 2:
            steps = (2 * n_pairs - done) // width
            carry = lax.fori_loop(
                0, steps, lambda i, c, done=done, width=width: group(fn, done + width * i, width, c), carry)
            done, width = done + steps * width, width // 2
        return carry

    def sub_reduce(x, op):
        return op(x, axis=0, keepdims=True)

    def score_chunk(c, carry):
        mn, mx = carry
        ki = kk_ref[rows(c), R_KI - R_KB:R_KI - R_KB + IDX_DIM]
        lgs = [_dot(ki, tq_ref[T_QI + h * IDX_DIM:T_QI + (h + 1) * IDX_DIM, :])
               for h in range(N_IDX_HEADS)]
        sc = functools.reduce(
            lambda a, b: a + b, [jnp.maximum(lg, 0.0) * ws[h:h + 1, :] for h, lg in enumerate(lgs)])
        causal = (c * BLK + row) <= t_lane
        sc_ref[rows(c), :] = jnp.where(causal, sc, -inf)
        xb_ref[rows(c), :] = jnp.where(causal, sc, -inf).astype(BF16)
        mn = jnp.minimum(mn, _fold(jnp.where(causal, sc, inf), jnp.minimum))
        mx = jnp.maximum(mx, _fold(jnp.where(causal, sc, -inf), jnp.maximum))
        return mn, mx

    score_init = (jnp.full((8, DSA_Q), inf, F32), jnp.full((8, DSA_Q), -inf, F32))

    def scores():
        return each_chunk(score_chunk, score_init, widest=8)

    def count(pred):
        def body(c, cnt):
            return cnt + _fold(pred(sc_ref[rows(c), :], c).astype(I32), jnp.add)
        return sub_reduce(each_chunk(body, jnp.zeros((8, DSA_Q), I32)), jnp.sum)

    def halve(_, bounds):
        lo, hi, n_lo, n_hi = bounds
        mid = 0.5 * lo + 0.5 * hi
        cnt = count(lambda x, c: x >= mid)
        active = lo < hi
        up = active & (cnt >= k_sel)
        down = active & (cnt <= k_sel)
        return (jnp.where(up, mid, lo), jnp.where(down, mid, hi),
                jnp.where(up, cnt, n_lo), jnp.where(down, cnt, n_hi))

    def coarse(_, bounds):
        lo, hi = bounds
        mid = (0.5 * lo + 0.5 * hi).astype(BF16)

        def body(c, cnt):
            hit = jnp.where(xb_ref[rows(c), :] >= mid, jnp.ones((), BF16), jnp.zeros((), BF16))
            return cnt + functools.reduce(lambda a, b: a + b, [hit[r:r + 16] for r in range(0, BLK, 16)])

        cnt = each_chunk(body, jnp.zeros((16, DSA_Q), BF16))
        cnt = jnp.sum(cnt.astype(F32), axis=0, keepdims=True)
        mid = mid.astype(F32)
        active = lo < hi
        enough = cnt >= k_sel
        margin = jnp.abs(mid) * 2.0 ** -7 + 1e-30
        return (jnp.where(active & enough, jnp.maximum(lo, mid - margin), lo),
                jnp.where(active & jnp.logical_not(enough), jnp.minimum(hi, mid), hi))

    def snap(bounds):
        lo, hi, n_lo, n_hi = bounds
        mid = 0.5 * lo + 0.5 * hi
        mid = jnp.where(mid > lo, mid, hi)

        def body(c, carry):
            cnt, above, below = carry
            x = sc_ref[rows(c), :]
            ge = x >= mid
            return (cnt + _fold(ge.astype(I32), jnp.add),
                    jnp.minimum(above, _fold(jnp.where(ge, x, inf), jnp.minimum)),
                    jnp.maximum(below, _fold(jnp.where(ge, -inf, x), jnp.maximum)))

        cnt, above, below = each_chunk(body, (jnp.zeros((8, DSA_Q), I32),
                                              jnp.full((8, DSA_Q), inf, F32),
                                              jnp.full((8, DSA_Q), -inf, F32)))
        cnt = sub_reduce(cnt, jnp.sum)
        above = sub_reduce(above, jnp.min)
        below = sub_reduce(below, jnp.max)
        active = lo < hi
        enough = cnt >= k_sel
        up = active & enough
        down = active & (cnt <= k_sel)
        return (jnp.where(up, above, lo), jnp.where(down, jnp.where(enough, above, below), hi),
                jnp.where(up, cnt, n_lo), jnp.where(down, cnt, n_hi))

    def snaps(bounds):
        for _ in range(snap_unroll):
            bounds = snap(bounds)
        return bounds

    def unsettled(bounds):
        return jnp.max((bounds[0] < bounds[1]).astype(F32)) > 0

    n_idx_bits = seq.bit_length() - 1
    assert 1 << n_idx_bits == seq

    def select(extremes):
        mn, mx = extremes
        lo = jnp.where(few, -inf, sub_reduce(mn, jnp.min))
        hi = jnp.where(few, -inf, sub_reduce(mx, jnp.max))
        lo, hi = lax.fori_loop(0, DSA_COARSE, coarse, (lo, hi))

        def recount(c, carry):
            x = sc_ref[rows(c), :]
            return (carry[0] + _fold((x >= lo).astype(I32), jnp.add),
                    carry[1] + _fold((x > hi).astype(I32), jnp.add))

        n_lo, n_hi = each_chunk(recount, (jnp.zeros((8, DSA_Q), I32),) * 2)
        bounds = (lo, hi, sub_reduce(n_lo, jnp.sum), sub_reduce(n_hi, jnp.sum))
        bounds = lax.fori_loop(0, DSA_HALVINGS, halve, bounds)

        def state(bounds):
            lo, hi, n_lo, _ = bounds
            tied = (n_lo > k_sel) & jnp.logical_not(few)
            return jnp.max(jnp.where(lo < hi, 2.0, tied.astype(F32)))

        def snap_more(bounds):
            bounds = lax.while_loop(unsettled, snaps, bounds)
            return bounds + (state(bounds),)

        bounds = snaps(bounds)
        code = state(bounds)
        tau, _, n_ge, n_gt, code = lax.cond(code >= 2, lambda: snap_more(bounds), lambda: bounds + (code,))
        need = k_sel - n_gt
        tied = (n_ge > k_sel) & jnp.logical_not(few)

        def tie_break():
            def idx_step(it, jp):
                cand = jp + lax.shift_left(jnp.int32(1), n_idx_bits - 1 - it)
                below = count(lambda x, c: (x == tau) & ((c * BLK + row) < cand))
                return jnp.where(below < need, cand, jp)
            last = lax.fori_loop(0, n_idx_bits, idx_step, jnp.zeros((1, DSA_Q), I32))
            return jnp.where(tied, last, seq)

        surplus = code == 1
        tie_last = lax.cond(surplus, tie_break, lambda: jnp.full((1, DSA_Q), seq, I32))
        return tau, tie_last, surplus

    def qk(selection):
        tau, tie_last, surplus = selection

        @pl.when(j == 0)
        def _():
            pos = lax.broadcasted_iota(I32, (seq, LANES), 0)
            ln = lax.broadcasted_iota(I32, (seq, LANES), 1) - HEAD_DIM
            feat = jnp.where(ln == 0, pos // POS_SPLIT,
                             jnp.where(ln == 1, pos % POS_SPLIT,
                                       jnp.where(ln == 2, pos // POS_SPLIT,
                                                 jnp.where(ln == 3, pos % POS_SPLIT, 0))))
            kpos_ref[...] = kk_ref[:, 0:LANES] + feat.astype(BF16)

        q_row = lax.broadcasted_iota(I32, (HEAD_DIM, DSA_Q), 0)
        q_aug = []
        for h in range(N_HEADS_B):
            c_hi, c_lo = _bf16_pair(slopes[h] * LOG2E)
            feat = jnp.where(q_row == 0, POS_SPLIT * c_hi,
                             jnp.where(q_row == 1, c_hi,
                                       jnp.where(q_row == 2, POS_SPLIT * c_lo,
                                                 jnp.where(q_row == 3, c_lo, 0.0))))
            q_aug.append(jnp.concatenate(
                [tq_ref[T_QB + h * HEAD_DIM:T_QB + (h + 1) * HEAD_DIM, :], feat.astype(BF16)], axis=0))

        tau_all = jnp.where(few, float(jnp.finfo(F32).min), tau)

        def qk_phase(with_ties):
            def qk_chunk(c, ms):
                kb = kpos_ref[rows(c), :]
                x = sc_ref[rows(c), :]
                if with_ties:
                    s_idx = c * BLK + row
                    tie = jnp.where(x == tau, jnp.where(s_idx <= tie_last, 0.0, NEG), NEG)
                    sb = jnp.where(s_idx <= t_lane, jnp.where(x > tau, 0.0, tie), NEG)
                else:
                    sb = jnp.where(x >= tau_all, 0.0, NEG)
                out = []
                for h in range(N_HEADS_B):
                    s = _dot(kb, q_aug[h]) + sb
                    s_ref[h, rows(c), :] = s
                    out.append(jnp.maximum(ms[h], _fold(s, jnp.maximum)))
                return tuple(out)

            return each_chunk(qk_chunk, tuple(jnp.full((8, DSA_Q), NEG, F32) for _ in range(N_HEADS_B)),
                              widest=8)

        ms = lax.cond(surplus, lambda: qk_phase(True), lambda: qk_phase(False))
        acc_ref[...] = jnp.zeros((W_B, DSA_Q), F32)
        return tuple(sub_reduce(m, jnp.max) for m in ms)

    def exp_pv(ms, first_pair, n, ls):
        ls = list(ls)
        pvs = []
        for u in range(n):
            pair = first_pair + u
            ps = []
            for h in range(N_HEADS_B):
                halves = []
                for c in (2 * pair, 2 * pair + 1):
                    p = jnp.exp2(s_ref[h, rows(c), :] - ms[h])
                    ls[h] = ls[h] + _fold(p, jnp.add)
                    halves.append(p.astype(BF16))
                ps.append(jnp.concatenate(halves, axis=0))
            vt = vt_ref[:, pl.ds(pl.multiple_of(pair * 2 * BLK, 2 * BLK), 2 * BLK)]
            pvs.append([_dot(vt, ps[h]) for h in range(N_HEADS_B)])
        for h in range(N_HEADS_B):
            acc_ref[h * HEAD_DIM:(h + 1) * HEAD_DIM, :] += functools.reduce(
                lambda a, b: a + b, [pv[h] for pv in pvs])
        return tuple(ls)

    exp_init = tuple(jnp.zeros((8, DSA_Q), F32) for _ in range(N_HEADS_B))

    def finish(ls):
        l_all = jnp.concatenate(
            [jnp.broadcast_to(sub_reduce(ls[h], jnp.sum), (HEAD_DIM, DSA_Q)) for h in range(N_HEADS_B)],
            axis=0)
        o_ref[...] = (acc_ref[...] / l_all).T

    return types.SimpleNamespace(scores=scores, select=select, qk=qk, exp_pv=exp_pv, finish=finish,
                                 score_chunk=score_chunk, score_init=score_init, exp_init=exp_init,
                                 group=group, n_pairs=n_pairs)


def _dsa_kernel(kk_ref, tq_ref, vt_ref, wt_ref, o_ref, sc_ref, s_ref, acc_ref, kpos_ref, xb_ref, **static):
    j = pl.program_id(1)
    a, b = (_dsa_row(j, *(ref.at[i] for ref in (kk_ref, tq_ref, vt_ref, wt_ref, o_ref, sc_ref, s_ref,
                                                 acc_ref, kpos_ref, xb_ref)), **static) for i in range(2))
    ms = a.qk(a.select(a.scores()))

    def both(first_pair, n_pair, carry):
        ls, extremes = carry
        ls = a.exp_pv(ms, first_pair, n_pair, ls)
        return ls, b.group(b.score_chunk, 2 * first_pair, 2 * n_pair, extremes)

    carry = (a.exp_init, b.score_init)
    carry = lax.fori_loop(0, a.n_pairs // 2, lambda i, c: both(2 * i, 2, c), carry)
    ls, extremes = lax.fori_loop(0, a.n_pairs % 2, lambda i, c: both(a.n_pairs - 1, 1, c), carry)
    a.finish(ls)

    ms = b.qk(b.select(extremes))
    ls = b.exp_init
    ls = lax.fori_loop(0, b.n_pairs // 2, lambda i, c: b.exp_pv(ms, 2 * i, 2, c), ls)
    ls = lax.fori_loop(0, b.n_pairs % 2, lambda i, c: b.exp_pv(ms, b.n_pairs - 1, 1, c), ls)
    b.finish(ls)


def _dsa(proj_r, proj_t, w_t):
    B, T, _ = proj_r.shape
    k_sel = min(TOPK_MAX, T // 4)
    slopes = _static_alibi_slopes(N_HEADS_B)
    assert T // POS_SPLIT <= 256 and POS_SPLIT <= 256 and B % 2 == 0
    return pl.pallas_call(
        functools.partial(_dsa_kernel, seq=T, k_sel=k_sel, snap_unroll=4, slopes=slopes),
        grid=(B // 2, T // DSA_Q),
        in_specs=[
            pl.BlockSpec((2, T, 2 * LANES), lambda b, q: (b, 0, R_KB // (2 * LANES))),
            pl.BlockSpec((2, T_VB, DSA_Q), lambda b, q: (b, 0, q)),
            pl.BlockSpec((2, HEAD_DIM, T), lambda b, q: (b, T_VB // HEAD_DIM, 0)),
            pl.BlockSpec((2, WI_ROWS, DSA_Q), lambda b, q: (b, 0, q)),
        ],
        out_specs=pl.BlockSpec((2, DSA_Q, W_B), lambda b, q: (b, q, 0)),
        out_shape=jax.ShapeDtypeStruct((B, T, W_B), F32),
        scratch_shapes=[pltpu.VMEM((2, T, DSA_Q), F32),
                        pltpu.VMEM((2, N_HEADS_B, T, DSA_Q), F32),
                        pltpu.VMEM((2, W_B, DSA_Q), F32),
                        pltpu.VMEM((2, T, LANES), BF16),
                        pltpu.VMEM((2, T, DSA_Q), BF16)],
        compiler_params=pltpu.CompilerParams(
            dimension_semantics=("parallel", "arbitrary"), vmem_limit_bytes=VMEM_LIMIT),
        name="dsa",
    )(proj_r, proj_t, proj_t, w_t)


def _out_kernel(x_ref, oa_ref, ob_ref, gate_ref, qm_ref, mem_ref, gm_ref, wm_ref, wo_ref, g_ref, y_ref,
                mix_ref, kv_ref, *, sub):
    tm = x_ref.shape[0]

    @pl.when(pl.program_id(1) == 0)
    def _():
        kv_ref[...] = _dot(_rms(mem_ref[...], gm_ref[...]).astype(BF16), wm_ref[...]).astype(BF16)

    subs = [slice(r0, r0 + sub) for r0 in range(0, tm, sub)]
    lane = lax.broadcasted_iota(I32, (1, W_M), 1)
    head_lanes = [(lane >= h * HEAD_DIM) & (lane < (h + 1) * HEAD_DIM) for h in range(N_HEADS_MEM)]
    km = kv_ref[:, 0:W_M]
    vm = kv_ref[:, W_M:2 * W_M]
    zero = jnp.zeros((), BF16)
    vm_heads = [jnp.where(head_lanes[h], vm, zero) for h in range(N_HEADS_MEM)]

    def silu(g):
        return g * jax.nn.sigmoid(g)

    def gated(o, rows, lo, width):
        return (o * silu(gate_ref[rows, lo:lo + width].astype(F32))).astype(BF16)

    for rows in subs:
        qm = qm_ref[rows, :]
        om = None
        for h in range(N_HEADS_MEM):
            s = _dot_nt(jnp.where(head_lanes[h], qm, zero), km)
            p = jnp.exp2(s - jnp.max(s, axis=1, keepdims=True))
            p = p / jnp.sum(p, axis=1, keepdims=True)
            o = _dot(p.astype(BF16), vm_heads[h])
            om = o if om is None else om + o
        mix_ref[rows, W_A + W_B:] = gated(om, rows, W_A + W_B, W_M)

    for rows in subs:
        half = W_A // 2
        mix_ref[rows, 0:half] = gated(oa_ref[rows, 0:half], rows, 0, half)
        mix_ref[rows, half:W_A] = gated(oa_ref[rows, half:W_A], rows, half, half)
        mix_ref[rows, W_A:W_A + W_B] = gated(ob_ref[rows, :], rows, W_A, W_B)
        y_ref[rows, :] = _dot(mix_ref[rows, :], wo_ref[...])

    for rows in subs:
        y_ref[rows, :] = _rms(x_ref[rows, :] + y_ref[rows, :], g_ref[...])


def _out(x, o_a, o_b, proj_r, mem, g_mem, w_mem, w_out, g, tm=1024, sub=256):
    B, T, D = x.shape
    M = mem.shape[1]
    row = lambda b, i: (b, i, 0)
    return pl.pallas_call(
        functools.partial(_out_kernel, sub=sub),
        grid=(B, T // tm),
        in_specs=[
            pl.BlockSpec((None, tm, D), row),
            pl.BlockSpec((None, tm, W_A), row),
            pl.BlockSpec((None, tm, W_B), row),
            pl.BlockSpec((None, tm, MIX_WIDTH), lambda b, i: (b, i, R_GATE // MIX_WIDTH)),
            pl.BlockSpec((None, tm, W_M), lambda b, i: (b, i, R_QM // W_M)),
            pl.BlockSpec((None, M, D), lambda b, i: (b, 0, 0)),
            pl.BlockSpec((1, D), lambda b, i: (0, 0)),
            pl.BlockSpec(w_mem.shape, lambda b, i: (0, 0)),
            pl.BlockSpec(w_out.shape, lambda b, i: (0, 0)),
            pl.BlockSpec((1, D), lambda b, i: (0, 0)),
        ],
        out_specs=pl.BlockSpec((None, tm, D), row),
        out_shape=jax.ShapeDtypeStruct((B, T, D), F32),
        scratch_shapes=[pltpu.VMEM((tm, MIX_WIDTH), BF16), pltpu.VMEM((M, 2 * W_M), BF16)],
        compiler_params=pltpu.CompilerParams(
            dimension_semantics=("parallel", "arbitrary"), vmem_limit_bytes=VMEM_LIMIT),
        name="out",
    )(x, o_a, o_b, proj_r, proj_r, mem, g_mem, w_mem, w_out, g)


def _alibi_slopes(n):
    return 2.0 ** (-8.0 * jnp.arange(1, n + 1, dtype=F32) / n)


def _static_alibi_slopes(n):
    return tuple(2.0 ** (-8.0 * i / n) for i in range(1, n + 1))


def _wprep_kernel(w_ref, wa_ref, wr_ref, wt_ref, *, bounds):
    q_a, k_a, v_a, q_b, k_b, v_b, q_m, gate, q_i, k_i, w_i = bounds[:-1]
    scale = HEAD_DIM ** -0.5 * LOG2E
    idx_scale = IDX_DIM ** -0.5
    rows = w_ref.shape[0]
    wa_ref[:, 0:W_A] = (w_ref[:, q_a:k_a] * scale).astype(BF16)
    wa_ref[:, W_A:3 * W_A] = w_ref[:, k_a:q_b].astype(BF16)
    wr_ref[:, R_GATE:R_GATE + MIX_WIDTH] = w_ref[:, gate:q_i].astype(BF16)
    wr_ref[:, R_QM:R_QM + W_M] = (w_ref[:, q_m:gate] * scale).astype(BF16)
    zeros = jnp.zeros((rows, LANES - HEAD_DIM), BF16)
    wr_ref[:, R_KB:R_KB + HEAD_DIM] = w_ref[:, k_b:v_b].astype(BF16)
    wr_ref[:, R_KB + HEAD_DIM:R_KB + LANES] = zeros
    wr_ref[:, R_KI:R_KI + IDX_DIM] = w_ref[:, k_i:w_i].astype(BF16)
    wr_ref[:, R_KI + IDX_DIM:R_KI + LANES] = zeros
    wt_ref[T_QI:T_QB, :] = (w_ref[:, q_i:k_i] * idx_scale).T.astype(BF16)
    wt_ref[T_QB:T_VB, :] = (w_ref[:, q_b:k_b] * scale).T.astype(BF16)
    wt_ref[T_VB:T_ROWS, :] = w_ref[:, k_b:q_m].T[HEAD_DIM:, :].astype(BF16)


def _split_weights(w):
    D = w.shape[0]
    rows = LANES
    bounds = [0]
    for s in SPLIT_SIZES:
        bounds.append(bounds[-1] + s)
    wa, wr, wt = pl.pallas_call(
        functools.partial(_wprep_kernel, bounds=tuple(bounds)),
        grid=(D // rows,),
        in_specs=[pl.BlockSpec((rows, w.shape[1]), lambda i: (i, 0))],
        out_specs=[
            pl.BlockSpec((rows, 3 * W_A), lambda i: (i, 0)),
            pl.BlockSpec((rows, R_COLS), lambda i: (i, 0)),
            pl.BlockSpec((T_ROWS, rows), lambda i: (0, i)),
        ],
        out_shape=[
            jax.ShapeDtypeStruct((D, 3 * W_A), BF16),
            jax.ShapeDtypeStruct((D, R_COLS), BF16),
            jax.ShapeDtypeStruct((T_ROWS, D), BF16),
        ],
        compiler_params=pltpu.CompilerParams(
            dimension_semantics=("parallel",), vmem_limit_bytes=VMEM_LIMIT),
        name="wprep",
    )(w)
    w_i = w[:, bounds[-2]:bounds[-1]]
    ww = jnp.concatenate([w_i.T, jnp.zeros((WI_ROWS - N_IDX_HEADS, D), w.dtype)], axis=0)
    return wa, wr, wt, ww.astype(BF16)


def kernel(x, mem, g_in, g_mem, w_in, w_mem_kv, w_out, g_final):
    assert g_in.shape[0] == 1, "single-layer block: the final RMSNorm is fused into the output kernel"
    wa, wr, wt, ww = _split_weights(w_in[0])
    qkv_a, proj_r, proj_t, w_t = _proj(x, g_in, wa, wr, wt, ww)
    o_a = _attn_a(_alibi_slopes(N_HEADS_A), qkv_a)
    o_b = _dsa(proj_r, proj_t, w_t)
    return _out(x, o_a, o_b, proj_r, mem, g_mem, w_mem_kv[0].astype(BF16), w_out[0].astype(BF16),
                g_final[None, :])
```

```python
import functools
import math
import types

import numpy as np

import jax
import jax.numpy as jnp
from jax import lax
from jax.experimental import pallas as pl
from jax.experimental.pallas import tpu as pltpu

F32 = jnp.float32
BF16 = jnp.bfloat16
I32 = jnp.int32

D_MODEL = 1024
HEAD_DIM = 64
N_HEADS_A = 8
N_HEADS_B = 4
N_HEADS_MEM = 4
W_A = N_HEADS_A * HEAD_DIM
W_B = N_HEADS_B * HEAD_DIM
W_M = N_HEADS_MEM * HEAD_DIM
MIX_WIDTH = W_A + W_B + W_M
DILATIONS = (1, 4, 16)
BAND = 128
N_IDX_HEADS = 8
IDX_DIM = 64
TOPK_MAX = 256
RMS_EPS = 1e-6
SPLIT_SIZES = (W_A, W_A, W_A, W_B, HEAD_DIM, HEAD_DIM, W_M, MIX_WIDTH,
               N_IDX_HEADS * IDX_DIM, IDX_DIM, N_IDX_HEADS)

LANES = 128
BLK = 128
NEG = -1e30
VMEM_LIMIT = 48 * 1024 * 1024

R_GATE = 0
R_QM = MIX_WIDTH
R_KB = R_QM + W_M
R_KI = R_KB + LANES
R_COLS = R_KI + LANES
T_QI = 0
T_QB = N_IDX_HEADS * IDX_DIM
T_VB = T_QB + W_B
T_ROWS = T_VB + HEAD_DIM
WI_ROWS = 16
DSA_Q = 256
DSA_COARSE = 10
DSA_HALVINGS = 3
POS_SPLIT = 64
LOG2E = math.log2(math.e)


def _dot(a, b):
    return jnp.dot(a, b, preferred_element_type=F32)


def _dot_nt(a, b):
    return lax.dot_general(a, b, (((1,), (1,)), ((), ())), preferred_element_type=F32)


def _rms(x, g):
    return x * lax.rsqrt(jnp.mean(x * x, axis=-1, keepdims=True) + RMS_EPS) * g


def _proj_kernel(x_ref, g_ref, wa_ref, wr_ref, wt_ref, ww_ref,
                 oa_ref, oa4_ref, oa16_ref, or_ref, ot_ref, ow_ref, res_ref, res4_ref):
    hb = _rms(x_ref[0], g_ref[...]).astype(BF16)
    tm = hb.shape[0]
    d4, d16 = DILATIONS[1], DILATIONS[2]
    step = d16 // d4
    res = _dot(hb, wa_ref[...])
    oa_ref[0] = res.astype(BF16)
    for grp in range(res.shape[1] // LANES):
        lanes = slice(grp * LANES, (grp + 1) * LANES)
        res_ref[grp] = res[:, lanes]
        for r in range(d4):
            rows4 = res_ref[grp, pl.ds(r, tm // d4, stride=d4), :]
            oa4_ref[0, r, :, lanes] = rows4.astype(BF16)
            res4_ref[grp, r * (tm // d4):(r + 1) * (tm // d4), :] = rows4
        for r in range(d16):
            start = (r % d4) * (tm // d4) + r // d4
            oa16_ref[0, r, :, lanes] = res4_ref[grp, pl.ds(start, tm // d16, stride=step), :].astype(BF16)
    or_ref[0] = _dot(hb, wr_ref[...]).astype(BF16)
    ot_ref[0] = _dot_nt(wt_ref[...], hb).astype(BF16)
    ow_ref[0] = _dot_nt(ww_ref[...], hb)


def _proj(x, g, wa, wr, wt, ww, tm=512):
    B, T, D = x.shape
    const = lambda b, i: (0, 0)
    d4, d16 = DILATIONS[1], DILATIONS[2]
    nat, cm4, cm16, proj_r, proj_t, w_t = pl.pallas_call(
        _proj_kernel,
        grid=(B, T // tm),
        in_specs=[
            pl.BlockSpec((1, tm, D), lambda b, i: (b, i, 0)),
            pl.BlockSpec((1, D), const),
            pl.BlockSpec(wa.shape, const),
            pl.BlockSpec(wr.shape, const),
            pl.BlockSpec(wt.shape, const),
            pl.BlockSpec(ww.shape, const),
        ],
        out_specs=[
            pl.BlockSpec((1, tm, 3 * W_A), lambda b, i: (b, i, 0)),
            pl.BlockSpec((1, d4, tm // d4, 3 * W_A), lambda b, i: (b, 0, i, 0)),
            pl.BlockSpec((1, d16, tm // d16, 3 * W_A), lambda b, i: (b, 0, i, 0)),
            pl.BlockSpec((1, tm, R_COLS), lambda b, i: (b, i, 0)),
            pl.BlockSpec((1, T_ROWS, tm), lambda b, i: (b, 0, i)),
            pl.BlockSpec((1, WI_ROWS, tm), lambda b, i: (b, 0, i)),
        ],
        out_shape=[
            jax.ShapeDtypeStruct((B, T, 3 * W_A), BF16),
            jax.ShapeDtypeStruct((B, d4, T // d4, 3 * W_A), BF16),
            jax.ShapeDtypeStruct((B, d16, T // d16, 3 * W_A), BF16),
            jax.ShapeDtypeStruct((B, T, R_COLS), BF16),
            jax.ShapeDtypeStruct((B, T_ROWS, T), BF16),
            jax.ShapeDtypeStruct((B, WI_ROWS, T), F32),
        ],
        scratch_shapes=[pltpu.VMEM((3 * W_A // LANES, tm, LANES), F32)] * 2,
        compiler_params=pltpu.CompilerParams(
            dimension_semantics=("parallel", "parallel"), vmem_limit_bytes=VMEM_LIMIT),
        name="proj",
    )(x, g, wa, wr, wt, ww)
    return (nat, cm4.reshape(nat.shape), cm16.reshape(nat.shape)), proj_r, proj_t, w_t


def _attn_a_kernel(slopes_ref, *refs, seq, unroll):
    n_pat = len(DILATIONS)
    qkv = [refs[3 * p:3 * p + 3] for p in range(n_pat)]
    o_ref, acc_ref, m_ref, bias_ref, p_ref = refs[3 * n_pat:]
    hp = pl.program_id(1)
    lane = lax.broadcasted_iota(I32, (1, LANES), 1)
    head_lanes = (lane < HEAD_DIM, lane >= HEAD_DIM)
    row = lax.broadcasted_iota(I32, (BLK, BLK), 0)
    col = lax.broadcasted_iota(I32, (BLK, BLK), 1)
    d_cur = (row - col).astype(F32)
    d_prev = d_cur + float(BAND)
    n_all = seq // BLK
    zero = jnp.zeros((), BF16)
    one = jnp.ones((), BF16)

    def n_blocks(p):
        return n_all // DILATIONS[p]

    def n_keys(p):
        return BLK if n_blocks(p) == 1 else 2 * BLK

    def block(g):
        return pl.ds(pl.multiple_of(g * BLK, BLK), BLK)

    def keys(p, g):
        return pl.ds(pl.multiple_of(jnp.maximum((g + 1) * BLK - n_keys(p), 0), BLK), n_keys(p))

    def variant(p, g):
        g = jnp.asarray(g, I32)
        return jnp.where(g == 0, 2, jnp.where(g % n_blocks(p) == 0, 1, 0))

    def pitch(dil):
        return seq // dil + 1 if seq // dil == BLK and dil > 1 else None

    def put(ref, h, p, g, val):
        dil = DILATIONS[p]
        start = g // n_blocks(p) + (g % n_blocks(p)) * (dil * BLK)
        if dil == 1:
            ref[h, p, pl.ds(pl.multiple_of(start, BLK), BLK), :] = val
        elif pitch(dil):
            ref[h, p, pl.ds(start * pitch(dil), BLK), :] = val
        else:
            ref[h, p, pl.ds(start, BLK, stride=dil), :] = val

    def probs(p):
        q_ref, k_ref, _ = qkv[p]

        def body(g):
            kw = k_ref[keys(p, g), :]
            q = q_ref[block(g), :]
            for h in range(2):
                s = _dot_nt(jnp.where(head_lanes[h], q, zero), kw)
                s = s + bias_ref[p, h, variant(p, g), :, 2 * BLK - n_keys(p):]
                m = jnp.max(s, axis=1, keepdims=True)
                p_ref[p % 2, h, g, :, :n_keys(p)] = jnp.exp2(s - m).astype(BF16)
                put(m_ref, h, p, g, jnp.broadcast_to(m, (BLK, LANES)))
        return body

    def values(p):
        v_ref = qkv[p][2]

        def body(g):
            vw = v_ref[keys(p, g), :]
            for h in range(2):
                put(acc_ref, h, p, g,
                    _dot(p_ref[p % 2, h, g, :, :n_keys(p)], jnp.where(head_lanes[h], vw, one)))
        return body

    def run(*bodies):
        def step(g, carry):
            for body in bodies:
                body(g)
            return carry
        lax.fori_loop(0, n_all, step, 0, unroll=unroll)

    for p, dil in enumerate(DILATIONS):
        for h in range(2):
            sd = slopes_ref[hp * 2 + h] * (float(dil) * LOG2E)
            cur = jnp.where(row >= col, -sd * d_cur, NEG)
            masked = jnp.full((BLK, BLK), NEG, F32)
            bias_ref[p, h, 0, :, BLK:] = cur
            bias_ref[p, h, 1, :, BLK:] = cur
            bias_ref[p, h, 2, :, BLK:] = cur if n_keys(p) == BLK else masked
            if n_keys(p) > BLK:
                bias_ref[p, h, 0, :, :BLK] = jnp.where(col >= row, -sd * d_prev, NEG)
                bias_ref[p, h, 1, :, :BLK] = masked
                bias_ref[p, h, 2, :, :BLK] = cur

    assert n_pat == 3
    run(probs(0))
    run(values(0), probs(1))
    run(values(1), probs(2))
    run(values(2))

    rows_per_step = 2 * BLK

    def natural_rows(ref, h, p, i):
        dil = DILATIONS[p]
        if not pitch(dil):
            return ref[h, p, pl.ds(pl.multiple_of(i * rows_per_step, rows_per_step), rows_per_step), :]
        per = rows_per_step // dil
        return jnp.concatenate(
            [ref[h, p, pl.ds(i * per + u, dil, stride=pitch(dil)), :] for u in range(per)], axis=0)

    def merge(i, carry):
        sl = pl.ds(pl.multiple_of(i * rows_per_step, rows_per_step), rows_per_step)
        nums = []
        for h in range(2):
            ms = [natural_rows(m_ref, h, p, i) for p in range(n_pat)]
            mx = functools.reduce(jnp.maximum, ms)
            nums.append(sum(jnp.exp2(ms[p] - mx) * natural_rows(acc_ref, h, p, i) for p in range(n_pat)))
        acc = jnp.where(head_lanes[0], nums[0], nums[1])
        den = pltpu.roll(jnp.where(head_lanes[0], nums[1], nums[0]), HEAD_DIM, axis=1)
        o_ref[sl, :] = acc / den
        return carry

    lax.fori_loop(0, seq // rows_per_step, merge, 0, unroll=4)


def _attn_a(slopes, qkv_layouts, unroll=16):
    B, T, _ = qkv_layouts[0].shape
    n_pairs = N_HEADS_A // 2
    n_pat = len(DILATIONS)

    def spec(off):
        return pl.BlockSpec((None, T, LANES), lambda b, h: (b, 0, off + h))

    return pl.pallas_call(
        functools.partial(_attn_a_kernel, seq=T, unroll=unroll),
        grid=(B, n_pairs),
        in_specs=[pl.BlockSpec(memory_space=pltpu.SMEM)]
        + [spec(part * n_pairs) for _ in range(n_pat) for part in range(3)],
        out_specs=pl.BlockSpec((None, T, LANES), lambda b, h: (b, 0, h)),
        out_shape=jax.ShapeDtypeStruct((B, T, W_A), F32),
        scratch_shapes=[pltpu.VMEM((2, n_pat, T + max(DILATIONS), LANES), F32),
                        pltpu.VMEM((2, n_pat, T + max(DILATIONS), LANES), F32),
                        pltpu.VMEM((n_pat, 2, 3, BLK, 2 * BLK), F32),
                        pltpu.VMEM((2, 2, T // BLK, BLK, 2 * BLK), BF16)],
        compiler_params=pltpu.CompilerParams(
            dimension_semantics=("parallel", "parallel"), vmem_limit_bytes=VMEM_LIMIT),
        name="attn_a",
    )(slopes, *[a for a in qkv_layouts for _ in range(3)])


def _bf16_pair(c):
    hi = float(np.asarray(c, np.float32).astype(BF16).astype(np.float32))
    lo = float(np.asarray(c - hi, np.float32).astype(BF16).astype(np.float32))
    return hi, lo


def _fold(x, op):
    return functools.reduce(op, [x[r:r + 8] for r in range(0, x.shape[0], 8)])


def _dsa_row(j, kk_ref, tq_ref, vt_ref, wt_ref, o_ref, sc_ref, s_ref, acc_ref, kpos_ref, xb_ref,
             *, seq, k_sel, snap_unroll, slopes):
    n_pairs = j + 1
    t_lane = j * DSA_Q + lax.broadcasted_iota(I32, (1, DSA_Q), 1)
    row = lax.broadcasted_iota(I32, (BLK, DSA_Q), 0)
    ws = wt_ref[0:N_IDX_HEADS, :] * (N_IDX_HEADS ** -0.5)
    inf = float("inf")
    few = t_lane < k_sel

    def rows(c):
        return pl.ds(pl.multiple_of(c * BLK, BLK), BLK)

    def group(fn, first, n, carry):
        for u in range(n):
            carry = fn(first + u, carry)
        return carry

    def each_chunk(fn, init, widest=4):
        carry, done, width = init, 0, widest
        while width >= 2:
            steps = (2 * n_pairs - done) // width
            carry = lax.fori_loop(
                0, steps, lambda i, c, done=done, width=width: group(fn, done + width * i, width, c), carry)
            done, width = done + steps * width, width // 2
        return carry

    def sub_reduce(x, op):
        return op(x, axis=0, keepdims=True)

    def score_chunk(c, carry):
        mn, mx = carry
        ki = kk_ref[rows(c), R_KI - R_KB:R_KI - R_KB + IDX_DIM]
        lgs = [_dot(ki, tq_ref[T_QI + h * IDX_DIM:T_QI + (h + 1) * IDX_DIM, :])
               for h in range(N_IDX_HEADS)]
        sc = functools.reduce(
            lambda a, b: a + b, [jnp.maximum(lg, 0.0) * ws[h:h + 1, :] for h, lg in enumerate(lgs)])
        causal = (c * BLK + row) <= t_lane
        sc_ref[rows(c), :] = jnp.where(causal, sc, -inf)
        xb_ref[rows(c), :] = jnp.where(causal, sc, -inf).astype(BF16)
        mn = jnp.minimum(mn, _fold(jnp.where(causal, sc, inf), jnp.minimum))
        mx = jnp.maximum(mx, _fold(jnp.where(causal, sc, -inf), jnp.maximum))
        return mn, mx

    score_init = (jnp.full((8, DSA_Q), inf, F32), jnp.full((8, DSA_Q), -inf, F32))

    def scores():
        return each_chunk(score_chunk, score_init, widest=8)

    def count(pred):
        def body(c, cnt):
            return cnt + _fold(pred(sc_ref[rows(c), :], c).astype(I32), jnp.add)
        return sub_reduce(each_chunk(body, jnp.zeros((8, DSA_Q), I32)), jnp.sum)

    def halve(_, bounds):
        lo, hi, n_lo, n_hi = bounds
        mid = 0.5 * lo + 0.5 * hi
        cnt = count(lambda x, c: x >= mid)
        active = lo < hi
        up = active & (cnt >= k_sel)
        down = active & (cnt <= k_sel)
        return (jnp.where(up, mid, lo), jnp.where(down, mid, hi),
                jnp.where(up, cnt, n_lo), jnp.where(down, cnt, n_hi))

    def coarse(_, bounds):
        lo, hi = bounds
        mid = (0.5 * lo + 0.5 * hi).astype(BF16)

        def body(c, cnt):
            hit = jnp.where(xb_ref[rows(c), :] >= mid, jnp.ones((), BF16), jnp.zeros((), BF16))
            return cnt + functools.reduce(lambda a, b: a + b, [hit[r:r + 16] for r in range(0, BLK, 16)])

        cnt = each_chunk(body, jnp.zeros((16, DSA_Q), BF16))
        cnt = jnp.sum(cnt.astype(F32), axis=0, keepdims=True)
        mid = mid.astype(F32)
        active = lo < hi
        enough = cnt >= k_sel
        margin = jnp.abs(mid) * 2.0 ** -7 + 1e-30
        return (jnp.where(active & enough, jnp.maximum(lo, mid - margin), lo),
                jnp.where(active & jnp.logical_not(enough), jnp.minimum(hi, mid), hi))

    def snap(bounds):
        lo, hi, n_lo, n_hi = bounds
        mid = 0.5 * lo + 0.5 * hi
        mid = jnp.where(mid > lo, mid, hi)

        def body(c, carry):
            cnt, above, below = carry
            x = sc_ref[rows(c), :]
            ge = x >= mid
            return (cnt + _fold(ge.astype(I32), jnp.add),
                    jnp.minimum(above, _fold(jnp.where(ge, x, inf), jnp.minimum)),
                    jnp.maximum(below, _fold(jnp.where(ge, -inf, x), jnp.maximum)))

        cnt, above, below = each_chunk(body, (jnp.zeros((8, DSA_Q), I32),
                                              jnp.full((8, DSA_Q), inf, F32),
                                              jnp.full((8, DSA_Q), -inf, F32)))
        cnt = sub_reduce(cnt, jnp.sum)
        above = sub_reduce(above, jnp.min)
        below = sub_reduce(below, jnp.max)
        active = lo < hi
        enough = cnt >= k_sel
        up = active & enough
        down = active & (cnt <= k_sel)
        return (jnp.where(up, above, lo), jnp.where(down, jnp.where(enough, above, below), hi),
                jnp.where(up, cnt, n_lo), jnp.where(down, cnt, n_hi))

    def snaps(bounds):
        for _ in range(snap_unroll):
            bounds = snap(bounds)
        return bounds

    def unsettled(bounds):
        return jnp.max((bounds[0] < bounds[1]).astype(F32)) > 0

    n_idx_bits = seq.bit_length() - 1
    assert 1 << n_idx_bits == seq

    def select(extremes):
        mn, mx = extremes
        lo = jnp.where(few, -inf, sub_reduce(mn, jnp.min))
        hi = jnp.where(few, -inf, sub_reduce(mx, jnp.max))
        lo, hi = lax.fori_loop(0, DSA_COARSE, coarse, (lo, hi))

        def recount(c, carry):
            x = sc_ref[rows(c), :]
            return (carry[0] + _fold((x >= lo).astype(I32), jnp.add),
                    carry[1] + _fold((x > hi).astype(I32), jnp.add))

        n_lo, n_hi = each_chunk(recount, (jnp.zeros((8, DSA_Q), I32),) * 2)
        bounds = (lo, hi, sub_reduce(n_lo, jnp.sum), sub_reduce(n_hi, jnp.sum))
        bounds = lax.fori_loop(0, DSA_HALVINGS, halve, bounds)

        def state(bounds):
            lo, hi, n_lo, _ = bounds
            tied = (n_lo > k_sel) & jnp.logical_not(few)
            return jnp.max(jnp.where(lo < hi, 2.0, tied.astype(F32)))

        def snap_more(bounds):
            bounds = lax.while_loop(unsettled, snaps, bounds)
            return bounds + (state(bounds),)

        bounds = snaps(bounds)
        code = state(bounds)
        tau, _, n_ge, n_gt, code = lax.cond(code >= 2, lambda: snap_more(bounds), lambda: bounds + (code,))
        need = k_sel - n_gt
        tied = (n_ge > k_sel) & jnp.logical_not(few)

        def tie_break():
            def idx_step(it, jp):
                cand = jp + lax.shift_left(jnp.int32(1), n_idx_bits - 1 - it)
                below = count(lambda x, c: (x == tau) & ((c * BLK + row) < cand))
                return jnp.where(below < need, cand, jp)
            last = lax.fori_loop(0, n_idx_bits, idx_step, jnp.zeros((1, DSA_Q), I32))
            return jnp.where(tied, last, seq)

        surplus = code == 1
        tie_last = lax.cond(surplus, tie_break, lambda: jnp.full((1, DSA_Q), seq, I32))
        return tau, tie_last, surplus

    def qk(selection):
        tau, tie_last, surplus = selection

        @pl.when(j == 0)
        def _():
            pos = lax.broadcasted_iota(I32, (seq, LANES), 0)
            ln = lax.broadcasted_iota(I32, (seq, LANES), 1) - HEAD_DIM
            feat = jnp.where(ln == 0, pos // POS_SPLIT,
                             jnp.where(ln == 1, pos % POS_SPLIT,
                                       jnp.where(ln == 2, pos // POS_SPLIT,
                                                 jnp.where(ln == 3, pos % POS_SPLIT, 0))))
            kpos_ref[...] = kk_ref[:, 0:LANES] + feat.astype(BF16)

        q_row = lax.broadcasted_iota(I32, (HEAD_DIM, DSA_Q), 0)
        q_aug = []
        for h in range(N_HEADS_B):
            c_hi, c_lo = _bf16_pair(slopes[h] * LOG2E)
            feat = jnp.where(q_row == 0, POS_SPLIT * c_hi,
                             jnp.where(q_row == 1, c_hi,
                                       jnp.where(q_row == 2, POS_SPLIT * c_lo,
                                                 jnp.where(q_row == 3, c_lo, 0.0))))
            q_aug.append(jnp.concatenate(
                [tq_ref[T_QB + h * HEAD_DIM:T_QB + (h + 1) * HEAD_DIM, :], feat.astype(BF16)], axis=0))

        tau_all = jnp.where(few, float(jnp.finfo(F32).min), tau)

        def qk_phase(with_ties):
            def qk_chunk(c, ms):
                kb = kpos_ref[rows(c), :]
                x = sc_ref[rows(c), :]
                if with_ties:
                    s_idx = c * BLK + row
                    tie = jnp.where(x == tau, jnp.where(s_idx <= tie_last, 0.0, NEG), NEG)
                    sb = jnp.where(s_idx <= t_lane, jnp.where(x > tau, 0.0, tie), NEG)
                else:
                    sb = jnp.where(x >= tau_all, 0.0, NEG)
                out = []
                for h in range(N_HEADS_B):
                    s = _dot(kb, q_aug[h]) + sb
                    s_ref[h, rows(c), :] = s
                    out.append(jnp.maximum(ms[h], _fold(s, jnp.maximum)))
                return tuple(out)

            return each_chunk(qk_chunk, tuple(jnp.full((8, DSA_Q), NEG, F32) for _ in range(N_HEADS_B)),
                              widest=8)

        ms = lax.cond(surplus, lambda: qk_phase(True), lambda: qk_phase(False))
        acc_ref[...] = jnp.zeros((W_B, DSA_Q), F32)
        return tuple(sub_reduce(m, jnp.max) for m in ms)

    def exp_pv(ms, first_pair, n, ls):
        ls = list(ls)
        pvs = []
        for u in range(n):
            pair = first_pair + u
            ps = []
            for h in range(N_HEADS_B):
                halves = []
                for c in (2 * pair, 2 * pair + 1):
                    p = jnp.exp2(s_ref[h, rows(c), :] - ms[h])
                    ls[h] = ls[h] + _fold(p, jnp.add)
                    halves.append(p.astype(BF16))
                ps.append(jnp.concatenate(halves, axis=0))
            vt = vt_ref[:, pl.ds(pl.multiple_of(pair * 2 * BLK, 2 * BLK), 2 * BLK)]
            pvs.append([_dot(vt, ps[h]) for h in range(N_HEADS_B)])
        for h in range(N_HEADS_B):
            acc_ref[h * HEAD_DIM:(h + 1) * HEAD_DIM, :] += functools.reduce(
                lambda a, b: a + b, [pv[h] for pv in pvs])
        return tuple(ls)

    exp_init = tuple(jnp.zeros((8, DSA_Q), F32) for _ in range(N_HEADS_B))

    def finish(ls):
        l_all = jnp.concatenate(
            [jnp.broadcast_to(sub_reduce(ls[h], jnp.sum), (HEAD_DIM, DSA_Q)) for h in range(N_HEADS_B)],
            axis=0)
        o_ref[...] = (acc_ref[...] / l_all).T

    return types.SimpleNamespace(scores=scores, select=select, qk=qk, exp_pv=exp_pv, finish=finish,
                                 score_chunk=score_chunk, score_init=score_init, exp_init=exp_init,
                                 group=group, n_pairs=n_pairs)


def _dsa_kernel(kk_ref, tq_ref, vt_ref, wt_ref, o_ref, sc_ref, s_ref, acc_ref, kpos_ref, xb_ref, **static):
    j = pl.program_id(1)
    a, b = (_dsa_row(j, *(ref.at[i] for ref in (kk_ref, tq_ref, vt_ref, wt_ref, o_ref, sc_ref, s_ref,
                                                 acc_ref, kpos_ref, xb_ref)), **static) for i in range(2))
    ms = a.qk(a.select(a.scores()))

    def both(first_pair, n_pair, carry):
        ls, extremes = carry
        ls = a.exp_pv(ms, first_pair, n_pair, ls)
        return ls, b.group(b.score_chunk, 2 * first_pair, 2 * n_pair, extremes)

    carry = (a.exp_init, b.score_init)
    carry = lax.fori_loop(0, a.n_pairs // 2, lambda i, c: both(2 * i, 2, c), carry)
    ls, extremes = lax.fori_loop(0, a.n_pairs % 2, lambda i, c: both(a.n_pairs - 1, 1, c), carry)
    a.finish(ls)

    ms = b.qk(b.select(extremes))
    ls = b.exp_init
    ls = lax.fori_loop(0, b.n_pairs // 2, lambda i, c: b.exp_pv(ms, 2 * i, 2, c), ls)
    ls = lax.fori_loop(0, b.n_pairs % 2, lambda i, c: b.exp_pv(ms, b.n_pairs - 1, 1, c), ls)
    b.finish(ls)


def _dsa(proj_r, proj_t, w_t):
    B, T, _ = proj_r.shape
    k_sel = min(TOPK_MAX, T // 4)
    slopes = _static_alibi_slopes(N_HEADS_B)
    assert T // POS_SPLIT <= 256 and POS_SPLIT <= 256 and B % 2 == 0
    return pl.pallas_call(
        functools.partial(_dsa_kernel, seq=T, k_sel=k_sel, snap_unroll=4, slopes=slopes),
        grid=(B // 2, T // DSA_Q),
        in_specs=[
            pl.BlockSpec((2, T, 2 * LANES), lambda b, q: (b, 0, R_KB // (2 * LANES))),
            pl.BlockSpec((2, T_VB, DSA_Q), lambda b, q: (b, 0, q)),
            pl.BlockSpec((2, HEAD_DIM, T), lambda b, q: (b, T_VB // HEAD_DIM, 0)),
            pl.BlockSpec((2, WI_ROWS, DSA_Q), lambda b, q: (b, 0, q)),
        ],
        out_specs=pl.BlockSpec((2, DSA_Q, W_B), lambda b, q: (b, q, 0)),
        out_shape=jax.ShapeDtypeStruct((B, T, W_B), F32),
        scratch_shapes=[pltpu.VMEM((2, T, DSA_Q), F32),
                        pltpu.VMEM((2, N_HEADS_B, T, DSA_Q), F32),
                        pltpu.VMEM((2, W_B, DSA_Q), F32),
                        pltpu.VMEM((2, T, LANES), BF16),
                        pltpu.VMEM((2, T, DSA_Q), BF16)],
        compiler_params=pltpu.CompilerParams(
            dimension_semantics=("parallel", "arbitrary"), vmem_limit_bytes=VMEM_LIMIT),
        name="dsa",
    )(proj_r, proj_t, proj_t, w_t)


def _out_kernel(x_ref, oa_ref, ob_ref, gate_ref, qm_ref, mem_ref, gm_ref, wm_ref, wo_ref, g_ref, y_ref,
                mix_ref, kv_ref, *, sub):
    tm = x_ref.shape[0]

    @pl.when(pl.program_id(1) == 0)
    def _():
        kv_ref[...] = _dot(_rms(mem_ref[...], gm_ref[...]).astype(BF16), wm_ref[...]).astype(BF16)

    subs = [slice(r0, r0 + sub) for r0 in range(0, tm, sub)]
    lane = lax.broadcasted_iota(I32, (1, W_M), 1)
    head_lanes = [(lane >= h * HEAD_DIM) & (lane < (h + 1) * HEAD_DIM) for h in range(N_HEADS_MEM)]
    km = kv_ref[:, 0:W_M]
    vm = kv_ref[:, W_M:2 * W_M]
    zero = jnp.zeros((), BF16)
    vm_heads = [jnp.where(head_lanes[h], vm, zero) for h in range(N_HEADS_MEM)]

    def silu(g):
        return g * jax.nn.sigmoid(g)

    def gated(o, rows, lo, width):
        return (o * silu(gate_ref[rows, lo:lo + width].astype(F32))).astype(BF16)

    for rows in subs:
        qm = qm_ref[rows, :]
        om = None
        for h in range(N_HEADS_MEM):
            s = _dot_nt(jnp.where(head_lanes[h], qm, zero), km)
            p = jnp.exp2(s - jnp.max(s, axis=1, keepdims=True))
            p = p / jnp.sum(p, axis=1, keepdims=True)
            o = _dot(p.astype(BF16), vm_heads[h])
            om = o if om is None else om + o
        mix_ref[rows, W_A + W_B:] = gated(om, rows, W_A + W_B, W_M)

    for rows in subs:
        half = W_A // 2
        mix_ref[rows, 0:half] = gated(oa_ref[rows, 0:half], rows, 0, half)
        mix_ref[rows, half:W_A] = gated(oa_ref[rows, half:W_A], rows, half, half)
        mix_ref[rows, W_A:W_A + W_B] = gated(ob_ref[rows, :], rows, W_A, W_B)
        y_ref[rows, :] = _dot(mix_ref[rows, :], wo_ref[...])

    for rows in subs:
        y_ref[rows, :] = _rms(x_ref[rows, :] + y_ref[rows, :], g_ref[...])


def _out(x, o_a, o_b, proj_r, mem, g_mem, w_mem, w_out, g, tm=1024, sub=256):
    B, T, D = x.shape
    M = mem.shape[1]
    row = lambda b, i: (b, i, 0)
    return pl.pallas_call(
        functools.partial(_out_kernel, sub=sub),
        grid=(B, T // tm),
        in_specs=[
            pl.BlockSpec((None, tm, D), row),
            pl.BlockSpec((None, tm, W_A), row),
            pl.BlockSpec((None, tm, W_B), row),
            pl.BlockSpec((None, tm, MIX_WIDTH), lambda b, i: (b, i, R_GATE // MIX_WIDTH)),
            pl.BlockSpec((None, tm, W_M), lambda b, i: (b, i, R_QM // W_M)),
            pl.BlockSpec((None, M, D), lambda b, i: (b, 0, 0)),
            pl.BlockSpec((1, D), lambda b, i: (0, 0)),
            pl.BlockSpec(w_mem.shape, lambda b, i: (0, 0)),
            pl.BlockSpec(w_out.shape, lambda b, i: (0, 0)),
            pl.BlockSpec((1, D), lambda b, i: (0, 0)),
        ],
        out_specs=pl.BlockSpec((None, tm, D), row),
        out_shape=jax.ShapeDtypeStruct((B, T, D), F32),
        scratch_shapes=[pltpu.VMEM((tm, MIX_WIDTH), BF16), pltpu.VMEM((M, 2 * W_M), BF16)],
        compiler_params=pltpu.CompilerParams(
            dimension_semantics=("parallel", "arbitrary"), vmem_limit_bytes=VMEM_LIMIT),
        name="out",
    )(x, o_a, o_b, proj_r, proj_r, mem, g_mem, w_mem, w_out, g)


def _alibi_slopes(n):
    return 2.0 ** (-8.0 * jnp.arange(1, n + 1, dtype=F32) / n)


def _static_alibi_slopes(n):
    return tuple(2.0 ** (-8.0 * i / n) for i in range(1, n + 1))


def _wprep_kernel(w_ref, wa_ref, wr_ref, wt_ref, ww_ref, *, bounds):
    q_a, k_a, v_a, q_b, k_b, v_b, q_m, gate, q_i, k_i, w_i, end = bounds
    scale = HEAD_DIM ** -0.5 * LOG2E
    idx_scale = IDX_DIM ** -0.5
    cols = w_ref.shape[1]

    def padded_t(lo, hi):
        return jnp.concatenate([w_ref[lo:hi, :], jnp.zeros((LANES - (hi - lo), cols), F32)], axis=0).T

    wa_ref[:, 0:W_A] = (w_ref[q_a:k_a, :] * scale).T.astype(BF16)
    wa_ref[:, W_A:2 * W_A] = w_ref[k_a:v_a, :].T.astype(BF16)
    wa_ref[:, 2 * W_A:3 * W_A] = w_ref[v_a:q_b, :].T.astype(BF16)
    half = MIX_WIDTH // 2
    wr_ref[:, R_GATE:R_GATE + half] = w_ref[gate:gate + half, :].T.astype(BF16)
    wr_ref[:, R_GATE + half:R_GATE + MIX_WIDTH] = w_ref[gate + half:q_i, :].T.astype(BF16)
    wr_ref[:, R_QM:R_QM + W_M] = (w_ref[q_m:gate, :] * scale).T.astype(BF16)
    wr_ref[:, R_KB:R_KB + LANES] = padded_t(k_b, v_b).astype(BF16)
    wr_ref[:, R_KI:R_KI + LANES] = padded_t(k_i, w_i).astype(BF16)
    wt_ref[T_QI:T_QB, :] = (w_ref[q_i:k_i, :] * idx_scale).astype(BF16)
    wt_ref[T_QB:T_VB, :] = (w_ref[q_b:k_b, :] * scale).astype(BF16)
    wt_ref[T_VB:T_ROWS, :] = w_ref[v_b:q_m, :].astype(BF16)
    ww_ref[...] = jnp.concatenate(
        [w_ref[w_i:end, :], jnp.zeros((WI_ROWS - N_IDX_HEADS, cols), F32)], axis=0).astype(BF16)


def _split_weights(w):
    D = w.shape[0]
    cols = LANES
    bounds = [0]
    for s in SPLIT_SIZES:
        bounds.append(bounds[-1] + s)
    return pl.pallas_call(
        functools.partial(_wprep_kernel, bounds=tuple(bounds)),
        grid=(D // cols,),
        in_specs=[pl.BlockSpec((w.shape[1], cols), lambda i: (0, i))],
        out_specs=[
            pl.BlockSpec((cols, 3 * W_A), lambda i: (i, 0)),
            pl.BlockSpec((cols, R_COLS), lambda i: (i, 0)),
            pl.BlockSpec((T_ROWS, cols), lambda i: (0, i)),
            pl.BlockSpec((WI_ROWS, cols), lambda i: (0, i)),
        ],
        out_shape=[
            jax.ShapeDtypeStruct((D, 3 * W_A), BF16),
            jax.ShapeDtypeStruct((D, R_COLS), BF16),
            jax.ShapeDtypeStruct((T_ROWS, D), BF16),
            jax.ShapeDtypeStruct((WI_ROWS, D), BF16),
        ],
        compiler_params=pltpu.CompilerParams(
            dimension_semantics=("parallel",), vmem_limit_bytes=VMEM_LIMIT),
        name="wprep",
    )(w.T)


def kernel(x, mem, g_in, g_mem, w_in, w_mem_kv, w_out, g_final):
    assert g_in.shape[0] == 1, "single-layer block: the final RMSNorm is fused into the output kernel"
    wa, wr, wt, ww = _split_weights(w_in[0])
    qkv_a, proj_r, proj_t, w_t = _proj(x, g_in, wa, wr, wt, ww)
    o_a = _attn_a(_alibi_slopes(N_HEADS_A), qkv_a)
    o_b = _dsa(proj_r, proj_t, w_t)
    return _out(x, o_a, o_b, proj_r, mem, g_mem, w_mem_kv[0].astype(BF16), w_out[0].astype(BF16),
                g_final[None, :])
```

```python
import functools
import math
import types

import numpy as np

import jax
import jax.numpy as jnp
from jax import lax
from jax.experimental import pallas as pl
from jax.experimental.pallas import tpu as pltpu

F32 = jnp.float32
BF16 = jnp.bfloat16
I32 = jnp.int32

D_MODEL = 1024
HEAD_DIM = 64
N_HEADS_A = 8
N_HEADS_B = 4
N_HEADS_MEM = 4
W_A = N_HEADS_A * HEAD_DIM
W_B = N_HEADS_B * HEAD_DIM
W_M = N_HEADS_MEM * HEAD_DIM
MIX_WIDTH = W_A + W_B + W_M
DILATIONS = (1, 4, 16)
BAND = 128
N_IDX_HEADS = 8
IDX_DIM = 64
TOPK_MAX = 256
RMS_EPS = 1e-6
SPLIT_SIZES = (W_A, W_A, W_A, W_B, HEAD_DIM, HEAD_DIM, W_M, MIX_WIDTH,
               N_IDX_HEADS * IDX_DIM, IDX_DIM, N_IDX_HEADS)

LANES = 128
BLK = 128
NEG = -1e30
VMEM_LIMIT = 48 * 1024 * 1024

R_GATE = 0
R_QM = MIX_WIDTH
R_KB = R_QM + W_M
R_KI = R_KB + LANES
R_COLS = R_KI + LANES
T_QI = 0
T_QB = N_IDX_HEADS * IDX_DIM
T_VB = T_QB + W_B
T_ROWS = T_VB + HEAD_DIM
WI_ROWS = 16
DSA_Q = 256
DSA_COARSE = 10
DSA_HALVINGS = 3
POS_SPLIT = 64
LOG2E = math.log2(math.e)


def _dot(a, b):
    return jnp.dot(a, b, preferred_element_type=F32)


def _dot_nt(a, b):
    return lax.dot_general(a, b, (((1,), (1,)), ((), ())), preferred_element_type=F32)


def _rms(x, g):
    return x * lax.rsqrt(jnp.mean(x * x, axis=-1, keepdims=True) + RMS_EPS) * g


def _proj_kernel(x_ref, g_ref, wa_ref, wr_ref, wt_ref, ww_ref,
                 oa_ref, oa4_ref, oa16_ref, or_ref, ot_ref, ow_ref, res_ref, res4_ref):
    hb = _rms(x_ref[0], g_ref[...]).astype(BF16)
    tm = hb.shape[0]
    d4, d16 = DILATIONS[1], DILATIONS[2]
    step = d16 // d4
    res = _dot(hb, wa_ref[...])
    oa_ref[0] = res.astype(BF16)
    for grp in range(res.shape[1] // LANES):
        lanes = slice(grp * LANES, (grp + 1) * LANES)
        res_ref[grp] = res[:, lanes]
        for r in range(d4):
            rows4 = res_ref[grp, pl.ds(r, tm // d4, stride=d4), :]
            oa4_ref[0, r, :, lanes] = rows4.astype(BF16)
            res4_ref[grp, r * (tm // d4):(r + 1) * (tm // d4), :] = rows4
        for r in range(d16):
            start = (r % d4) * (tm // d4) + r // d4
            oa16_ref[0, r, :, lanes] = res4_ref[grp, pl.ds(start, tm // d16, stride=step), :].astype(BF16)
    or_ref[0] = _dot(hb, wr_ref[...]).astype(BF16)
    ot_ref[0] = _dot_nt(wt_ref[...], hb).astype(BF16)
    ow_ref[0] = _dot_nt(ww_ref[...], hb)


def _proj(x, g, wa, wr, wt, ww, tm=512):
    B, T, D = x.shape
    const = lambda b, i: (0, 0)
    d4, d16 = DILATIONS[1], DILATIONS[2]
    nat, cm4, cm16, proj_r, proj_t, w_t = pl.pallas_call(
        _proj_kernel,
        grid=(B, T // tm),
        in_specs=[
            pl.BlockSpec((1, tm, D), lambda b, i: (b, i, 0)),
            pl.BlockSpec((1, D), const),
            pl.BlockSpec(wa.shape, const),
            pl.BlockSpec(wr.shape, const),
            pl.BlockSpec(wt.shape, const),
            pl.BlockSpec(ww.shape, const),
        ],
        out_specs=[
            pl.BlockSpec((1, tm, 3 * W_A), lambda b, i: (b, i, 0)),
            pl.BlockSpec((1, d4, tm // d4, 3 * W_A), lambda b, i: (b, 0, i, 0)),
            pl.BlockSpec((1, d16, tm // d16, 3 * W_A), lambda b, i: (b, 0, i, 0)),
            pl.BlockSpec((1, tm, R_COLS), lambda b, i: (b, i, 0)),
            pl.BlockSpec((1, T_ROWS, tm), lambda b, i: (b, 0, i)),
            pl.BlockSpec((1, WI_ROWS, tm), lambda b, i: (b, 0, i)),
        ],
        out_shape=[
            jax.ShapeDtypeStruct((B, T, 3 * W_A), BF16),
            jax.ShapeDtypeStruct((B, d4, T // d4, 3 * W_A), BF16),
            jax.ShapeDtypeStruct((B, d16, T // d16, 3 * W_A), BF16),
            jax.ShapeDtypeStruct((B, T, R_COLS), BF16),
            jax.ShapeDtypeStruct((B, T_ROWS, T), BF16),
            jax.ShapeDtypeStruct((B, WI_ROWS, T), F32),
        ],
        scratch_shapes=[pltpu.VMEM((3 * W_A // LANES, tm, LANES), F32)] * 2,
        compiler_params=pltpu.CompilerParams(
            dimension_semantics=("parallel", "parallel"), vmem_limit_bytes=VMEM_LIMIT),
        name="proj",
    )(x, g, wa, wr, wt, ww)
    return (nat, cm4.reshape(nat.shape), cm16.reshape(nat.shape)), proj_r, proj_t, w_t


def _attn_a_kernel(slopes_ref, *refs, seq, unroll):
    n_pat = len(DILATIONS)
    qkv = [refs[3 * p:3 * p + 3] for p in range(n_pat)]
    o_ref, acc_ref, m_ref, bias_ref, p_ref = refs[3 * n_pat:]
    hp = pl.program_id(1)
    lane = lax.broadcasted_iota(I32, (1, LANES), 1)
    head_lanes = (lane < HEAD_DIM, lane >= HEAD_DIM)
    row = lax.broadcasted_iota(I32, (BLK, BLK), 0)
    col = lax.broadcasted_iota(I32, (BLK, BLK), 1)
    d_cur = (row - col).astype(F32)
    d_prev = d_cur + float(BAND)
    n_all = seq // BLK
    zero = jnp.zeros((), BF16)
    one = jnp.ones((), BF16)

    def n_blocks(p):
        return n_all // DILATIONS[p]

    def n_keys(p):
        return BLK if n_blocks(p) == 1 else 2 * BLK

    def block(g):
        return pl.ds(pl.multiple_of(g * BLK, BLK), BLK)

    def keys(p, g):
        return pl.ds(pl.multiple_of(jnp.maximum((g + 1) * BLK - n_keys(p), 0), BLK), n_keys(p))

    def variant(p, g):
        g = jnp.asarray(g, I32)
        return jnp.where(g == 0, 2, jnp.where(g % n_blocks(p) == 0, 1, 0))

    def pitch(dil):
        return seq // dil + 1 if seq // dil == BLK and dil > 1 else None

    def put(ref, h, p, g, val):
        dil = DILATIONS[p]
        start = g // n_blocks(p) + (g % n_blocks(p)) * (dil * BLK)
        if dil == 1:
            ref[h, p, pl.ds(pl.multiple_of(start, BLK), BLK), :] = val
        elif pitch(dil):
            ref[h, p, pl.ds(start * pitch(dil), BLK), :] = val
        else:
            ref[h, p, pl.ds(start, BLK, stride=dil), :] = val

    def probs(p):
        q_ref, k_ref, _ = qkv[p]

        def body(g):
            kw = k_ref[keys(p, g), :]
            q = q_ref[block(g), :]
            for h in range(2):
                s = _dot_nt(jnp.where(head_lanes[h], q, zero), kw)
                s = s + bias_ref[p, h, variant(p, g), :, 2 * BLK - n_keys(p):]
                m = jnp.max(s, axis=1, keepdims=True)
                p_ref[p % 2, h, g, :, :n_keys(p)] = jnp.exp2(s - m).astype(BF16)
                put(m_ref, h, p, g, jnp.broadcast_to(m, (BLK, LANES)))
        return body

    def values(p):
        v_ref = qkv[p][2]

        def body(g):
            vw = v_ref[keys(p, g), :]
            for h in range(2):
                put(acc_ref, h, p, g,
                    _dot(p_ref[p % 2, h, g, :, :n_keys(p)], jnp.where(head_lanes[h], vw, one)))
        return body

    def run(*bodies):
        def step(g, carry):
            for body in bodies:
                body(g)
            return carry
        lax.fori_loop(0, n_all, step, 0, unroll=unroll)

    for p, dil in enumerate(DILATIONS):
        for h in range(2):
            sd = slopes_ref[hp * 2 + h] * (float(dil) * LOG2E)
            cur = jnp.where(row >= col, -sd * d_cur, NEG)
            masked = jnp.full((BLK, BLK), NEG, F32)
            bias_ref[p, h, 0, :, BLK:] = cur
            bias_ref[p, h, 1, :, BLK:] = cur
            bias_ref[p, h, 2, :, BLK:] = cur if n_keys(p) == BLK else masked
            if n_keys(p) > BLK:
                bias_ref[p, h, 0, :, :BLK] = jnp.where(col >= row, -sd * d_prev, NEG)
                bias_ref[p, h, 1, :, :BLK] = masked
                bias_ref[p, h, 2, :, :BLK] = cur

    assert n_pat == 3
    run(probs(0))
    run(values(0), probs(1))
    run(values(1), probs(2))
    run(values(2))

    rows_per_step = 2 * BLK

    def natural_rows(ref, h, p, i):
        dil = DILATIONS[p]
        if not pitch(dil):
            return ref[h, p, pl.ds(pl.multiple_of(i * rows_per_step, rows_per_step), rows_per_step), :]
        per = rows_per_step // dil
        return jnp.concatenate(
            [ref[h, p, pl.ds(i * per + u, dil, stride=pitch(dil)), :] for u in range(per)], axis=0)

    def merge(i, carry):
        sl = pl.ds(pl.multiple_of(i * rows_per_step, rows_per_step), rows_per_step)
        nums = []
        for h in range(2):
            ms = [natural_rows(m_ref, h, p, i) for p in range(n_pat)]
            mx = functools.reduce(jnp.maximum, ms)
            nums.append(sum(jnp.exp2(ms[p] - mx) * natural_rows(acc_ref, h, p, i) for p in range(n_pat)))
        acc = jnp.where(head_lanes[0], nums[0], nums[1])
        den = pltpu.roll(jnp.where(head_lanes[0], nums[1], nums[0]), HEAD_DIM, axis=1)
        o_ref[sl, :] = acc / den
        return carry

    lax.fori_loop(0, seq // rows_per_step, merge, 0, unroll=4)


def _attn_a(slopes, qkv_layouts, unroll=16):
    B, T, _ = qkv_layouts[0].shape
    n_pairs = N_HEADS_A // 2
    n_pat = len(DILATIONS)

    def spec(off):
        return pl.BlockSpec((None, T, LANES), lambda b, h: (b, 0, off + h))

    return pl.pallas_call(
        functools.partial(_attn_a_kernel, seq=T, unroll=unroll),
        grid=(B, n_pairs),
        in_specs=[pl.BlockSpec(memory_space=pltpu.SMEM)]
        + [spec(part * n_pairs) for _ in range(n_pat) for part in range(3)],
        out_specs=pl.BlockSpec((None, T, LANES), lambda b, h: (b, 0, h)),
        out_shape=jax.ShapeDtypeStruct((B, T, W_A), F32),
        scratch_shapes=[pltpu.VMEM((2, n_pat, T + max(DILATIONS), LANES), F32),
                        pltpu.VMEM((2, n_pat, T + max(DILATIONS), LANES), F32),
                        pltpu.VMEM((n_pat, 2, 3, BLK, 2 * BLK), F32),
                        pltpu.VMEM((2, 2, T // BLK, BLK, 2 * BLK), BF16)],
        compiler_params=pltpu.CompilerParams(
            dimension_semantics=("parallel", "parallel"), vmem_limit_bytes=VMEM_LIMIT),
        name="attn_a",
    )(slopes, *[a for a in qkv_layouts for _ in range(3)])


def _bf16_pair(c):
    hi = float(np.asarray(c, np.float32).astype(BF16).astype(np.float32))
    lo = float(np.asarray(c - hi, np.float32).astype(BF16).astype(np.float32))
    return hi, lo


def _fold(x, op):
    return functools.reduce(op, [x[r:r + 8] for r in range(0, x.shape[0], 8)])


def _dsa_row(j, kk_ref, tq_ref, vt_ref, wt_ref, o_ref, sc_ref, s_ref, acc_ref, kpos_ref, xb_ref,
             *, seq, k_sel, snap_unroll, slopes):
    n_pairs = j + 1
    t_lane = j * DSA_Q + lax.broadcasted_iota(I32, (1, DSA_Q), 1)
    row = lax.broadcasted_iota(I32, (BLK, DSA_Q), 0)
    ws = wt_ref[0:N_IDX_HEADS, :] * (N_IDX_HEADS ** -0.5)
    inf = float("inf")
    few = t_lane < k_sel

    def rows(c):
        return pl.ds(pl.multiple_of(c * BLK, BLK), BLK)

    def group(fn, first, n, carry):
        for u in range(n):
            carry = fn(first + u, carry)
        return carry

    def each_chunk(fn, init, widest=4):
        carry, done, width = init, 0, widest
        while width >= 2:
            steps = (2 * n_pairs - done) // width
            carry = lax.fori_loop(
                0, steps, lambda i, c, done=done, width=width: group(fn, done + width * i, width, c), carry)
            done, width = done + steps * width, width // 2
        return carry

    def sub_reduce(x, op):
        return op(x, axis=0, keepdims=True)

    def score_chunk(c, carry):
        mn, mx = carry
        ki = kk_ref[rows(c), R_KI - R_KB:R_KI - R_KB + IDX_DIM]
        lgs = [_dot(ki, tq_ref[T_QI + h * IDX_DIM:T_QI + (h + 1) * IDX_DIM, :])
               for h in range(N_IDX_HEADS)]
        sc = functools.reduce(
            lambda a, b: a + b, [jnp.maximum(lg, 0.0) * ws[h:h + 1, :] for h, lg in enumerate(lgs)])
        causal = (c * BLK + row) <= t_lane
        sc_ref[rows(c), :] = jnp.where(causal, sc, -inf)
        xb_ref[rows(c), :] = jnp.where(causal, sc, -inf).astype(BF16)
        mn = jnp.minimum(mn, _fold(jnp.where(causal, sc, inf), jnp.minimum))
        mx = jnp.maximum(mx, _fold(jnp.where(causal, sc, -inf), jnp.maximum))
        return mn, mx

    score_init = (jnp.full((8, DSA_Q), inf, F32), jnp.full((8, DSA_Q), -inf, F32))

    def scores():
        return each_chunk(score_chunk, score_init, widest=8)

    def count(pred):
        def body(c, cnt):
            return cnt + _fold(pred(sc_ref[rows(c), :], c).astype(I32), jnp.add)
        return sub_reduce(each_chunk(body, jnp.zeros((8, DSA_Q), I32)), jnp.sum)

    def halve(_, bounds):
        lo, hi, n_lo, n_hi = bounds
        mid = 0.5 * lo + 0.5 * hi
        cnt = count(lambda x, c: x >= mid)
        active = lo < hi
        up = active & (cnt >= k_sel)
        down = active & (cnt <= k_sel)
        return (jnp.where(up, mid, lo), jnp.where(down, mid, hi),
                jnp.where(up, cnt, n_lo), jnp.where(down, cnt, n_hi))

    def coarse(_, bounds):
        lo, hi = bounds
        mid = (0.5 * lo + 0.5 * hi).astype(BF16)

        def body(c, cnt):
            hit = jnp.where(xb_ref[rows(c), :] >= mid, jnp.ones((), BF16), jnp.zeros((), BF16))
            return cnt + functools.reduce(lambda a, b: a + b, [hit[r:r + 16] for r in range(0, BLK, 16)])

        cnt = each_chunk(body, jnp.zeros((16, DSA_Q), BF16))
        cnt = jnp.sum(cnt.astype(F32), axis=0, keepdims=True)
        mid = mid.astype(F32)
        active = lo < hi
        enough = cnt >= k_sel
        margin = jnp.abs(mid) * 2.0 ** -7 + 1e-30
        return (jnp.where(active & enough, jnp.maximum(lo, mid - margin), lo),
                jnp.where(active & jnp.logical_not(enough), jnp.minimum(hi, mid), hi))

    def snap(bounds):
        lo, hi, n_lo, n_hi = bounds
        mid = 0.5 * lo + 0.5 * hi
        mid = jnp.where(mid > lo, mid, hi)

        def body(c, carry):
            cnt, above, below = carry
            x = sc_ref[rows(c), :]
            ge = x >= mid
            return (cnt + _fold(ge.astype(I32), jnp.add),
                    jnp.minimum(above, _fold(jnp.where(ge, x, inf), jnp.minimum)),
                    jnp.maximum(below, _fold(jnp.where(ge, -inf, x), jnp.maximum)))

        cnt, above, below = each_chunk(body, (jnp.zeros((8, DSA_Q), I32),
                                              jnp.full((8, DSA_Q), inf, F32),
                                              jnp.full((8, DSA_Q), -inf, F32)))
        cnt = sub_reduce(cnt, jnp.sum)
        above = sub_reduce(above, jnp.min)
        below = sub_reduce(below, jnp.max)
        active = lo < hi
        enough = cnt >= k_sel
        up = active & enough
        down = active & (cnt <= k_sel)
        return (jnp.where(up, above, lo), jnp.where(down, jnp.where(enough, above, below), hi),
                jnp.where(up, cnt, n_lo), jnp.where(down, cnt, n_hi))

    def snaps(bounds):
        for _ in range(snap_unroll):
            bounds = snap(bounds)
        return bounds

    def unsettled(bounds):
        return jnp.max((bounds[0] < bounds[1]).astype(F32)) > 0

    n_idx_bits = seq.bit_length() - 1
    assert 1 << n_idx_bits == seq

    def select(extremes):
        mn, mx = extremes
        lo = jnp.where(few, -inf, sub_reduce(mn, jnp.min))
        hi = jnp.where(few, -inf, sub_reduce(mx, jnp.max))
        lo, hi = lax.fori_loop(0, DSA_COARSE, coarse, (lo, hi))

        def recount(c, carry):
            x = sc_ref[rows(c), :]
            return (carry[0] + _fold((x >= lo).astype(I32), jnp.add),
                    carry[1] + _fold((x > hi).astype(I32), jnp.add))

        n_lo, n_hi = each_chunk(recount, (jnp.zeros((8, DSA_Q), I32),) * 2)
        bounds = (lo, hi, sub_reduce(n_lo, jnp.sum), sub_reduce(n_hi, jnp.sum))
        bounds = lax.fori_loop(0, DSA_HALVINGS, halve, bounds)

        def state(bounds):
            lo, hi, n_lo, _ = bounds
            tied = (n_lo > k_sel) & jnp.logical_not(few)
            return jnp.max(jnp.where(lo < hi, 2.0, tied.astype(F32)))

        def snap_more(bounds):
            bounds = lax.while_loop(unsettled, snaps, bounds)
            return bounds + (state(bounds),)

        bounds = snaps(bounds)
        code = state(bounds)
        tau, _, n_ge, n_gt, code = lax.cond(code >= 2, lambda: snap_more(bounds), lambda: bounds + (code,))
        need = k_sel - n_gt
        tied = (n_ge > k_sel) & jnp.logical_not(few)

        def tie_break():
            def idx_step(it, jp):
                cand = jp + lax.shift_left(jnp.int32(1), n_idx_bits - 1 - it)
                below = count(lambda x, c: (x == tau) & ((c * BLK + row) < cand))
                return jnp.where(below < need, cand, jp)
            last = lax.fori_loop(0, n_idx_bits, idx_step, jnp.zeros((1, DSA_Q), I32))
            return jnp.where(tied, last, seq)

        surplus = code == 1
        tie_last = lax.cond(surplus, tie_break, lambda: jnp.full((1, DSA_Q), seq, I32))
        return tau, tie_last, surplus

    def qk(selection):
        tau, tie_last, surplus = selection

        @pl.when(j == 0)
        def _():
            pos = lax.broadcasted_iota(I32, (seq, LANES), 0)
            ln = lax.broadcasted_iota(I32, (seq, LANES), 1) - HEAD_DIM
            feat = jnp.where(ln == 0, pos // POS_SPLIT,
                             jnp.where(ln == 1, pos % POS_SPLIT,
                                       jnp.where(ln == 2, pos // POS_SPLIT,
                                                 jnp.where(ln == 3, pos % POS_SPLIT, 0))))
            kpos_ref[...] = kk_ref[:, 0:LANES] + feat.astype(BF16)

        q_row = lax.broadcasted_iota(I32, (HEAD_DIM, DSA_Q), 0)
        q_aug = []
        for h in range(N_HEADS_B):
            c_hi, c_lo = _bf16_pair(slopes[h] * LOG2E)
            feat = jnp.where(q_row == 0, POS_SPLIT * c_hi,
                             jnp.where(q_row == 1, c_hi,
                                       jnp.where(q_row == 2, POS_SPLIT * c_lo,
                                                 jnp.where(q_row == 3, c_lo, 0.0))))
            q_aug.append(jnp.concatenate(
                [tq_ref[T_QB + h * HEAD_DIM:T_QB + (h + 1) * HEAD_DIM, :], feat.astype(BF16)], axis=0))

        tau_all = jnp.where(few, float(jnp.finfo(F32).min), tau)

        def qk_phase(with_ties):
            def qk_chunk(c, ms):
                kb = kpos_ref[rows(c), :]
                x = sc_ref[rows(c), :]
                if with_ties:
                    s_idx = c * BLK + row
                    tie = jnp.where(x == tau, jnp.where(s_idx <= tie_last, 0.0, NEG), NEG)
                    sb = jnp.where(s_idx <= t_lane, jnp.where(x > tau, 0.0, tie), NEG)
                else:
                    sb = jnp.where(x >= tau_all, 0.0, NEG)
                out = []
                for h in range(N_HEADS_B):
                    s = _dot(kb, q_aug[h]) + sb
                    s_ref[h, rows(c), :] = s
                    out.append(jnp.maximum(ms[h], _fold(s, jnp.maximum)))
                return tuple(out)

            return each_chunk(qk_chunk, tuple(jnp.full((8, DSA_Q), NEG, F32) for _ in range(N_HEADS_B)),
                              widest=8)

        ms = lax.cond(surplus, lambda: qk_phase(True), lambda: qk_phase(False))
        acc_ref[...] = jnp.zeros((W_B, DSA_Q), F32)
        return tuple(sub_reduce(m, jnp.max) for m in ms)

    def exp_pv(ms, first_pair, n, ls):
        ls = list(ls)
        pvs = []
        for u in range(n):
            pair = first_pair + u
            ps = []
            for h in range(N_HEADS_B):
                halves = []
                for c in (2 * pair, 2 * pair + 1):
                    p = jnp.exp2(s_ref[h, rows(c), :] - ms[h])
                    ls[h] = ls[h] + _fold(p, jnp.add)
                    halves.append(p.astype(BF16))
                ps.append(jnp.concatenate(halves, axis=0))
            vt = vt_ref[:, pl.ds(pl.multiple_of(pair * 2 * BLK, 2 * BLK), 2 * BLK)]
            pvs.append([_dot(vt, ps[h]) for h in range(N_HEADS_B)])
        for h in range(N_HEADS_B):
            acc_ref[h * HEAD_DIM:(h + 1) * HEAD_DIM, :] += functools.reduce(
                lambda a, b: a + b, [pv[h] for pv in pvs])
        return tuple(ls)

    exp_init = tuple(jnp.zeros((8, DSA_Q), F32) for _ in range(N_HEADS_B))

    def finish(ls):
        l_all = jnp.concatenate(
            [jnp.broadcast_to(sub_reduce(ls[h], jnp.sum), (HEAD_DIM, DSA_Q)) for h in range(N_HEADS_B)],
            axis=0)
        o_ref[...] = (acc_ref[...] / l_all).T

    return types.SimpleNamespace(scores=scores, select=select, qk=qk, exp_pv=exp_pv, finish=finish,
                                 score_chunk=score_chunk, score_init=score_init, exp_init=exp_init,
                                 group=group, n_pairs=n_pairs)


def _dsa_kernel(kk_ref, tq_ref, vt_ref, wt_ref, o_ref, sc_ref, s_ref, acc_ref, kpos_ref, xb_ref, **static):
    j = pl.program_id(1)
    a, b = (_dsa_row(j, *(ref.at[i] for ref in (kk_ref, tq_ref, vt_ref, wt_ref, o_ref, sc_ref, s_ref,
                                                 acc_ref, kpos_ref, xb_ref)), **static) for i in range(2))
    ms = a.qk(a.select(a.scores()))

    def both(first_pair, n_pair, carry):
        ls, extremes = carry
        ls = a.exp_pv(ms, first_pair, n_pair, ls)
        return ls, b.group(b.score_chunk, 2 * first_pair, 2 * n_pair, extremes)

    carry = (a.exp_init, b.score_init)
    carry = lax.fori_loop(0, a.n_pairs // 2, lambda i, c: both(2 * i, 2, c), carry)
    ls, extremes = lax.fori_loop(0, a.n_pairs % 2, lambda i, c: both(a.n_pairs - 1, 1, c), carry)
    a.finish(ls)

    ms = b.qk(b.select(extremes))
    ls = b.exp_init
    ls = lax.fori_loop(0, b.n_pairs // 2, lambda i, c: b.exp_pv(ms, 2 * i, 2, c), ls)
    ls = lax.fori_loop(0, b.n_pairs % 2, lambda i, c: b.exp_pv(ms, b.n_pairs - 1, 1, c), ls)
    b.finish(ls)


def _dsa(proj_r, proj_t, w_t):
    B, T, _ = proj_r.shape
    k_sel = min(TOPK_MAX, T // 4)
    slopes = _static_alibi_slopes(N_HEADS_B)
    assert T // POS_SPLIT <= 256 and POS_SPLIT <= 256 and B % 2 == 0
    return pl.pallas_call(
        functools.partial(_dsa_kernel, seq=T, k_sel=k_sel, snap_unroll=4, slopes=slopes),
        grid=(B // 2, T // DSA_Q),
        in_specs=[
            pl.BlockSpec((2, T, 2 * LANES), lambda b, q: (b, 0, R_KB // (2 * LANES))),
            pl.BlockSpec((2, T_VB, DSA_Q), lambda b, q: (b, 0, q)),
            pl.BlockSpec((2, HEAD_DIM, T), lambda b, q: (b, T_VB // HEAD_DIM, 0)),
            pl.BlockSpec((2, WI_ROWS, DSA_Q), lambda b, q: (b, 0, q)),
        ],
        out_specs=pl.BlockSpec((2, DSA_Q, W_B), lambda b, q: (b, q, 0)),
        out_shape=jax.ShapeDtypeStruct((B, T, W_B), F32),
        scratch_shapes=[pltpu.VMEM((2, T, DSA_Q), F32),
                        pltpu.VMEM((2, N_HEADS_B, T, DSA_Q), F32),
                        pltpu.VMEM((2, W_B, DSA_Q), F32),
                        pltpu.VMEM((2, T, LANES), BF16),
                        pltpu.VMEM((2, T, DSA_Q), BF16)],
        compiler_params=pltpu.CompilerParams(
            dimension_semantics=("parallel", "arbitrary"), vmem_limit_bytes=VMEM_LIMIT),
        name="dsa",
    )(proj_r, proj_t, proj_t, w_t)


def _out_kernel(x_ref, oa_ref, ob_ref, gate_ref, qm_ref, mem_ref, gm_ref, wm_ref, wo_ref, g_ref, y_ref,
                mix_ref, kv_ref, *, sub):
    tm = x_ref.shape[0]

    kv_ref[...] = _dot(_rms(mem_ref[...], gm_ref[...]).astype(BF16), wm_ref[...]).astype(BF16)
    subs = [slice(r0, r0 + sub) for r0 in range(0, tm, sub)]
    lane = lax.broadcasted_iota(I32, (1, W_M), 1)
    head_lanes = [(lane >= h * HEAD_DIM) & (lane < (h + 1) * HEAD_DIM) for h in range(N_HEADS_MEM)]
    km = kv_ref[:, 0:W_M]
    vm = kv_ref[:, W_M:2 * W_M]
    zero = jnp.zeros((), BF16)
    vm_heads = [jnp.where(head_lanes[h], vm, zero) for h in range(N_HEADS_MEM)]

    def silu(g):
        return g * jax.nn.sigmoid(g)

    def gated(o, rows, lo, width):
        return (o * silu(gate_ref[rows, lo:lo + width].astype(F32))).astype(BF16)

    for rows in subs:
        qm = qm_ref[rows, :]
        om = None
        for h in range(N_HEADS_MEM):
            s = _dot_nt(jnp.where(head_lanes[h], qm, zero), km)
            p = jnp.exp2(s - jnp.max(s, axis=1, keepdims=True))
            p = p / jnp.sum(p, axis=1, keepdims=True)
            o = _dot(p.astype(BF16), vm_heads[h])
            om = o if om is None else om + o
        mix_ref[rows, W_A + W_B:] = gated(om, rows, W_A + W_B, W_M)

    for rows in subs:
        half = W_A // 2
        mix_ref[rows, 0:half] = gated(oa_ref[rows, 0:half], rows, 0, half)
        mix_ref[rows, half:W_A] = gated(oa_ref[rows, half:W_A], rows, half, half)
        mix_ref[rows, W_A:W_A + W_B] = gated(ob_ref[rows, :], rows, W_A, W_B)
        y_ref[rows, :] = _dot(mix_ref[rows, :], wo_ref[...])

    for rows in subs:
        y_ref[rows, :] = _rms(x_ref[rows, :] + y_ref[rows, :], g_ref[...])


def _out(x, o_a, o_b, proj_r, mem, g_mem, w_mem, w_out, g, tm=1024, sub=256):
    B, T, D = x.shape
    M = mem.shape[1]
    row = lambda b, i: (b, i, 0)
    return pl.pallas_call(
        functools.partial(_out_kernel, sub=sub),
        grid=(B, T // tm),
        in_specs=[
            pl.BlockSpec((None, tm, D), row),
            pl.BlockSpec((None, tm, W_A), row),
            pl.BlockSpec((None, tm, W_B), row),
            pl.BlockSpec((None, tm, MIX_WIDTH), lambda b, i: (b, i, R_GATE // MIX_WIDTH)),
            pl.BlockSpec((None, tm, W_M), lambda b, i: (b, i, R_QM // W_M)),
            pl.BlockSpec((None, M, D), lambda b, i: (b, 0, 0)),
            pl.BlockSpec((1, D), lambda b, i: (0, 0)),
            pl.BlockSpec(w_mem.shape, lambda b, i: (0, 0)),
            pl.BlockSpec(w_out.shape, lambda b, i: (0, 0)),
            pl.BlockSpec((1, D), lambda b, i: (0, 0)),
        ],
        out_specs=pl.BlockSpec((None, tm, D), row),
        out_shape=jax.ShapeDtypeStruct((B, T, D), F32),
        scratch_shapes=[pltpu.VMEM((tm, MIX_WIDTH), BF16), pltpu.VMEM((M, 2 * W_M), BF16)],
        compiler_params=pltpu.CompilerParams(
            dimension_semantics=("parallel", "arbitrary"), vmem_limit_bytes=VMEM_LIMIT),
        name="out",
    )(x, o_a, o_b, proj_r, proj_r, mem, g_mem, w_mem, w_out, g)


def _alibi_slopes(n):
    return 2.0 ** (-8.0 * jnp.arange(1, n + 1, dtype=F32) / n)


def _static_alibi_slopes(n):
    return tuple(2.0 ** (-8.0 * i / n) for i in range(1, n + 1))


def _wprep_kernel(w_ref, wa_ref, wr_ref, wt_ref, ww_ref, *, bounds):
    q_a, k_a, v_a, q_b, k_b, v_b, q_m, gate, q_i, k_i, w_i, end = bounds
    scale = HEAD_DIM ** -0.5 * LOG2E
    idx_scale = IDX_DIM ** -0.5
    cols = w_ref.shape[1]

    def padded_t(lo, hi):
        return jnp.concatenate([w_ref[lo:hi, :], jnp.zeros((LANES - (hi - lo), cols), F32)], axis=0).T

    wa_ref[:, 0:W_A] = (w_ref[q_a:k_a, :] * scale).T.astype(BF16)
    wa_ref[:, W_A:2 * W_A] = w_ref[k_a:v_a, :].T.astype(BF16)
    wa_ref[:, 2 * W_A:3 * W_A] = w_ref[v_a:q_b, :].T.astype(BF16)
    half = MIX_WIDTH // 2
    wr_ref[:, R_GATE:R_GATE + half] = w_ref[gate:gate + half, :].T.astype(BF16)
    wr_ref[:, R_GATE + half:R_GATE + MIX_WIDTH] = w_ref[gate + half:q_i, :].T.astype(BF16)
    wr_ref[:, R_QM:R_QM + W_M] = (w_ref[q_m:gate, :] * scale).T.astype(BF16)
    wr_ref[:, R_KB:R_KB + LANES] = padded_t(k_b, v_b).astype(BF16)
    wr_ref[:, R_KI:R_KI + LANES] = padded_t(k_i, w_i).astype(BF16)
    wt_ref[T_QI:T_QB, :] = (w_ref[q_i:k_i, :] * idx_scale).astype(BF16)
    wt_ref[T_QB:T_VB, :] = (w_ref[q_b:k_b, :] * scale).astype(BF16)
    wt_ref[T_VB:T_ROWS, :] = w_ref[v_b:q_m, :].astype(BF16)
    ww_ref[...] = jnp.concatenate(
        [w_ref[w_i:end, :], jnp.zeros((WI_ROWS - N_IDX_HEADS, cols), F32)], axis=0).astype(BF16)


def _split_weights(w):
    D = w.shape[0]
    cols = LANES
    bounds = [0]
    for s in SPLIT_SIZES:
        bounds.append(bounds[-1] + s)
    return pl.pallas_call(
        functools.partial(_wprep_kernel, bounds=tuple(bounds)),
        grid=(D // cols,),
        in_specs=[pl.BlockSpec((w.shape[1], cols), lambda i: (0, i))],
        out_specs=[
            pl.BlockSpec((cols, 3 * W_A), lambda i: (i, 0)),
            pl.BlockSpec((cols, R_COLS), lambda i: (i, 0)),
            pl.BlockSpec((T_ROWS, cols), lambda i: (0, i)),
            pl.BlockSpec((WI_ROWS, cols), lambda i: (0, i)),
        ],
        out_shape=[
            jax.ShapeDtypeStruct((D, 3 * W_A), BF16),
            jax.ShapeDtypeStruct((D, R_COLS), BF16),
            jax.ShapeDtypeStruct((T_ROWS, D), BF16),
            jax.ShapeDtypeStruct((WI_ROWS, D), BF16),
        ],
        compiler_params=pltpu.CompilerParams(
            dimension_semantics=("parallel",), vmem_limit_bytes=VMEM_LIMIT),
        name="wprep",
    )(w.T)


def kernel(x, mem, g_in, g_mem, w_in, w_mem_kv, w_out, g_final):
    assert g_in.shape[0] == 1, "single-layer block: the final RMSNorm is fused into the output kernel"
    wa, wr, wt, ww = _split_weights(w_in[0])
    qkv_a, proj_r, proj_t, w_t = _proj(x, g_in, wa, wr, wt, ww)
    o_a = _attn_a(_alibi_slopes(N_HEADS_A), qkv_a)
    o_b = _dsa(proj_r, proj_t, w_t)
    return _out(x, o_a, o_b, proj_r, mem, g_mem, w_mem_kv[0].astype(BF16), w_out[0].astype(BF16),
                g_final[None, :])
```

```python
import functools
import math
import types

import numpy as np

import jax
import jax.numpy as jnp
from jax import lax
from jax.experimental import pallas as pl
from jax.experimental.pallas import tpu as pltpu

F32 = jnp.float32
BF16 = jnp.bfloat16
I32 = jnp.int32

D_MODEL = 1024
HEAD_DIM = 64
N_HEADS_A = 8
N_HEADS_B = 4
N_HEADS_MEM = 4
W_A = N_HEADS_A * HEAD_DIM
W_B = N_HEADS_B * HEAD_DIM
W_M = N_HEADS_MEM * HEAD_DIM
MIX_WIDTH = W_A + W_B + W_M
DILATIONS = (1, 4, 16)
BAND = 128
N_IDX_HEADS = 8
IDX_DIM = 64
TOPK_MAX = 256
RMS_EPS = 1e-6
SPLIT_SIZES = (W_A, W_A, W_A, W_B, HEAD_DIM, HEAD_DIM, W_M, MIX_WIDTH,
               N_IDX_HEADS * IDX_DIM, IDX_DIM, N_IDX_HEADS)

LANES = 128
BLK = 128
NEG = -1e30
VMEM_LIMIT = 48 * 1024 * 1024

R_GATE = 0
R_QM = MIX_WIDTH
R_KB = R_QM + W_M
R_KI = R_KB + LANES
R_COLS = R_KI + LANES
T_QI = 0
T_QB = N_IDX_HEADS * IDX_DIM
T_VB = T_QB + W_B
T_ROWS = T_VB + HEAD_DIM
WI_ROWS = 16
DSA_Q = 256
DSA_COARSE = 10
DSA_HALVINGS = 3
POS_SPLIT = 64
LOG2E = math.log2(math.e)


def _dot(a, b):
    return jnp.dot(a, b, preferred_element_type=F32)


def _dot_nt(a, b):
    return lax.dot_general(a, b, (((1,), (1,)), ((), ())), preferred_element_type=F32)


def _rms(x, g):
    return x * lax.rsqrt(jnp.mean(x * x, axis=-1, keepdims=True) + RMS_EPS) * g


def _proj_kernel(x_ref, g_ref, wa_ref, wr_ref, wt_ref, ww_ref,
                 oa_ref, oa4_ref, oa16_ref, or_ref, ot_ref, ow_ref, res_ref, res4_ref):
    hb = _rms(x_ref[0], g_ref[...]).astype(BF16)
    tm = hb.shape[0]
    d4, d16 = DILATIONS[1], DILATIONS[2]
    step = d16 // d4
    res = None
    for grp in range(wa_ref.shape[1] // LANES):
        lanes = slice(grp * LANES, (grp + 1) * LANES)
        if grp % 2 == 0:
            pair = slice(grp * LANES, (grp + 2) * LANES)
            res = _dot(hb, wa_ref[:, pair])
            oa_ref[0, :, pair] = res.astype(BF16)
        res_ref[grp] = res[:, (grp % 2) * LANES:(grp % 2 + 1) * LANES]
        for r in range(d4):
            rows4 = res_ref[grp, pl.ds(r, tm // d4, stride=d4), :]
            oa4_ref[0, r, :, lanes] = rows4.astype(BF16)
            res4_ref[grp, r * (tm // d4):(r + 1) * (tm // d4), :] = rows4
        for r in range(d16):
            start = (r % d4) * (tm // d4) + r // d4
            oa16_ref[0, r, :, lanes] = res4_ref[grp, pl.ds(start, tm // d16, stride=step), :].astype(BF16)
    or_ref[0] = _dot(hb, wr_ref[...]).astype(BF16)
    ot_ref[0] = _dot_nt(wt_ref[...], hb).astype(BF16)
    ow_ref[0] = _dot_nt(ww_ref[...], hb)


def _proj(x, g, wa, wr, wt, ww, tm=512):
    B, T, D = x.shape
    const = lambda b, i: (0, 0)
    d4, d16 = DILATIONS[1], DILATIONS[2]
    nat, cm4, cm16, proj_r, proj_t, w_t = pl.pallas_call(
        _proj_kernel,
        grid=(B, T // tm),
        in_specs=[
            pl.BlockSpec((1, tm, D), lambda b, i: (b, i, 0)),
            pl.BlockSpec((1, D), const),
            pl.BlockSpec(wa.shape, const),
            pl.BlockSpec(wr.shape, const),
            pl.BlockSpec(wt.shape, const),
            pl.BlockSpec(ww.shape, const),
        ],
        out_specs=[
            pl.BlockSpec((1, tm, 3 * W_A), lambda b, i: (b, i, 0)),
            pl.BlockSpec((1, d4, tm // d4, 3 * W_A), lambda b, i: (b, 0, i, 0)),
            pl.BlockSpec((1, d16, tm // d16, 3 * W_A), lambda b, i: (b, 0, i, 0)),
            pl.BlockSpec((1, tm, R_COLS), lambda b, i: (b, i, 0)),
            pl.BlockSpec((1, T_ROWS, tm), lambda b, i: (b, 0, i)),
            pl.BlockSpec((1, WI_ROWS, tm), lambda b, i: (b, 0, i)),
        ],
        out_shape=[
            jax.ShapeDtypeStruct((B, T, 3 * W_A), BF16),
            jax.ShapeDtypeStruct((B, d4, T // d4, 3 * W_A), BF16),
            jax.ShapeDtypeStruct((B, d16, T // d16, 3 * W_A), BF16),
            jax.ShapeDtypeStruct((B, T, R_COLS), BF16),
            jax.ShapeDtypeStruct((B, T_ROWS, T), BF16),
            jax.ShapeDtypeStruct((B, WI_ROWS, T), F32),
        ],
        scratch_shapes=[pltpu.VMEM((3 * W_A // LANES, tm, LANES), F32)] * 2,
        compiler_params=pltpu.CompilerParams(
            dimension_semantics=("parallel", "parallel"), vmem_limit_bytes=VMEM_LIMIT),
        name="proj",
    )(x, g, wa, wr, wt, ww)
    return (nat, cm4.reshape(nat.shape), cm16.reshape(nat.shape)), proj_r, proj_t, w_t


def _attn_a_kernel(slopes_ref, *refs, seq, unroll):
    n_pat = len(DILATIONS)
    qkv = [refs[3 * p:3 * p + 3] for p in range(n_pat)]
    o_ref, acc_ref, m_ref, bias_ref, p_ref = refs[3 * n_pat:]
    hp = pl.program_id(1)
    lane = lax.broadcasted_iota(I32, (1, LANES), 1)
    head_lanes = (lane < HEAD_DIM, lane >= HEAD_DIM)
    row = lax.broadcasted_iota(I32, (BLK, BLK), 0)
    col = lax.broadcasted_iota(I32, (BLK, BLK), 1)
    d_cur = (row - col).astype(F32)
    d_prev = d_cur + float(BAND)
    n_all = seq // BLK
    zero = jnp.zeros((), BF16)
    one = jnp.ones((), BF16)

    def n_blocks(p):
        return n_all // DILATIONS[p]

    def n_keys(p):
        return BLK if n_blocks(p) == 1 else 2 * BLK

    def block(g):
        return pl.ds(pl.multiple_of(g * BLK, BLK), BLK)

    def keys(p, g):
        return pl.ds(pl.multiple_of(jnp.maximum((g + 1) * BLK - n_keys(p), 0), BLK), n_keys(p))

    def variant(p, g):
        g = jnp.asarray(g, I32)
        return jnp.where(g == 0, 2, jnp.where(g % n_blocks(p) == 0, 1, 0))

    def pitch(dil):
        return seq // dil + 1 if seq // dil == BLK and dil > 1 else None

    def put(ref, h, p, g, val):
        dil = DILATIONS[p]
        start = g // n_blocks(p) + (g % n_blocks(p)) * (dil * BLK)
        if dil == 1:
            ref[h, p, pl.ds(pl.multiple_of(start, BLK), BLK), :] = val
        elif pitch(dil):
            ref[h, p, pl.ds(start * pitch(dil), BLK), :] = val
        else:
            ref[h, p, pl.ds(start, BLK, stride=dil), :] = val

    def probs(p):
        q_ref, k_ref, _ = qkv[p]

        def body(g):
            kw = k_ref[keys(p, g), :]
            q = q_ref[block(g), :]
            for h in range(2):
                s = _dot_nt(jnp.where(head_lanes[h], q, zero), kw)
                s = s + bias_ref[p, h, variant(p, g), :, 2 * BLK - n_keys(p):]
                m = jnp.max(s, axis=1, keepdims=True)
                p_ref[p % 2, h, g, :, :n_keys(p)] = jnp.exp2(s - m).astype(BF16)
                put(m_ref, h, p, g, jnp.broadcast_to(m, (BLK, LANES)))
        return body

    def values(p):
        v_ref = qkv[p][2]

        def body(g):
            vw = v_ref[keys(p, g), :]
            for h in range(2):
                put(acc_ref, h, p, g,
                    _dot(p_ref[p % 2, h, g, :, :n_keys(p)], jnp.where(head_lanes[h], vw, one)))
        return body

    def run(*bodies):
        def step(g, carry):
            for body in bodies:
                body(g)
            return carry
        lax.fori_loop(0, n_all, step, 0, unroll=unroll)

    for p, dil in enumerate(DILATIONS):
        for h in range(2):
            sd = slopes_ref[hp * 2 + h] * (float(dil) * LOG2E)
            cur = jnp.where(row >= col, -sd * d_cur, NEG)
            masked = jnp.full((BLK, BLK), NEG, F32)
            bias_ref[p, h, 0, :, BLK:] = cur
            bias_ref[p, h, 1, :, BLK:] = cur
            bias_ref[p, h, 2, :, BLK:] = cur if n_keys(p) == BLK else masked
            if n_keys(p) > BLK:
                bias_ref[p, h, 0, :, :BLK] = jnp.where(col >= row, -sd * d_prev, NEG)
                bias_ref[p, h, 1, :, :BLK] = masked
                bias_ref[p, h, 2, :, :BLK] = cur

    assert n_pat == 3
    run(probs(0))
    run(values(0), probs(1))
    run(values(1), probs(2))
    run(values(2))

    rows_per_step = 2 * BLK

    def natural_rows(ref, h, p, i):
        dil = DILATIONS[p]
        if not pitch(dil):
            return ref[h, p, pl.ds(pl.multiple_of(i * rows_per_step, rows_per_step), rows_per_step), :]
        per = rows_per_step // dil
        return jnp.concatenate(
            [ref[h, p, pl.ds(i * per + u, dil, stride=pitch(dil)), :] for u in range(per)], axis=0)

    def merge(i, carry):
        sl = pl.ds(pl.multiple_of(i * rows_per_step, rows_per_step), rows_per_step)
        nums = []
        for h in range(2):
            ms = [natural_rows(m_ref, h, p, i) for p in range(n_pat)]
            mx = functools.reduce(jnp.maximum, ms)
            nums.append(sum(jnp.exp2(ms[p] - mx) * natural_rows(acc_ref, h, p, i) for p in range(n_pat)))
        acc = jnp.where(head_lanes[0], nums[0], nums[1])
        den = pltpu.roll(jnp.where(head_lanes[0], nums[1], nums[0]), HEAD_DIM, axis=1)
        o_ref[sl, :] = acc / den
        return carry

    lax.fori_loop(0, seq // rows_per_step, merge, 0, unroll=4)


def _attn_a(slopes, qkv_layouts, unroll=16):
    B, T, _ = qkv_layouts[0].shape
    n_pairs = N_HEADS_A // 2
    n_pat = len(DILATIONS)

    def spec(off):
        return pl.BlockSpec((None, T, LANES), lambda b, h: (b, 0, off + h))

    return pl.pallas_call(
        functools.partial(_attn_a_kernel, seq=T, unroll=unroll),
        grid=(B, n_pairs),
        in_specs=[pl.BlockSpec(memory_space=pltpu.SMEM)]
        + [spec(part * n_pairs) for _ in range(n_pat) for part in range(3)],
        out_specs=pl.BlockSpec((None, T, LANES), lambda b, h: (b, 0, h)),
        out_shape=jax.ShapeDtypeStruct((B, T, W_A), F32),
        scratch_shapes=[pltpu.VMEM((2, n_pat, T + max(DILATIONS), LANES), F32),
                        pltpu.VMEM((2, n_pat, T + max(DILATIONS), LANES), F32),
                        pltpu.VMEM((n_pat, 2, 3, BLK, 2 * BLK), F32),
                        pltpu.VMEM((2, 2, T // BLK, BLK, 2 * BLK), BF16)],
        compiler_params=pltpu.CompilerParams(
            dimension_semantics=("parallel", "parallel"), vmem_limit_bytes=VMEM_LIMIT),
        name="attn_a",
    )(slopes, *[a for a in qkv_layouts for _ in range(3)])


def _bf16_pair(c):
    hi = float(np.asarray(c, np.float32).astype(BF16).astype(np.float32))
    lo = float(np.asarray(c - hi, np.float32).astype(BF16).astype(np.float32))
    return hi, lo


def _fold(x, op):
    return functools.reduce(op, [x[r:r + 8] for r in range(0, x.shape[0], 8)])


def _dsa_row(j, kk_ref, tq_ref, vt_ref, wt_ref, o_ref, sc_ref, s_ref, acc_ref, kpos_ref, xb_ref,
             *, seq, k_sel, snap_unroll, slopes):
    n_pairs = j + 1
    t_lane = j * DSA_Q + lax.broadcasted_iota(I32, (1, DSA_Q), 1)
    row = lax.broadcasted_iota(I32, (BLK, DSA_Q), 0)
    ws = wt_ref[0:N_IDX_HEADS, :] * (N_IDX_HEADS ** -0.5)
    inf = float("inf")
    few = t_lane < k_sel

    def rows(c):
        return pl.ds(pl.multiple_of(c * BLK, BLK), BLK)

    def group(fn, first, n, carry):
        for u in range(n):
            carry = fn(first + u, carry)
        return carry

    def each_chunk(fn, init, widest=4):
        carry, done, width = init, 0, widest
        while width >= 2:
            steps = (2 * n_pairs - done) // width
            carry = lax.fori_loop(
                0, steps, lambda i, c, done=done, width=width: group(fn, done + width * i, width, c), carry)
            done, width = done + steps * width, width // 2
        return carry

    def sub_reduce(x, op):
        return op(x, axis=0, keepdims=True)

    def score_chunk(c, carry):
        mn, mx = carry
        ki = kk_ref[rows(c), R_KI - R_KB:R_KI - R_KB + IDX_DIM]
        lgs = [_dot(ki, tq_ref[T_QI + h * IDX_DIM:T_QI + (h + 1) * IDX_DIM, :])
               for h in range(N_IDX_HEADS)]
        sc = functools.reduce(
            lambda a, b: a + b, [jnp.maximum(lg, 0.0) * ws[h:h + 1, :] for h, lg in enumerate(lgs)])
        causal = (c * BLK + row) <= t_lane
        sc_ref[rows(c), :] = jnp.where(causal, sc, -inf)
        xb_ref[rows(c), :] = jnp.where(causal, sc, -inf).astype(BF16)
        mn = jnp.minimum(mn, _fold(jnp.where(causal, sc, inf), jnp.minimum))
        mx = jnp.maximum(mx, _fold(jnp.where(causal, sc, -inf), jnp.maximum))
        return mn, mx

    score_init = (jnp.full((8, DSA_Q), inf, F32), jnp.full((8, DSA_Q), -inf, F32))

    def scores():
        return each_chunk(score_chunk, score_init, widest=8)

    def count(pred):
        def body(c, cnt):
            return cnt + _fold(pred(sc_ref[rows(c), :], c).astype(I32), jnp.add)
        return sub_reduce(each_chunk(body, jnp.zeros((8, DSA_Q), I32)), jnp.sum)

    def halve(_, bounds):
        lo, hi, n_lo, n_hi = bounds
        mid = 0.5 * lo + 0.5 * hi
        cnt = count(lambda x, c: x >= mid)
        active = lo < hi
        up = active & (cnt >= k_sel)
        down = active & (cnt <= k_sel)
        return (jnp.where(up, mid, lo), jnp.where(down, mid, hi),
                jnp.where(up, cnt, n_lo), jnp.where(down, cnt, n_hi))

    def coarse(_, bounds):
        lo, hi = bounds
        mid = (0.5 * lo + 0.5 * hi).astype(BF16)

        def body(c, cnt):
            hit = jnp.where(xb_ref[rows(c), :] >= mid, jnp.ones((), BF16), jnp.zeros((), BF16))
            return cnt + functools.reduce(lambda a, b: a + b, [hit[r:r + 16] for r in range(0, BLK, 16)])

        cnt = each_chunk(body, jnp.zeros((16, DSA_Q), BF16))
        cnt = jnp.sum(cnt.astype(F32), axis=0, keepdims=True)
        mid = mid.astype(F32)
        active = lo < hi
        enough = cnt >= k_sel
        margin = jnp.abs(mid) * 2.0 ** -7 + 1e-30
        return (jnp.where(active & enough, jnp.maximum(lo, mid - margin), lo),
                jnp.where(active & jnp.logical_not(enough), jnp.minimum(hi, mid), hi))

    def snap(bounds):
        lo, hi, n_lo, n_hi = bounds
        mid = 0.5 * lo + 0.5 * hi
        mid = jnp.where(mid > lo, mid, hi)

        def body(c, carry):
            cnt, above, below = carry
            x = sc_ref[rows(c), :]
            ge = x >= mid
            return (cnt + _fold(ge.astype(I32), jnp.add),
                    jnp.minimum(above, _fold(jnp.where(ge, x, inf), jnp.minimum)),
                    jnp.maximum(below, _fold(jnp.where(ge, -inf, x), jnp.maximum)))

        cnt, above, below = each_chunk(body, (jnp.zeros((8, DSA_Q), I32),
                                              jnp.full((8, DSA_Q), inf, F32),
                                              jnp.full((8, DSA_Q), -inf, F32)))
        cnt = sub_reduce(cnt, jnp.sum)
        above = sub_reduce(above, jnp.min)
        below = sub_reduce(below, jnp.max)
        active = lo < hi
        enough = cnt >= k_sel
        up = active & enough
        down = active & (cnt <= k_sel)
        return (jnp.where(up, above, lo), jnp.where(down, jnp.where(enough, above, below), hi),
                jnp.where(up, cnt, n_lo), jnp.where(down, cnt, n_hi))

    def snaps(bounds):
        for _ in range(snap_unroll):
            bounds = snap(bounds)
        return bounds

    def unsettled(bounds):
        return jnp.max((bounds[0] < bounds[1]).astype(F32)) > 0

    n_idx_bits = seq.bit_length() - 1
    assert 1 << n_idx_bits == seq

    def select(extremes):
        mn, mx = extremes
        lo = jnp.where(few, -inf, sub_reduce(mn, jnp.min))
        hi = jnp.where(few, -inf, sub_reduce(mx, jnp.max))
        lo, hi = lax.fori_loop(0, DSA_COARSE, coarse, (lo, hi))

        def recount(c, carry):
            x = sc_ref[rows(c), :]
            return (carry[0] + _fold((x >= lo).astype(I32), jnp.add),
                    carry[1] + _fold((x > hi).astype(I32), jnp.add))

        n_lo, n_hi = each_chunk(recount, (jnp.zeros((8, DSA_Q), I32),) * 2)
        bounds = (lo, hi, sub_reduce(n_lo, jnp.sum), sub_reduce(n_hi, jnp.sum))
        bounds = lax.fori_loop(0, DSA_HALVINGS, halve, bounds)

        def state(bounds):
            lo, hi, n_lo, _ = bounds
            tied = (n_lo > k_sel) & jnp.logical_not(few)
            return jnp.max(jnp.where(lo < hi, 2.0, tied.astype(F32)))

        def snap_more(bounds):
            bounds = lax.while_loop(unsettled, snaps, bounds)
            return bounds + (state(bounds),)

        bounds = snaps(bounds)
        code = state(bounds)
        tau, _, n_ge, n_gt, code = lax.cond(code >= 2, lambda: snap_more(bounds), lambda: bounds + (code,))
        need = k_sel - n_gt
        tied = (n_ge > k_sel) & jnp.logical_not(few)

        def tie_break():
            def idx_step(it, jp):
                cand = jp + lax.shift_left(jnp.int32(1), n_idx_bits - 1 - it)
                below = count(lambda x, c: (x == tau) & ((c * BLK + row) < cand))
                return jnp.where(below < need, cand, jp)
            last = lax.fori_loop(0, n_idx_bits, idx_step, jnp.zeros((1, DSA_Q), I32))
            return jnp.where(tied, last, seq)

        surplus = code == 1
        tie_last = lax.cond(surplus, tie_break, lambda: jnp.full((1, DSA_Q), seq, I32))
        return tau, tie_last, surplus

    def qk(selection):
        tau, tie_last, surplus = selection

        @pl.when(j == 0)
        def _():
            pos = lax.broadcasted_iota(I32, (seq, LANES), 0)
            ln = lax.broadcasted_iota(I32, (seq, LANES), 1) - HEAD_DIM
            feat = jnp.where(ln == 0, pos // POS_SPLIT,
                             jnp.where(ln == 1, pos % POS_SPLIT,
                                       jnp.where(ln == 2, pos // POS_SPLIT,
                                                 jnp.where(ln == 3, pos % POS_SPLIT, 0))))
            kpos_ref[...] = kk_ref[:, 0:LANES] + feat.astype(BF16)

        q_row = lax.broadcasted_iota(I32, (HEAD_DIM, DSA_Q), 0)
        q_aug = []
        for h in range(N_HEADS_B):
            c_hi, c_lo = _bf16_pair(slopes[h] * LOG2E)
            feat = jnp.where(q_row == 0, POS_SPLIT * c_hi,
                             jnp.where(q_row == 1, c_hi,
                                       jnp.where(q_row == 2, POS_SPLIT * c_lo,
                                                 jnp.where(q_row == 3, c_lo, 0.0))))
            q_aug.append(jnp.concatenate(
                [tq_ref[T_QB + h * HEAD_DIM:T_QB + (h + 1) * HEAD_DIM, :], feat.astype(BF16)], axis=0))

        tau_all = jnp.where(few, float(jnp.finfo(F32).min), tau)

        def qk_phase(with_ties):
            def qk_chunk(c, ms):
                kb = kpos_ref[rows(c), :]
                x = sc_ref[rows(c), :]
                if with_ties:
                    s_idx = c * BLK + row
                    tie = jnp.where(x == tau, jnp.where(s_idx <= tie_last, 0.0, NEG), NEG)
                    sb = jnp.where(s_idx <= t_lane, jnp.where(x > tau, 0.0, tie), NEG)
                else:
                    sb = jnp.where(x >= tau_all, 0.0, NEG)
                out = []
                for h in range(N_HEADS_B):
                    s = _dot(kb, q_aug[h]) + sb
                    s_ref[h, rows(c), :] = s
                    out.append(jnp.maximum(ms[h], _fold(s, jnp.maximum)))
                return tuple(out)

            return each_chunk(qk_chunk, tuple(jnp.full((8, DSA_Q), NEG, F32) for _ in range(N_HEADS_B)),
                              widest=8)

        ms = lax.cond(surplus, lambda: qk_phase(True), lambda: qk_phase(False))
        acc_ref[...] = jnp.zeros((W_B, DSA_Q), F32)
        return tuple(sub_reduce(m, jnp.max) for m in ms)

    def exp_pv(ms, first_pair, n, ls):
        ls = list(ls)
        pvs = []
        for u in range(n):
            pair = first_pair + u
            ps = []
            for h in range(N_HEADS_B):
                halves = []
                for c in (2 * pair, 2 * pair + 1):
                    p = jnp.exp2(s_ref[h, rows(c), :] - ms[h])
                    ls[h] = ls[h] + _fold(p, jnp.add)
                    halves.append(p.astype(BF16))
                ps.append(jnp.concatenate(halves, axis=0))
            vt = vt_ref[:, pl.ds(pl.multiple_of(pair * 2 * BLK, 2 * BLK), 2 * BLK)]
            pvs.append([_dot(vt, ps[h]) for h in range(N_HEADS_B)])
        for h in range(N_HEADS_B):
            acc_ref[h * HEAD_DIM:(h + 1) * HEAD_DIM, :] += functools.reduce(
                lambda a, b: a + b, [pv[h] for pv in pvs])
        return tuple(ls)

    exp_init = tuple(jnp.zeros((8, DSA_Q), F32) for _ in range(N_HEADS_B))

    def finish(ls):
        l_all = jnp.concatenate(
            [jnp.broadcast_to(sub_reduce(ls[h], jnp.sum), (HEAD_DIM, DSA_Q)) for h in range(N_HEADS_B)],
            axis=0)
        o_ref[...] = (acc_ref[...] / l_all).T

    return types.SimpleNamespace(scores=scores, select=select, qk=qk, exp_pv=exp_pv, finish=finish,
                                 score_chunk=score_chunk, score_init=score_init, exp_init=exp_init,
                                 group=group, n_pairs=n_pairs)


def _dsa_kernel(kk_ref, tq_ref, vt_ref, wt_ref, o_ref, sc_ref, s_ref, acc_ref, kpos_ref, xb_ref, **static):
    j = pl.program_id(1)
    a, b = (_dsa_row(j, *(ref.at[i] for ref in (kk_ref, tq_ref, vt_ref, wt_ref, o_ref, sc_ref, s_ref,
                                                 acc_ref, kpos_ref, xb_ref)), **static) for i in range(2))
    ms = a.qk(a.select(a.scores()))

    def both(first_pair, n_pair, carry):
        ls, extremes = carry
        ls = a.exp_pv(ms, first_pair, n_pair, ls)
        return ls, b.group(b.score_chunk, 2 * first_pair, 2 * n_pair, extremes)

    carry = (a.exp_init, b.score_init)
    carry = lax.fori_loop(0, a.n_pairs // 2, lambda i, c: both(2 * i, 2, c), carry)
    ls, extremes = lax.fori_loop(0, a.n_pairs % 2, lambda i, c: both(a.n_pairs - 1, 1, c), carry)
    a.finish(ls)

    ms = b.qk(b.select(extremes))
    ls = b.exp_init
    ls = lax.fori_loop(0, b.n_pairs // 2, lambda i, c: b.exp_pv(ms, 2 * i, 2, c), ls)
    ls = lax.fori_loop(0, b.n_pairs % 2, lambda i, c: b.exp_pv(ms, b.n_pairs - 1, 1, c), ls)
    b.finish(ls)


def _dsa(proj_r, proj_t, w_t):
    B, T, _ = proj_r.shape
    k_sel = min(TOPK_MAX, T // 4)
    slopes = _static_alibi_slopes(N_HEADS_B)
    assert T // POS_SPLIT <= 256 and POS_SPLIT <= 256 and B % 2 == 0
    return pl.pallas_call(
        functools.partial(_dsa_kernel, seq=T, k_sel=k_sel, snap_unroll=4, slopes=slopes),
        grid=(B // 2, T // DSA_Q),
        in_specs=[
            pl.BlockSpec((2, T, 2 * LANES), lambda b, q: (b, 0, R_KB // (2 * LANES))),
            pl.BlockSpec((2, T_VB, DSA_Q), lambda b, q: (b, 0, q)),
            pl.BlockSpec((2, HEAD_DIM, T), lambda b, q: (b, T_VB // HEAD_DIM, 0)),
            pl.BlockSpec((2, WI_ROWS, DSA_Q), lambda b, q: (b, 0, q)),
        ],
        out_specs=pl.BlockSpec((2, DSA_Q, W_B), lambda b, q: (b, q, 0)),
        out_shape=jax.ShapeDtypeStruct((B, T, W_B), F32),
        scratch_shapes=[pltpu.VMEM((2, T, DSA_Q), F32),
                        pltpu.VMEM((2, N_HEADS_B, T, DSA_Q), F32),
                        pltpu.VMEM((2, W_B, DSA_Q), F32),
                        pltpu.VMEM((2, T, LANES), BF16),
                        pltpu.VMEM((2, T, DSA_Q), BF16)],
        compiler_params=pltpu.CompilerParams(
            dimension_semantics=("parallel", "arbitrary"), vmem_limit_bytes=VMEM_LIMIT),
        name="dsa",
    )(proj_r, proj_t, proj_t, w_t)


def _out_kernel(x_ref, oa_ref, ob_ref, gate_ref, qm_ref, mem_ref, gm_ref, wm_ref, wo_ref, g_ref, y_ref,
                mix_ref, kv_ref, *, sub):
    tm = x_ref.shape[0]

    kv_ref[...] = _dot(_rms(mem_ref[...], gm_ref[...]).astype(BF16), wm_ref[...]).astype(BF16)
    subs = [slice(r0, r0 + sub) for r0 in range(0, tm, sub)]
    lane = lax.broadcasted_iota(I32, (1, W_M), 1)
    head_lanes = [(lane >= h * HEAD_DIM) & (lane < (h + 1) * HEAD_DIM) for h in range(N_HEADS_MEM)]
    km = kv_ref[:, 0:W_M]
    vm = kv_ref[:, W_M:2 * W_M]
    zero = jnp.zeros((), BF16)
    vm_heads = [jnp.where(head_lanes[h], vm, zero) for h in range(N_HEADS_MEM)]

    def silu(g):
        return g * jax.nn.sigmoid(g)

    def gated(o, rows, lo, width):
        return (o * silu(gate_ref[rows, lo:lo + width].astype(F32))).astype(BF16)

    for rows in subs:
        qm = qm_ref[rows, :]
        om = None
        for h in range(N_HEADS_MEM):
            s = _dot_nt(jnp.where(head_lanes[h], qm, zero), km)
            p = jnp.exp2(s - jnp.max(s, axis=1, keepdims=True))
            p = p / jnp.sum(p, axis=1, keepdims=True)
            o = _dot(p.astype(BF16), vm_heads[h])
            om = o if om is None else om + o
        mix_ref[rows, W_A + W_B:] = gated(om, rows, W_A + W_B, W_M)

    for rows in subs:
        half = W_A // 2
        mix_ref[rows, 0:half] = gated(oa_ref[rows, 0:half], rows, 0, half)
        mix_ref[rows, half:W_A] = gated(oa_ref[rows, half:W_A], rows, half, half)
        mix_ref[rows, W_A:W_A + W_B] = gated(ob_ref[rows, :], rows, W_A, W_B)
        y_ref[rows, :] = _dot(mix_ref[rows, :], wo_ref[...])

    for rows in subs:
        y_ref[rows, :] = _rms(x_ref[rows, :] + y_ref[rows, :], g_ref[...])


def _out(x, o_a, o_b, proj_r, mem, g_mem, w_mem, w_out, g, tm=1024, sub=256):
    B, T, D = x.shape
    M = mem.shape[1]
    row = lambda b, i: (b, i, 0)
    return pl.pallas_call(
        functools.partial(_out_kernel, sub=sub),
        grid=(B, T // tm),
        in_specs=[
            pl.BlockSpec((None, tm, D), row),
            pl.BlockSpec((None, tm, W_A), row),
            pl.BlockSpec((None, tm, W_B), row),
            pl.BlockSpec((None, tm, MIX_WIDTH), lambda b, i: (b, i, R_GATE // MIX_WIDTH)),
            pl.BlockSpec((None, tm, W_M), lambda b, i: (b, i, R_QM // W_M)),
            pl.BlockSpec((None, M, D), lambda b, i: (b, 0, 0)),
            pl.BlockSpec((1, D), lambda b, i: (0, 0)),
            pl.BlockSpec(w_mem.shape, lambda b, i: (0, 0)),
            pl.BlockSpec(w_out.shape, lambda b, i: (0, 0)),
            pl.BlockSpec((1, D), lambda b, i: (0, 0)),
        ],
        out_specs=pl.BlockSpec((None, tm, D), row),
        out_shape=jax.ShapeDtypeStruct((B, T, D), F32),
        scratch_shapes=[pltpu.VMEM((tm, MIX_WIDTH), BF16), pltpu.VMEM((M, 2 * W_M), BF16)],
        compiler_params=pltpu.CompilerParams(
            dimension_semantics=("parallel", "arbitrary"), vmem_limit_bytes=VMEM_LIMIT),
        name="out",
    )(x, o_a, o_b, proj_r, proj_r, mem, g_mem, w_mem, w_out, g)


def _alibi_slopes(n):
    return 2.0 ** (-8.0 * jnp.arange(1, n + 1, dtype=F32) / n)


def _static_alibi_slopes(n):
    return tuple(2.0 ** (-8.0 * i / n) for i in range(1, n + 1))


def _wprep_kernel(w_ref, wa_ref, wr_ref, wt_ref, ww_ref, *, bounds):
    q_a, k_a, v_a, q_b, k_b, v_b, q_m, gate, q_i, k_i, w_i, end = bounds
    scale = HEAD_DIM ** -0.5 * LOG2E
    idx_scale = IDX_DIM ** -0.5
    cols = w_ref.shape[1]

    def padded_t(lo, hi):
        return jnp.concatenate([w_ref[lo:hi, :], jnp.zeros((LANES - (hi - lo), cols), F32)], axis=0).T

    wa_ref[:, 0:W_A] = (w_ref[q_a:k_a, :] * scale).T.astype(BF16)
    wa_ref[:, W_A:2 * W_A] = w_ref[k_a:v_a, :].T.astype(BF16)
    wa_ref[:, 2 * W_A:3 * W_A] = w_ref[v_a:q_b, :].T.astype(BF16)
    half = MIX_WIDTH // 2
    wr_ref[:, R_GATE:R_GATE + half] = w_ref[gate:gate + half, :].T.astype(BF16)
    wr_ref[:, R_GATE + half:R_GATE + MIX_WIDTH] = w_ref[gate + half:q_i, :].T.astype(BF16)
    wr_ref[:, R_QM:R_QM + W_M] = (w_ref[q_m:gate, :] * scale).T.astype(BF16)
    wr_ref[:, R_KB:R_KB + LANES] = padded_t(k_b, v_b).astype(BF16)
    wr_ref[:, R_KI:R_KI + LANES] = padded_t(k_i, w_i).astype(BF16)
    wt_ref[T_QI:T_QB, :] = (w_ref[q_i:k_i, :] * idx_scale).astype(BF16)
    wt_ref[T_QB:T_VB, :] = (w_ref[q_b:k_b, :] * scale).astype(BF16)
    wt_ref[T_VB:T_ROWS, :] = w_ref[v_b:q_m, :].astype(BF16)
    ww_ref[...] = jnp.concatenate(
        [w_ref[w_i:end, :], jnp.zeros((WI_ROWS - N_IDX_HEADS, cols), F32)], axis=0).astype(BF16)


def _split_weights(w):
    D = w.shape[0]
    cols = LANES
    bounds = [0]
    for s in SPLIT_SIZES:
        bounds.append(bounds[-1] + s)
    return pl.pallas_call(
        functools.partial(_wprep_kernel, bounds=tuple(bounds)),
        grid=(D // cols,),
        in_specs=[pl.BlockSpec((w.shape[1], cols), lambda i: (0, i))],
        out_specs=[
            pl.BlockSpec((cols, 3 * W_A), lambda i: (i, 0)),
            pl.BlockSpec((cols, R_COLS), lambda i: (i, 0)),
            pl.BlockSpec((T_ROWS, cols), lambda i: (0, i)),
            pl.BlockSpec((WI_ROWS, cols), lambda i: (0, i)),
        ],
        out_shape=[
            jax.ShapeDtypeStruct((D, 3 * W_A), BF16),
            jax.ShapeDtypeStruct((D, R_COLS), BF16),
            jax.ShapeDtypeStruct((T_ROWS, D), BF16),
            jax.ShapeDtypeStruct((WI_ROWS, D), BF16),
        ],
        compiler_params=pltpu.CompilerParams(
            dimension_semantics=("parallel",), vmem_limit_bytes=VMEM_LIMIT),
        name="wprep",
    )(w.T)


def kernel(x, mem, g_in, g_mem, w_in, w_mem_kv, w_out, g_final):
    assert g_in.shape[0] == 1, "single-layer block: the final RMSNorm is fused into the output kernel"
    wa, wr, wt, ww = _split_weights(w_in[0])
    qkv_a, proj_r, proj_t, w_t = _proj(x, g_in, wa, wr, wt, ww)
    o_a = _attn_a(_alibi_slopes(N_HEADS_A), qkv_a)
    o_b = _dsa(proj_r, proj_t, w_t)
    return _out(x, o_a, o_b, proj_r, mem, g_mem, w_mem_kv[0].astype(BF16), w_out[0].astype(BF16),
                g_final[None, :])
```
